```python
import math
import jax, jax.numpy as jnp
from jax import lax
import numpy as np

D_MODEL = 1024
BATCH = 8
SEQ = 2048
DEPTH = 2

HEAD_DIM = 64
A_HEADS = 4
A_KV_HEADS = 2
B_HEADS = 6
C_HEADS = 6
D_MIX = (A_HEADS + B_HEADS + C_HEADS) * HEAD_DIM
D_FF = 2816
GRID_W = 64
ROPE_THETA = 10000.0
Q_BLOCK = 128
DIL_PAIRS = ((128, 1), (512, 4), (2048, 16))
REL_BUCKETS = 32
REL_MAX_DIST = 1024
CONV_K = 5
CHUNK = 64
NORM_EPS = 1e-6
NEG_INF = -1e30

A_QW = A_HEADS * HEAD_DIM
A_KVW = A_KV_HEADS * HEAD_DIM
B_W = B_HEADS * HEAD_DIM
C_W = C_HEADS * HEAD_DIM
IN_SIZES = (A_QW, A_KVW, A_KVW, B_W, B_W, B_W, C_W, C_W, C_W, C_W, 2 * C_HEADS, 2 * C_HEADS)
N_IN = sum(IN_SIZES)
IN_SPLITS = [int(s) for s in np.cumsum(IN_SIZES)[:-1]]

kernel_name = 'hymba_style_hybrid_encoder'

F32 = jnp.float32


def rms_norm(x, g):
    xf = x.astype(F32)
    y = xf * lax.rsqrt(jnp.mean(xf * xf, axis=-1, keepdims=True) + NORM_EPS)
    return (y * g.astype(F32)).astype(x.dtype)


def l2norm(x):
    return x * lax.rsqrt(jnp.sum(x * x, axis=-1, keepdims=True) + NORM_EPS)


def swiglu(x, wg, wu, wd):
    return (jax.nn.silu(x @ wg) * (x @ wu)) @ wd


def axial_rope(seq):
    rows = seq // GRID_W
    row = jnp.repeat(jnp.arange(rows), GRID_W).astype(F32)
    col = jnp.tile(jnp.arange(GRID_W), rows).astype(F32)
    n_freq = HEAD_DIM // 4
    inv = ROPE_THETA ** (-jnp.arange(n_freq, dtype=F32) / n_freq)
    ang = jnp.concatenate([row[:, None] * inv, col[:, None] * inv], axis=-1)
    return jnp.cos(ang), jnp.sin(ang)


def apply_rope(x, cos, sin):
    xf = x.astype(F32)
    half = HEAD_DIM // 2
    x1, x2 = xf[..., :half], xf[..., half:]
    c, s = cos[None, :, None, :], sin[None, :, None, :]
    return jnp.concatenate([x1 * c - x2 * s, x1 * s + x2 * c], axis=-1).astype(x.dtype)


def dense_gqa(q, k, v):
    B, S, Hq, D = q.shape
    Hkv = k.shape[2]
    G = Hq // Hkv
    nb = S // Q_BLOCK
    qb = q.reshape(B, nb, Q_BLOCK, Hkv, G, D).transpose(1, 0, 2, 3, 4, 5)

    def block(qi):
        s = jnp.einsum('bqhgd,bkhd->bhgqk', qi, k).astype(F32) * (D ** -0.5)
        p = jax.nn.softmax(s, axis=-1)
        return jnp.einsum('bhgqk,bkhd->bqhgd', p.astype(v.dtype), v)

    o = lax.map(block, qb)
    return o.transpose(1, 0, 2, 3, 4, 5).reshape(B, S, Hq * D)


def t5_bucket(rel):
    half = REL_BUCKETS // 2
    exact = half // 2
    sign = jnp.where(rel > 0, half, 0)
    n = jnp.abs(rel)
    nf = jnp.maximum(n, 1).astype(F32)
    large = exact + (jnp.log(nf / exact) / math.log(REL_MAX_DIST / exact) * (half - exact)).astype(jnp.int32)
    large = jnp.minimum(large, half - 1)
    return sign + jnp.where(n < exact, n, large)


def dilated_branch(q, k, v, rel_bias, window, dil):
    B, S, H, D = q.shape
    side = window // (2 * dil)
    blk = side
    L = S // dil
    nb = -(-L // blk)
    Lp = nb * blk

    def to_sub(x):
        x = x.reshape(B, L, dil, H, D).transpose(0, 2, 1, 3, 4)
        return jnp.pad(x, ((0, 0), (0, 0), (0, Lp - L), (0, 0), (0, 0)))

    def windows(x):
        xb = to_sub(x).reshape(B, dil, nb, blk, H, D)
        xb = jnp.pad(xb, ((0, 0), (0, 0), (1, 1), (0, 0), (0, 0), (0, 0)))
        return jnp.concatenate([xb[:, :, :-2], xb[:, :, 1:-1], xb[:, :, 2:]], axis=3)

    qs = to_sub(q).reshape(B, dil, nb, blk, H, D)
    kw, vw = windows(k), windows(v)
    s = jnp.einsum('brnqhd,brnkhd->brnhqk', qs, kw).astype(F32) * (D ** -0.5)
    qi = jnp.arange(blk)
    kj = jnp.arange(3 * blk)
    delta = kj[None, :] - blk - qi[:, None]
    key_sub = jnp.arange(nb)[:, None] * blk - blk + kj[None, :]
    mask = (jnp.abs(delta) <= side)[None] & ((key_sub >= 0) & (key_sub < L))[:, None, :]
    bias = rel_bias[t5_bucket(delta * dil)].astype(F32).transpose(2, 0, 1)
    s = jnp.where(mask[:, None], s + bias, NEG_INF)
    lse = jax.nn.logsumexp(s, axis=-1)
    p = jnp.exp(s - lse[..., None])
    o = jnp.einsum('brnhqk,brnkhd->brnqhd', p.astype(v.dtype), vw)
    o = o.reshape(B, dil, Lp, H, D)[:, :, :L].transpose(0, 2, 1, 3, 4).reshape(B, S, H, D)
    lse = lse.transpose(0, 1, 2, 4, 3).reshape(B, dil, Lp, H)[:, :, :L].transpose(0, 2, 1, 3).reshape(B, S, H)
    return o, lse


def dilated_mixture(q, k, v, rel_bias):
    outs, lses = [], []
    for window, dil in DIL_PAIRS:
        o, l = dilated_branch(q, k, v, rel_bias, window, dil)
        outs.append(o)
        lses.append(l)
    wts = jax.nn.softmax(jnp.stack(lses, axis=-1), axis=-1)
    o = jnp.einsum('bshgd,bshg->bshd', jnp.stack(outs, axis=3), wts.astype(outs[0].dtype))
    B, S, H, D = q.shape
    return o.reshape(B, S, H * D)


def short_conv(x, w):
    K, C = w.shape
    return lax.conv_general_dilated(x, w[:, None, :].astype(x.dtype), window_strides=(1,),
                                    padding=[(K // 2, K // 2)],
                                    dimension_numbers=('NWC', 'WIO', 'NWC'),
                                    feature_group_count=C)


def gated_delta_chunked(q, k, v, g, beta):
    B, H, S, Dk = q.shape
    Dv = v.shape[-1]
    N = S // CHUNK
    qc = q.reshape(B, H, N, CHUNK, Dk)
    kc = k.reshape(B, H, N, CHUNK, Dk)
    vc = v.reshape(B, H, N, CHUNK, Dv)
    gc = jnp.cumsum(g.reshape(B, H, N, CHUNK), axis=-1)
    bc = beta.reshape(B, H, N, CHUNK)
    tril = jnp.tril(jnp.ones((CHUNK, CHUNK), dtype=bool))
    strict = jnp.tril(jnp.ones((CHUNK, CHUNK), dtype=bool), -1)
    diff = gc[..., :, None] - gc[..., None, :]
    decay = jnp.where(tril, jnp.exp(jnp.where(tril, diff, 0.0)), 0.0)
    kb = kc * bc[..., None]
    vb = vc * bc[..., None]
    M = jnp.where(strict, jnp.einsum('bhnid,bhnjd->bhnij', kb, kc) * decay, 0.0)
    eye = jnp.eye(CHUNK, dtype=F32)
    T = lax.linalg.triangular_solve(eye + M, jnp.broadcast_to(eye, M.shape), left_side=True,
                                    lower=True, unit_diagonal=True)
    u = jnp.einsum('bhnij,bhnjv->bhniv', T, vb)
    w = jnp.einsum('bhnij,bhnjk->bhnik', T, kb * jnp.exp(gc)[..., None])
    a_intra = jnp.einsum('bhnid,bhnjd->bhnij', qc, kc) * decay

    def step(state, xs):
        q_i, k_i, u_i, w_i, g_i, a_i = xs
        v_new = u_i - jnp.einsum('bhck,bhkv->bhcv', w_i, state)
        o = (jnp.einsum('bhck,bhkv->bhcv', q_i * jnp.exp(g_i)[..., None], state)
             + jnp.einsum('bhij,bhjv->bhiv', a_i, v_new))
        g_last = g_i[..., -1:]
        state = (state * jnp.exp(g_last)[..., None]
                 + jnp.einsum('bhck,bhcv->bhkv', k_i * jnp.exp(g_last - g_i)[..., None], v_new))
        return state, o

    xs = tuple(jnp.moveaxis(t, 2, 0) for t in (qc, kc, u, w, gc, a_intra))
    state0 = jnp.zeros((B, H, Dk, Dv), F32)
    _, o = lax.scan(step, state0, xs)
    return jnp.moveaxis(o, 0, 2).reshape(B, H, S, Dv)


def gated_deltanet_bidir(cq, ck, cv, cz, cb, ca, conv_w, A_log, dt_bias, out_gain):
    B, S, _ = cq.shape
    qkv = jax.nn.silu(short_conv(jnp.concatenate([cq, ck, cv], axis=-1), conv_w)).astype(F32)
    q, k, v = jnp.split(qkv, 3, axis=-1)

    def heads(t):
        return t.reshape(B, S, C_HEADS, HEAD_DIM).transpose(0, 2, 1, 3)

    q = l2norm(heads(q)) * (HEAD_DIM ** -0.5)
    k = l2norm(heads(k))
    v = heads(v)
    beta = jax.nn.sigmoid(cb.astype(F32)).reshape(B, S, 2, C_HEADS).transpose(2, 0, 3, 1)
    g = (-jnp.exp(A_log.astype(F32))
         * jax.nn.softplus(ca.astype(F32).reshape(B, S, 2, C_HEADS) + dt_bias.astype(F32)))
    g = g.transpose(2, 0, 3, 1)
    flip = lambda t: jnp.flip(t, axis=2)
    qq = jnp.concatenate([q, flip(q)], axis=0)
    kk = jnp.concatenate([k, flip(k)], axis=0)
    vv = jnp.concatenate([v, flip(v)], axis=0)
    gg = jnp.concatenate([g[0], flip(g[1])], axis=0)
    bb = jnp.concatenate([beta[0], flip(beta[1])], axis=0)
    o = gated_delta_chunked(qq, kk, vv, gg, bb)
    o = o[:B] + flip(o[B:])
    o = rms_norm(o, out_gain) * jax.nn.silu(heads(cz.astype(F32)))
    return o.transpose(0, 2, 1, 3).reshape(B, S, C_W).astype(cq.dtype)


def hybrid_mixer(h, rel_bias, w_in, a_qn, a_kn, b_qn, b_kn, c_conv, c_A_log, c_dt_bias, c_out_norm, w_out):
    B, S, _ = h.shape
    proj = h @ w_in
    aq, ak, av, bq, bk, bv, cq, ck, cv, cz, cb, ca = jnp.split(proj, IN_SPLITS, axis=-1)
    cos, sin = axial_rope(S)
    aq = apply_rope(rms_norm(aq.reshape(B, S, A_HEADS, HEAD_DIM), a_qn), cos, sin)
    ak = apply_rope(rms_norm(ak.reshape(B, S, A_KV_HEADS, HEAD_DIM), a_kn), cos, sin)
    av = av.reshape(B, S, A_KV_HEADS, HEAD_DIM)
    out_a = dense_gqa(aq, ak, av)
    bq = rms_norm(bq.reshape(B, S, B_HEADS, HEAD_DIM), b_qn)
    bk = rms_norm(bk.reshape(B, S, B_HEADS, HEAD_DIM), b_kn)
    bv = bv.reshape(B, S, B_HEADS, HEAD_DIM)
    out_b = dilated_mixture(bq, bk, bv, rel_bias).astype(h.dtype)
    out_c = gated_deltanet_bidir(cq, ck, cv, cz, cb, ca, c_conv, c_A_log, c_dt_bias, c_out_norm)
    return jnp.concatenate([out_a, out_b, out_c], axis=-1) @ w_out


def setup_inputs(seed: int = 0) -> dict:
    key = jax.random.key(seed)
    ks = iter(jax.random.split(key, 40))
    L = DEPTH

    def nrm(shape, scale):
        return jax.random.normal(next(ks), shape, F32) * scale

    def gain(shape):
        return 1.0 + 0.02 * jax.random.normal(next(ks), shape, F32)

    x = nrm((BATCH, SEQ, D_MODEL), 1.0)
    rel_bias = nrm((REL_BUCKETS, B_HEADS), 0.5)
    ffn1_norm = gain((L, D_MODEL))
    ffn1_w_gate = nrm((L, D_MODEL, D_FF), D_MODEL ** -0.5)
    ffn1_w_up = nrm((L, D_MODEL, D_FF), D_MODEL ** -0.5)
    ffn1_w_down = nrm((L, D_FF, D_MODEL), D_FF ** -0.5)
    mix_norm = gain((L, D_MODEL))
    w_in = nrm((L, D_MODEL, N_IN), D_MODEL ** -0.5)
    a_q_norm = gain((L, HEAD_DIM))
    a_k_norm = gain((L, HEAD_DIM))
    b_q_norm = gain((L, HEAD_DIM))
    b_k_norm = gain((L, HEAD_DIM))
    c_conv = nrm((L, CONV_K, 3 * C_W), CONV_K ** -0.5)
    c_A_log = jnp.log(jax.random.uniform(next(ks), (L, 2, C_HEADS), F32, 1.0, 16.0))
    dt = jnp.exp(jax.random.uniform(next(ks), (L, 2, C_HEADS), F32, math.log(1e-3), math.log(1e-1)))
    c_dt_bias = dt + jnp.log(-jnp.expm1(-dt))
    c_out_norm = gain((L, HEAD_DIM))
    w_out = nrm((L, D_MIX, D_MODEL), D_MIX ** -0.5)
    ffn2_norm = gain((L, D_MODEL))
    ffn2_w_gate = nrm((L, D_MODEL, D_FF), D_MODEL ** -0.5)
    ffn2_w_up = nrm((L, D_MODEL, D_FF), D_MODEL ** -0.5)
    ffn2_w_down = nrm((L, D_FF, D_MODEL), D_FF ** -0.5)
    return {'x': x, 'rel_bias': rel_bias,
            'ffn1_norm': ffn1_norm, 'ffn1_w_gate': ffn1_w_gate, 'ffn1_w_up': ffn1_w_up, 'ffn1_w_down': ffn1_w_down,
            'mix_norm': mix_norm, 'w_in': w_in,
            'a_q_norm': a_q_norm, 'a_k_norm': a_k_norm, 'b_q_norm': b_q_norm, 'b_k_norm': b_k_norm,
            'c_conv': c_conv, 'c_A_log': c_A_log, 'c_dt_bias': c_dt_bias, 'c_out_norm': c_out_norm,
            'w_out': w_out,
            'ffn2_norm': ffn2_norm, 'ffn2_w_gate': ffn2_w_gate, 'ffn2_w_up': ffn2_w_up, 'ffn2_w_down': ffn2_w_down}


def reference(x, rel_bias, ffn1_norm, ffn1_w_gate, ffn1_w_up, ffn1_w_down, mix_norm, w_in,
              a_q_norm, a_k_norm, b_q_norm, b_k_norm, c_conv, c_A_log, c_dt_bias, c_out_norm,
              w_out, ffn2_norm, ffn2_w_gate, ffn2_w_up, ffn2_w_down):
    h = x
    for l in range(DEPTH):
        h = h + 0.5 * swiglu(rms_norm(h, ffn1_norm[l]), ffn1_w_gate[l], ffn1_w_up[l], ffn1_w_down[l])
        h = h + hybrid_mixer(rms_norm(h, mix_norm[l]), rel_bias, w_in[l], a_q_norm[l], a_k_norm[l],
                             b_q_norm[l], b_k_norm[l], c_conv[l], c_A_log[l], c_dt_bias[l],
                             c_out_norm[l], w_out[l])
        h = h + 0.5 * swiglu(rms_norm(h, ffn2_norm[l]), ffn2_w_gate[l], ffn2_w_up[l], ffn2_w_down[l])
    return h
```

```python
import functools
import math

import numpy as np
import jax
import jax.numpy as jnp
from jax import lax
from jax.experimental import pallas as pl
from jax.experimental.pallas import tpu as pltpu

F32 = jnp.float32
BF16 = jnp.bfloat16

D_MODEL = 1024
HEAD_DIM = 64
A_HEADS = 4
A_KV_HEADS = 2
B_HEADS = 6
C_HEADS = 6
D_FF = 2816
GRID_W = 64
ROPE_THETA = 10000.0
DIL_PAIRS = ((128, 1), (512, 4), (2048, 16))
REL_BUCKETS = 32
REL_MAX_DIST = 1024
CONV_K = 5
CHUNK = 64
NORM_EPS = 1e-6
NEG_INF = -1e30

A_QW = A_HEADS * HEAD_DIM
A_KVW = A_KV_HEADS * HEAD_DIM
B_W = B_HEADS * HEAD_DIM
C_W = C_HEADS * HEAD_DIM
C_PAIRS = C_HEADS // 2
LANES = 128
GATE_W = C_PAIRS * LANES

OFF_A = 0
OFF_B = OFF_A + A_QW + 2 * A_KVW
OFF_C = OFF_B + 3 * B_W
OFF_Z = OFF_C + 3 * C_W
OFF_G = OFF_Z + C_W
N_PROJ = OFF_G + GATE_W

TOKEN_TILE = 512
A_Q_TILE = 256
B_Q_TILE = 128
B_K_TILE = 256
B_SIDE = 64
CONV_TILE = 256
VMEM_LIMIT = 56 * 1024 * 1024


def _cparams(*sem):
    return pltpu.CompilerParams(dimension_semantics=sem, vmem_limit_bytes=VMEM_LIMIT)


def _resident(shape):
    return pl.BlockSpec(shape, lambda *_: (0,) * len(shape), pipeline_mode=pl.Buffered(1))


def _dot(a, b):
    return jnp.dot(a, b, preferred_element_type=F32)


def _dot_nt(a, b):
    return lax.dot_general(a, b, (((1,), (1,)), ((), ())), preferred_element_type=F32)


def _dot_tn(a, b):
    return lax.dot_general(a, b, (((0,), (0,)), ((), ())), preferred_element_type=F32)


def _silu(x):
    return x / (1.0 + jnp.exp(-x))


def _split_dot(x, m):
    hi = x.astype(BF16)
    lo = (x - hi.astype(F32)).astype(BF16)
    return _dot(hi, m) + _dot(lo, m)


def _rms_rows(x, g):
    ms = jnp.mean(x * x, axis=-1, keepdims=True)
    return x * lax.rsqrt(ms + NORM_EPS) * g


def _ffn_kernel(h_ref, g_ref, wg_ref, wu_ref, wd_ref, o_ref):
    x = h_ref[...]
    xn = _rms_rows(x, g_ref[...]).astype(BF16)
    gate = _dot(xn, wg_ref[...])
    up = _dot(xn, wu_ref[...])
    act = (_silu(gate) * up).astype(BF16)
    o_ref[...] = x + 0.5 * _dot(act, wd_ref[...])


def _ffn(h, g, wg, wu, wd):
    t, d = h.shape
    f = wg.shape[1]
    tm = min(TOKEN_TILE, t)
    return pl.pallas_call(
        _ffn_kernel,
        out_shape=jax.ShapeDtypeStruct((t, d), F32),
        grid=(t // tm,),
        in_specs=[pl.BlockSpec((tm, d), lambda i: (i, 0)),
                  _resident((1, d)), _resident((d, f)), _resident((d, f)), _resident((f, d))],
        out_specs=pl.BlockSpec((tm, d), lambda i: (i, 0)),
        compiler_params=_cparams("parallel"),
        name="ffn",
    )(h, g, wg, wu, wd)


def _head_norm(x, gain, mblk):
    ms = _split_dot(x * x, mblk)
    return x * lax.rsqrt(ms + NORM_EPS) * gain


def _rope(x, cos, sin):
    w = x.shape[1]
    lane = lax.broadcasted_iota(jnp.int32, x.shape, 1)
    first = (lane & (HEAD_DIM - 1)) < HEAD_DIM // 2
    half = HEAD_DIM // 2
    partner = jnp.where(first, pltpu.roll(x, w - half, 1), pltpu.roll(x, half, 1))
    return x * cos + partner * sin


def _proj_kernel(h_ref, g_ref, w_ref, cos_ref, sin_ref, gaq_ref, gak_ref, gbq_ref, gbk_ref,
                 mblk_ref, aq_o, ak_o, av_o, bq_o, bk_o, bv_o, cx_o, cz_o, gt_o):
    xn = _rms_rows(h_ref[...], g_ref[...]).astype(BF16)

    def proj(lo, width):
        return _dot(xn, w_ref[:, lo:lo + width])

    cos = cos_ref[...]
    sin = sin_ref[...]
    aq = _head_norm(proj(OFF_A, A_QW), gaq_ref[...], mblk_ref[:A_QW, :A_QW])
    aq_o[...] = _rope(aq, cos, sin).astype(BF16)
    ak = _head_norm(proj(OFF_A + A_QW, A_KVW), gak_ref[...], mblk_ref[:A_KVW, :A_KVW])
    ak_o[...] = _rope(ak, cos[:, :A_KVW], sin[:, :A_KVW]).astype(BF16)
    av_o[...] = proj(OFF_A + A_QW + A_KVW, A_KVW).astype(BF16)
    bq_o[...] = _head_norm(proj(OFF_B, B_W), gbq_ref[...], mblk_ref[...]).astype(BF16)
    bk_o[...] = _head_norm(proj(OFF_B + B_W, B_W), gbk_ref[...], mblk_ref[...]).astype(BF16)
    bv_o[...] = proj(OFF_B + 2 * B_W, B_W).astype(BF16)
    cx_o[...] = proj(OFF_C, 3 * C_W)
    cz_o[...] = proj(OFF_Z, C_W).astype(BF16)
    gt_o[...] = proj(OFF_G, GATE_W)


def _proj(h, g, w, cos, sin, gaq, gak, gbq, gbk, mblk, seq):
    t, d = h.shape
    tm = min(TOKEN_TILE, seq)
    per_seq = seq // tm
    row = lambda i: (i, 0)
    pos = lambda i: (i % per_seq, 0)
    widths = (A_QW, A_KVW, A_KVW, B_W, B_W, B_W, 3 * C_W, C_W, GATE_W)
    dtypes = (BF16, BF16, BF16, BF16, BF16, BF16, F32, BF16, F32)
    return pl.pallas_call(
        _proj_kernel,
        out_shape=[jax.ShapeDtypeStruct((t, wd), dt) for wd, dt in zip(widths, dtypes)],
        grid=(t // tm,),
        in_specs=[pl.BlockSpec((tm, d), row),
                  _resident((1, d)), _resident((d, N_PROJ)),
                  pl.BlockSpec((tm, A_QW), pos),
                  pl.BlockSpec((tm, A_QW), pos),
                  _resident((1, A_QW)), _resident((1, A_KVW)), _resident((1, B_W)),
                  _resident((1, B_W)), _resident((B_W, B_W))],
        out_specs=[pl.BlockSpec((tm, wd), row) for wd in widths],
        compiler_params=_cparams("parallel"),
        name="mix_in",
    )(h, g, w, cos, sin, gaq, gak, gbq, gbk, mblk)


def _attn_a_kernel(q_ref, k_ref, v_ref, o_ref):
    group = A_HEADS // A_KV_HEADS
    outs = []
    for hk in range(A_KV_HEADS):
        k = k_ref[:, hk * HEAD_DIM:(hk + 1) * HEAD_DIM]
        v = v_ref[:, hk * HEAD_DIM:(hk + 1) * HEAD_DIM]
        for gi in range(group):
            hq = hk * group + gi
            q = q_ref[:, hq * HEAD_DIM:(hq + 1) * HEAD_DIM]
            s = _dot_nt(q, k)
            m = jnp.max(s, axis=-1, keepdims=True)
            p = jnp.exp(s - m)
            l = jnp.sum(p, axis=-1, keepdims=True)
            outs.append(_dot(p.astype(BF16), v) / l)
    o_ref[...] = jnp.concatenate(outs, axis=-1).astype(BF16)


def _attn_a(q, k, v, seq):
    t = q.shape[0]
    tq = min(A_Q_TILE, seq)
    per_seq = seq // tq
    return pl.pallas_call(
        _attn_a_kernel,
        out_shape=jax.ShapeDtypeStruct((t, A_QW), BF16),
        grid=(t // seq, per_seq),
        in_specs=[pl.BlockSpec((tq, A_QW), lambda b, i: (b * per_seq + i, 0)),
                  pl.BlockSpec((seq, A_KVW), lambda b, i: (b, 0)),
                  pl.BlockSpec((seq, A_KVW), lambda b, i: (b, 0))],
        out_specs=pl.BlockSpec((tq, A_QW), lambda b, i: (b * per_seq + i, 0)),
        compiler_params=_cparams("parallel", "parallel"),
        name="attn_a",
    )(q, k, v)


def _t5_bucket_np(rel):
    half = REL_BUCKETS // 2
    exact = half // 2
    sign = np.where(rel > 0, half, 0)
    n = np.abs(rel)
    nf = np.maximum(n, 1).astype(np.float32)
    large = exact + (np.log(nf / np.float32(exact)) / np.float32(math.log(REL_MAX_DIST / exact))
                     * np.float32(half - exact)).astype(np.int32)
    large = np.minimum(large, half - 1)
    return sign + np.where(n < exact, n, large)


def _branch_tiles(sub_len):
    if sub_len >= B_K_TILE:
        return B_Q_TILE, B_K_TILE, (0, -B_SIDE, B_Q_TILE - B_K_TILE)
    return sub_len, sub_len, (0,)


def _branch_bias(rel_bias, dil, sub_len):
    tq, tk, offs = _branch_tiles(sub_len)
    tabs = []
    for off in offs:
        delta = np.arange(tk)[None, :] + off - np.arange(tq)[:, None]
        bucket = _t5_bucket_np(delta * dil)
        bias = jnp.transpose(rel_bias.astype(F32)[bucket], (2, 0, 1))
        tabs.append(jnp.where((np.abs(delta) <= B_SIDE)[None], bias, NEG_INF))
    return jnp.stack(tabs)


def _attn_b_kernel(q_ref, k_ref, v_ref, bias_ref, o_ref, lse_ref, *, sub_len):
    tq, tk, offs = _branch_tiles(sub_len)
    n_tiles = sub_len // tq

    def tile(t, carry):
        q0 = pl.multiple_of(t * tq, tq)
        if n_tiles > 1:
            k0 = pl.multiple_of(jnp.clip(q0 - B_SIDE, 0, sub_len - tk), B_SIDE)
            tab = jnp.where(t == 0, 0, jnp.where(t == n_tiles - 1, 2, 1))
        else:
            k0, tab = 0, 0
        outs, lses = [], []
        for h in range(B_HEADS):
            cols = slice(h * HEAD_DIM, (h + 1) * HEAD_DIM)
            q = q_ref[0, pl.ds(q0, tq), cols]
            k = k_ref[0, pl.ds(k0, tk), cols]
            v = v_ref[0, pl.ds(k0, tk), cols]
            s = _dot_nt(q, k) + bias_ref[tab, h]
            m = jnp.max(s, axis=-1, keepdims=True)
            p = jnp.exp(s - m)
            l = jnp.sum(p, axis=-1, keepdims=True)
            outs.append(_dot(p.astype(BF16), v) / l)
            lses.append(jnp.broadcast_to(m + jnp.log(l), (tq, HEAD_DIM)))
        o_ref[0, pl.ds(q0, tq), :] = jnp.concatenate(outs, axis=-1).astype(BF16)
        lse_ref[0, pl.ds(q0, tq), :] = jnp.concatenate(lses, axis=-1)
        return carry

    lax.fori_loop(0, n_tiles, tile, 0)


def _attn_b_branch(q, k, v, bias, batch, seq, dil):
    sub_len = seq // dil
    view = lambda x: x.reshape(batch, sub_len, dil * B_W)
    blk = pl.BlockSpec((1, sub_len, B_W), lambda b, r: (b, 0, r))
    o, lse = pl.pallas_call(
        functools.partial(_attn_b_kernel, sub_len=sub_len),
        out_shape=[jax.ShapeDtypeStruct((batch, sub_len, dil * B_W), BF16),
                   jax.ShapeDtypeStruct((batch, sub_len, dil * B_W), F32)],
        grid=(batch, dil),
        in_specs=[blk, blk, blk,
                  _resident(bias.shape)],
        out_specs=[blk, blk],
        compiler_params=_cparams("parallel", "parallel"),
        name=f"attn_b_dil{dil}",
    )(view(q), view(k), view(v), bias)
    return o.reshape(batch * seq, B_W), lse.reshape(batch * seq, B_W)


def _inv_unit_triangular(m, rows, cols):
    eye = (rows == cols).astype(F32)

    def same_block(size):
        shift = size.bit_length() - 1
        return (rows >> shift) == (cols >> shift)

    b16 = lambda x: x.astype(BF16)
    m8 = jnp.where(same_block(8), m, 0.0)
    p2 = _dot(b16(m8), b16(m8))
    p4 = _dot(b16(p2), b16(p2))
    p6 = _dot(b16(p2), b16(p4))
    even = eye + p2 + p4 + p6
    inv = even - _dot(b16(m8), b16(even))
    size = 8
    while size < CHUNK:
        off = jnp.where(same_block(2 * size) & jnp.logical_not(same_block(size)), m, 0.0)
        inv = inv - _dot(b16(inv), b16(_dot(b16(off), b16(inv))))
        size *= 2
    return inv


def _delta_kernel(xq_ref, xk_ref, xv_ref, wq_ref, wk_ref, wv_ref, z_ref, gt_ref,
                  alog_ref, dtb_ref, gain_ref, o_ref,
                  pad_s, q_s, k_s, v_s, beta_s, g_s, of_s, ob_s, st_s, *, seq):
    n_chunks = seq // CHUNK
    pad = 8
    lane = lax.broadcasted_iota(jnp.int32, (1, LANES), 1)
    head0 = lane < HEAD_DIM

    def per_head_sum(x):
        s0 = jnp.sum(jnp.where(head0, x, 0.0), axis=-1, keepdims=True)
        s1 = jnp.sum(jnp.where(head0, 0.0, x), axis=-1, keepdims=True)
        return jnp.where(head0, s0, s1)

    zeros = jnp.zeros((pad, LANES), F32)
    for j, (x_ref, w_ref, dst, scale) in enumerate(((xq_ref, wq_ref, q_s, HEAD_DIM ** -0.5),
                                                     (xk_ref, wk_ref, k_s, 1.0),
                                                     (xv_ref, wv_ref, v_s, None))):
        pad_s[j, 0:pad, :] = zeros
        pad_s[j, pad + seq:pad + seq + pad, :] = zeros
        pad_s[j, pad:pad + seq, :] = x_ref[...]

        def conv_tile(t, carry, j=j, w_ref=w_ref, dst=dst, scale=scale):
            r0 = pl.multiple_of(t * CONV_TILE, CONV_TILE)
            halo = pad_s[j, pl.ds(r0, CONV_TILE + 2 * pad), :]
            acc = jnp.zeros((CONV_TILE, LANES), F32)
            for d in range(CONV_K):
                lo = pad - CONV_K // 2 + d
                acc = acc + w_ref[d:d + 1, :] * halo[lo:lo + CONV_TILE]
            y = _silu(acc)
            if scale is not None:
                y = y * lax.rsqrt(per_head_sum(y * y) + NORM_EPS) * scale
            dst[pl.ds(r0, CONV_TILE), :] = y
            return carry

        lax.fori_loop(0, seq // CONV_TILE, conv_tile, 0)

    gt = gt_ref[...]
    beta_s[...] = 1.0 / (1.0 + jnp.exp(-gt))
    gx = gt + dtb_ref[0]
    softplus = jnp.maximum(gx, 0.0) + jnp.log(1.0 + jnp.exp(-jnp.abs(gx)))
    g_s[...] = -jnp.exp(alog_ref[0]) * softplus
    st_s[...] = jnp.zeros(st_s.shape, F32)

    rows = lax.broadcasted_iota(jnp.int32, (CHUNK, CHUNK), 0)
    cols = lax.broadcasted_iota(jnp.int32, (CHUNK, CHUNK), 1)
    eye = rows == cols

    def chunk_step(i, carry):
        for direction, dst in ((0, of_s), (1, ob_s)):
            n = i if direction == 0 else n_chunks - 1 - i
            r0 = pl.multiple_of(n * CHUNK, CHUNK)
            sl = pl.ds(r0, CHUNK)
            if direction == 0:
                incl, strict = rows >= cols, rows > cols
            else:
                incl, strict = rows <= cols, rows < cols
            gcum = _split_dot_left(incl.astype(BF16), g_s[sl, :])
            q2, k2, v2 = q_s[sl, :], k_s[sl, :], v_s[sl, :]
            beta2 = beta_s[sl, :]
            total_row = CHUNK - 1 if direction == 0 else 0
            head_outs = []
            for j in range(2):
                hs = slice(j * HEAD_DIM, (j + 1) * HEAD_DIM)
                q, k, v = q2[:, hs], k2[:, hs], v2[:, hs]
                cb = 2 * direction + j
                cg = 4 + 2 * direction + j
                beta = jnp.broadcast_to(beta2[:, cb:cb + 1], (CHUNK, CHUNK))
                gc = jnp.broadcast_to(gcum[:, cg:cg + 1], (CHUNK, CHUNK))
                gc_row = jnp.sum(jnp.where(eye, gc, 0.0), axis=0, keepdims=True)
                decay = jnp.where(incl, jnp.exp(gc - gc_row), 0.0)
                k16 = k.astype(BF16)
                kk = _dot_nt(k16, k16)
                m = jnp.where(strict, beta * kk * decay, 0.0)
                inv = _inv_unit_triangular(m, rows, cols)
                egc = jnp.exp(gc)
                y = jnp.concatenate([v * beta, k * (beta * egc)], axis=-1)
                x = _dot(inv.astype(BF16), y.astype(BF16))
                u, w = x[:, :HEAD_DIM], x[:, HEAD_DIM:]
                a_intra = _dot_nt(q.astype(BF16), k16) * decay
                state = st_s[2 * direction + j]
                lhs = jnp.concatenate([w, q * egc], axis=0).astype(BF16)
                rs = _dot(lhs, state.astype(BF16))
                v_new = u - rs[:CHUNK]
                v16 = v_new.astype(BF16)
                head_outs.append(rs[CHUNK:] + _dot(a_intra.astype(BF16), v16))
                g_last = gc[total_row:total_row + 1, :]
                k_dec = k * jnp.exp(g_last - gc)
                st_s[2 * direction + j] = state * jnp.exp(g_last) + _dot_tn(k_dec.astype(BF16), v16)
            dst[sl, :] = jnp.concatenate(head_outs, axis=-1)
        return carry

    lax.fori_loop(0, n_chunks, chunk_step, 0)

    o = of_s[...] + ob_s[...]
    ms = per_head_sum(o * o) * (1.0 / HEAD_DIM)
    o = o * lax.rsqrt(ms + NORM_EPS) * gain_ref[...]
    o_ref[...] = (o * _silu(z_ref[...].astype(F32))).astype(BF16)


def _split_dot_left(m, x):
    hi = x.astype(BF16)
    lo = (x - hi.astype(F32)).astype(BF16)
    return _dot(m, hi) + _dot(m, lo)


def _delta(cx, cz, gt, conv_w, alog, dtb, gain, batch, seq):
    t = cx.shape[0]
    xspec = lambda part: pl.BlockSpec((seq, LANES), lambda b, p: (b, part * C_PAIRS + p))
    wspec = lambda part: pl.BlockSpec((CONV_K, LANES), lambda b, p: (0, part * C_PAIRS + p))
    pair = pl.BlockSpec((seq, LANES), lambda b, p: (b, p))
    prow = pl.BlockSpec((1, 1, LANES), lambda b, p: (p, 0, 0))
    return pl.pallas_call(
        functools.partial(_delta_kernel, seq=seq),
        out_shape=jax.ShapeDtypeStruct((t, C_W), BF16),
        grid=(batch, C_PAIRS),
        in_specs=[xspec(0), xspec(1), xspec(2), wspec(0), wspec(1), wspec(2),
                  pair, pair, prow, prow,
                  _resident((1, LANES))],
        out_specs=pair,
        scratch_shapes=[pltpu.VMEM((3, seq + 16, LANES), F32),
                        pltpu.VMEM((seq, LANES), F32),
                        pltpu.VMEM((seq, LANES), F32),
                        pltpu.VMEM((seq, LANES), F32),
                        pltpu.VMEM((seq, LANES), F32),
                        pltpu.VMEM((seq, LANES), F32),
                        pltpu.VMEM((seq, LANES), F32),
                        pltpu.VMEM((seq, LANES), F32),
                        pltpu.VMEM((4, HEAD_DIM, HEAD_DIM), F32)],
        compiler_params=_cparams("parallel", "parallel"),
        name="delta",
    )(cx, cx, cx, conv_w, conv_w, conv_w, cz, gt, alog, dtb, gain)


def _out_kernel(h_ref, a_ref, o1_ref, o2_ref, o3_ref, l1_ref, l2_ref, l3_ref, c_ref, w_ref, o_ref):
    l1, l2, l3 = l1_ref[...], l2_ref[...], l3_ref[...]
    m = jnp.maximum(jnp.maximum(l1, l2), l3)
    e1, e2, e3 = jnp.exp(l1 - m), jnp.exp(l2 - m), jnp.exp(l3 - m)
    ob = (e1 * o1_ref[...].astype(F32) + e2 * o2_ref[...].astype(F32)
          + e3 * o3_ref[...].astype(F32)) / (e1 + e2 + e3)
    y = _dot(a_ref[...], w_ref[0:A_QW, :])
    y = y + _dot(ob.astype(BF16), w_ref[A_QW:A_QW + B_W, :])
    y = y + _dot(c_ref[...], w_ref[A_QW + B_W:, :])
    o_ref[...] = h_ref[...] + y


def _mix_out(h, a, obs, lses, c, w):
    t, d = h.shape
    tm = min(TOKEN_TILE, t)
    row = lambda width: pl.BlockSpec((tm, width), lambda i: (i, 0))
    return pl.pallas_call(
        _out_kernel,
        out_shape=jax.ShapeDtypeStruct((t, d), F32),
        grid=(t // tm,),
        in_specs=[row(d), row(A_QW)] + [row(B_W)] * 6 + [row(C_W), _resident(w.shape)],
        out_specs=row(d),
        compiler_params=_cparams("parallel"),
        name="mix_out",
    )(h, a, *obs, *lses, c, w)


def _rope_tables(seq):
    rows = seq // GRID_W
    row = jnp.repeat(jnp.arange(rows), GRID_W).astype(F32)
    col = jnp.tile(jnp.arange(GRID_W), rows).astype(F32)
    n_freq = HEAD_DIM // 4
    inv = ROPE_THETA ** (-jnp.arange(n_freq, dtype=F32) / n_freq)
    ang = jnp.concatenate([row[:, None] * inv, col[:, None] * inv], axis=-1)
    cos, sin = jnp.cos(ang), jnp.sin(ang)
    cos_h = jnp.concatenate([cos, cos], axis=-1)
    sin_h = jnp.concatenate([-sin, sin], axis=-1)
    return jnp.tile(cos_h, (1, A_HEADS)), jnp.tile(sin_h, (1, A_HEADS))


def _pair_columns(x):
    lead = x.shape[:-1]
    x = x.reshape(lead + (2, C_PAIRS, 2))
    return jnp.moveaxis(x, -2, -3).reshape(lead + (C_PAIRS, 4))


def _pad_w_in(w_in):
    d = w_in.shape[0]
    main = w_in[:, :OFF_G]
    cb = _pair_columns(w_in[:, OFF_G:OFF_G + 2 * C_HEADS])
    ca = _pair_columns(w_in[:, OFF_G + 2 * C_HEADS:])
    gates = jnp.concatenate([cb, ca, jnp.zeros((d, C_PAIRS, LANES - 8), w_in.dtype)], axis=-1)
    return jnp.concatenate([main, gates.reshape(d, GATE_W)], axis=-1).astype(BF16)


def _pair_row(p):
    v = _pair_columns(p.reshape(2 * C_HEADS).astype(F32))
    z4 = jnp.zeros((C_PAIRS, 4), F32)
    return jnp.concatenate([z4, v, jnp.zeros((C_PAIRS, LANES - 8), F32)], axis=-1)[:, None, :]


def kernel(x, rel_bias, ffn1_norm, ffn1_w_gate, ffn1_w_up, ffn1_w_down, mix_norm, w_in, a_q_norm, a_k_norm, b_q_norm, b_k_norm, c_conv, c_A_log, c_dt_bias, c_out_norm, w_out, ffn2_norm, ffn2_w_gate, ffn2_w_up, ffn2_w_down):
    batch, seq, d = x.shape
    depth = w_in.shape[0]
    scale = HEAD_DIM ** -0.5
    h = x.reshape(batch * seq, d)
    cos, sin = _rope_tables(seq)
    idx = np.arange(B_W)
    mblk = jnp.asarray((idx[:, None] // HEAD_DIM == idx[None, :] // HEAD_DIM) / HEAD_DIM, BF16)
    biases = [_branch_bias(rel_bias, dil, seq // dil) for _, dil in DIL_PAIRS]
    row = lambda v: v.reshape(1, -1).astype(F32)
    for l in range(depth):
        h = _ffn(h, row(ffn1_norm[l]), ffn1_w_gate[l].astype(BF16), ffn1_w_up[l].astype(BF16),
                 ffn1_w_down[l].astype(BF16))
        aq, ak, av, bq, bk, bv, cx, cz, gt = _proj(
            h, row(mix_norm[l]), _pad_w_in(w_in[l]), cos, sin,
            row(jnp.tile(a_q_norm[l], A_HEADS)) * scale, row(jnp.tile(a_k_norm[l], A_KV_HEADS)),
            row(jnp.tile(b_q_norm[l], B_HEADS)) * scale, row(jnp.tile(b_k_norm[l], B_HEADS)),
            mblk, seq)
        out_a = _attn_a(aq, ak, av, seq)
        obs, lses = [], []
        for (_, dil), bias in zip(DIL_PAIRS, biases):
            o, lse = _attn_b_branch(bq, bk, bv, bias, batch, seq, dil)
            obs.append(o)
            lses.append(lse)
        out_c = _delta(cx, cz, gt, c_conv[l].astype(F32), _pair_row(c_A_log[l]), _pair_row(c_dt_bias[l]),
                       row(jnp.tile(c_out_norm[l], 2)), batch, seq)
        h = _mix_out(h, out_a, obs, lses, out_c, w_out[l].astype(BF16))
        h = _ffn(h, row(ffn2_norm[l]), ffn2_w_gate[l].astype(BF16), ffn2_w_up[l].astype(BF16),
                 ffn2_w_down[l].astype(BF16))
    return h.reshape(batch, seq, d)
```

```python
import functools
import math

import numpy as np
import jax
import jax.numpy as jnp
from jax import lax
from jax.experimental import pallas as pl
from jax.experimental.pallas import tpu as pltpu

F32 = jnp.float32
BF16 = jnp.bfloat16

D_MODEL = 1024
HEAD_DIM = 64
A_HEADS = 4
A_KV_HEADS = 2
B_HEADS = 6
C_HEADS = 6
D_FF = 2816
GRID_W = 64
ROPE_THETA = 10000.0
DIL_PAIRS = ((128, 1), (512, 4), (2048, 16))
REL_BUCKETS = 32
REL_MAX_DIST = 1024
CONV_K = 5
CHUNK = 64
NORM_EPS = 1e-6
NEG_INF = -1e30

A_QW = A_HEADS * HEAD_DIM
A_KVW = A_KV_HEADS * HEAD_DIM
B_W = B_HEADS * HEAD_DIM
C_W = C_HEADS * HEAD_DIM
C_PAIRS = C_HEADS // 2
LANES = 128
GATE_W = C_PAIRS * LANES

OFF_A = 0
OFF_B = OFF_A + A_QW + 2 * A_KVW
OFF_C = OFF_B + 3 * B_W
OFF_Z = OFF_C + 3 * C_W
OFF_G = OFF_Z + C_W
N_PROJ = OFF_G + GATE_W

TOKEN_TILE = 512
A_Q_TILE = 256
B_Q_TILE = 128
B_K_TILE = 256
B_SIDE = 64
CONV_TILE = 256
CHUNK_GROUP = 4
VMEM_LIMIT = 56 * 1024 * 1024


def _cparams(*sem):
    return pltpu.CompilerParams(dimension_semantics=sem, vmem_limit_bytes=VMEM_LIMIT)


def _resident(shape):
    return pl.BlockSpec(shape, lambda *_: (0,) * len(shape), pipeline_mode=pl.Buffered(1))


def _dot(a, b):
    return jnp.dot(a, b, preferred_element_type=F32)


def _dot_nt(a, b):
    return lax.dot_general(a, b, (((1,), (1,)), ((), ())), preferred_element_type=F32)


def _dot_tn(a, b):
    return lax.dot_general(a, b, (((0,), (0,)), ((), ())), preferred_element_type=F32)


def _silu(x):
    return x / (1.0 + jnp.exp(-x))


def _split_dot(x, m):
    hi = x.astype(BF16)
    lo = (x - hi.astype(F32)).astype(BF16)
    return _dot(hi, m) + _dot(lo, m)


def _rms_rows(x, g):
    ms = jnp.mean(x * x, axis=-1, keepdims=True)
    return x * lax.rsqrt(ms + NORM_EPS) * g


def _ffn_kernel(h_ref, g_ref, wg_ref, wu_ref, wd_ref, o_ref):
    x = h_ref[...]
    xn = _rms_rows(x, g_ref[...]).astype(BF16)
    gate = _dot(xn, wg_ref[...])
    up = _dot(xn, wu_ref[...])
    act = (_silu(gate) * up).astype(BF16)
    o_ref[...] = x + 0.5 * _dot(act, wd_ref[...])


def _ffn(h, g, wg, wu, wd):
    t, d = h.shape
    f = wg.shape[1]
    tm = min(TOKEN_TILE, t)
    return pl.pallas_call(
        _ffn_kernel,
        out_shape=jax.ShapeDtypeStruct((t, d), F32),
        grid=(t // tm,),
        in_specs=[pl.BlockSpec((tm, d), lambda i: (i, 0)),
                  _resident((1, d)), _resident((d, f)), _resident((d, f)), _resident((f, d))],
        out_specs=pl.BlockSpec((tm, d), lambda i: (i, 0)),
        compiler_params=_cparams("parallel"),
        name="ffn",
    )(h, g, wg, wu, wd)


def _head_norm(x, gain, mblk):
    ms = _split_dot(x * x, mblk)
    return x * lax.rsqrt(ms + NORM_EPS) * gain


def _rope(x, cos, sin):
    w = x.shape[1]
    lane = lax.broadcasted_iota(jnp.int32, x.shape, 1)
    first = (lane & (HEAD_DIM - 1)) < HEAD_DIM // 2
    half = HEAD_DIM // 2
    partner = jnp.where(first, pltpu.roll(x, w - half, 1), pltpu.roll(x, half, 1))
    return x * cos + partner * sin


def _proj_kernel(h_ref, g_ref, w_ref, cos_ref, sin_ref, gaq_ref, gak_ref, gbq_ref, gbk_ref,
                 mblk_ref, aq_o, ak_o, av_o, bq_o, bk_o, bv_o, cx_o, cz_o, gt_o):
    xn = _rms_rows(h_ref[...], g_ref[...]).astype(BF16)

    def proj(lo, width):
        return _dot(xn, w_ref[:, lo:lo + width])

    cos = cos_ref[...]
    sin = sin_ref[...]
    aq = _head_norm(proj(OFF_A, A_QW), gaq_ref[...], mblk_ref[:A_QW, :A_QW])
    aq_o[...] = _rope(aq, cos, sin).astype(BF16)
    ak = _head_norm(proj(OFF_A + A_QW, A_KVW), gak_ref[...], mblk_ref[:A_KVW, :A_KVW])
    ak_o[...] = _rope(ak, cos[:, :A_KVW], sin[:, :A_KVW]).astype(BF16)
    av_o[...] = proj(OFF_A + A_QW + A_KVW, A_KVW).astype(BF16)
    bq_o[...] = _head_norm(proj(OFF_B, B_W), gbq_ref[...], mblk_ref[...]).astype(BF16)
    bk_o[...] = _head_norm(proj(OFF_B + B_W, B_W), gbk_ref[...], mblk_ref[...]).astype(BF16)
    bv_o[...] = proj(OFF_B + 2 * B_W, B_W).astype(BF16)
    cx_o[...] = proj(OFF_C, 3 * C_W)
    cz_o[...] = proj(OFF_Z, C_W).astype(BF16)
    gt_o[...] = proj(OFF_G, GATE_W)


def _proj(h, g, w, cos, sin, gaq, gak, gbq, gbk, mblk, seq):
    t, d = h.shape
    tm = min(TOKEN_TILE, seq)
    per_seq = seq // tm
    row = lambda i: (i, 0)
    pos = lambda i: (i % per_seq, 0)
    widths = (A_QW, A_KVW, A_KVW, B_W, B_W, B_W, 3 * C_W, C_W, GATE_W)
    dtypes = (BF16, BF16, BF16, BF16, BF16, BF16, F32, BF16, F32)
    return pl.pallas_call(
        _proj_kernel,
        out_shape=[jax.ShapeDtypeStruct((t, wd), dt) for wd, dt in zip(widths, dtypes)],
        grid=(t // tm,),
        in_specs=[pl.BlockSpec((tm, d), row),
                  _resident((1, d)), _resident((d, N_PROJ)),
                  pl.BlockSpec((tm, A_QW), pos),
                  pl.BlockSpec((tm, A_QW), pos),
                  _resident((1, A_QW)), _resident((1, A_KVW)), _resident((1, B_W)),
                  _resident((1, B_W)), _resident((B_W, B_W))],
        out_specs=[pl.BlockSpec((tm, wd), row) for wd in widths],
        compiler_params=_cparams("parallel"),
        name="mix_in",
    )(h, g, w, cos, sin, gaq, gak, gbq, gbk, mblk)


def _attn_a_kernel(q_ref, k_ref, v_ref, o_ref):
    group = A_HEADS // A_KV_HEADS
    outs = []
    for hk in range(A_KV_HEADS):
        k = k_ref[:, hk * HEAD_DIM:(hk + 1) * HEAD_DIM]
        v = v_ref[:, hk * HEAD_DIM:(hk + 1) * HEAD_DIM]
        for gi in range(group):
            hq = hk * group + gi
            q = q_ref[:, hq * HEAD_DIM:(hq + 1) * HEAD_DIM]
            s = _dot_nt(q, k)
            m = jnp.max(s, axis=-1, keepdims=True)
            p = jnp.exp(s - m)
            l = jnp.sum(p, axis=-1, keepdims=True)
            outs.append(_dot(p.astype(BF16), v) / l)
    o_ref[...] = jnp.concatenate(outs, axis=-1).astype(BF16)


def _attn_a(q, k, v, seq):
    t = q.shape[0]
    tq = min(A_Q_TILE, seq)
    per_seq = seq // tq
    return pl.pallas_call(
        _attn_a_kernel,
        out_shape=jax.ShapeDtypeStruct((t, A_QW), BF16),
        grid=(t // seq, per_seq),
        in_specs=[pl.BlockSpec((tq, A_QW), lambda b, i: (b * per_seq + i, 0)),
                  pl.BlockSpec((seq, A_KVW), lambda b, i: (b, 0)),
                  pl.BlockSpec((seq, A_KVW), lambda b, i: (b, 0))],
        out_specs=pl.BlockSpec((tq, A_QW), lambda b, i: (b * per_seq + i, 0)),
        compiler_params=_cparams("parallel", "parallel"),
        name="attn_a",
    )(q, k, v)


def _t5_bucket_np(rel):
    half = REL_BUCKETS // 2
    exact = half // 2
    sign = np.where(rel > 0, half, 0)
    n = np.abs(rel)
    nf = np.maximum(n, 1).astype(np.float32)
    large = exact + (np.log(nf / np.float32(exact)) / np.float32(math.log(REL_MAX_DIST / exact))
                     * np.float32(half - exact)).astype(np.int32)
    large = np.minimum(large, half - 1)
    return sign + np.where(n < exact, n, large)


def _branch_tiles(sub_len):
    if sub_len >= B_K_TILE:
        return B_Q_TILE, B_K_TILE, (0, -B_SIDE, B_Q_TILE - B_K_TILE)
    return sub_len, sub_len, (0,)


def _branch_bias(rel_bias, dil, sub_len):
    tq, tk, offs = _branch_tiles(sub_len)
    period = tq + tk
    slot = np.arange(period)
    tabs = []
    for off in offs:
        delta = np.where(slot < tk, slot, slot - period) + off
        bucket = _t5_bucket_np(delta * dil)
        vec = jnp.where((np.abs(delta) <= B_SIDE)[:, None], rel_bias.astype(F32)[bucket], NEG_INF).T
        flat = jnp.tile(vec, (1, tq))[:, :tq * (period - 1)]
        tabs.append(flat.reshape(B_HEADS, tq, period - 1)[:, :, :tk])
    return jnp.stack(tabs)


def _attn_b_kernel(q_ref, k_ref, v_ref, bias_ref, o_ref, lse_ref, *, sub_len):
    tq, tk, offs = _branch_tiles(sub_len)
    n_tiles = sub_len // tq

    def tile(t, carry):
        q0 = pl.multiple_of(t * tq, tq)
        if n_tiles > 1:
            k0 = pl.multiple_of(jnp.clip(q0 - B_SIDE, 0, sub_len - tk), B_SIDE)
            tab = jnp.where(t == 0, 0, jnp.where(t == n_tiles - 1, 2, 1))
        else:
            k0, tab = 0, 0
        outs, lses = [], []
        for h in range(B_HEADS):
            cols = slice(h * HEAD_DIM, (h + 1) * HEAD_DIM)
            q = q_ref[0, pl.ds(q0, tq), cols]
            k = k_ref[0, pl.ds(k0, tk), cols]
            v = v_ref[0, pl.ds(k0, tk), cols]
            s = _dot_nt(q, k) + bias_ref[tab, h]
            m = jnp.max(s, axis=-1, keepdims=True)
            p = jnp.exp(s - m)
            l = jnp.sum(p, axis=-1, keepdims=True)
            outs.append(_dot(p.astype(BF16), v) / l)
            lses.append(jnp.broadcast_to(m + jnp.log(l), (tq, HEAD_DIM)))
        o_ref[0, pl.ds(q0, tq), :] = jnp.concatenate(outs, axis=-1).astype(BF16)
        lse_ref[0, pl.ds(q0, tq), :] = jnp.concatenate(lses, axis=-1)
        return carry

    lax.fori_loop(0, n_tiles, tile, 0)


def _attn_b_branch(q, k, v, bias, batch, seq, dil):
    sub_len = seq // dil
    view = lambda x: x.reshape(batch, sub_len, dil * B_W)
    blk = pl.BlockSpec((1, sub_len, B_W), lambda b, r: (b, 0, r))
    o, lse = pl.pallas_call(
        functools.partial(_attn_b_kernel, sub_len=sub_len),
        out_shape=[jax.ShapeDtypeStruct((batch, sub_len, dil * B_W), BF16),
                   jax.ShapeDtypeStruct((batch, sub_len, dil * B_W), F32)],
        grid=(batch, dil),
        in_specs=[blk, blk, blk,
                  _resident(bias.shape)],
        out_specs=[blk, blk],
        compiler_params=_cparams("parallel", "parallel"),
        name=f"attn_b_dil{dil}",
    )(view(q), view(k), view(v), bias)
    return o.reshape(batch * seq, B_W), lse.reshape(batch * seq, B_W)


BASE_BLOCK = 8


def _pair_block_diag(x, same_head):
    return jnp.where(same_head, jnp.concatenate([x, x], axis=0), 0.0).astype(BF16)


def _inv_unit_triangular_pairs(ms, same_block, same_head, eye):
    bd = lambda x: _pair_block_diag(x, same_head)
    b16 = lambda x: x.astype(BF16)
    m8 = [b16(jnp.where(same_block[BASE_BLOCK], m, 0.0)) for m in ms]
    p2 = [_dot(a, bd(a.astype(F32))) for a in m8]
    p2h = [b16(p) for p in p2]
    p4 = [_dot(a, bd(p)) for a, p in zip(p2h, p2)]
    p6 = [_dot(a, bd(p)) for a, p in zip(p2h, p4)]
    even = [eye + a + b + c for a, b, c in zip(p2, p4, p6)]
    inv = [e - _dot(a, bd(e)) for a, e in zip(m8, even)]
    size = BASE_BLOCK
    while size < CHUNK:
        joins = same_block[2 * size] & jnp.logical_not(same_block[size])
        t = [_dot(b16(jnp.where(joins, m, 0.0)), bd(x)) for m, x in zip(ms, inv)]
        inv = [x - _dot(b16(x), bd(y)) for x, y in zip(inv, t)]
        size *= 2
    return inv


def _delta_kernel(xq_ref, xk_ref, xv_ref, wq_ref, wk_ref, wv_ref, z_ref, gt_ref,
                  alog_ref, dtb_ref, gain_ref, o_ref,
                  pad_s, q_s, k_s, v_s, beta_s, g_s, of_s, ob_s, st_s,
                  u_s, lq_s, a_s, kd_s, eg_s, *, seq):
    n_chunks = seq // CHUNK
    pad = 8
    lane = lax.broadcasted_iota(jnp.int32, (1, LANES), 1)
    head0 = lane < HEAD_DIM

    def per_head_sum(x):
        s0 = jnp.sum(jnp.where(head0, x, 0.0), axis=-1, keepdims=True)
        s1 = jnp.sum(jnp.where(head0, 0.0, x), axis=-1, keepdims=True)
        return jnp.where(head0, s0, s1)

    zeros = jnp.zeros((pad, LANES), F32)
    for j, (x_ref, w_ref, dst, scale) in enumerate(((xq_ref, wq_ref, q_s, HEAD_DIM ** -0.5),
                                                     (xk_ref, wk_ref, k_s, 1.0),
                                                     (xv_ref, wv_ref, v_s, None))):
        pad_s[j, 0:pad, :] = zeros
        pad_s[j, pad + seq:pad + seq + pad, :] = zeros
        pad_s[j, pad:pad + seq, :] = x_ref[...]

        def conv_tile(t, carry, j=j, w_ref=w_ref, dst=dst, scale=scale):
            r0 = pl.multiple_of(t * CONV_TILE, CONV_TILE)
            halo = pad_s[j, pl.ds(r0, CONV_TILE + 2 * pad), :]
            acc = jnp.zeros((CONV_TILE, LANES), F32)
            for d in range(CONV_K):
                lo = pad - CONV_K // 2 + d
                acc = acc + w_ref[d:d + 1, :] * halo[lo:lo + CONV_TILE]
            y = _silu(acc)
            if scale is not None:
                y = y * lax.rsqrt(per_head_sum(y * y) + NORM_EPS) * scale
            dst[pl.ds(r0, CONV_TILE), :] = y
            return carry

        lax.fori_loop(0, seq // CONV_TILE, conv_tile, 0)

    gt = gt_ref[...]
    beta_s[...] = 1.0 / (1.0 + jnp.exp(-gt))
    gx = gt + dtb_ref[0]
    softplus = jnp.maximum(gx, 0.0) + jnp.log(1.0 + jnp.exp(-jnp.abs(gx)))
    g_s[...] = -jnp.exp(alog_ref[0]) * softplus
    st_s[...] = jnp.zeros(st_s.shape, F32)

    rows = lax.broadcasted_iota(jnp.int32, (CHUNK, LANES), 0)
    cols = lax.broadcasted_iota(jnp.int32, (CHUNK, LANES), 1) & (HEAD_DIM - 1)
    head0_w = lax.broadcasted_iota(jnp.int32, (CHUNK, LANES), 1) < HEAD_DIM
    eye_b = rows == cols
    eye = eye_b.astype(F32)
    same_block = {}
    size = BASE_BLOCK
    while size <= CHUNK:
        shift = size.bit_length() - 1
        same_block[size] = (rows >> shift) == (cols >> shift)
        size *= 2
    sq_r = lax.broadcasted_iota(jnp.int32, (LANES, LANES), 0)
    sq_c = lax.broadcasted_iota(jnp.int32, (LANES, LANES), 1)
    same_head = (sq_r < HEAD_DIM) == (sq_c < HEAD_DIM)
    r64 = lax.broadcasted_iota(jnp.int32, (CHUNK, CHUNK), 0)
    c64 = lax.broadcasted_iota(jnp.int32, (CHUNK, CHUNK), 1)
    tri = ((r64 >= c64).astype(BF16), (r64 <= c64).astype(BF16))
    incl = (rows >= cols, rows <= cols)
    strict = (rows > cols, rows < cols)
    total_row = (CHUNK - 1, 0)
    bd = lambda x: _pair_block_diag(x, same_head)

    def pair_bcast(x, c0):
        return jnp.where(head0_w, jnp.broadcast_to(x[:, c0:c0 + 1], (CHUNK, LANES)),
                         jnp.broadcast_to(x[:, c0 + 1:c0 + 2], (CHUNK, LANES)))

    def chunk_group(it, carry):
        sls = [pl.ds(pl.multiple_of((it * CHUNK_GROUP + g) * CHUNK, CHUNK), CHUNK)
               for g in range(CHUNK_GROUP)]
        chains = [(g, d) for g in range(CHUNK_GROUP) for d in range(2)]
        k2 = [k_s[sl, :] for sl in sls]
        q2 = [q_s[sl, :] for sl in sls]
        kbd = [bd(k) for k in k2]
        gcum = {(g, d): _split_dot_left(tri[d], g_s[sls[g], :]) for g, d in chains}
        kk = [_dot_nt(k.astype(BF16), b) for k, b in zip(k2, kbd)]
        qk = [_dot_nt(q.astype(BF16), b) for q, b in zip(q2, kbd)]
        beta, gc, egc, ms = {}, {}, {}, []
        for g, d in chains:
            beta[g, d] = pair_bcast(beta_s[sls[g], :], 2 * d)
            gc[g, d] = pair_bcast(gcum[g, d], 4 + 2 * d)
            gc_row = jnp.sum(jnp.where(eye_b, gc[g, d], 0.0), axis=0, keepdims=True)
            decay = jnp.where(incl[d], jnp.exp(gc[g, d] - gc_row), 0.0)
            ms.append(jnp.where(strict[d], beta[g, d] * kk[g] * decay, 0.0))
            a_s[d, sls[g], :] = (qk[g] * decay).astype(BF16)
            egc[g, d] = jnp.exp(gc[g, d])
            g_last = gc[g, d][total_row[d]:total_row[d] + 1, :]
            kd_s[d, sls[g], :] = (k2[g] * jnp.exp(g_last - gc[g, d])).astype(BF16)
            eg_s[d, pl.ds(pl.multiple_of((it * CHUNK_GROUP + g) * 8, 8), 8), :] = (
                jnp.broadcast_to(jnp.exp(g_last), (8, LANES)))
        inv = [x.astype(BF16) for x in _inv_unit_triangular_pairs(ms, same_block, same_head, eye)]
        us = [_dot(x, bd(v_s[sls[g], :] * beta[g, d])) for x, (g, d) in zip(inv, chains)]
        ws = [_dot(x, bd(k2[g] * (beta[g, d] * egc[g, d]))) for x, (g, d) in zip(inv, chains)]
        for u, w, (g, d) in zip(us, ws, chains):
            u_s[d, sls[g], :] = u
            lq_s[d, it * CHUNK_GROUP + g] = jnp.concatenate([w, q2[g] * egc[g, d]], axis=0).astype(BF16)
        return carry

    lax.fori_loop(0, n_chunks // CHUNK_GROUP, chunk_group, 0)

    def scan_step(i, carry):
        ns = (i, n_chunks - 1 - i)
        sls = [pl.ds(pl.multiple_of(n * CHUNK, CHUNK), CHUNK) for n in ns]
        state = [st_s[d] for d in range(2)]
        rs = [_dot(lq_s[d, ns[d]], state[d].astype(BF16)) for d in range(2)]
        v_new = [u_s[d, sls[d], :] - rs[d][:CHUNK] for d in range(2)]
        outs = [rs[d][CHUNK:] + _dot(a_s[d, sls[d], :], bd(v_new[d])) for d in range(2)]
        upd = [_dot_tn(kd_s[d, sls[d], :], v_new[d].astype(BF16)) for d in range(2)]
        for d, dst in ((0, of_s), (1, ob_s)):
            eg = eg_s[d, pl.ds(pl.multiple_of(ns[d] * 8, 8), 8), :][0:1]
            st_s[d] = state[d] * eg + jnp.where(same_head, upd[d], 0.0)
            dst[sls[d], :] = outs[d]
        return carry

    lax.fori_loop(0, n_chunks, scan_step, 0)

    o = of_s[...] + ob_s[...]
    ms = per_head_sum(o * o) * (1.0 / HEAD_DIM)
    o = o * lax.rsqrt(ms + NORM_EPS) * gain_ref[...]
    o_ref[...] = (o * _silu(z_ref[...].astype(F32))).astype(BF16)


def _split_dot_left(m, x):
    hi = x.astype(BF16)
    lo = (x - hi.astype(F32)).astype(BF16)
    return _dot(m, hi) + _dot(m, lo)


def _delta(cx, cz, gt, conv_w, alog, dtb, gain, batch, seq):
    t = cx.shape[0]
    xspec = lambda part: pl.BlockSpec((seq, LANES), lambda b, p: (b, part * C_PAIRS + p))
    wspec = lambda part: pl.BlockSpec((CONV_K, LANES), lambda b, p: (0, part * C_PAIRS + p))
    pair = pl.BlockSpec((seq, LANES), lambda b, p: (b, p))
    prow = pl.BlockSpec((1, 1, LANES), lambda b, p: (p, 0, 0))
    return pl.pallas_call(
        functools.partial(_delta_kernel, seq=seq),
        out_shape=jax.ShapeDtypeStruct((t, C_W), BF16),
        grid=(batch, C_PAIRS),
        in_specs=[xspec(0), xspec(1), xspec(2), wspec(0), wspec(1), wspec(2),
                  pair, pair, prow, prow,
                  _resident((1, LANES))],
        out_specs=pair,
        scratch_shapes=[pltpu.VMEM((3, seq + 16, LANES), F32),
                        pltpu.VMEM((seq, LANES), F32),
                        pltpu.VMEM((seq, LANES), F32),
                        pltpu.VMEM((seq, LANES), F32),
                        pltpu.VMEM((seq, LANES), F32),
                        pltpu.VMEM((seq, LANES), F32),
                        pltpu.VMEM((seq, LANES), F32),
                        pltpu.VMEM((seq, LANES), F32),
                        pltpu.VMEM((2, LANES, LANES), F32),
                        pltpu.VMEM((2, seq, LANES), F32),
                        pltpu.VMEM((2, seq // CHUNK, LANES, LANES), BF16),
                        pltpu.VMEM((2, seq, LANES), BF16),
                        pltpu.VMEM((2, seq, LANES), BF16),
                        pltpu.VMEM((2, seq // CHUNK * 8, LANES), F32)],
        compiler_params=_cparams("parallel", "parallel"),
        name="delta",
    )(cx, cx, cx, conv_w, conv_w, conv_w, cz, gt, alog, dtb, gain)


def _out_kernel(h_ref, a_ref, o1_ref, o2_ref, o3_ref, l1_ref, l2_ref, l3_ref, c_ref, w_ref, o_ref):
    l1, l2, l3 = l1_ref[...], l2_ref[...], l3_ref[...]
    m = jnp.maximum(jnp.maximum(l1, l2), l3)
    e1, e2, e3 = jnp.exp(l1 - m), jnp.exp(l2 - m), jnp.exp(l3 - m)
    ob = (e1 * o1_ref[...].astype(F32) + e2 * o2_ref[...].astype(F32)
          + e3 * o3_ref[...].astype(F32)) / (e1 + e2 + e3)
    y = _dot(a_ref[...], w_ref[0:A_QW, :])
    y = y + _dot(ob.astype(BF16), w_ref[A_QW:A_QW + B_W, :])
    y = y + _dot(c_ref[...], w_ref[A_QW + B_W:, :])
    o_ref[...] = h_ref[...] + y


def _mix_out(h, a, obs, lses, c, w):
    t, d = h.shape
    tm = min(TOKEN_TILE, t)
    row = lambda width: pl.BlockSpec((tm, width), lambda i: (i, 0))
    return pl.pallas_call(
        _out_kernel,
        out_shape=jax.ShapeDtypeStruct((t, d), F32),
        grid=(t // tm,),
        in_specs=[row(d), row(A_QW)] + [row(B_W)] * 6 + [row(C_W), _resident(w.shape)],
        out_specs=row(d),
        compiler_params=_cparams("parallel"),
        name="mix_out",
    )(h, a, *obs, *lses, c, w)


def _rope_tables(seq):
    rows = seq // GRID_W
    row = jnp.repeat(jnp.arange(rows), GRID_W).astype(F32)
    col = jnp.tile(jnp.arange(GRID_W), rows).astype(F32)
    n_freq = HEAD_DIM // 4
    inv = ROPE_THETA ** (-jnp.arange(n_freq, dtype=F32) / n_freq)
    ang = jnp.concatenate([row[:, None] * inv, col[:, None] * inv], axis=-1)
    cos, sin = jnp.cos(ang), jnp.sin(ang)
    cos_h = jnp.concatenate([cos, cos], axis=-1)
    sin_h = jnp.concatenate([-sin, sin], axis=-1)
    return jnp.tile(cos_h, (1, A_HEADS)), jnp.tile(sin_h, (1, A_HEADS))


def _pair_columns(x):
    lead = x.shape[:-1]
    x = x.reshape(lead + (2, C_PAIRS, 2))
    return jnp.moveaxis(x, -2, -3).reshape(lead + (C_PAIRS, 4))


def _pad_w_in(w_in):
    d = w_in.shape[0]
    main = w_in[:, :OFF_G]
    cb = _pair_columns(w_in[:, OFF_G:OFF_G + 2 * C_HEADS])
    ca = _pair_columns(w_in[:, OFF_G + 2 * C_HEADS:])
    gates = jnp.concatenate([cb, ca, jnp.zeros((d, C_PAIRS, LANES - 8), w_in.dtype)], axis=-1)
    return jnp.concatenate([main, gates.reshape(d, GATE_W)], axis=-1).astype(BF16)


def _pair_row(p):
    v = _pair_columns(p.reshape(2 * C_HEADS).astype(F32))
    z4 = jnp.zeros((C_PAIRS, 4), F32)
    return jnp.concatenate([z4, v, jnp.zeros((C_PAIRS, LANES - 8), F32)], axis=-1)[:, None, :]


def kernel(x, rel_bias, ffn1_norm, ffn1_w_gate, ffn1_w_up, ffn1_w_down, mix_norm, w_in, a_q_norm, a_k_norm, b_q_norm, b_k_norm, c_conv, c_A_log, c_dt_bias, c_out_norm, w_out, ffn2_norm, ffn2_w_gate, ffn2_w_up, ffn2_w_down):
    batch, seq, d = x.shape
    depth = w_in.shape[0]
    scale = HEAD_DIM ** -0.5
    h = x.reshape(batch * seq, d)
    cos, sin = _rope_tables(seq)
    idx = np.arange(B_W)
    mblk = jnp.asarray((idx[:, None] // HEAD_DIM == idx[None, :] // HEAD_DIM) / HEAD_DIM, BF16)
    biases = [_branch_bias(rel_bias, dil, seq // dil) for _, dil in DIL_PAIRS]
    row = lambda v: v.reshape(1, -1).astype(F32)
    for l in range(depth):
        h = _ffn(h, row(ffn1_norm[l]), ffn1_w_gate[l].astype(BF16), ffn1_w_up[l].astype(BF16),
                 ffn1_w_down[l].astype(BF16))
        aq, ak, av, bq, bk, bv, cx, cz, gt = _proj(
            h, row(mix_norm[l]), _pad_w_in(w_in[l]), cos, sin,
            row(jnp.tile(a_q_norm[l], A_HEADS)) * scale, row(jnp.tile(a_k_norm[l], A_KV_HEADS)),
            row(jnp.tile(b_q_norm[l], B_HEADS)) * scale, row(jnp.tile(b_k_norm[l], B_HEADS)),
            mblk, seq)
        out_a = _attn_a(aq, ak, av, seq)
        obs, lses = [], []
        for (_, dil), bias in zip(DIL_PAIRS, biases):
            o, lse = _attn_b_branch(bq, bk, bv, bias, batch, seq, dil)
            obs.append(o)
            lses.append(lse)
        out_c = _delta(cx, cz, gt, c_conv[l].astype(F32), _pair_row(c_A_log[l]), _pair_row(c_dt_bias[l]),
                       row(jnp.tile(c_out_norm[l], 2)), batch, seq)
        h = _mix_out(h, out_a, obs, lses, out_c, w_out[l].astype(BF16))
        h = _ffn(h, row(ffn2_norm[l]), ffn2_w_gate[l].astype(BF16), ffn2_w_up[l].astype(BF16),
                 ffn2_w_down[l].astype(BF16))
    return h.reshape(batch, seq, d)
```

```python
import functools
import math

import numpy as np
import jax
import jax.numpy as jnp
from jax import lax
from jax.experimental import pallas as pl
from jax.experimental.pallas import tpu as pltpu

F32 = jnp.float32
BF16 = jnp.bfloat16

D_MODEL = 1024
HEAD_DIM = 64
A_HEADS = 4
A_KV_HEADS = 2
B_HEADS = 6
C_HEADS = 6
D_FF = 2816
GRID_W = 64
ROPE_THETA = 10000.0
DIL_PAIRS = ((128, 1), (512, 4), (2048, 16))
REL_BUCKETS = 32
REL_MAX_DIST = 1024
CONV_K = 5
CHUNK = 64
NORM_EPS = 1e-6
NEG_INF = -1e30

A_QW = A_HEADS * HEAD_DIM
A_KVW = A_KV_HEADS * HEAD_DIM
B_W = B_HEADS * HEAD_DIM
C_W = C_HEADS * HEAD_DIM
C_PAIRS = C_HEADS // 2
LANES = 128
GATE_W = C_PAIRS * LANES

OFF_A = 0
OFF_B = OFF_A + A_QW + 2 * A_KVW
OFF_C = OFF_B + 3 * B_W
OFF_Z = OFF_C + 3 * C_W
OFF_G = OFF_Z + C_W
N_PROJ = OFF_G + GATE_W

TOKEN_TILE = 512
A_Q_TILE = 256
B_Q_TILE = 128
B_K_TILE = 256
B_SIDE = 64
CONV_TILE = 256
CHUNK_GROUP = 4
VMEM_LIMIT = 56 * 1024 * 1024


def _cparams(*sem):
    return pltpu.CompilerParams(dimension_semantics=sem, vmem_limit_bytes=VMEM_LIMIT)


def _resident(shape):
    return pl.BlockSpec(shape, lambda *_: (0,) * len(shape), pipeline_mode=pl.Buffered(1))


def _dot(a, b):
    return jnp.dot(a, b, preferred_element_type=F32)


def _dot_nt(a, b):
    return lax.dot_general(a, b, (((1,), (1,)), ((), ())), preferred_element_type=F32)


def _dot_tn(a, b):
    return lax.dot_general(a, b, (((0,), (0,)), ((), ())), preferred_element_type=F32)


def _silu(x):
    return x / (1.0 + jnp.exp(-x))


def _split_dot(x, m):
    hi = x.astype(BF16)
    lo = (x - hi.astype(F32)).astype(BF16)
    return _dot(hi, m) + _dot(lo, m)


def _rms_rows(x, g):
    ms = jnp.mean(x * x, axis=-1, keepdims=True)
    return x * lax.rsqrt(ms + NORM_EPS) * g


def _ffn_kernel(h_ref, g_ref, wg_ref, wu_ref, wd_ref, o_ref):
    x = h_ref[...]
    xn = _rms_rows(x, g_ref[...]).astype(BF16)
    gate = _dot(xn, wg_ref[...])
    up = _dot(xn, wu_ref[...])
    act = (_silu(gate) * up).astype(BF16)
    o_ref[...] = x + 0.5 * _dot(act, wd_ref[...])


def _ffn(h, g, wg, wu, wd):
    t, d = h.shape
    f = wg.shape[1]
    tm = min(TOKEN_TILE, t)
    return pl.pallas_call(
        _ffn_kernel,
        out_shape=jax.ShapeDtypeStruct((t, d), F32),
        grid=(t // tm,),
        in_specs=[pl.BlockSpec((tm, d), lambda i: (i, 0)),
                  _resident((1, d)), _resident((d, f)), _resident((d, f)), _resident((f, d))],
        out_specs=pl.BlockSpec((tm, d), lambda i: (i, 0)),
        compiler_params=_cparams("parallel"),
        name="ffn",
    )(h, g, wg, wu, wd)


def _head_norm(x, gain, mblk):
    ms = _split_dot(x * x, mblk)
    return x * lax.rsqrt(ms + NORM_EPS) * gain


def _rope(x, cos, sin):
    w = x.shape[1]
    lane = lax.broadcasted_iota(jnp.int32, x.shape, 1)
    first = (lane & (HEAD_DIM - 1)) < HEAD_DIM // 2
    half = HEAD_DIM // 2
    partner = jnp.where(first, pltpu.roll(x, w - half, 1), pltpu.roll(x, half, 1))
    return x * cos + partner * sin


def _proj_kernel(h_ref, g_ref, w_ref, cos_ref, sin_ref, gaq_ref, gak_ref, gbq_ref, gbk_ref,
                 mblk_ref, aq_o, ak_o, av_o, bq_o, bk_o, bv_o, cx_o, cz_o, gt_o):
    xn = _rms_rows(h_ref[...], g_ref[...]).astype(BF16)

    def proj(lo, width):
        return _dot(xn, w_ref[:, lo:lo + width])

    cos = cos_ref[...]
    sin = sin_ref[...]
    aq = _head_norm(proj(OFF_A, A_QW), gaq_ref[...], mblk_ref[:A_QW, :A_QW])
    aq_o[...] = _rope(aq, cos, sin).astype(BF16)
    ak = _head_norm(proj(OFF_A + A_QW, A_KVW), gak_ref[...], mblk_ref[:A_KVW, :A_KVW])
    ak_o[...] = _rope(ak, cos[:, :A_KVW], sin[:, :A_KVW]).astype(BF16)
    av_o[...] = proj(OFF_A + A_QW + A_KVW, A_KVW).astype(BF16)
    bq_o[...] = _head_norm(proj(OFF_B, B_W), gbq_ref[...], mblk_ref[...])
    bk_o[...] = _head_norm(proj(OFF_B + B_W, B_W), gbk_ref[...], mblk_ref[...])
    bv_o[...] = proj(OFF_B + 2 * B_W, B_W)
    cx_o[...] = proj(OFF_C, 3 * C_W)
    cz_o[...] = proj(OFF_Z, C_W).astype(BF16)
    gt_o[...] = proj(OFF_G, GATE_W)


def _proj(h, g, w, cos, sin, gaq, gak, gbq, gbk, mblk, seq):
    t, d = h.shape
    tm = min(TOKEN_TILE, seq)
    per_seq = seq // tm
    row = lambda i: (i, 0)
    pos = lambda i: (i % per_seq, 0)
    widths = (A_QW, A_KVW, A_KVW, B_W, B_W, B_W, 3 * C_W, C_W, GATE_W)
    dtypes = (BF16, BF16, BF16, F32, F32, F32, F32, BF16, F32)
    return pl.pallas_call(
        _proj_kernel,
        out_shape=[jax.ShapeDtypeStruct((t, wd), dt) for wd, dt in zip(widths, dtypes)],
        grid=(t // tm,),
        in_specs=[pl.BlockSpec((tm, d), row),
                  _resident((1, d)), _resident((d, N_PROJ)),
                  pl.BlockSpec((tm, A_QW), pos),
                  pl.BlockSpec((tm, A_QW), pos),
                  _resident((1, A_QW)), _resident((1, A_KVW)), _resident((1, B_W)),
                  _resident((1, B_W)), _resident((B_W, B_W))],
        out_specs=[pl.BlockSpec((tm, wd), row) for wd in widths],
        compiler_params=_cparams("parallel"),
        name="mix_in",
    )(h, g, w, cos, sin, gaq, gak, gbq, gbk, mblk)


def _attn_a_kernel(q_ref, k_ref, v_ref, o_ref):
    group = A_HEADS // A_KV_HEADS
    outs = []
    for hk in range(A_KV_HEADS):
        k = k_ref[:, hk * HEAD_DIM:(hk + 1) * HEAD_DIM]
        v = v_ref[:, hk * HEAD_DIM:(hk + 1) * HEAD_DIM]
        for gi in range(group):
            hq = hk * group + gi
            q = q_ref[:, hq * HEAD_DIM:(hq + 1) * HEAD_DIM]
            s = _dot_nt(q, k)
            m = jnp.max(s, axis=-1, keepdims=True)
            p = jnp.exp(s - m)
            l = jnp.sum(p, axis=-1, keepdims=True)
            outs.append(_dot(p.astype(BF16), v) / l)
    o_ref[...] = jnp.concatenate(outs, axis=-1).astype(BF16)


def _attn_a(q, k, v, seq):
    t = q.shape[0]
    tq = min(A_Q_TILE, seq)
    per_seq = seq // tq
    return pl.pallas_call(
        _attn_a_kernel,
        out_shape=jax.ShapeDtypeStruct((t, A_QW), BF16),
        grid=(t // seq, per_seq),
        in_specs=[pl.BlockSpec((tq, A_QW), lambda b, i: (b * per_seq + i, 0)),
                  pl.BlockSpec((seq, A_KVW), lambda b, i: (b, 0)),
                  pl.BlockSpec((seq, A_KVW), lambda b, i: (b, 0))],
        out_specs=pl.BlockSpec((tq, A_QW), lambda b, i: (b * per_seq + i, 0)),
        compiler_params=_cparams("parallel", "parallel"),
        name="attn_a",
    )(q, k, v)


def _t5_bucket_np(rel):
    half = REL_BUCKETS // 2
    exact = half // 2
    sign = np.where(rel > 0, half, 0)
    n = np.abs(rel)
    nf = np.maximum(n, 1).astype(np.float32)
    large = exact + (np.log(nf / np.float32(exact)) / np.float32(math.log(REL_MAX_DIST / exact))
                     * np.float32(half - exact)).astype(np.int32)
    large = np.minimum(large, half - 1)
    return sign + np.where(n < exact, n, large)


def _branch_tiles(sub_len):
    if sub_len >= B_K_TILE:
        return B_Q_TILE, B_K_TILE, (0, -B_SIDE, B_Q_TILE - B_K_TILE)
    return sub_len, sub_len, (0,)


def _branch_bias(rel_bias, dil, sub_len):
    tq, tk, offs = _branch_tiles(sub_len)
    period = tq + tk
    slot = np.arange(period)
    tabs = []
    for off in offs:
        delta = np.where(slot < tk, slot, slot - period) + off
        bucket = _t5_bucket_np(delta * dil)
        vec = jnp.where((np.abs(delta) <= B_SIDE)[:, None], rel_bias.astype(F32)[bucket], NEG_INF).T
        flat = jnp.tile(vec, (1, tq))[:, :tq * (period - 1)]
        tabs.append(flat.reshape(B_HEADS, tq, period - 1)[:, :, :tk])
    return jnp.stack(tabs)


def _attn_b_kernel(q_ref, k_ref, v_ref, b0_ref, b1_ref, b2_ref, o_ref, num_s, mx_s, den_s, *, seq):
    head0 = lax.broadcasted_iota(jnp.int32, (1, LANES), 1) < HEAD_DIM
    pick = lambda a, b: jnp.where(head0, a, b)

    def tile_attn(qt, kt, vt, bias_ref, tab):
        k16, v16 = kt.astype(BF16), vt.astype(BF16)
        parts = []
        for h in range(2):
            qh = (pick(qt, 0.0) if h == 0 else pick(0.0, qt)).astype(BF16)
            s = _dot_nt(qh, k16) + bias_ref[tab, h]
            m = jnp.max(s, axis=-1, keepdims=True)
            p = jnp.exp(s - m)
            l = jnp.sum(p, axis=-1, keepdims=True)
            parts.append((_dot(p.astype(BF16), v16), m, l))
        return tuple(pick(a, b) for a, b in zip(*parts))

    for g, ((_, dil), bias_ref) in enumerate(zip(DIL_PAIRS, (b0_ref, b1_ref, b2_ref))):
        sub_len = seq // dil
        tq, tk, _ = _branch_tiles(sub_len)
        n_tiles = sub_len // tq

        for r in range(dil):
            for t in range(n_tiles):
                if n_tiles == 1:
                    k0, tab = 0, 0
                else:
                    k0 = min(max(t * tq - B_SIDE, 0), sub_len - tk)
                    tab = 0 if t == 0 else 2 if t == n_tiles - 1 else 1
                rows = pl.ds(r + dil * t * tq, tq, stride=dil)
                keys = pl.ds(r + dil * k0, tk, stride=dil)
                num, mx, den = tile_attn(q_ref[rows, :], k_ref[keys, :], v_ref[keys, :], bias_ref, tab)
                num_s.at[g][rows, :] = num
                mx_s.at[g][rows, :] = mx
                den_s.at[g][rows, :] = den

    def merge(t, carry):
        rows = pl.ds(pl.multiple_of(t * CONV_TILE, CONV_TILE), CONV_TILE)
        ms = [mx_s[g, rows, :] for g in range(3)]
        top = jnp.maximum(jnp.maximum(ms[0], ms[1]), ms[2])
        es = [jnp.exp(m - top) for m in ms]
        num = sum(e * num_s[g, rows, :] for g, e in enumerate(es))
        den = sum(e * den_s[g, rows, :] for g, e in enumerate(es))
        o_ref[rows, :] = (num / den).astype(BF16)
        return carry

    lax.fori_loop(0, seq // CONV_TILE, merge, 0)


def _attn_b(q, k, v, biases, batch, seq):
    t = q.shape[0]
    pair = pl.BlockSpec((seq, LANES), lambda b, p: (b, p))
    bias_spec = lambda x: pl.BlockSpec((x.shape[0], 2) + x.shape[2:], lambda b, p: (0, p, 0, 0))
    return pl.pallas_call(
        functools.partial(_attn_b_kernel, seq=seq),
        out_shape=jax.ShapeDtypeStruct((t, B_W), BF16),
        grid=(batch, B_HEADS // 2),
        in_specs=[pair, pair, pair] + [bias_spec(x) for x in biases],
        out_specs=pair,
        scratch_shapes=[pltpu.VMEM((len(DIL_PAIRS), seq, LANES), F32)] * 3,
        compiler_params=_cparams("parallel", "parallel"),
        name="attn_b",
    )(q, k, v, *biases)


BASE_BLOCK = 8


def _pair_block_diag(x, same_head):
    return jnp.where(same_head, jnp.concatenate([x, x], axis=0), 0.0).astype(BF16)


def _inv_unit_triangular_pairs(ms, same_block, same_head, eye):
    bd = lambda x: _pair_block_diag(x, same_head)
    b16 = lambda x: x.astype(BF16)
    m8 = [b16(jnp.where(same_block[BASE_BLOCK], m, 0.0)) for m in ms]
    p2 = [_dot(a, bd(a.astype(F32))) for a in m8]
    p2h = [b16(p) for p in p2]
    p4 = [_dot(a, bd(p)) for a, p in zip(p2h, p2)]
    p6 = [_dot(a, bd(p)) for a, p in zip(p2h, p4)]
    even = [eye + a + b + c for a, b, c in zip(p2, p4, p6)]
    inv = [e - _dot(a, bd(e)) for a, e in zip(m8, even)]
    size = BASE_BLOCK
    while size < CHUNK:
        joins = same_block[2 * size] & jnp.logical_not(same_block[size])
        t = [_dot(b16(jnp.where(joins, m, 0.0)), bd(x)) for m, x in zip(ms, inv)]
        inv = [x - _dot(b16(x), bd(y)) for x, y in zip(inv, t)]
        size *= 2
    return inv


def _delta_kernel(xq_ref, xk_ref, xv_ref, wq_ref, wk_ref, wv_ref, z_ref, gt_ref,
                  alog_ref, dtb_ref, gain_ref, o_ref,
                  pad_s, q_s, k_s, v_s, beta_s, g_s, of_s, ob_s, st_s,
                  u_s, lq_s, a_s, kd_s, eg_s, *, seq):
    n_chunks = seq // CHUNK
    pad = 8
    lane = lax.broadcasted_iota(jnp.int32, (1, LANES), 1)
    head0 = lane < HEAD_DIM

    def per_head_sum(x):
        s0 = jnp.sum(jnp.where(head0, x, 0.0), axis=-1, keepdims=True)
        s1 = jnp.sum(jnp.where(head0, 0.0, x), axis=-1, keepdims=True)
        return jnp.where(head0, s0, s1)

    zeros = jnp.zeros((pad, LANES), F32)
    for j, (x_ref, w_ref, dst, scale) in enumerate(((xq_ref, wq_ref, q_s, HEAD_DIM ** -0.5),
                                                     (xk_ref, wk_ref, k_s, 1.0),
                                                     (xv_ref, wv_ref, v_s, None))):
        pad_s[j, 0:pad, :] = zeros
        pad_s[j, pad + seq:pad + seq + pad, :] = zeros
        pad_s[j, pad:pad + seq, :] = x_ref[...]

        def conv_tile(t, carry, j=j, w_ref=w_ref, dst=dst, scale=scale):
            r0 = pl.multiple_of(t * CONV_TILE, CONV_TILE)
            halo = pad_s[j, pl.ds(r0, CONV_TILE + 2 * pad), :]
            acc = jnp.zeros((CONV_TILE, LANES), F32)
            for d in range(CONV_K):
                lo = pad - CONV_K // 2 + d
                acc = acc + w_ref[d:d + 1, :] * halo[lo:lo + CONV_TILE]
            y = _silu(acc)
            if scale is not None:
                y = y * lax.rsqrt(per_head_sum(y * y) + NORM_EPS) * scale
            dst[pl.ds(r0, CONV_TILE), :] = y
            return carry

        lax.fori_loop(0, seq // CONV_TILE, conv_tile, 0)

    gt = gt_ref[...]
    beta_s[...] = 1.0 / (1.0 + jnp.exp(-gt))
    gx = gt + dtb_ref[0]
    softplus = jnp.maximum(gx, 0.0) + jnp.log(1.0 + jnp.exp(-jnp.abs(gx)))
    g_s[...] = -jnp.exp(alog_ref[0]) * softplus
    st_s[...] = jnp.zeros(st_s.shape, F32)

    rows = lax.broadcasted_iota(jnp.int32, (CHUNK, LANES), 0)
    cols = lax.broadcasted_iota(jnp.int32, (CHUNK, LANES), 1) & (HEAD_DIM - 1)
    head0_w = lax.broadcasted_iota(jnp.int32, (CHUNK, LANES), 1) < HEAD_DIM
    eye_b = rows == cols
    eye = eye_b.astype(F32)
    same_block = {}
    size = BASE_BLOCK
    while size <= CHUNK:
        shift = size.bit_length() - 1
        same_block[size] = (rows >> shift) == (cols >> shift)
        size *= 2
    sq_r = lax.broadcasted_iota(jnp.int32, (LANES, LANES), 0)
    sq_c = lax.broadcasted_iota(jnp.int32, (LANES, LANES), 1)
    same_head = (sq_r < HEAD_DIM) == (sq_c < HEAD_DIM)
    r64 = lax.broadcasted_iota(jnp.int32, (CHUNK, CHUNK), 0)
    c64 = lax.broadcasted_iota(jnp.int32, (CHUNK, CHUNK), 1)
    tri = ((r64 >= c64).astype(BF16), (r64 <= c64).astype(BF16))
    incl = (rows >= cols, rows <= cols)
    strict = (rows > cols, rows < cols)
    total_row = (CHUNK - 1, 0)
    bd = lambda x: _pair_block_diag(x, same_head)

    def pair_bcast(x, c0):
        return jnp.where(head0_w, jnp.broadcast_to(x[:, c0:c0 + 1], (CHUNK, LANES)),
                         jnp.broadcast_to(x[:, c0 + 1:c0 + 2], (CHUNK, LANES)))

    def chunk_group(it, carry):
        sls = [pl.ds(pl.multiple_of((it * CHUNK_GROUP + g) * CHUNK, CHUNK), CHUNK)
               for g in range(CHUNK_GROUP)]
        chains = [(g, d) for g in range(CHUNK_GROUP) for d in range(2)]
        k2 = [k_s[sl, :] for sl in sls]
        q2 = [q_s[sl, :] for sl in sls]
        kbd = [bd(k) for k in k2]
        gcum = {(g, d): _split_dot_left(tri[d], g_s[sls[g], :]) for g, d in chains}
        kk = [_dot_nt(k.astype(BF16), b) for k, b in zip(k2, kbd)]
        qk = [_dot_nt(q.astype(BF16), b) for q, b in zip(q2, kbd)]
        beta, gc, egc, ms = {}, {}, {}, []
        for g, d in chains:
            beta[g, d] = pair_bcast(beta_s[sls[g], :], 2 * d)
            gc[g, d] = pair_bcast(gcum[g, d], 4 + 2 * d)
            gc_row = jnp.sum(jnp.where(eye_b, gc[g, d], 0.0), axis=0, keepdims=True)
            decay = jnp.where(incl[d], jnp.exp(gc[g, d] - gc_row), 0.0)
            ms.append(jnp.where(strict[d], beta[g, d] * kk[g] * decay, 0.0))
            a_s[d, sls[g], :] = (qk[g] * decay).astype(BF16)
            egc[g, d] = jnp.exp(gc[g, d])
            g_last = gc[g, d][total_row[d]:total_row[d] + 1, :]
            kd_s[d, sls[g], :] = (k2[g] * jnp.exp(g_last - gc[g, d])).astype(BF16)
            eg_s[d, pl.ds(pl.multiple_of((it * CHUNK_GROUP + g) * 8, 8), 8), :] = (
                jnp.broadcast_to(jnp.exp(g_last), (8, LANES)))
        inv = [x.astype(BF16) for x in _inv_unit_triangular_pairs(ms, same_block, same_head, eye)]
        us = [_dot(x, bd(v_s[sls[g], :] * beta[g, d])) for x, (g, d) in zip(inv, chains)]
        ws = [_dot(x, bd(k2[g] * (beta[g, d] * egc[g, d]))) for x, (g, d) in zip(inv, chains)]
        for u, w, (g, d) in zip(us, ws, chains):
            u_s[d, sls[g], :] = u
            lq_s[d, it * CHUNK_GROUP + g] = jnp.concatenate([w, q2[g] * egc[g, d]], axis=0).astype(BF16)
        return carry

    lax.fori_loop(0, n_chunks // CHUNK_GROUP, chunk_group, 0)

    def scan_step(i, carry):
        ns = (i, n_chunks - 1 - i)
        sls = [pl.ds(pl.multiple_of(n * CHUNK, CHUNK), CHUNK) for n in ns]
        state = [st_s[d] for d in range(2)]
        rs = [_dot(lq_s[d, ns[d]], state[d].astype(BF16)) for d in range(2)]
        v_new = [u_s[d, sls[d], :] - rs[d][:CHUNK] for d in range(2)]
        outs = [rs[d][CHUNK:] + _dot(a_s[d, sls[d], :], bd(v_new[d])) for d in range(2)]
        upd = [_dot_tn(kd_s[d, sls[d], :], v_new[d].astype(BF16)) for d in range(2)]
        for d, dst in ((0, of_s), (1, ob_s)):
            eg = eg_s[d, pl.ds(pl.multiple_of(ns[d] * 8, 8), 8), :][0:1]
            st_s[d] = state[d] * eg + jnp.where(same_head, upd[d], 0.0)
            dst[sls[d], :] = outs[d]
        return carry

    lax.fori_loop(0, n_chunks, scan_step, 0)

    o = of_s[...] + ob_s[...]
    ms = per_head_sum(o * o) * (1.0 / HEAD_DIM)
    o = o * lax.rsqrt(ms + NORM_EPS) * gain_ref[...]
    o_ref[...] = (o * _silu(z_ref[...].astype(F32))).astype(BF16)


def _split_dot_left(m, x):
    hi = x.astype(BF16)
    lo = (x - hi.astype(F32)).astype(BF16)
    return _dot(m, hi) + _dot(m, lo)


def _delta(cx, cz, gt, conv_w, alog, dtb, gain, batch, seq):
    t = cx.shape[0]
    xspec = lambda part: pl.BlockSpec((seq, LANES), lambda b, p: (b, part * C_PAIRS + p))
    wspec = lambda part: pl.BlockSpec((CONV_K, LANES), lambda b, p: (0, part * C_PAIRS + p))
    pair = pl.BlockSpec((seq, LANES), lambda b, p: (b, p))
    prow = pl.BlockSpec((1, 1, LANES), lambda b, p: (p, 0, 0))
    return pl.pallas_call(
        functools.partial(_delta_kernel, seq=seq),
        out_shape=jax.ShapeDtypeStruct((t, C_W), BF16),
        grid=(batch, C_PAIRS),
        in_specs=[xspec(0), xspec(1), xspec(2), wspec(0), wspec(1), wspec(2),
                  pair, pair, prow, prow,
                  _resident((1, LANES))],
        out_specs=pair,
        scratch_shapes=[pltpu.VMEM((3, seq + 16, LANES), F32),
                        pltpu.VMEM((seq, LANES), F32),
                        pltpu.VMEM((seq, LANES), F32),
                        pltpu.VMEM((seq, LANES), F32),
                        pltpu.VMEM((seq, LANES), F32),
                        pltpu.VMEM((seq, LANES), F32),
                        pltpu.VMEM((seq, LANES), F32),
                        pltpu.VMEM((seq, LANES), F32),
                        pltpu.VMEM((2, LANES, LANES), F32),
                        pltpu.VMEM((2, seq, LANES), F32),
                        pltpu.VMEM((2, seq // CHUNK, LANES, LANES), BF16),
                        pltpu.VMEM((2, seq, LANES), BF16),
                        pltpu.VMEM((2, seq, LANES), BF16),
                        pltpu.VMEM((2, seq // CHUNK * 8, LANES), F32)],
        compiler_params=_cparams("parallel", "parallel"),
        name="delta",
    )(cx, cx, cx, conv_w, conv_w, conv_w, cz, gt, alog, dtb, gain)


def _out_kernel(h_ref, a_ref, b_ref, c_ref, w_ref, o_ref):
    y = _dot(a_ref[...], w_ref[0:A_QW, :])
    y = y + _dot(b_ref[...], w_ref[A_QW:A_QW + B_W, :])
    y = y + _dot(c_ref[...], w_ref[A_QW + B_W:, :])
    o_ref[...] = h_ref[...] + y


def _mix_out(h, a, b, c, w):
    t, d = h.shape
    tm = min(TOKEN_TILE, t)
    row = lambda width: pl.BlockSpec((tm, width), lambda i: (i, 0))
    return pl.pallas_call(
        _out_kernel,
        out_shape=jax.ShapeDtypeStruct((t, d), F32),
        grid=(t // tm,),
        in_specs=[row(d), row(A_QW), row(B_W), row(C_W), _resident(w.shape)],
        out_specs=row(d),
        compiler_params=_cparams("parallel"),
        name="mix_out",
    )(h, a, b, c, w)


def _rope_tables(seq):
    rows = seq // GRID_W
    row = jnp.repeat(jnp.arange(rows), GRID_W).astype(F32)
    col = jnp.tile(jnp.arange(GRID_W), rows).astype(F32)
    n_freq = HEAD_DIM // 4
    inv = ROPE_THETA ** (-jnp.arange(n_freq, dtype=F32) / n_freq)
    ang = jnp.concatenate([row[:, None] * inv, col[:, None] * inv], axis=-1)
    cos, sin = jnp.cos(ang), jnp.sin(ang)
    cos_h = jnp.concatenate([cos, cos], axis=-1)
    sin_h = jnp.concatenate([-sin, sin], axis=-1)
    return jnp.tile(cos_h, (1, A_HEADS)), jnp.tile(sin_h, (1, A_HEADS))


def _pair_columns(x):
    lead = x.shape[:-1]
    x = x.reshape(lead + (2, C_PAIRS, 2))
    return jnp.moveaxis(x, -2, -3).reshape(lead + (C_PAIRS, 4))


def _pad_w_in(w_in):
    d = w_in.shape[0]
    main = w_in[:, :OFF_G]
    cb = _pair_columns(w_in[:, OFF_G:OFF_G + 2 * C_HEADS])
    ca = _pair_columns(w_in[:, OFF_G + 2 * C_HEADS:])
    gates = jnp.concatenate([cb, ca, jnp.zeros((d, C_PAIRS, LANES - 8), w_in.dtype)], axis=-1)
    return jnp.concatenate([main, gates.reshape(d, GATE_W)], axis=-1).astype(BF16)


def _pair_row(p):
    v = _pair_columns(p.reshape(2 * C_HEADS).astype(F32))
    z4 = jnp.zeros((C_PAIRS, 4), F32)
    return jnp.concatenate([z4, v, jnp.zeros((C_PAIRS, LANES - 8), F32)], axis=-1)[:, None, :]


def kernel(x, rel_bias, ffn1_norm, ffn1_w_gate, ffn1_w_up, ffn1_w_down, mix_norm, w_in, a_q_norm, a_k_norm, b_q_norm, b_k_norm, c_conv, c_A_log, c_dt_bias, c_out_norm, w_out, ffn2_norm, ffn2_w_gate, ffn2_w_up, ffn2_w_down):
    batch, seq, d = x.shape
    depth = w_in.shape[0]
    scale = HEAD_DIM ** -0.5
    h = x.reshape(batch * seq, d)
    cos, sin = _rope_tables(seq)
    idx = np.arange(B_W)
    mblk = jnp.asarray((idx[:, None] // HEAD_DIM == idx[None, :] // HEAD_DIM) / HEAD_DIM, BF16)
    biases = [_branch_bias(rel_bias, dil, seq // dil) for _, dil in DIL_PAIRS]
    row = lambda v: v.reshape(1, -1).astype(F32)
    for l in range(depth):
        h = _ffn(h, row(ffn1_norm[l]), ffn1_w_gate[l].astype(BF16), ffn1_w_up[l].astype(BF16),
                 ffn1_w_down[l].astype(BF16))
        aq, ak, av, bq, bk, bv, cx, cz, gt = _proj(
            h, row(mix_norm[l]), _pad_w_in(w_in[l]), cos, sin,
            row(jnp.tile(a_q_norm[l], A_HEADS)) * scale, row(jnp.tile(a_k_norm[l], A_KV_HEADS)),
            row(jnp.tile(b_q_norm[l], B_HEADS)) * scale, row(jnp.tile(b_k_norm[l], B_HEADS)),
            mblk, seq)
        out_a = _attn_a(aq, ak, av, seq)
        out_b = _attn_b(bq, bk, bv, biases, batch, seq)
        out_c = _delta(cx, cz, gt, c_conv[l].astype(F32), _pair_row(c_A_log[l]), _pair_row(c_dt_bias[l]),
                       row(jnp.tile(c_out_norm[l], 2)), batch, seq)
        h = _mix_out(h, out_a, out_b, out_c, w_out[l].astype(BF16))
        h = _ffn(h, row(ffn2_norm[l]), ffn2_w_gate[l].astype(BF16), ffn2_w_up[l].astype(BF16),
                 ffn2_w_down[l].astype(BF16))
    return h.reshape(batch, seq, d)
```

```python
import functools
import math

import numpy as np
import jax
import jax.numpy as jnp
from jax import lax
from jax.experimental import pallas as pl
from jax.experimental.pallas import tpu as pltpu

F32 = jnp.float32
BF16 = jnp.bfloat16

D_MODEL = 1024
HEAD_DIM = 64
A_HEADS = 4
A_KV_HEADS = 2
B_HEADS = 6
C_HEADS = 6
D_FF = 2816
GRID_W = 64
ROPE_THETA = 10000.0
DIL_PAIRS = ((128, 1), (512, 4), (2048, 16))
REL_BUCKETS = 32
REL_MAX_DIST = 1024
CONV_K = 5
CHUNK = 64
NORM_EPS = 1e-6
NEG_INF = -1e30

A_QW = A_HEADS * HEAD_DIM
A_KVW = A_KV_HEADS * HEAD_DIM
B_W = B_HEADS * HEAD_DIM
C_W = C_HEADS * HEAD_DIM
C_PAIRS = C_HEADS // 2
LANES = 128
GATE_W = C_PAIRS * LANES

OFF_A = 0
OFF_B = OFF_A + A_QW + 2 * A_KVW
OFF_C = OFF_B + 3 * B_W
OFF_Z = OFF_C + 3 * C_W
OFF_G = OFF_Z + C_W
N_PROJ = OFF_G + GATE_W

TOKEN_TILE = 512
A_Q_TILE = 256
B_Q_TILE = 128
B_K_TILE = 256
B_SIDE = 64
CONV_TILE = 256
CHUNK_GROUP = 8
VMEM_LIMIT = 56 * 1024 * 1024


def _cparams(*sem):
    return pltpu.CompilerParams(dimension_semantics=sem, vmem_limit_bytes=VMEM_LIMIT)


def _resident(shape):
    return pl.BlockSpec(shape, lambda *_: (0,) * len(shape), pipeline_mode=pl.Buffered(1))


def _dot(a, b):
    return jnp.dot(a, b, preferred_element_type=F32)


def _dot_nt(a, b):
    return lax.dot_general(a, b, (((1,), (1,)), ((), ())), preferred_element_type=F32)


def _dot_tn(a, b):
    return lax.dot_general(a, b, (((0,), (0,)), ((), ())), preferred_element_type=F32)


def _silu(x):
    return x / (1.0 + jnp.exp(-x))


def _split_dot(x, m):
    hi = x.astype(BF16)
    lo = (x - hi.astype(F32)).astype(BF16)
    return _dot(hi, m) + _dot(lo, m)


def _rms_rows(x, g):
    ms = jnp.mean(x * x, axis=-1, keepdims=True)
    return x * lax.rsqrt(ms + NORM_EPS) * g


def _ffn_kernel(h_ref, g_ref, wg_ref, wu_ref, wd_ref, o_ref):
    x = h_ref[...]
    xn = _rms_rows(x, g_ref[...]).astype(BF16)
    gate = _dot(xn, wg_ref[...])
    up = _dot(xn, wu_ref[...])
    act = (_silu(gate) * up).astype(BF16)
    o_ref[...] = x + 0.5 * _dot(act, wd_ref[...])


def _ffn(h, g, wg, wu, wd):
    t, d = h.shape
    f = wg.shape[1]
    tm = min(TOKEN_TILE, t)
    return pl.pallas_call(
        _ffn_kernel,
        out_shape=jax.ShapeDtypeStruct((t, d), F32),
        grid=(t // tm,),
        in_specs=[pl.BlockSpec((tm, d), lambda i: (i, 0)),
                  _resident((1, d)), _resident((d, f)), _resident((d, f)), _resident((f, d))],
        out_specs=pl.BlockSpec((tm, d), lambda i: (i, 0)),
        compiler_params=_cparams("parallel"),
        name="ffn",
    )(h, g, wg, wu, wd)


def _head_norm(x, gain, mblk):
    ms = _split_dot(x * x, mblk)
    return x * lax.rsqrt(ms + NORM_EPS) * gain


def _rope(x, cos, sin):
    w = x.shape[1]
    lane = lax.broadcasted_iota(jnp.int32, x.shape, 1)
    first = (lane & (HEAD_DIM - 1)) < HEAD_DIM // 2
    half = HEAD_DIM // 2
    partner = jnp.where(first, pltpu.roll(x, w - half, 1), pltpu.roll(x, half, 1))
    return x * cos + partner * sin


def _proj_kernel(h_ref, g_ref, w_ref, wvt_ref, cos_ref, sin_ref, gaq_ref, gak_ref, gbq_ref, gbk_ref,
                 mblk_ref, aq_o, ak_o, avt_o, bq_o, bk_o, bv_o, cx_o, cz_o, gt_o):
    xn = _rms_rows(h_ref[...], g_ref[...]).astype(BF16)

    def proj(lo, width):
        return _dot(xn, w_ref[:, lo:lo + width])

    cos = cos_ref[...]
    sin = sin_ref[...]
    aq = _head_norm(proj(OFF_A, A_QW), gaq_ref[...], mblk_ref[:A_QW, :A_QW])
    aq_o[...] = _rope(aq, cos, sin).astype(BF16)
    ak = _head_norm(proj(OFF_A + A_QW, A_KVW), gak_ref[...], mblk_ref[:A_KVW, :A_KVW])
    ak_o[...] = _rope(ak, cos[:, :A_KVW], sin[:, :A_KVW]).astype(BF16)
    avt_o[...] = _dot_nt(wvt_ref[...], xn).astype(BF16)
    bq_o[...] = _head_norm(proj(OFF_B, B_W), gbq_ref[...], mblk_ref[...])
    bk_o[...] = _head_norm(proj(OFF_B + B_W, B_W), gbk_ref[...], mblk_ref[...])
    bv_o[...] = proj(OFF_B + 2 * B_W, B_W)
    cx_o[...] = proj(OFF_C, 3 * C_W)
    cz_o[...] = proj(OFF_Z, C_W).astype(BF16)
    gt_o[...] = proj(OFF_G, GATE_W)


def _proj(h, g, w, wvt, cos, sin, gaq, gak, gbq, gbk, mblk, seq):
    t, d = h.shape
    tm = min(TOKEN_TILE, seq)
    per_seq = seq // tm
    row = lambda i: (i, 0)
    pos = lambda i: (i % per_seq, 0)
    widths = (A_QW, A_KVW, None, B_W, B_W, B_W, 3 * C_W, C_W, GATE_W)
    dtypes = (BF16, BF16, BF16, F32, F32, F32, F32, BF16, F32)
    shapes = [(t, wd) if wd else (A_KVW, t) for wd in widths]
    specs = [pl.BlockSpec((tm, wd), row) if wd else pl.BlockSpec((A_KVW, tm), lambda i: (0, i))
             for wd in widths]
    return pl.pallas_call(
        _proj_kernel,
        out_shape=[jax.ShapeDtypeStruct(s, dt) for s, dt in zip(shapes, dtypes)],
        grid=(t // tm,),
        in_specs=[pl.BlockSpec((tm, d), row),
                  _resident((1, d)), _resident((d, N_PROJ)), _resident((A_KVW, d)),
                  pl.BlockSpec((tm, A_QW), pos),
                  pl.BlockSpec((tm, A_QW), pos),
                  _resident((1, A_QW)), _resident((1, A_KVW)), _resident((1, B_W)),
                  _resident((1, B_W)), _resident((B_W, B_W))],
        out_specs=specs,
        compiler_params=_cparams("parallel"),
        name="mix_in",
    )(h, g, w, wvt, cos, sin, gaq, gak, gbq, gbk, mblk)


A_Q_HEAD_ORDER = tuple(hk * (A_HEADS // A_KV_HEADS) + g
                       for g in range(A_HEADS // A_KV_HEADS) for hk in range(A_KV_HEADS))


def _attn_a_kernel(q_ref, k_ref, vt_ref, o_ref):
    k2 = k_ref[...]
    lane_hi = lax.broadcasted_iota(jnp.int32, (1, LANES), 1) >= HEAD_DIM
    row_hi = lax.broadcasted_iota(jnp.int32, (LANES, 1), 0) >= HEAD_DIM
    vt = vt_ref[...].astype(F32)
    vt_ones = [jnp.where(row_hi == (hk == 1), vt, 1.0).astype(BF16) for hk in range(A_KV_HEADS)]
    heads = [(g, hk) for g in range(A_HEADS // A_KV_HEADS) for hk in range(A_KV_HEADS)]

    def scores(g, hk):
        qb = q_ref[:, g * LANES:(g + 1) * LANES].astype(F32)
        qm = jnp.where(lane_hi == (hk == 1), qb, 0.0).astype(BF16)
        return _dot_nt(k2, qm)

    st_next = scores(*heads[0])
    halves = []
    for i, (g, hk) in enumerate(heads):
        st = st_next
        if i + 1 < len(heads):
            st_next = scores(*heads[i + 1])
        m = jnp.max(st, axis=0, keepdims=True)
        p = jnp.exp(st - m).astype(BF16)
        ov = _dot(vt_ones[hk], p)
        den_row = (1 - hk) * HEAD_DIM
        halves.append(ov / ov[den_row:den_row + 1, :])
        if hk == A_KV_HEADS - 1:
            ot = jnp.where(row_hi, halves[1], halves[0])
            o_ref[:, g * LANES:(g + 1) * LANES] = ot.T.astype(BF16)
            halves = []


def _attn_a(q, k, vt, seq):
    t = q.shape[0]
    tq = min(A_Q_TILE, seq)
    per_seq = seq // tq
    return pl.pallas_call(
        _attn_a_kernel,
        out_shape=jax.ShapeDtypeStruct((t, A_QW), BF16),
        grid=(t // seq, per_seq),
        in_specs=[pl.BlockSpec((tq, A_QW), lambda b, i: (b * per_seq + i, 0)),
                  pl.BlockSpec((seq, A_KVW), lambda b, i: (b, 0)),
                  pl.BlockSpec((A_KVW, seq), lambda b, i: (0, b))],
        out_specs=pl.BlockSpec((tq, A_QW), lambda b, i: (b * per_seq + i, 0)),
        compiler_params=_cparams("parallel", "parallel"),
        name="attn_a",
    )(q, k, vt)


def _t5_bucket_np(rel):
    half = REL_BUCKETS // 2
    exact = half // 2
    sign = np.where(rel > 0, half, 0)
    n = np.abs(rel)
    nf = np.maximum(n, 1).astype(np.float32)
    large = exact + (np.log(nf / np.float32(exact)) / np.float32(math.log(REL_MAX_DIST / exact))
                     * np.float32(half - exact)).astype(np.int32)
    large = np.minimum(large, half - 1)
    return sign + np.where(n < exact, n, large)


def _branch_tiles(sub_len):
    if sub_len >= B_K_TILE:
        return B_Q_TILE, B_K_TILE, (0, -B_SIDE, B_Q_TILE - B_K_TILE)
    return sub_len, sub_len, (0,)


def _branch_bias(rel_bias, dil, sub_len):
    tq, tk, offs = _branch_tiles(sub_len)
    period = tq + tk
    slot = np.arange(period)
    tabs = []
    for off in offs:
        delta = np.where(slot < tk, slot, slot - period) + off
        bucket = _t5_bucket_np(delta * dil)
        vec = jnp.where((np.abs(delta) <= B_SIDE)[:, None], rel_bias.astype(F32)[bucket], NEG_INF).T
        flat = jnp.tile(vec, (1, tq))[:, :tq * (period - 1)]
        tabs.append(flat.reshape(B_HEADS, tq, period - 1)[:, :, :tk])
    return jnp.stack(tabs)


def _attn_b_kernel(q_ref, k_ref, v_ref, b0_ref, b1_ref, b2_ref, o_ref, num_s, mx_s, den_s, *, seq):
    head0 = lax.broadcasted_iota(jnp.int32, (1, LANES), 1) < HEAD_DIM
    pick = lambda a, b: jnp.where(head0, a, b)

    def tile_attn(qt, kt, vt, bias_ref, tab):
        k16, v16 = kt.astype(BF16), vt.astype(BF16)
        parts = []
        for h in range(2):
            qh = (pick(qt, 0.0) if h == 0 else pick(0.0, qt)).astype(BF16)
            s = _dot_nt(qh, k16) + bias_ref[tab, h]
            m = jnp.max(s, axis=-1, keepdims=True)
            p = jnp.exp(s - m)
            l = jnp.sum(p, axis=-1, keepdims=True)
            parts.append((_dot(p.astype(BF16), v16), m, l))
        return tuple(pick(a, b) for a, b in zip(*parts))

    for g, ((_, dil), bias_ref) in enumerate(zip(DIL_PAIRS, (b0_ref, b1_ref, b2_ref))):
        sub_len = seq // dil
        tq, tk, _ = _branch_tiles(sub_len)
        n_tiles = sub_len // tq

        for r in range(dil):
            for t in range(n_tiles):
                if n_tiles == 1:
                    k0, tab = 0, 0
                else:
                    k0 = min(max(t * tq - B_SIDE, 0), sub_len - tk)
                    tab = 0 if t == 0 else 2 if t == n_tiles - 1 else 1
                rows = pl.ds(r + dil * t * tq, tq, stride=dil)
                keys = pl.ds(r + dil * k0, tk, stride=dil)
                num, mx, den = tile_attn(q_ref[rows, :], k_ref[keys, :], v_ref[keys, :], bias_ref, tab)
                num_s.at[g][rows, :] = num
                mx_s.at[g][rows, :] = mx
                den_s.at[g][rows, :] = den

    def merge(t, carry):
        rows = pl.ds(pl.multiple_of(t * CONV_TILE, CONV_TILE), CONV_TILE)
        ms = [mx_s[g, rows, :] for g in range(3)]
        top = jnp.maximum(jnp.maximum(ms[0], ms[1]), ms[2])
        es = [jnp.exp(m - top) for m in ms]
        num = sum(e * num_s[g, rows, :] for g, e in enumerate(es))
        den = sum(e * den_s[g, rows, :] for g, e in enumerate(es))
        o_ref[rows, :] = (num / den).astype(BF16)
        return carry

    lax.fori_loop(0, seq // CONV_TILE, merge, 0)


def _attn_b(q, k, v, biases, batch, seq):
    t = q.shape[0]
    pair = pl.BlockSpec((seq, LANES), lambda b, p: (b, p))
    bias_spec = lambda x: pl.BlockSpec((x.shape[0], 2) + x.shape[2:], lambda b, p: (0, p, 0, 0))
    return pl.pallas_call(
        functools.partial(_attn_b_kernel, seq=seq),
        out_shape=jax.ShapeDtypeStruct((t, B_W), BF16),
        grid=(batch, B_HEADS // 2),
        in_specs=[pair, pair, pair] + [bias_spec(x) for x in biases],
        out_specs=pair,
        scratch_shapes=[pltpu.VMEM((len(DIL_PAIRS), seq, LANES), F32)] * 3,
        compiler_params=_cparams("parallel", "parallel"),
        name="attn_b",
    )(q, k, v, *biases)


BASE_BLOCK = 8


def _pair_block_diag(x, same_head):
    return jnp.where(same_head, jnp.concatenate([x, x], axis=0), 0.0).astype(BF16)


def _inv_unit_triangular_pairs(ms, same_block, same_head, eye):
    bd = lambda x: _pair_block_diag(x, same_head)
    b16 = lambda x: x.astype(BF16)
    m8 = [b16(jnp.where(same_block[BASE_BLOCK], m, 0.0)) for m in ms]
    p2 = [_dot(a, bd(a.astype(F32))) for a in m8]
    p2h = [b16(p) for p in p2]
    p4 = [_dot(a, bd(p)) for a, p in zip(p2h, p2)]
    p6 = [_dot(a, bd(p)) for a, p in zip(p2h, p4)]
    even = [eye + a + b + c for a, b, c in zip(p2, p4, p6)]
    inv = [e - _dot(a, bd(e)) for a, e in zip(m8, even)]
    size = BASE_BLOCK
    while size < CHUNK:
        joins = same_block[2 * size] & jnp.logical_not(same_block[size])
        t = [_dot(b16(jnp.where(joins, m, 0.0)), bd(x)) for m, x in zip(ms, inv)]
        inv = [x - _dot(b16(x), bd(y)) for x, y in zip(inv, t)]
        size *= 2
    return inv


def _delta_kernel(xq_ref, xk_ref, xv_ref, wq_ref, wk_ref, wv_ref, z_ref, gt_ref,
                  alog_ref, dtb_ref, gain_ref, o_ref,
                  pad_s, q_s, k_s, v_s, beta_s, g_s, of_s, ob_s, st_s,
                  u_s, lq_s, a_s, kd_s, eg_s, *, seq):
    n_chunks = seq // CHUNK
    pad = 8
    lane = lax.broadcasted_iota(jnp.int32, (1, LANES), 1)
    head0 = lane < HEAD_DIM

    def per_head_sum(x):
        s0 = jnp.sum(jnp.where(head0, x, 0.0), axis=-1, keepdims=True)
        s1 = jnp.sum(jnp.where(head0, 0.0, x), axis=-1, keepdims=True)
        return jnp.where(head0, s0, s1)

    zeros = jnp.zeros((pad, LANES), F32)
    for j, (x_ref, w_ref, dst, scale) in enumerate(((xq_ref, wq_ref, q_s, HEAD_DIM ** -0.5),
                                                     (xk_ref, wk_ref, k_s, 1.0),
                                                     (xv_ref, wv_ref, v_s, None))):
        pad_s[j, 0:pad, :] = zeros
        pad_s[j, pad + seq:pad + seq + pad, :] = zeros
        pad_s[j, pad:pad + seq, :] = x_ref[...]

        def conv_tile(t, carry, j=j, w_ref=w_ref, dst=dst, scale=scale):
            r0 = pl.multiple_of(t * CONV_TILE, CONV_TILE)
            halo = pad_s[j, pl.ds(r0, CONV_TILE + 2 * pad), :]
            acc = jnp.zeros((CONV_TILE, LANES), F32)
            for d in range(CONV_K):
                lo = pad - CONV_K // 2 + d
                acc = acc + w_ref[d:d + 1, :] * halo[lo:lo + CONV_TILE]
            y = _silu(acc)
            if scale is not None:
                y = y * lax.rsqrt(per_head_sum(y * y) + NORM_EPS) * scale
            dst[pl.ds(r0, CONV_TILE), :] = y
            return carry

        lax.fori_loop(0, seq // CONV_TILE, conv_tile, 0)

    gt = gt_ref[...]
    beta_s[...] = 1.0 / (1.0 + jnp.exp(-gt))
    gx = gt + dtb_ref[0]
    softplus = jnp.maximum(gx, 0.0) + jnp.log(1.0 + jnp.exp(-jnp.abs(gx)))
    g_s[...] = -jnp.exp(alog_ref[0]) * softplus
    st_s[...] = jnp.zeros(st_s.shape, F32)

    rows = lax.broadcasted_iota(jnp.int32, (CHUNK, LANES), 0)
    cols = lax.broadcasted_iota(jnp.int32, (CHUNK, LANES), 1) & (HEAD_DIM - 1)
    head0_w = lax.broadcasted_iota(jnp.int32, (CHUNK, LANES), 1) < HEAD_DIM
    eye_b = rows == cols
    eye = eye_b.astype(F32)
    same_block = {}
    size = BASE_BLOCK
    while size <= CHUNK:
        shift = size.bit_length() - 1
        same_block[size] = (rows >> shift) == (cols >> shift)
        size *= 2
    sq_r = lax.broadcasted_iota(jnp.int32, (LANES, LANES), 0)
    sq_c = lax.broadcasted_iota(jnp.int32, (LANES, LANES), 1)
    same_head = (sq_r < HEAD_DIM) == (sq_c < HEAD_DIM)
    r64 = lax.broadcasted_iota(jnp.int32, (CHUNK, CHUNK), 0)
    c64 = lax.broadcasted_iota(jnp.int32, (CHUNK, CHUNK), 1)
    tri = ((r64 >= c64).astype(BF16), (r64 <= c64).astype(BF16))
    incl = (rows >= cols, rows <= cols)
    strict = (rows > cols, rows < cols)
    total_row = (CHUNK - 1, 0)
    bd = lambda x: _pair_block_diag(x, same_head)

    def pair_bcast(x, c0):
        return jnp.where(head0_w, jnp.broadcast_to(x[:, c0:c0 + 1], (CHUNK, LANES)),
                         jnp.broadcast_to(x[:, c0 + 1:c0 + 2], (CHUNK, LANES)))

    def chunk_group(it, carry):
        sls = [pl.ds(pl.multiple_of((it * CHUNK_GROUP + g) * CHUNK, CHUNK), CHUNK)
               for g in range(CHUNK_GROUP)]
        chains = [(g, d) for g in range(CHUNK_GROUP) for d in range(2)]
        k2 = [k_s[sl, :] for sl in sls]
        q2 = [q_s[sl, :] for sl in sls]
        kbd = [bd(k) for k in k2]
        gcum = {(g, d): _split_dot_left(tri[d], g_s[sls[g], :]) for g, d in chains}
        kk = [_dot_nt(k.astype(BF16), b) for k, b in zip(k2, kbd)]
        qk = [_dot_nt(q.astype(BF16), b) for q, b in zip(q2, kbd)]
        beta, gc, egc, ms = {}, {}, {}, []
        for g, d in chains:
            beta[g, d] = pair_bcast(beta_s[sls[g], :], 2 * d)
            gc[g, d] = pair_bcast(gcum[g, d], 4 + 2 * d)
            gc_row = jnp.sum(jnp.where(eye_b, gc[g, d], 0.0), axis=0, keepdims=True)
            decay = jnp.where(incl[d], jnp.exp(gc[g, d] - gc_row), 0.0)
            ms.append(jnp.where(strict[d], beta[g, d] * kk[g] * decay, 0.0))
            a_s[d, sls[g], :] = (qk[g] * decay).astype(BF16)
            egc[g, d] = jnp.exp(gc[g, d])
            g_last = gc[g, d][total_row[d]:total_row[d] + 1, :]
            kd_s[d, sls[g], :] = (k2[g] * jnp.exp(g_last - gc[g, d])).astype(BF16)
            eg_s[d, pl.ds(pl.multiple_of((it * CHUNK_GROUP + g) * 8, 8), 8), :] = (
                jnp.broadcast_to(jnp.exp(g_last), (8, LANES)))
        inv = [x.astype(BF16) for x in _inv_unit_triangular_pairs(ms, same_block, same_head, eye)]
        us = [_dot(x, bd(v_s[sls[g], :] * beta[g, d])) for x, (g, d) in zip(inv, chains)]
        ws = [_dot(x, bd(k2[g] * (beta[g, d] * egc[g, d]))) for x, (g, d) in zip(inv, chains)]
        for u, w, (g, d) in zip(us, ws, chains):
            u_s[d, sls[g], :] = u
            lq_s[d, it * CHUNK_GROUP + g] = jnp.concatenate([w, q2[g] * egc[g, d]], axis=0).astype(BF16)
        return carry

    lax.fori_loop(0, n_chunks // CHUNK_GROUP, chunk_group, 0)

    def scan_step(i, carry):
        ns = (i, n_chunks - 1 - i)
        sls = [pl.ds(pl.multiple_of(n * CHUNK, CHUNK), CHUNK) for n in ns]
        state = [st_s[d] for d in range(2)]
        rs = [_dot(lq_s[d, ns[d]], state[d].astype(BF16)) for d in range(2)]
        v_new = [u_s[d, sls[d], :] - rs[d][:CHUNK] for d in range(2)]
        outs = [rs[d][CHUNK:] + _dot(a_s[d, sls[d], :], bd(v_new[d])) for d in range(2)]
        upd = [_dot_tn(kd_s[d, sls[d], :], v_new[d].astype(BF16)) for d in range(2)]
        for d, dst in ((0, of_s), (1, ob_s)):
            eg = eg_s[d, pl.ds(pl.multiple_of(ns[d] * 8, 8), 8), :][0:1]
            st_s[d] = state[d] * eg + jnp.where(same_head, upd[d], 0.0)
            dst[sls[d], :] = outs[d]
        return carry

    lax.fori_loop(0, n_chunks, scan_step, 0)

    o = of_s[...] + ob_s[...]
    ms = per_head_sum(o * o) * (1.0 / HEAD_DIM)
    o = o * lax.rsqrt(ms + NORM_EPS) * gain_ref[...]
    o_ref[...] = (o * _silu(z_ref[...].astype(F32))).astype(BF16)


def _split_dot_left(m, x):
    hi = x.astype(BF16)
    lo = (x - hi.astype(F32)).astype(BF16)
    return _dot(m, hi) + _dot(m, lo)


def _delta(cx, cz, gt, conv_w, alog, dtb, gain, batch, seq):
    t = cx.shape[0]
    xspec = lambda part: pl.BlockSpec((seq, LANES), lambda b, p: (b, part * C_PAIRS + p))
    wspec = lambda part: pl.BlockSpec((CONV_K, LANES), lambda b, p: (0, part * C_PAIRS + p))
    pair = pl.BlockSpec((seq, LANES), lambda b, p: (b, p))
    prow = pl.BlockSpec((1, 1, LANES), lambda b, p: (p, 0, 0))
    return pl.pallas_call(
        functools.partial(_delta_kernel, seq=seq),
        out_shape=jax.ShapeDtypeStruct((t, C_W), BF16),
        grid=(batch, C_PAIRS),
        in_specs=[xspec(0), xspec(1), xspec(2), wspec(0), wspec(1), wspec(2),
                  pair, pair, prow, prow,
                  _resident((1, LANES))],
        out_specs=pair,
        scratch_shapes=[pltpu.VMEM((3, seq + 16, LANES), F32),
                        pltpu.VMEM((seq, LANES), F32),
                        pltpu.VMEM((seq, LANES), F32),
                        pltpu.VMEM((seq, LANES), F32),
                        pltpu.VMEM((seq, LANES), F32),
                        pltpu.VMEM((seq, LANES), F32),
                        pltpu.VMEM((seq, LANES), F32),
                        pltpu.VMEM((seq, LANES), F32),
                        pltpu.VMEM((2, LANES, LANES), F32),
                        pltpu.VMEM((2, seq, LANES), F32),
                        pltpu.VMEM((2, seq // CHUNK, LANES, LANES), BF16),
                        pltpu.VMEM((2, seq, LANES), BF16),
                        pltpu.VMEM((2, seq, LANES), BF16),
                        pltpu.VMEM((2, seq // CHUNK * 8, LANES), F32)],
        compiler_params=_cparams("parallel", "parallel"),
        name="delta",
    )(cx, cx, cx, conv_w, conv_w, conv_w, cz, gt, alog, dtb, gain)


def _out_kernel(h_ref, a_ref, b_ref, c_ref, w_ref, o_ref):
    y = _dot(a_ref[...], w_ref[0:A_QW, :])
    y = y + _dot(b_ref[...], w_ref[A_QW:A_QW + B_W, :])
    y = y + _dot(c_ref[...], w_ref[A_QW + B_W:, :])
    o_ref[...] = h_ref[...] + y


def _mix_out(h, a, b, c, w):
    t, d = h.shape
    tm = min(TOKEN_TILE, t)
    row = lambda width: pl.BlockSpec((tm, width), lambda i: (i, 0))
    return pl.pallas_call(
        _out_kernel,
        out_shape=jax.ShapeDtypeStruct((t, d), F32),
        grid=(t // tm,),
        in_specs=[row(d), row(A_QW), row(B_W), row(C_W), _resident(w.shape)],
        out_specs=row(d),
        compiler_params=_cparams("parallel"),
        name="mix_out",
    )(h, a, b, c, w)


def _rope_tables(seq):
    rows = seq // GRID_W
    row = jnp.repeat(jnp.arange(rows), GRID_W).astype(F32)
    col = jnp.tile(jnp.arange(GRID_W), rows).astype(F32)
    n_freq = HEAD_DIM // 4
    inv = ROPE_THETA ** (-jnp.arange(n_freq, dtype=F32) / n_freq)
    ang = jnp.concatenate([row[:, None] * inv, col[:, None] * inv], axis=-1)
    cos, sin = jnp.cos(ang), jnp.sin(ang)
    cos_h = jnp.concatenate([cos, cos], axis=-1)
    sin_h = jnp.concatenate([-sin, sin], axis=-1)
    return jnp.tile(cos_h, (1, A_HEADS)), jnp.tile(sin_h, (1, A_HEADS))


def _pair_columns(x):
    lead = x.shape[:-1]
    x = x.reshape(lead + (2, C_PAIRS, 2))
    return jnp.moveaxis(x, -2, -3).reshape(lead + (C_PAIRS, 4))


def _pad_w_in(w_in):
    d = w_in.shape[0]
    aq = [w_in[:, hq * HEAD_DIM:(hq + 1) * HEAD_DIM] for hq in A_Q_HEAD_ORDER]
    main = jnp.concatenate(aq + [w_in[:, A_QW:OFF_G]], axis=-1)
    cb = _pair_columns(w_in[:, OFF_G:OFF_G + 2 * C_HEADS])
    ca = _pair_columns(w_in[:, OFF_G + 2 * C_HEADS:])
    gates = jnp.concatenate([cb, ca, jnp.zeros((d, C_PAIRS, LANES - 8), w_in.dtype)], axis=-1)
    return jnp.concatenate([main, gates.reshape(d, GATE_W)], axis=-1).astype(BF16)


def _pair_row(p):
    v = _pair_columns(p.reshape(2 * C_HEADS).astype(F32))
    z4 = jnp.zeros((C_PAIRS, 4), F32)
    return jnp.concatenate([z4, v, jnp.zeros((C_PAIRS, LANES - 8), F32)], axis=-1)[:, None, :]


def kernel(x, rel_bias, ffn1_norm, ffn1_w_gate, ffn1_w_up, ffn1_w_down, mix_norm, w_in, a_q_norm, a_k_norm, b_q_norm, b_k_norm, c_conv, c_A_log, c_dt_bias, c_out_norm, w_out, ffn2_norm, ffn2_w_gate, ffn2_w_up, ffn2_w_down):
    batch, seq, d = x.shape
    depth = w_in.shape[0]
    scale = HEAD_DIM ** -0.5
    h = x.reshape(batch * seq, d)
    cos, sin = _rope_tables(seq)
    idx = np.arange(B_W)
    mblk = jnp.asarray((idx[:, None] // HEAD_DIM == idx[None, :] // HEAD_DIM) / HEAD_DIM, BF16)
    biases = [_branch_bias(rel_bias, dil, seq // dil) for _, dil in DIL_PAIRS]
    row = lambda v: v.reshape(1, -1).astype(F32)
    for l in range(depth):
        h = _ffn(h, row(ffn1_norm[l]), ffn1_w_gate[l].astype(BF16), ffn1_w_up[l].astype(BF16),
                 ffn1_w_down[l].astype(BF16))
        wvt = w_in[l][:, A_QW + A_KVW:OFF_B].T.astype(BF16)
        aq, ak, avt, bq, bk, bv, cx, cz, gt = _proj(
            h, row(mix_norm[l]), _pad_w_in(w_in[l]), wvt, cos, sin,
            row(jnp.tile(a_q_norm[l], A_HEADS)) * scale, row(jnp.tile(a_k_norm[l], A_KV_HEADS)),
            row(jnp.tile(b_q_norm[l], B_HEADS)) * scale, row(jnp.tile(b_k_norm[l], B_HEADS)),
            mblk, seq)
        out_a = _attn_a(aq, ak, avt, seq)
        out_b = _attn_b(bq, bk, bv, biases, batch, seq)
        out_c = _delta(cx, cz, gt, c_conv[l].astype(F32), _pair_row(c_A_log[l]), _pair_row(c_dt_bias[l]),
                       row(jnp.tile(c_out_norm[l], 2)), batch, seq)
        wo_a = [w_out[l][hq * HEAD_DIM:(hq + 1) * HEAD_DIM] for hq in A_Q_HEAD_ORDER]
        wo = jnp.concatenate(wo_a + [w_out[l][A_QW:]], axis=0).astype(BF16)
        h = _mix_out(h, out_a, out_b, out_c, wo)
        h = _ffn(h, row(ffn2_norm[l]), ffn2_w_gate[l].astype(BF16), ffn2_w_up[l].astype(BF16),
                 ffn2_w_down[l].astype(BF16))
    return h.reshape(batch, seq, d)
```

```python
import functools
import math

import numpy as np
import jax
import jax.numpy as jnp
from jax import lax
from jax.experimental import pallas as pl
from jax.experimental.pallas import tpu as pltpu

F32 = jnp.float32
BF16 = jnp.bfloat16

D_MODEL = 1024
HEAD_DIM = 64
A_HEADS = 4
A_KV_HEADS = 2
B_HEADS = 6
C_HEADS = 6
D_FF = 2816
GRID_W = 64
ROPE_THETA = 10000.0
DIL_PAIRS = ((128, 1), (512, 4), (2048, 16))
REL_BUCKETS = 32
REL_MAX_DIST = 1024
CONV_K = 5
CHUNK = 64
NORM_EPS = 1e-6
NEG_INF = -1e30

A_QW = A_HEADS * HEAD_DIM
A_KVW = A_KV_HEADS * HEAD_DIM
B_W = B_HEADS * HEAD_DIM
C_W = C_HEADS * HEAD_DIM
C_PAIRS = C_HEADS // 2
LANES = 128
MXU_W = 256
GATE_W = C_PAIRS * LANES

OFF_A = 0
OFF_B = OFF_A + A_QW + 2 * A_KVW
OFF_C = OFF_B + 3 * B_W
OFF_Z = OFF_C + 3 * C_W
OFF_G = OFF_Z + C_W
N_PROJ = OFF_G + GATE_W

TOKEN_TILE = 512
A_Q_TILE = 256
B_Q_TILE = 128
B_K_TILE = 256
B_SIDE = 64
CONV_TILE = 256
CHUNK_GROUP = 8
VMEM_LIMIT = 56 * 1024 * 1024


def _cparams(*sem):
    return pltpu.CompilerParams(dimension_semantics=sem, vmem_limit_bytes=VMEM_LIMIT)


def _resident(shape):
    return pl.BlockSpec(shape, lambda *_: (0,) * len(shape), pipeline_mode=pl.Buffered(1))


def _dot(a, b):
    return jnp.dot(a, b, preferred_element_type=F32)


def _dot_nt(a, b):
    return lax.dot_general(a, b, (((1,), (1,)), ((), ())), preferred_element_type=F32)


def _dot_tn(a, b):
    return lax.dot_general(a, b, (((0,), (0,)), ((), ())), preferred_element_type=F32)


def _silu(x):
    return x / (1.0 + jnp.exp(-x))


def _split_dot(x, m):
    hi = x.astype(BF16)
    lo = (x - hi.astype(F32)).astype(BF16)
    return _dot(hi, m) + _dot(lo, m)


def _rms_rows(x, g):
    ms = jnp.mean(x * x, axis=-1, keepdims=True)
    return x * lax.rsqrt(ms + NORM_EPS) * g


def _ffn_kernel(h_ref, g_ref, wg_ref, wu_ref, wd_ref, o_ref):
    x = h_ref[...]
    xn = _rms_rows(x, g_ref[...]).astype(BF16)
    gate = _dot(xn, wg_ref[...])
    up = _dot(xn, wu_ref[...])
    act = (_silu(gate) * up).astype(BF16)
    o_ref[...] = x + 0.5 * _dot(act, wd_ref[...])


def _ffn(h, g, wg, wu, wd):
    t, d = h.shape
    f = wg.shape[1]
    tm = min(TOKEN_TILE, t)
    return pl.pallas_call(
        _ffn_kernel,
        out_shape=jax.ShapeDtypeStruct((t, d), F32),
        grid=(t // tm,),
        in_specs=[pl.BlockSpec((tm, d), lambda i: (i, 0)),
                  _resident((1, d)), _resident((d, f)), _resident((d, f)), _resident((f, d))],
        out_specs=pl.BlockSpec((tm, d), lambda i: (i, 0)),
        compiler_params=_cparams("parallel"),
        name="ffn",
    )(h, g, wg, wu, wd)


def _rope(x, cos, sin):
    w = x.shape[1]
    lane = lax.broadcasted_iota(jnp.int32, x.shape, 1)
    first = (lane & (HEAD_DIM - 1)) < HEAD_DIM // 2
    half = HEAD_DIM // 2
    partner = jnp.where(first, pltpu.roll(x, w - half, 1), pltpu.roll(x, half, 1))
    return x * cos + partner * sin


def _proj_kernel(h_ref, g_ref, w_ref, wvt_ref, cos_ref, sin_ref, gaq_ref, gak_ref, gbq_ref, gbk_ref,
                 mblk_ref, aq_o, ak_o, avt_o, bq_o, bk_o, bv_o, cx_o, cz_o, gt_o):
    xn = _rms_rows(h_ref[...], g_ref[...]).astype(BF16)
    pr = _dot(xn, w_ref[...])
    avt_o[...] = _dot_nt(wvt_ref[...], xn).astype(BF16)

    def unit_rms(lo, width):
        outs = []
        for c in range(lo, lo + width, MXU_W):
            w = min(MXU_W, lo + width - c)
            x = pr[:, c:c + w]
            outs.append(x * lax.rsqrt(_split_dot(x * x, mblk_ref[:w, :w]) + NORM_EPS))
        return outs[0] if len(outs) == 1 else jnp.concatenate(outs, axis=-1)

    cos = cos_ref[...]
    sin = sin_ref[...]
    aq_o[...] = _rope(unit_rms(OFF_A, A_QW) * gaq_ref[...], cos, sin).astype(BF16)
    ak = unit_rms(OFF_A + A_QW, A_KVW) * gak_ref[...]
    ak_o[...] = _rope(ak, cos[:, :A_KVW], sin[:, :A_KVW]).astype(BF16)
    bqk = unit_rms(OFF_B, 2 * B_W)
    bq_o[...] = bqk[:, :B_W] * gbq_ref[...]
    bk_o[...] = bqk[:, B_W:] * gbk_ref[...]
    bv_o[...] = pr[:, OFF_B + 2 * B_W:OFF_C]
    cx_o[...] = pr[:, OFF_C:OFF_Z]
    cz_o[...] = pr[:, OFF_Z:OFF_G].astype(BF16)
    gt_o[...] = pr[:, OFF_G:]


def _proj(h, g, w, wvt, cos, sin, gaq, gak, gbq, gbk, mblk, seq):
    t, d = h.shape
    tm = min(TOKEN_TILE, seq)
    per_seq = seq // tm
    row = lambda i: (i, 0)
    pos = lambda i: (i % per_seq, 0)
    widths = (A_QW, A_KVW, None, B_W, B_W, B_W, 3 * C_W, C_W, GATE_W)
    dtypes = (BF16, BF16, BF16, F32, F32, F32, F32, BF16, F32)
    shapes = [(t, wd) if wd else (A_KVW, t) for wd in widths]
    specs = [pl.BlockSpec((tm, wd), row) if wd else pl.BlockSpec((A_KVW, tm), lambda i: (0, i))
             for wd in widths]
    return pl.pallas_call(
        _proj_kernel,
        out_shape=[jax.ShapeDtypeStruct(s, dt) for s, dt in zip(shapes, dtypes)],
        grid=(t // tm,),
        in_specs=[pl.BlockSpec((tm, d), row),
                  _resident((1, d)), _resident((d, N_PROJ)), _resident((A_KVW, d)),
                  pl.BlockSpec((tm, A_QW), pos),
                  pl.BlockSpec((tm, A_QW), pos),
                  _resident((1, A_QW)), _resident((1, A_KVW)), _resident((1, B_W)),
                  _resident((1, B_W)), _resident((MXU_W, MXU_W))],
        out_specs=specs,
        compiler_params=_cparams("parallel"),
        name="mix_in",
    )(h, g, w, wvt, cos, sin, gaq, gak, gbq, gbk, mblk)


A_Q_HEAD_ORDER = tuple(hk * (A_HEADS // A_KV_HEADS) + g
                       for g in range(A_HEADS // A_KV_HEADS) for hk in range(A_KV_HEADS))


def _attn_a_kernel(q_ref, k_ref, vt_ref, o_ref):
    k2 = k_ref[...]
    lane_hi = lax.broadcasted_iota(jnp.int32, (1, LANES), 1) >= HEAD_DIM
    row_hi = lax.broadcasted_iota(jnp.int32, (LANES, 1), 0) >= HEAD_DIM
    vt = vt_ref[...].astype(F32)
    vt_ones = [jnp.where(row_hi == (hk == 1), vt, 1.0).astype(BF16) for hk in range(A_KV_HEADS)]
    heads = [(g, hk) for g in range(A_HEADS // A_KV_HEADS) for hk in range(A_KV_HEADS)]

    def scores(g, hk):
        qb = q_ref[:, g * LANES:(g + 1) * LANES].astype(F32)
        qm = jnp.where(lane_hi == (hk == 1), qb, 0.0).astype(BF16)
        return _dot_nt(k2, qm)

    st_next = scores(*heads[0])
    halves = []
    for i, (g, hk) in enumerate(heads):
        st = st_next
        if i + 1 < len(heads):
            st_next = scores(*heads[i + 1])
        m = jnp.max(st, axis=0, keepdims=True)
        p = jnp.exp(st - m).astype(BF16)
        ov = _dot(vt_ones[hk], p)
        den_row = (1 - hk) * HEAD_DIM
        halves.append(ov / ov[den_row:den_row + 1, :])
        if hk == A_KV_HEADS - 1:
            ot = jnp.where(row_hi, halves[1], halves[0])
            o_ref[:, g * LANES:(g + 1) * LANES] = ot.T.astype(BF16)
            halves = []


def _attn_a(q, k, vt, seq):
    t = q.shape[0]
    tq = min(A_Q_TILE, seq)
    per_seq = seq // tq
    return pl.pallas_call(
        _attn_a_kernel,
        out_shape=jax.ShapeDtypeStruct((t, A_QW), BF16),
        grid=(t // seq, per_seq),
        in_specs=[pl.BlockSpec((tq, A_QW), lambda b, i: (b * per_seq + i, 0)),
                  pl.BlockSpec((seq, A_KVW), lambda b, i: (b, 0)),
                  pl.BlockSpec((A_KVW, seq), lambda b, i: (0, b))],
        out_specs=pl.BlockSpec((tq, A_QW), lambda b, i: (b * per_seq + i, 0)),
        compiler_params=_cparams("parallel", "parallel"),
        name="attn_a",
    )(q, k, vt)


def _t5_bucket_np(rel):
    half = REL_BUCKETS // 2
    exact = half // 2
    sign = np.where(rel > 0, half, 0)
    n = np.abs(rel)
    nf = np.maximum(n, 1).astype(np.float32)
    large = exact + (np.log(nf / np.float32(exact)) / np.float32(math.log(REL_MAX_DIST / exact))
                     * np.float32(half - exact)).astype(np.int32)
    large = np.minimum(large, half - 1)
    return sign + np.where(n < exact, n, large)


def _branch_tiles(sub_len):
    if sub_len >= B_K_TILE:
        return B_Q_TILE, B_K_TILE, (0, -B_SIDE, B_Q_TILE - B_K_TILE)
    return sub_len, sub_len, (0,)


def _branch_bias(rel_bias, dil, sub_len):
    tq, tk, offs = _branch_tiles(sub_len)
    period = tq + tk
    slot = np.arange(period)
    tabs = []
    for off in offs:
        delta = np.where(slot < tk, slot, slot - period) + off
        bucket = _t5_bucket_np(delta * dil)
        vec = jnp.where((np.abs(delta) <= B_SIDE)[:, None], rel_bias.astype(F32)[bucket], NEG_INF).T
        flat = jnp.tile(vec, (1, tq))[:, :tq * (period - 1)]
        tabs.append(flat.reshape(B_HEADS, tq, period - 1)[:, :, :tk])
    return jnp.stack(tabs)


def _attn_b_kernel(q_ref, k_ref, v_ref, b0_ref, b1_ref, b2_ref, o_ref, num_s, mx_s, den_s, *, seq):
    head0 = lax.broadcasted_iota(jnp.int32, (1, LANES), 1) < HEAD_DIM
    pick = lambda a, b: jnp.where(head0, a, b)

    def tile_attn(qt, kt, vt, bias_ref, tab):
        k16, v16 = kt.astype(BF16), vt.astype(BF16)
        parts = []
        for h in range(2):
            qh = (pick(qt, 0.0) if h == 0 else pick(0.0, qt)).astype(BF16)
            s = _dot_nt(qh, k16) + bias_ref[tab, h]
            m = jnp.max(s, axis=-1, keepdims=True)
            p = jnp.exp(s - m)
            l = jnp.sum(p, axis=-1, keepdims=True)
            parts.append((_dot(p.astype(BF16), v16), m, l))
        return tuple(pick(a, b) for a, b in zip(*parts))

    for g, ((_, dil), bias_ref) in enumerate(zip(DIL_PAIRS, (b0_ref, b1_ref, b2_ref))):
        sub_len = seq // dil
        tq, tk, _ = _branch_tiles(sub_len)
        n_tiles = sub_len // tq

        for r in range(dil):
            for t in range(n_tiles):
                if n_tiles == 1:
                    k0, tab = 0, 0
                else:
                    k0 = min(max(t * tq - B_SIDE, 0), sub_len - tk)
                    tab = 0 if t == 0 else 2 if t == n_tiles - 1 else 1
                rows = pl.ds(r + dil * t * tq, tq, stride=dil)
                keys = pl.ds(r + dil * k0, tk, stride=dil)
                num, mx, den = tile_attn(q_ref[rows, :], k_ref[keys, :], v_ref[keys, :], bias_ref, tab)
                num_s.at[g][rows, :] = num
                mx_s.at[g][rows, :] = mx
                den_s.at[g][rows, :] = den

    def merge(t, carry):
        rows = pl.ds(pl.multiple_of(t * CONV_TILE, CONV_TILE), CONV_TILE)
        ms = [mx_s[g, rows, :] for g in range(3)]
        top = jnp.maximum(jnp.maximum(ms[0], ms[1]), ms[2])
        es = [jnp.exp(m - top) for m in ms]
        num = sum(e * num_s[g, rows, :] for g, e in enumerate(es))
        den = sum(e * den_s[g, rows, :] for g, e in enumerate(es))
        o_ref[rows, :] = (num / den).astype(BF16)
        return carry

    lax.fori_loop(0, seq // CONV_TILE, merge, 0)


def _attn_b(q, k, v, biases, batch, seq):
    t = q.shape[0]
    pair = pl.BlockSpec((seq, LANES), lambda b, p: (b, p))
    bias_spec = lambda x: pl.BlockSpec((x.shape[0], 2) + x.shape[2:], lambda b, p: (0, p, 0, 0))
    return pl.pallas_call(
        functools.partial(_attn_b_kernel, seq=seq),
        out_shape=jax.ShapeDtypeStruct((t, B_W), BF16),
        grid=(batch, B_HEADS // 2),
        in_specs=[pair, pair, pair] + [bias_spec(x) for x in biases],
        out_specs=pair,
        scratch_shapes=[pltpu.VMEM((len(DIL_PAIRS), seq, LANES), F32)] * 3,
        compiler_params=_cparams("parallel", "parallel"),
        name="attn_b",
    )(q, k, v, *biases)


BASE_BLOCK = 8


def _pair_block_diag(x, same_head):
    return jnp.where(same_head, jnp.concatenate([x, x], axis=0), 0.0).astype(BF16)


def _inv_unit_triangular_pairs(ms, same_block, same_head, eye):
    bd = lambda x: _pair_block_diag(x, same_head)
    b16 = lambda x: x.astype(BF16)
    m8 = [b16(jnp.where(same_block[BASE_BLOCK], m, 0.0)) for m in ms]
    p2 = [_dot(a, bd(a.astype(F32))) for a in m8]
    p2h = [b16(p) for p in p2]
    p4 = [_dot(a, bd(p)) for a, p in zip(p2h, p2)]
    p6 = [_dot(a, bd(p)) for a, p in zip(p2h, p4)]
    even = [eye + a + b + c for a, b, c in zip(p2, p4, p6)]
    inv = [e - _dot(a, bd(e)) for a, e in zip(m8, even)]
    size = BASE_BLOCK
    while size < CHUNK:
        joins = same_block[2 * size] & jnp.logical_not(same_block[size])
        t = [_dot(b16(jnp.where(joins, m, 0.0)), bd(x)) for m, x in zip(ms, inv)]
        inv = [x - _dot(b16(x), bd(y)) for x, y in zip(inv, t)]
        size *= 2
    return inv


def _delta_kernel(xq_ref, xk_ref, xv_ref, wq_ref, wk_ref, wv_ref, z_ref, gt_ref,
                  alog_ref, dtb_ref, gain_ref, o_ref,
                  pad_s, q_s, k_s, v_s, beta_s, g_s, of_s, ob_s, st_s,
                  u_s, lq_s, a_s, kw_s, ku_s, sn_s, eg_s, *, seq):
    n_chunks = seq // CHUNK
    pad = 8
    sq_r = lax.broadcasted_iota(jnp.int32, (LANES, LANES), 0)
    sq_c = lax.broadcasted_iota(jnp.int32, (LANES, LANES), 1)
    same_head = (sq_r < HEAD_DIM) == (sq_c < HEAD_DIM)
    head_ones = same_head.astype(BF16)

    zeros = jnp.zeros((pad, LANES), F32)
    for j, (x_ref, w_ref, dst, scale) in enumerate(((xq_ref, wq_ref, q_s, HEAD_DIM ** -0.5),
                                                     (xk_ref, wk_ref, k_s, 1.0),
                                                     (xv_ref, wv_ref, v_s, None))):
        pad_s[j, 0:pad, :] = zeros
        pad_s[j, pad + seq:pad + seq + pad, :] = zeros
        pad_s[j, pad:pad + seq, :] = x_ref[...]
        for r0 in range(0, seq, CONV_TILE):
            acc = jnp.zeros((CONV_TILE, LANES), F32)
            for d in range(CONV_K):
                lo = r0 + pad - CONV_K // 2 + d
                acc = acc + w_ref[d:d + 1, :] * pad_s[j, lo:lo + CONV_TILE, :]
            y = _silu(acc)
            if scale is not None:
                y = y * lax.rsqrt(_split_dot(y * y, head_ones) + NORM_EPS) * scale
            dst[r0:r0 + CONV_TILE, :] = y

    gt = gt_ref[...]
    beta_s[...] = 1.0 / (1.0 + jnp.exp(-gt))
    gx = gt + dtb_ref[0]
    softplus = jnp.maximum(gx, 0.0) + jnp.log(1.0 + jnp.exp(-jnp.abs(gx)))
    g_s[...] = -jnp.exp(alog_ref[0]) * softplus
    st_s[...] = jnp.zeros(st_s.shape, F32)

    rows = lax.broadcasted_iota(jnp.int32, (CHUNK, LANES), 0)
    cols = lax.broadcasted_iota(jnp.int32, (CHUNK, LANES), 1) & (HEAD_DIM - 1)
    head0_w = lax.broadcasted_iota(jnp.int32, (CHUNK, LANES), 1) < HEAD_DIM
    eye_b = rows == cols
    eye = eye_b.astype(F32)
    same_block = {}
    size = BASE_BLOCK
    while size <= CHUNK:
        shift = size.bit_length() - 1
        same_block[size] = (rows >> shift) == (cols >> shift)
        size *= 2
    r64 = lax.broadcasted_iota(jnp.int32, (CHUNK, CHUNK), 0)
    c64 = lax.broadcasted_iota(jnp.int32, (CHUNK, CHUNK), 1)
    tri = ((r64 >= c64).astype(BF16), (r64 <= c64).astype(BF16))
    incl = (rows >= cols, rows <= cols)
    strict = (rows > cols, rows < cols)
    total_row = (CHUNK - 1, 0)
    bd = lambda x: _pair_block_diag(x, same_head)

    def pair_bcast(x, c0):
        return jnp.where(head0_w, jnp.broadcast_to(x[:, c0:c0 + 1], (CHUNK, LANES)),
                         jnp.broadcast_to(x[:, c0 + 1:c0 + 2], (CHUNK, LANES)))

    def chunk_group(it, carry):
        sls = [pl.ds(pl.multiple_of((it * CHUNK_GROUP + g) * CHUNK, CHUNK), CHUNK)
               for g in range(CHUNK_GROUP)]
        chains = [(g, d) for g in range(CHUNK_GROUP) for d in range(2)]
        k2 = [k_s[sl, :] for sl in sls]
        q2 = [q_s[sl, :] for sl in sls]
        kbd = [bd(k) for k in k2]
        gcum = {(g, d): _split_dot_left(tri[d], g_s[sls[g], :]) for g, d in chains}
        kk = [_dot_nt(k.astype(BF16), b) for k, b in zip(k2, kbd)]
        qk = [_dot_nt(q.astype(BF16), b) for q, b in zip(q2, kbd)]
        beta, gc, egc, kd, ms = {}, {}, {}, {}, []
        for g, d in chains:
            beta[g, d] = pair_bcast(beta_s[sls[g], :], 2 * d)
            gc[g, d] = pair_bcast(gcum[g, d], 4 + 2 * d)
            gc_row = jnp.sum(jnp.where(eye_b, gc[g, d], 0.0), axis=0, keepdims=True)
            decay = jnp.where(incl[d], jnp.exp(gc[g, d] - gc_row), 0.0)
            ms.append(jnp.where(strict[d], beta[g, d] * kk[g] * decay, 0.0))
            a_s[d, sls[g], :] = (qk[g] * decay).astype(BF16)
            egc[g, d] = jnp.exp(gc[g, d])
            g_last = gc[g, d][total_row[d]:total_row[d] + 1, :]
            kd[g, d] = (k2[g] * jnp.exp(g_last - gc[g, d])).astype(BF16)
            eg_s[d, pl.ds(pl.multiple_of((it * CHUNK_GROUP + g) * 8, 8), 8), :] = (
                jnp.broadcast_to(jnp.exp(g_last), (8, LANES)))
        inv = [x.astype(BF16) for x in _inv_unit_triangular_pairs(ms, same_block, same_head, eye)]
        us = [_dot(x, bd(v_s[sls[g], :] * beta[g, d])) for x, (g, d) in zip(inv, chains)]
        ws = [_dot(x, bd(k2[g] * (beta[g, d] * egc[g, d]))) for x, (g, d) in zip(inv, chains)]
        wu = [_dot_tn(kd[c], jnp.concatenate([w, u], axis=-1).astype(BF16)) for w, u, c in zip(ws, us, chains)]
        for u, w, x, (g, d) in zip(us, ws, wu, chains):
            n = it * CHUNK_GROUP + g
            u_s[d, sls[g], :] = u
            lq_s[d, n] = jnp.concatenate([w, q2[g] * egc[g, d]], axis=0).astype(BF16)
            kw_s[d, n] = jnp.where(same_head, x[:, :LANES], 0.0).astype(BF16)
            ku_s[d, n] = jnp.where(same_head, x[:, LANES:], 0.0)
        return carry

    lax.fori_loop(0, n_chunks // CHUNK_GROUP, chunk_group, 0)

    def scan_step(i, carry):
        ns = (i, n_chunks - 1 - i)
        state = [st_s[d] for d in range(2)]
        s16 = [s.astype(BF16) for s in state]
        drop = [_dot(kw_s[d, ns[d]], s16[d]) for d in range(2)]
        for d in range(2):
            eg = eg_s[d, pl.ds(pl.multiple_of(ns[d] * 8, 8), 8), :][0:1]
            sn_s[d, ns[d]] = s16[d]
            st_s[d] = state[d] * eg - drop[d] + ku_s[d, ns[d]]
        return carry

    lax.fori_loop(0, n_chunks, scan_step, 0)

    def output_group(it, carry):
        todo = [(it * CHUNK_GROUP + g, d) for g in range(CHUNK_GROUP) for d in range(2)]
        sls = [pl.ds(pl.multiple_of(n * CHUNK, CHUNK), CHUNK) for n, _ in todo]
        rs = [_dot(lq_s[d, n], sn_s[d, n]) for n, d in todo]
        v_new = [u_s[d, sl, :] - r[:CHUNK] for r, sl, (_, d) in zip(rs, sls, todo)]
        outs = [r[CHUNK:] + _dot(a_s[d, sl, :], bd(v)) for r, v, sl, (_, d) in zip(rs, v_new, sls, todo)]
        for o, sl, (_, d) in zip(outs, sls, todo):
            (of_s, ob_s)[d][sl, :] = o
        return carry

    lax.fori_loop(0, n_chunks // CHUNK_GROUP, output_group, 0)

    o = of_s[...] + ob_s[...]
    ms = _split_dot(o * o, head_ones) * (1.0 / HEAD_DIM)
    o = o * lax.rsqrt(ms + NORM_EPS) * gain_ref[...]
    o_ref[...] = (o * _silu(z_ref[...].astype(F32))).astype(BF16)


def _split_dot_left(m, x):
    hi = x.astype(BF16)
    lo = (x - hi.astype(F32)).astype(BF16)
    return _dot(m, hi) + _dot(m, lo)


def _delta(cx, cz, gt, conv_w, alog, dtb, gain, batch, seq):
    t = cx.shape[0]
    xspec = lambda part: pl.BlockSpec((seq, LANES), lambda b, p: (b, part * C_PAIRS + p))
    wspec = lambda part: pl.BlockSpec((CONV_K, LANES), lambda b, p: (0, part * C_PAIRS + p))
    pair = pl.BlockSpec((seq, LANES), lambda b, p: (b, p))
    prow = pl.BlockSpec((1, 1, LANES), lambda b, p: (p, 0, 0))
    return pl.pallas_call(
        functools.partial(_delta_kernel, seq=seq),
        out_shape=jax.ShapeDtypeStruct((t, C_W), BF16),
        grid=(batch, C_PAIRS),
        in_specs=[xspec(0), xspec(1), xspec(2), wspec(0), wspec(1), wspec(2),
                  pair, pair, prow, prow,
                  _resident((1, LANES))],
        out_specs=pair,
        scratch_shapes=[pltpu.VMEM((3, seq + 16, LANES), F32),
                        pltpu.VMEM((seq, LANES), F32),
                        pltpu.VMEM((seq, LANES), F32),
                        pltpu.VMEM((seq, LANES), F32),
                        pltpu.VMEM((seq, LANES), F32),
                        pltpu.VMEM((seq, LANES), F32),
                        pltpu.VMEM((seq, LANES), F32),
                        pltpu.VMEM((seq, LANES), F32),
                        pltpu.VMEM((2, LANES, LANES), F32),
                        pltpu.VMEM((2, seq, LANES), F32),
                        pltpu.VMEM((2, seq // CHUNK, LANES, LANES), BF16),
                        pltpu.VMEM((2, seq, LANES), BF16),
                        pltpu.VMEM((2, seq // CHUNK, LANES, LANES), BF16),
                        pltpu.VMEM((2, seq // CHUNK, LANES, LANES), F32),
                        pltpu.VMEM((2, seq // CHUNK, LANES, LANES), BF16),
                        pltpu.VMEM((2, seq // CHUNK * 8, LANES), F32)],
        compiler_params=_cparams("parallel", "parallel"),
        name="delta",
    )(cx, cx, cx, conv_w, conv_w, conv_w, cz, gt, alog, dtb, gain)


def _out_kernel(h_ref, a_ref, b_ref, c_ref, w_ref, o_ref):
    y = _dot(a_ref[...], w_ref[0:A_QW, :])
    y = y + _dot(b_ref[...], w_ref[A_QW:A_QW + B_W, :])
    y = y + _dot(c_ref[...], w_ref[A_QW + B_W:, :])
    o_ref[...] = h_ref[...] + y


def _mix_out(h, a, b, c, w):
    t, d = h.shape
    tm = min(TOKEN_TILE, t)
    row = lambda width: pl.BlockSpec((tm, width), lambda i: (i, 0))
    return pl.pallas_call(
        _out_kernel,
        out_shape=jax.ShapeDtypeStruct((t, d), F32),
        grid=(t // tm,),
        in_specs=[row(d), row(A_QW), row(B_W), row(C_W), _resident(w.shape)],
        out_specs=row(d),
        compiler_params=_cparams("parallel"),
        name="mix_out",
    )(h, a, b, c, w)


def _rope_tables(seq):
    rows = seq // GRID_W
    row = jnp.repeat(jnp.arange(rows), GRID_W).astype(F32)
    col = jnp.tile(jnp.arange(GRID_W), rows).astype(F32)
    n_freq = HEAD_DIM // 4
    inv = ROPE_THETA ** (-jnp.arange(n_freq, dtype=F32) / n_freq)
    ang = jnp.concatenate([row[:, None] * inv, col[:, None] * inv], axis=-1)
    cos, sin = jnp.cos(ang), jnp.sin(ang)
    cos_h = jnp.concatenate([cos, cos], axis=-1)
    sin_h = jnp.concatenate([-sin, sin], axis=-1)
    return jnp.tile(cos_h, (1, A_HEADS)), jnp.tile(sin_h, (1, A_HEADS))


def _pair_columns(x):
    lead = x.shape[:-1]
    x = x.reshape(lead + (2, C_PAIRS, 2))
    return jnp.moveaxis(x, -2, -3).reshape(lead + (C_PAIRS, 4))


def _pad_w_in(w_in):
    d = w_in.shape[0]
    aq = [w_in[:, hq * HEAD_DIM:(hq + 1) * HEAD_DIM] for hq in A_Q_HEAD_ORDER]
    main = jnp.concatenate(aq + [w_in[:, A_QW:OFF_G]], axis=-1)
    cb = _pair_columns(w_in[:, OFF_G:OFF_G + 2 * C_HEADS])
    ca = _pair_columns(w_in[:, OFF_G + 2 * C_HEADS:])
    gates = jnp.concatenate([cb, ca, jnp.zeros((d, C_PAIRS, LANES - 8), w_in.dtype)], axis=-1)
    return jnp.concatenate([main, gates.reshape(d, GATE_W)], axis=-1).astype(BF16)


def _pair_row(p):
    v = _pair_columns(p.reshape(2 * C_HEADS).astype(F32))
    z4 = jnp.zeros((C_PAIRS, 4), F32)
    return jnp.concatenate([z4, v, jnp.zeros((C_PAIRS, LANES - 8), F32)], axis=-1)[:, None, :]


def kernel(x, rel_bias, ffn1_norm, ffn1_w_gate, ffn1_w_up, ffn1_w_down, mix_norm, w_in, a_q_norm, a_k_norm, b_q_norm, b_k_norm, c_conv, c_A_log, c_dt_bias, c_out_norm, w_out, ffn2_norm, ffn2_w_gate, ffn2_w_up, ffn2_w_down):
    batch, seq, d = x.shape
    depth = w_in.shape[0]
    scale = HEAD_DIM ** -0.5
    h = x.reshape(batch * seq, d)
    cos, sin = _rope_tables(seq)
    idx = np.arange(MXU_W)
    mblk = jnp.asarray((idx[:, None] // HEAD_DIM == idx[None, :] // HEAD_DIM) / HEAD_DIM, BF16)
    biases = [_branch_bias(rel_bias, dil, seq // dil) for _, dil in DIL_PAIRS]
    row = lambda v: v.reshape(1, -1).astype(F32)
    for l in range(depth):
        h = _ffn(h, row(ffn1_norm[l]), ffn1_w_gate[l].astype(BF16), ffn1_w_up[l].astype(BF16),
                 ffn1_w_down[l].astype(BF16))
        wvt = w_in[l][:, A_QW + A_KVW:OFF_B].T.astype(BF16)
        aq, ak, avt, bq, bk, bv, cx, cz, gt = _proj(
            h, row(mix_norm[l]), _pad_w_in(w_in[l]), wvt, cos, sin,
            row(jnp.tile(a_q_norm[l], A_HEADS)) * scale, row(jnp.tile(a_k_norm[l], A_KV_HEADS)),
            row(jnp.tile(b_q_norm[l], B_HEADS)) * scale, row(jnp.tile(b_k_norm[l], B_HEADS)),
            mblk, seq)
        out_a = _attn_a(aq, ak, avt, seq)
        out_b = _attn_b(bq, bk, bv, biases, batch, seq)
        out_c = _delta(cx, cz, gt, c_conv[l].astype(F32), _pair_row(c_A_log[l]), _pair_row(c_dt_bias[l]),
                       row(jnp.tile(c_out_norm[l], 2)), batch, seq)
        wo_a = [w_out[l][hq * HEAD_DIM:(hq + 1) * HEAD_DIM] for hq in A_Q_HEAD_ORDER]
        wo = jnp.concatenate(wo_a + [w_out[l][A_QW:]], axis=0).astype(BF16)
        h = _mix_out(h, out_a, out_b, out_c, wo)
        h = _ffn(h, row(ffn2_norm[l]), ffn2_w_gate[l].astype(BF16), ffn2_w_up[l].astype(BF16),
                 ffn2_w_down[l].astype(BF16))
    return h.reshape(batch, seq, d)
```

```python
import functools
import math

import numpy as np
import jax
import jax.numpy as jnp
from jax import lax
from jax.experimental import pallas as pl
from jax.experimental.pallas import tpu as pltpu

F32 = jnp.float32
BF16 = jnp.bfloat16

D_MODEL = 1024
HEAD_DIM = 64
A_HEADS = 4
A_KV_HEADS = 2
B_HEADS = 6
C_HEADS = 6
D_FF = 2816
GRID_W = 64
ROPE_THETA = 10000.0
DIL_PAIRS = ((128, 1), (512, 4), (2048, 16))
REL_BUCKETS = 32
REL_MAX_DIST = 1024
CONV_K = 5
CHUNK = 64
NORM_EPS = 1e-6
NEG_INF = -1e30

A_QW = A_HEADS * HEAD_DIM
A_KVW = A_KV_HEADS * HEAD_DIM
B_W = B_HEADS * HEAD_DIM
C_W = C_HEADS * HEAD_DIM
C_PAIRS = C_HEADS // 2
LANES = 128
MXU_W = 256
GATE_W = C_PAIRS * LANES

OFF_A = 0
OFF_B = OFF_A + A_QW + 2 * A_KVW
OFF_C = OFF_B + 3 * B_W
OFF_Z = OFF_C + 3 * C_W
OFF_G = OFF_Z + C_W
N_PROJ = OFF_G + GATE_W

TOKEN_TILE = 512
A_Q_TILE = 256
B_Q_TILE = 128
B_K_TILE = 256
B_LOOKAHEAD = 2
B_SIDE = 64
CONV_TILE = 256
CHUNK_GROUP = 8
VMEM_LIMIT = 56 * 1024 * 1024


def _cparams(*sem):
    return pltpu.CompilerParams(dimension_semantics=sem, vmem_limit_bytes=VMEM_LIMIT)


def _resident(shape):
    return pl.BlockSpec(shape, lambda *_: (0,) * len(shape), pipeline_mode=pl.Buffered(1))


def _dot(a, b):
    return jnp.dot(a, b, preferred_element_type=F32)


def _dot_nt(a, b):
    return lax.dot_general(a, b, (((1,), (1,)), ((), ())), preferred_element_type=F32)


def _dot_tn(a, b):
    return lax.dot_general(a, b, (((0,), (0,)), ((), ())), preferred_element_type=F32)


def _silu(x):
    return x / (1.0 + jnp.exp(-x))


def _split_dot(x, m):
    hi = x.astype(BF16)
    lo = (x - hi.astype(F32)).astype(BF16)
    return _dot(hi, m) + _dot(lo, m)


def _rms_rows(x, g):
    ms = jnp.mean(x * x, axis=-1, keepdims=True)
    return x * lax.rsqrt(ms + NORM_EPS) * g


def _layer_resident(stacked, layer):
    shape = stacked.shape[1:]
    return pl.BlockSpec((None,) + shape, lambda *_: (layer,) + (0,) * len(shape),
                        pipeline_mode=pl.Buffered(1))


def _swiglu_residual(x, g_ref, wg_ref, wu_ref, wd_ref):
    xn = _rms_rows(x, g_ref[...]).astype(BF16)
    gate = _dot(xn, wg_ref[...])
    up = _dot(xn, wu_ref[...])
    act = (_silu(gate) * up).astype(BF16)
    return x + 0.5 * _dot(act, wd_ref[...])


def _ffn_kernel(h_ref, g_ref, wg_ref, wu_ref, wd_ref, o_ref):
    o_ref[...] = _swiglu_residual(h_ref[...], g_ref, wg_ref, wu_ref, wd_ref)


def _ffn(h, layer, g, wg, wu, wd):
    t, d = h.shape
    tm = min(TOKEN_TILE, t)
    return pl.pallas_call(
        _ffn_kernel,
        out_shape=jax.ShapeDtypeStruct((t, d), F32),
        grid=(t // tm,),
        in_specs=[pl.BlockSpec((tm, d), lambda i: (i, 0))]
                 + [_layer_resident(p, layer) for p in (g, wg, wu, wd)],
        out_specs=pl.BlockSpec((tm, d), lambda i: (i, 0)),
        compiler_params=_cparams("parallel"),
        name="ffn",
    )(h, g, wg, wu, wd)


def _rope(x, cos, sin):
    w = x.shape[1]
    lane = lax.broadcasted_iota(jnp.int32, x.shape, 1)
    first = (lane & (HEAD_DIM - 1)) < HEAD_DIM // 2
    half = HEAD_DIM // 2
    partner = jnp.where(first, pltpu.roll(x, w - half, 1), pltpu.roll(x, half, 1))
    return x * cos + partner * sin


def _proj_kernel(h_ref, g_ref, w_ref, wvt_ref, cos_ref, sin_ref, gaq_ref, gak_ref, gbq_ref, gbk_ref,
                 mblk_ref, aq_o, ak_o, avt_o, bq_o, bk_o, bv_o, cx_o, cz_o, gt_o):
    xn = _rms_rows(h_ref[...], g_ref[...]).astype(BF16)
    pr = _dot(xn, w_ref[...])
    avt_o[...] = _dot_nt(wvt_ref[...], xn).astype(BF16)

    def unit_rms(lo, width):
        outs = []
        for c in range(lo, lo + width, MXU_W):
            w = min(MXU_W, lo + width - c)
            x = pr[:, c:c + w]
            outs.append(x * lax.rsqrt(_split_dot(x * x, mblk_ref[:w, :w]) + NORM_EPS))
        return outs[0] if len(outs) == 1 else jnp.concatenate(outs, axis=-1)

    cos = cos_ref[...]
    sin = sin_ref[...]
    aq_o[...] = _rope(unit_rms(OFF_A, A_QW) * gaq_ref[...], cos, sin).astype(BF16)
    ak = unit_rms(OFF_A + A_QW, A_KVW) * gak_ref[...]
    ak_o[...] = _rope(ak, cos[:, :A_KVW], sin[:, :A_KVW]).astype(BF16)
    bqk = unit_rms(OFF_B, 2 * B_W)
    bq_o[...] = bqk[:, :B_W] * gbq_ref[...]
    bk_o[...] = bqk[:, B_W:] * gbk_ref[...]
    bv_o[...] = pr[:, OFF_B + 2 * B_W:OFF_C]
    cx_o[...] = pr[:, OFF_C:OFF_Z]
    cz_o[...] = pr[:, OFF_Z:OFF_G].astype(BF16)
    gt_o[...] = pr[:, OFF_G:]


def _proj(h, layer, g, w, wvt, cos, sin, gaq, gak, gbq, gbk, mblk, seq):
    t, d = h.shape
    tm = min(TOKEN_TILE, seq)
    per_seq = seq // tm
    row = lambda i: (i, 0)
    pos = lambda i: (i % per_seq, 0)
    widths = (A_QW, A_KVW, None, B_W, B_W, B_W, 3 * C_W, C_W, GATE_W)
    dtypes = (BF16, BF16, BF16, F32, F32, F32, F32, BF16, F32)
    shapes = [(t, wd) if wd else (A_KVW, t) for wd in widths]
    specs = [pl.BlockSpec((tm, wd), row) if wd else pl.BlockSpec((A_KVW, tm), lambda i: (0, i))
             for wd in widths]
    return pl.pallas_call(
        _proj_kernel,
        out_shape=[jax.ShapeDtypeStruct(s, dt) for s, dt in zip(shapes, dtypes)],
        grid=(t // tm,),
        in_specs=[pl.BlockSpec((tm, d), row),
                  _layer_resident(g, layer), _layer_resident(w, layer), _layer_resident(wvt, layer),
                  pl.BlockSpec((tm, A_QW), pos),
                  pl.BlockSpec((tm, A_QW), pos),
                  _resident((1, A_QW)), _resident((1, A_KVW)), _resident((1, B_W)),
                  _resident((1, B_W)), _resident((MXU_W, MXU_W))],
        out_specs=specs,
        compiler_params=_cparams("parallel"),
        name="mix_in",
    )(h, g, w, wvt, cos, sin, gaq, gak, gbq, gbk, mblk)


A_Q_HEAD_ORDER = tuple(hk * (A_HEADS // A_KV_HEADS) + g
                       for g in range(A_HEADS // A_KV_HEADS) for hk in range(A_KV_HEADS))


def _attn_a_kernel(q_ref, k_ref, vt_ref, o_ref):
    k2 = k_ref[...]
    lane_hi = lax.broadcasted_iota(jnp.int32, (1, LANES), 1) >= HEAD_DIM
    row_hi = lax.broadcasted_iota(jnp.int32, (LANES, 1), 0) >= HEAD_DIM
    vt = vt_ref[...].astype(F32)
    vt_ones = [jnp.where(row_hi == (hk == 1), vt, 1.0).astype(BF16) for hk in range(A_KV_HEADS)]
    heads = [(g, hk) for g in range(A_HEADS // A_KV_HEADS) for hk in range(A_KV_HEADS)]

    def scores(g, hk):
        qb = q_ref[:, g * LANES:(g + 1) * LANES].astype(F32)
        qm = jnp.where(lane_hi == (hk == 1), qb, 0.0).astype(BF16)
        return _dot_nt(k2, qm)

    st_next = scores(*heads[0])
    halves = []
    for i, (g, hk) in enumerate(heads):
        st = st_next
        if i + 1 < len(heads):
            st_next = scores(*heads[i + 1])
        m = jnp.max(st, axis=0, keepdims=True)
        p = jnp.exp(st - m).astype(BF16)
        ov = _dot(vt_ones[hk], p)
        den_row = (1 - hk) * HEAD_DIM
        halves.append(ov / ov[den_row:den_row + 1, :])
        if hk == A_KV_HEADS - 1:
            ot = jnp.where(row_hi, halves[1], halves[0])
            o_ref[:, g * LANES:(g + 1) * LANES] = ot.T.astype(BF16)
            halves = []


def _attn_a(q, k, vt, seq):
    t = q.shape[0]
    tq = min(A_Q_TILE, seq)
    per_seq = seq // tq
    return pl.pallas_call(
        _attn_a_kernel,
        out_shape=jax.ShapeDtypeStruct((t, A_QW), BF16),
        grid=(t // seq, per_seq),
        in_specs=[pl.BlockSpec((tq, A_QW), lambda b, i: (b * per_seq + i, 0)),
                  pl.BlockSpec((seq, A_KVW), lambda b, i: (b, 0)),
                  pl.BlockSpec((A_KVW, seq), lambda b, i: (0, b))],
        out_specs=pl.BlockSpec((tq, A_QW), lambda b, i: (b * per_seq + i, 0)),
        compiler_params=_cparams("parallel", "parallel"),
        name="attn_a",
    )(q, k, vt)


def _t5_bucket_np(rel):
    half = REL_BUCKETS // 2
    exact = half // 2
    sign = np.where(rel > 0, half, 0)
    n = np.abs(rel)
    nf = np.maximum(n, 1).astype(np.float32)
    large = exact + (np.log(nf / np.float32(exact)) / np.float32(math.log(REL_MAX_DIST / exact))
                     * np.float32(half - exact)).astype(np.int32)
    large = np.minimum(large, half - 1)
    return sign + np.where(n < exact, n, large)


def _branch_tiles(sub_len):
    if sub_len >= B_K_TILE:
        return B_Q_TILE, B_K_TILE, (0, -B_SIDE, B_Q_TILE - B_K_TILE)
    return sub_len, sub_len, (0,)


def _branch_bias(rel_bias, dil, sub_len):
    tq, tk, offs = _branch_tiles(sub_len)
    period = tq + tk
    slot = np.arange(period)
    tabs = []
    for off in offs:
        delta = np.where(slot < tk, slot, slot - period) + off
        bucket = _t5_bucket_np(delta * dil)
        vec = jnp.where((np.abs(delta) <= B_SIDE)[:, None], rel_bias.astype(F32)[bucket], NEG_INF).T
        flat = jnp.tile(vec, (1, tq))[:, :tq * (period - 1)]
        tabs.append(flat.reshape(B_HEADS, tq, period - 1)[:, :, :tk])
    return jnp.stack(tabs)


def _attn_b_kernel(q_ref, k_ref, v_ref, b0_ref, b1_ref, b2_ref, o_ref, num_s, mx_s, den_s, *, seq):
    head0 = lax.broadcasted_iota(jnp.int32, (1, LANES), 1) < HEAD_DIM
    pick = lambda a, b: jnp.where(head0, a, b)

    tiles = []
    for g, ((_, dil), bias_ref) in enumerate(zip(DIL_PAIRS, (b0_ref, b1_ref, b2_ref))):
        sub_len = seq // dil
        tq, tk, _ = _branch_tiles(sub_len)
        n_tiles = sub_len // tq
        for r in range(dil):
            for t in range(n_tiles):
                if n_tiles == 1:
                    k0, tab = 0, 0
                else:
                    k0 = min(max(t * tq - B_SIDE, 0), sub_len - tk)
                    tab = 0 if t == 0 else 2 if t == n_tiles - 1 else 1
                tiles.append((g, bias_ref, tab, pl.ds(r + dil * t * tq, tq, stride=dil),
                              pl.ds(r + dil * k0, tk, stride=dil)))

    def scores(tile):
        _, _, _, rows, keys = tile
        qt = q_ref[rows, :]
        k16 = k_ref[keys, :].astype(BF16)
        return [_dot_nt((pick(qt, 0.0) if h == 0 else pick(0.0, qt)).astype(BF16), k16) for h in range(2)]

    def finish(tile, s2):
        g, bias_ref, tab, rows, keys = tile
        v16 = v_ref[keys, :].astype(BF16)
        parts = []
        for h in range(2):
            s = s2[h] + bias_ref[tab, h]
            m = jnp.max(s, axis=-1, keepdims=True)
            p = jnp.exp(s - m)
            l = jnp.sum(p, axis=-1, keepdims=True)
            parts.append((_dot(p.astype(BF16), v16), m, l))
        num, mx, den = (pick(a, b) for a, b in zip(*parts))
        num_s.at[g][rows, :] = num
        mx_s.at[g][rows, :] = mx
        den_s.at[g][rows, :] = den

    in_flight = []
    for i in range(len(tiles) + B_LOOKAHEAD):
        if i < len(tiles):
            in_flight.append((tiles[i], scores(tiles[i])))
        if i >= B_LOOKAHEAD:
            finish(*in_flight.pop(0))

    def merge(t, carry):
        rows = pl.ds(pl.multiple_of(t * CONV_TILE, CONV_TILE), CONV_TILE)
        ms = [mx_s[g, rows, :] for g in range(3)]
        top = jnp.maximum(jnp.maximum(ms[0], ms[1]), ms[2])
        es = [jnp.exp(m - top) for m in ms]
        num = sum(e * num_s[g, rows, :] for g, e in enumerate(es))
        den = sum(e * den_s[g, rows, :] for g, e in enumerate(es))
        o_ref[rows, :] = (num / den).astype(BF16)
        return carry

    lax.fori_loop(0, seq // CONV_TILE, merge, 0)


def _attn_b(q, k, v, biases, batch, seq):
    t = q.shape[0]
    pair = pl.BlockSpec((seq, LANES), lambda b, p: (b, p))
    bias_spec = lambda x: pl.BlockSpec((x.shape[0], 2) + x.shape[2:], lambda b, p: (0, p, 0, 0))
    return pl.pallas_call(
        functools.partial(_attn_b_kernel, seq=seq),
        out_shape=jax.ShapeDtypeStruct((t, B_W), BF16),
        grid=(batch, B_HEADS // 2),
        in_specs=[pair, pair, pair] + [bias_spec(x) for x in biases],
        out_specs=pair,
        scratch_shapes=[pltpu.VMEM((len(DIL_PAIRS), seq, LANES), F32)] * 3,
        compiler_params=_cparams("parallel", "parallel"),
        name="attn_b",
    )(q, k, v, *biases)


BASE_BLOCK = 8


def _pair_block_diag(x, same_head):
    reps = same_head.shape[0] // x.shape[0]
    return jnp.where(same_head, jnp.concatenate([x] * reps, axis=0), 0.0).astype(BF16)


def _inv_unit_triangular_pairs(ms, same_block, same_head, eye):
    bd = lambda x: _pair_block_diag(x, same_head)
    b16 = lambda x: x.astype(BF16)
    m8 = [b16(jnp.where(same_block[BASE_BLOCK], m, 0.0)) for m in ms]
    p2 = [_dot(a, bd(a.astype(F32))) for a in m8]
    p2h = [b16(p) for p in p2]
    p4 = [_dot(a, bd(p)) for a, p in zip(p2h, p2)]
    p6 = [_dot(a, bd(p)) for a, p in zip(p2h, p4)]
    even = [eye + a + b + c for a, b, c in zip(p2, p4, p6)]
    inv = [e - _dot(a, bd(e)) for a, e in zip(m8, even)]
    size = BASE_BLOCK
    while size < CHUNK:
        joins = same_block[2 * size] & jnp.logical_not(same_block[size])
        t = [_dot(b16(jnp.where(joins, m, 0.0)), bd(x)) for m, x in zip(ms, inv)]
        inv = [x - _dot(b16(x), bd(y)) for x, y in zip(inv, t)]
        size *= 2
    return inv


def _delta_kernel(xq_ref, xk_ref, xv_ref, wq_ref, wk_ref, wv_ref, z_ref, gt_ref,
                  alog_ref, dtb_ref, gain_ref, o_ref,
                  pad_s, q_s, k_s, v_s, beta_s, g_s, of_s, ob_s, st_s,
                  u_s, lq_s, a_s, kw_s, ku_s, sn_s, eg_s, *, seq):
    n_chunks = seq // CHUNK
    pad = 8
    sq_r = lax.broadcasted_iota(jnp.int32, (LANES, LANES), 0)
    sq_c = lax.broadcasted_iota(jnp.int32, (LANES, LANES), 1)
    same_head = (sq_r < HEAD_DIM) == (sq_c < HEAD_DIM)
    head_ones = same_head.astype(BF16)

    zeros = jnp.zeros((pad, LANES), F32)
    for j, (x_ref, w_ref, dst, scale) in enumerate(((xq_ref, wq_ref, q_s, HEAD_DIM ** -0.5),
                                                     (xk_ref, wk_ref, k_s, 1.0),
                                                     (xv_ref, wv_ref, v_s, None))):
        pad_s[j, 0:pad, :] = zeros
        pad_s[j, pad + seq:pad + seq + pad, :] = zeros
        pad_s[j, pad:pad + seq, :] = x_ref[...]
        for r0 in range(0, seq, CONV_TILE):
            acc = jnp.zeros((CONV_TILE, LANES), F32)
            for d in range(CONV_K):
                lo = r0 + pad - CONV_K // 2 + d
                acc = acc + w_ref[d:d + 1, :] * pad_s[j, lo:lo + CONV_TILE, :]
            y = _silu(acc)
            if scale is not None:
                y = y * lax.rsqrt(_split_dot(y * y, head_ones) + NORM_EPS) * scale
            dst[r0:r0 + CONV_TILE, :] = y

    gt = gt_ref[...]
    beta_s[...] = 1.0 / (1.0 + jnp.exp(-gt))
    gx = gt + dtb_ref[0]
    softplus = jnp.maximum(gx, 0.0) + jnp.log(1.0 + jnp.exp(-jnp.abs(gx)))
    g_s[...] = -jnp.exp(alog_ref[0]) * softplus
    st_s[...] = jnp.zeros(st_s.shape, F32)

    wide = 2 * LANES
    lane_w = lax.broadcasted_iota(jnp.int32, (CHUNK, wide), 1)
    rows = lax.broadcasted_iota(jnp.int32, (CHUNK, wide), 0)
    cols = lane_w & (HEAD_DIM - 1)
    fwd = lane_w < LANES
    fwd_row = lax.broadcasted_iota(jnp.int32, (1, wide), 1) < LANES
    head0 = lax.broadcasted_iota(jnp.int32, (CHUNK, LANES), 1) < HEAD_DIM
    eye_b = rows == cols
    rows_p = lax.broadcasted_iota(jnp.int32, (CHUNK, LANES), 0)
    cols_p = lax.broadcasted_iota(jnp.int32, (CHUNK, LANES), 1) & (HEAD_DIM - 1)
    eye = (rows_p == cols_p).astype(F32)
    same_block = {}
    size = BASE_BLOCK
    while size <= CHUNK:
        shift = size.bit_length() - 1
        same_block[size] = (rows_p >> shift) == (cols_p >> shift)
        size *= 2
    r2 = lax.broadcasted_iota(jnp.int32, (2 * CHUNK, CHUNK), 0)
    c2 = lax.broadcasted_iota(jnp.int32, (2 * CHUNK, CHUNK), 1)
    tri2 = (jnp.where(r2 < CHUNK, r2 - c2, c2 - (r2 - CHUNK)) >= 0).astype(BF16)
    below = jnp.where(fwd, rows - cols, cols - rows)
    incl = below >= 0
    strict = below > 0
    bd = lambda x: _pair_block_diag(x, same_head)
    both = lambda x: jnp.concatenate([x, x], axis=-1)
    halves = (slice(0, LANES), slice(LANES, wide))

    def spread(x, c0):
        col = lambda c: jnp.broadcast_to(x[:, c:c + 1], (CHUNK, LANES))
        return jnp.concatenate([jnp.where(head0, col(c0), col(c0 + 1)),
                                jnp.where(head0, col(c0 + 2), col(c0 + 3))], axis=-1)

    def chunk_group(it, carry):
        ns = [it * CHUNK_GROUP + g for g in range(CHUNK_GROUP)]
        sls = [pl.ds(pl.multiple_of(n * CHUNK, CHUNK), CHUNK) for n in ns]
        k2 = [k_s[sl, :] for sl in sls]
        q2 = [q_s[sl, :] for sl in sls]
        gsum = []
        for sl in sls:
            g2 = g_s[sl, :]
            hi = g2.astype(BF16)
            lo = (g2 - hi.astype(F32)).astype(BF16)
            gsum.append(_dot(tri2, jnp.concatenate([hi, lo], axis=-1)))
        kq = [_dot_nt(jnp.concatenate([k, q], axis=0).astype(BF16), bd(k)) for k, q in zip(k2, q2)]
        beta, egc, kd, ms = [], [], [], []
        for g, sl in enumerate(sls):
            gs = gsum[g][:, :LANES] + gsum[g][:, LANES:]
            gcum = jnp.concatenate([gs[:CHUNK], gs[CHUNK:]], axis=-1)
            beta.append(spread(beta_s[sl, :], 0))
            gc = jnp.where(fwd, spread(gcum[:, :LANES], 4), spread(gcum[:, LANES:], 4))
            gc_row = jnp.sum(jnp.where(eye_b, gc, 0.0), axis=0, keepdims=True)
            decay = jnp.where(incl, jnp.exp(gc - gc_row), 0.0)
            ms.append(jnp.where(strict, beta[g] * both(kq[g][:CHUNK]) * decay, 0.0))
            a = (both(kq[g][CHUNK:]) * decay).astype(BF16)
            egc.append(jnp.exp(gc))
            g_last = jnp.where(fwd_row, gc[CHUNK - 1:CHUNK, :], gc[0:1, :])
            kd.append((both(k2[g]) * jnp.exp(g_last - gc)).astype(BF16))
            eg = jnp.exp(g_last)
            rows8 = pl.ds(pl.multiple_of(ns[g] * 8, 8), 8)
            for d in range(2):
                a_s[d, sl, :] = a[:, d * LANES:(d + 1) * LANES]
                eg_s[d, rows8, :] = jnp.broadcast_to(eg[:, d * LANES:(d + 1) * LANES], (8, LANES))
        chains = [(g, d) for g in range(CHUNK_GROUP) for d in range(2)]
        inv = _inv_unit_triangular_pairs([ms[g][:, halves[d]] for g, d in chains], same_block, same_head, eye)
        ys = [jnp.concatenate([bd((k2[g] * (beta[g] * egc[g])[:, halves[d]])),
                               bd(v_s[sls[g], :] * beta[g][:, halves[d]])], axis=-1) for g, d in chains]
        wus = [_dot(x.astype(BF16), y) for x, y in zip(inv, ys)]
        kwu = [_dot_tn(kd[g][:, halves[d]], wu.astype(BF16)) for wu, (g, d) in zip(wus, chains)]
        for wu, x, (g, d) in zip(wus, kwu, chains):
            u_s[d, sls[g], :] = wu[:, LANES:]
            qg = q2[g] * egc[g][:, halves[d]]
            lq_s[d, ns[g]] = jnp.concatenate([wu[:, :LANES], qg], axis=0).astype(BF16)
            kw_s[d, ns[g]] = jnp.where(same_head, x[:, :LANES], 0.0).astype(BF16)
            ku_s[d, ns[g]] = jnp.where(same_head, x[:, LANES:], 0.0)
        return carry

    lax.fori_loop(0, n_chunks // CHUNK_GROUP, chunk_group, 0)

    def scan_step(i, carry):
        ns = (i, n_chunks - 1 - i)
        state = [st_s[d] for d in range(2)]
        s16 = [s.astype(BF16) for s in state]
        drop = [_dot(kw_s[d, ns[d]], s16[d]) for d in range(2)]
        for d in range(2):
            eg = eg_s[d, pl.ds(pl.multiple_of(ns[d] * 8, 8), 8), :][0:1]
            sn_s[d, ns[d]] = s16[d]
            st_s[d] = state[d] * eg - drop[d] + ku_s[d, ns[d]]
        return carry

    lax.fori_loop(0, n_chunks, scan_step, 0)

    def output_group(it, carry):
        todo = [(it * CHUNK_GROUP + g, d) for g in range(CHUNK_GROUP) for d in range(2)]
        sls = [pl.ds(pl.multiple_of(n * CHUNK, CHUNK), CHUNK) for n, _ in todo]
        rs = [_dot(lq_s[d, n], sn_s[d, n]) for n, d in todo]
        v_new = [u_s[d, sl, :] - r[:CHUNK] for r, sl, (_, d) in zip(rs, sls, todo)]
        outs = [r[CHUNK:] + _dot(a_s[d, sl, :], bd(v)) for r, v, sl, (_, d) in zip(rs, v_new, sls, todo)]
        for o, sl, (_, d) in zip(outs, sls, todo):
            (of_s, ob_s)[d][sl, :] = o
        return carry

    lax.fori_loop(0, n_chunks // CHUNK_GROUP, output_group, 0)

    o = of_s[...] + ob_s[...]
    ms = _split_dot(o * o, head_ones) * (1.0 / HEAD_DIM)
    o = o * lax.rsqrt(ms + NORM_EPS) * gain_ref[...]
    o_ref[...] = (o * _silu(z_ref[...].astype(F32))).astype(BF16)


def _split_dot_left(m, x):
    hi = x.astype(BF16)
    lo = (x - hi.astype(F32)).astype(BF16)
    return _dot(m, hi) + _dot(m, lo)


def _delta(cx, cz, gt, conv_w, alog, dtb, gain, batch, seq):
    t = cx.shape[0]
    xspec = lambda part: pl.BlockSpec((seq, LANES), lambda b, p: (b, part * C_PAIRS + p))
    wspec = lambda part: pl.BlockSpec((CONV_K, LANES), lambda b, p: (0, part * C_PAIRS + p))
    pair = pl.BlockSpec((seq, LANES), lambda b, p: (b, p))
    prow = pl.BlockSpec((1, 1, LANES), lambda b, p: (p, 0, 0))
    return pl.pallas_call(
        functools.partial(_delta_kernel, seq=seq),
        out_shape=jax.ShapeDtypeStruct((t, C_W), BF16),
        grid=(batch, C_PAIRS),
        in_specs=[xspec(0), xspec(1), xspec(2), wspec(0), wspec(1), wspec(2),
                  pair, pair, prow, prow,
                  _resident((1, LANES))],
        out_specs=pair,
        scratch_shapes=[pltpu.VMEM((3, seq + 16, LANES), F32),
                        pltpu.VMEM((seq, LANES), F32),
                        pltpu.VMEM((seq, LANES), F32),
                        pltpu.VMEM((seq, LANES), F32),
                        pltpu.VMEM((seq, LANES), F32),
                        pltpu.VMEM((seq, LANES), F32),
                        pltpu.VMEM((seq, LANES), F32),
                        pltpu.VMEM((seq, LANES), F32),
                        pltpu.VMEM((2, LANES, LANES), F32),
                        pltpu.VMEM((2, seq, LANES), F32),
                        pltpu.VMEM((2, seq // CHUNK, LANES, LANES), BF16),
                        pltpu.VMEM((2, seq, LANES), BF16),
                        pltpu.VMEM((2, seq // CHUNK, LANES, LANES), BF16),
                        pltpu.VMEM((2, seq // CHUNK, LANES, LANES), F32),
                        pltpu.VMEM((2, seq // CHUNK, LANES, LANES), BF16),
                        pltpu.VMEM((2, seq // CHUNK * 8, LANES), F32)],
        compiler_params=_cparams("parallel", "parallel"),
        name="delta",
    )(cx, cx, cx, conv_w, conv_w, conv_w, cz, gt, alog, dtb, gain)


def _out_ffn_kernel(h_ref, a_ref, b_ref, c_ref, w_ref, g_ref, wg_ref, wu_ref, wd_ref, o_ref):
    y = _dot(a_ref[...], w_ref[0:A_QW, :])
    y = y + _dot(b_ref[...], w_ref[A_QW:A_QW + B_W, :])
    y = y + _dot(c_ref[...], w_ref[A_QW + B_W:, :])
    o_ref[...] = _swiglu_residual(h_ref[...] + y, g_ref, wg_ref, wu_ref, wd_ref)


def _mix_out_ffn(h, a, b, c, layer, w, g, wg, wu, wd):
    t, d = h.shape
    tm = min(TOKEN_TILE, t)
    row = lambda width: pl.BlockSpec((tm, width), lambda i: (i, 0))
    return pl.pallas_call(
        _out_ffn_kernel,
        out_shape=jax.ShapeDtypeStruct((t, d), F32),
        grid=(t // tm,),
        in_specs=[row(d), row(A_QW), row(B_W), row(C_W)]
                 + [_layer_resident(p, layer) for p in (w, g, wg, wu, wd)],
        out_specs=row(d),
        compiler_params=_cparams("parallel"),
        name="mix_out_ffn",
    )(h, a, b, c, w, g, wg, wu, wd)


def _rope_tables(seq):
    rows = seq // GRID_W
    row = jnp.repeat(jnp.arange(rows), GRID_W).astype(F32)
    col = jnp.tile(jnp.arange(GRID_W), rows).astype(F32)
    n_freq = HEAD_DIM // 4
    inv = ROPE_THETA ** (-jnp.arange(n_freq, dtype=F32) / n_freq)
    ang = jnp.concatenate([row[:, None] * inv, col[:, None] * inv], axis=-1)
    cos, sin = jnp.cos(ang), jnp.sin(ang)
    cos_h = jnp.concatenate([cos, cos], axis=-1)
    sin_h = jnp.concatenate([-sin, sin], axis=-1)
    return jnp.tile(cos_h, (1, A_HEADS)), jnp.tile(sin_h, (1, A_HEADS))


def _pair_columns(x):
    lead = x.shape[:-1]
    x = x.reshape(lead + (2, C_PAIRS, 2))
    return jnp.moveaxis(x, -2, -3).reshape(lead + (C_PAIRS, 4))


def _pad_w_in(w_in):
    lead = w_in.shape[:-1]
    w_in = w_in.astype(BF16)
    aq = [w_in[..., hq * HEAD_DIM:(hq + 1) * HEAD_DIM] for hq in A_Q_HEAD_ORDER]
    cb = _pair_columns(w_in[..., OFF_G:OFF_G + 2 * C_HEADS])
    ca = _pair_columns(w_in[..., OFF_G + 2 * C_HEADS:])
    gates = jnp.concatenate([cb, ca, jnp.zeros(lead + (C_PAIRS, LANES - 8), BF16)], axis=-1)
    return jnp.concatenate(aq + [w_in[..., A_QW:OFF_G], gates.reshape(lead + (GATE_W,))], axis=-1)


def _pair_row(p):
    v = _pair_columns(p.reshape(2 * C_HEADS).astype(F32))
    z4 = jnp.zeros((C_PAIRS, 4), F32)
    return jnp.concatenate([z4, v, jnp.zeros((C_PAIRS, LANES - 8), F32)], axis=-1)[:, None, :]


def kernel(x, rel_bias, ffn1_norm, ffn1_w_gate, ffn1_w_up, ffn1_w_down, mix_norm, w_in, a_q_norm, a_k_norm, b_q_norm, b_k_norm, c_conv, c_A_log, c_dt_bias, c_out_norm, w_out, ffn2_norm, ffn2_w_gate, ffn2_w_up, ffn2_w_down):
    batch, seq, d = x.shape
    depth = w_in.shape[0]
    scale = HEAD_DIM ** -0.5
    h = x.reshape(batch * seq, d)
    cos, sin = _rope_tables(seq)
    idx = np.arange(MXU_W)
    mblk = jnp.asarray((idx[:, None] // HEAD_DIM == idx[None, :] // HEAD_DIM) / HEAD_DIM, BF16)
    biases = [_branch_bias(rel_bias, dil, seq // dil) for _, dil in DIL_PAIRS]
    row = lambda v: v.reshape(1, -1).astype(F32)
    ffn1 = (ffn1_norm.astype(F32)[:, None, :], ffn1_w_gate.astype(BF16), ffn1_w_up.astype(BF16),
            ffn1_w_down.astype(BF16))
    ffn2 = (ffn2_norm.astype(F32)[:, None, :], ffn2_w_gate.astype(BF16), ffn2_w_up.astype(BF16),
            ffn2_w_down.astype(BF16))
    wo_a = [w_out[:, hq * HEAD_DIM:(hq + 1) * HEAD_DIM] for hq in A_Q_HEAD_ORDER]
    wo = jnp.concatenate(wo_a + [w_out[:, A_QW:]], axis=1).astype(BF16)
    mix_g = mix_norm.astype(F32)[:, None, :]
    w_proj = _pad_w_in(w_in)
    wvt = jnp.swapaxes(w_in[:, :, A_QW + A_KVW:OFF_B], 1, 2).astype(BF16)
    for l in range(depth):
        h = _ffn(h, l, *ffn1)
        aq, ak, avt, bq, bk, bv, cx, cz, gt = _proj(
            h, l, mix_g, w_proj, wvt, cos, sin,
            row(jnp.tile(a_q_norm[l], A_HEADS)) * scale, row(jnp.tile(a_k_norm[l], A_KV_HEADS)),
            row(jnp.tile(b_q_norm[l], B_HEADS)) * scale, row(jnp.tile(b_k_norm[l], B_HEADS)),
            mblk, seq)
        out_a = _attn_a(aq, ak, avt, seq)
        out_b = _attn_b(bq, bk, bv, biases, batch, seq)
        out_c = _delta(cx, cz, gt, c_conv[l].astype(F32), _pair_row(c_A_log[l]), _pair_row(c_dt_bias[l]),
                       row(jnp.tile(c_out_norm[l], 2)), batch, seq)
        h = _mix_out_ffn(h, out_a, out_b, out_c, l, wo, *ffn2)
    return h.reshape(batch, seq, d)
```

```python
import functools
import math

import numpy as np
import jax
import jax.numpy as jnp
from jax import lax
from jax.experimental import pallas as pl
from jax.experimental.pallas import tpu as pltpu

F32 = jnp.float32
BF16 = jnp.bfloat16

D_MODEL = 1024
HEAD_DIM = 64
A_HEADS = 4
A_KV_HEADS = 2
B_HEADS = 6
C_HEADS = 6
D_FF = 2816
GRID_W = 64
ROPE_THETA = 10000.0
DIL_PAIRS = ((128, 1), (512, 4), (2048, 16))
REL_BUCKETS = 32
REL_MAX_DIST = 1024
CONV_K = 5
CHUNK = 64
NORM_EPS = 1e-6
NEG_INF = -1e30
LOG2E = math.log2(math.e)

A_QW = A_HEADS * HEAD_DIM
A_KVW = A_KV_HEADS * HEAD_DIM
B_W = B_HEADS * HEAD_DIM
C_W = C_HEADS * HEAD_DIM
C_PAIRS = C_HEADS // 2
LANES = 128
MXU_W = 256
GATE_W = C_PAIRS * LANES

OFF_A = 0
OFF_B = OFF_A + A_QW + 2 * A_KVW
OFF_C = OFF_B + 3 * B_W
OFF_Z = OFF_C + 3 * C_W
OFF_G = OFF_Z + C_W
N_PROJ = OFF_G + GATE_W

TOKEN_TILE = 512
A_Q_TILE = 256
B_Q_TILE = 128
B_K_TILE = 256
B_SIDE = 64
CONV_TILE = 256
CHUNK_GROUP = 16
VMEM_LIMIT = 56 * 1024 * 1024


def _cparams(*sem):
    return pltpu.CompilerParams(dimension_semantics=sem, vmem_limit_bytes=VMEM_LIMIT)


def _resident(shape):
    return pl.BlockSpec(shape, lambda *_: (0,) * len(shape), pipeline_mode=pl.Buffered(1))


def _dot(a, b):
    return jnp.dot(a, b, preferred_element_type=F32)


def _dot_nt(a, b):
    return lax.dot_general(a, b, (((1,), (1,)), ((), ())), preferred_element_type=F32)


def _dot_tn(a, b):
    return lax.dot_general(a, b, (((0,), (0,)), ((), ())), preferred_element_type=F32)


def _silu(x):
    return x / (1.0 + jnp.exp(-x))


def _split_dot(x, m):
    hi = x.astype(BF16)
    lo = (x - hi.astype(F32)).astype(BF16)
    return _dot(hi, m) + _dot(lo, m)


def _rms_rows(x, g):
    ms = jnp.mean(x * x, axis=-1, keepdims=True)
    return x * lax.rsqrt(ms + NORM_EPS) * g


def _layer_resident(stacked, layer):
    shape = stacked.shape[1:]
    return pl.BlockSpec((None,) + shape, lambda *_: (layer,) + (0,) * len(shape),
                        pipeline_mode=pl.Buffered(1))


def _swiglu_residual(x, g_ref, wg_ref, wu_ref, wd_ref):
    xn = _rms_rows(x, g_ref[...]).astype(BF16)
    gate = _dot(xn, wg_ref[...])
    up = _dot(xn, wu_ref[...])
    act = (_silu(gate) * up).astype(BF16)
    return x + 0.5 * _dot(act, wd_ref[...])


def _ffn_kernel(h_ref, g_ref, wg_ref, wu_ref, wd_ref, o_ref):
    o_ref[...] = _swiglu_residual(h_ref[...], g_ref, wg_ref, wu_ref, wd_ref)


def _ffn(h, layer, g, wg, wu, wd):
    t, d = h.shape
    tm = min(TOKEN_TILE, t)
    return pl.pallas_call(
        _ffn_kernel,
        out_shape=jax.ShapeDtypeStruct((t, d), F32),
        grid=(t // tm,),
        in_specs=[pl.BlockSpec((tm, d), lambda i: (i, 0))]
                 + [_layer_resident(p, layer) for p in (g, wg, wu, wd)],
        out_specs=pl.BlockSpec((tm, d), lambda i: (i, 0)),
        compiler_params=_cparams("parallel"),
        name="ffn",
    )(h, g, wg, wu, wd)


def _rope(x, cos, sin):
    w = x.shape[1]
    lane = lax.broadcasted_iota(jnp.int32, x.shape, 1)
    first = (lane & (HEAD_DIM - 1)) < HEAD_DIM // 2
    half = HEAD_DIM // 2
    partner = jnp.where(first, pltpu.roll(x, w - half, 1), pltpu.roll(x, half, 1))
    return x * cos + partner * sin


def _proj_kernel(h_ref, g_ref, w_ref, wvt_ref, cos_ref, sin_ref, gaq_ref, gak_ref, gbq_ref, gbk_ref,
                 mblk_ref, aq_o, ak_o, avt_o, bq_o, bk_o, bv_o, cx_o, cz_o, gt_o):
    xn = _rms_rows(h_ref[...], g_ref[...]).astype(BF16)
    pr = _dot(xn, w_ref[...])
    avt_o[...] = _dot_nt(wvt_ref[...], xn).astype(BF16)

    def unit_rms(lo, width):
        outs = []
        for c in range(lo, lo + width, MXU_W):
            w = min(MXU_W, lo + width - c)
            x = pr[:, c:c + w]
            outs.append(x * lax.rsqrt(_split_dot(x * x, mblk_ref[:w, :w]) + NORM_EPS))
        return outs[0] if len(outs) == 1 else jnp.concatenate(outs, axis=-1)

    cos = cos_ref[...]
    sin = sin_ref[...]
    aq_o[...] = _rope(unit_rms(OFF_A, A_QW) * gaq_ref[...], cos, sin).astype(BF16)
    ak = unit_rms(OFF_A + A_QW, A_KVW) * gak_ref[...]
    ak_o[...] = _rope(ak, cos[:, :A_KVW], sin[:, :A_KVW]).astype(BF16)
    bqk = unit_rms(OFF_B, 2 * B_W)
    bq_o[...] = bqk[:, :B_W] * gbq_ref[...]
    bk_o[...] = bqk[:, B_W:] * gbk_ref[...]
    bv_o[...] = pr[:, OFF_B + 2 * B_W:OFF_C]
    cx_o[...] = pr[:, OFF_C:OFF_Z]
    cz_o[...] = pr[:, OFF_Z:OFF_G].astype(BF16)
    gt_o[...] = pr[:, OFF_G:]


def _proj(h, layer, g, w, wvt, cos, sin, gaq, gak, gbq, gbk, mblk, seq):
    t, d = h.shape
    tm = min(TOKEN_TILE, seq)
    per_seq = seq // tm
    row = lambda i: (i, 0)
    pos = lambda i: (i % per_seq, 0)
    widths = (A_QW, A_KVW, None, B_W, B_W, B_W, 3 * C_W, C_W, GATE_W)
    dtypes = (BF16, BF16, BF16, F32, F32, F32, F32, BF16, F32)
    shapes = [(t, wd) if wd else (A_KVW, t) for wd in widths]
    specs = [pl.BlockSpec((tm, wd), row) if wd else pl.BlockSpec((A_KVW, tm), lambda i: (0, i))
             for wd in widths]
    return pl.pallas_call(
        _proj_kernel,
        out_shape=[jax.ShapeDtypeStruct(s, dt) for s, dt in zip(shapes, dtypes)],
        grid=(t // tm,),
        in_specs=[pl.BlockSpec((tm, d), row),
                  _layer_resident(g, layer), _layer_resident(w, layer), _layer_resident(wvt, layer),
                  pl.BlockSpec((tm, A_QW), pos),
                  pl.BlockSpec((tm, A_QW), pos),
                  _resident((1, A_QW)), _resident((1, A_KVW)), _resident((1, B_W)),
                  _resident((1, B_W)), _resident((MXU_W, MXU_W))],
        out_specs=specs,
        compiler_params=_cparams("parallel"),
        name="mix_in",
    )(h, g, w, wvt, cos, sin, gaq, gak, gbq, gbk, mblk)


A_Q_HEAD_ORDER = tuple(hk * (A_HEADS // A_KV_HEADS) + g
                       for g in range(A_HEADS // A_KV_HEADS) for hk in range(A_KV_HEADS))


def _attn_a_kernel(q_ref, k_ref, vt_ref, o_ref):
    k2 = k_ref[...]
    lane_hi = lax.broadcasted_iota(jnp.int32, (1, LANES), 1) >= HEAD_DIM
    row_hi = lax.broadcasted_iota(jnp.int32, (LANES, 1), 0) >= HEAD_DIM
    vt = vt_ref[...].astype(F32)
    vt_ones = [jnp.where(row_hi == (hk == 1), vt, 1.0).astype(BF16) for hk in range(A_KV_HEADS)]
    heads = [(g, hk) for g in range(A_HEADS // A_KV_HEADS) for hk in range(A_KV_HEADS)]

    def scores(g, hk):
        qb = q_ref[:, g * LANES:(g + 1) * LANES].astype(F32)
        qm = jnp.where(lane_hi == (hk == 1), qb, 0.0).astype(BF16)
        return _dot_nt(k2, qm)

    st_next = scores(*heads[0])
    halves = []
    for i, (g, hk) in enumerate(heads):
        st = st_next
        if i + 1 < len(heads):
            st_next = scores(*heads[i + 1])
        m = jnp.max(st, axis=0, keepdims=True)
        p = jnp.exp2(st - m).astype(BF16)
        ov = _dot(vt_ones[hk], p)
        den_row = (1 - hk) * HEAD_DIM
        halves.append(ov / ov[den_row:den_row + 1, :])
        if hk == A_KV_HEADS - 1:
            ot = jnp.where(row_hi, halves[1], halves[0])
            o_ref[:, g * LANES:(g + 1) * LANES] = ot.T.astype(BF16)
            halves = []


def _attn_a(q, k, vt, seq):
    t = q.shape[0]
    tq = min(A_Q_TILE, seq)
    per_seq = seq // tq
    return pl.pallas_call(
        _attn_a_kernel,
        out_shape=jax.ShapeDtypeStruct((t, A_QW), BF16),
        grid=(t // seq, per_seq),
        in_specs=[pl.BlockSpec((tq, A_QW), lambda b, i: (b * per_seq + i, 0)),
                  pl.BlockSpec((seq, A_KVW), lambda b, i: (b, 0)),
                  pl.BlockSpec((A_KVW, seq), lambda b, i: (0, b))],
        out_specs=pl.BlockSpec((tq, A_QW), lambda b, i: (b * per_seq + i, 0)),
        compiler_params=_cparams("parallel", "parallel"),
        name="attn_a",
    )(q, k, vt)


def _t5_bucket_np(rel):
    half = REL_BUCKETS // 2
    exact = half // 2
    sign = np.where(rel > 0, half, 0)
    n = np.abs(rel)
    nf = np.maximum(n, 1).astype(np.float32)
    large = exact + (np.log(nf / np.float32(exact)) / np.float32(math.log(REL_MAX_DIST / exact))
                     * np.float32(half - exact)).astype(np.int32)
    large = np.minimum(large, half - 1)
    return sign + np.where(n < exact, n, large)


def _branch_tiles(sub_len):
    if sub_len >= B_K_TILE:
        return B_Q_TILE, B_K_TILE, (0, -B_SIDE, B_Q_TILE - B_K_TILE)
    return sub_len, sub_len, (0,)


def _branch_bias(rel_bias, dil, sub_len):
    tq, tk, offs = _branch_tiles(sub_len)
    period = tq + tk
    slot = np.arange(period)
    tabs = []
    for off in offs:
        delta = np.where(slot < tk, slot, slot - period) + off
        bucket = _t5_bucket_np(delta * dil)
        vec = jnp.where((np.abs(delta) <= B_SIDE)[:, None], rel_bias.astype(F32)[bucket] * LOG2E, NEG_INF).T
        flat = jnp.tile(vec, (1, tq))[:, :tq * (period - 1)]
        tabs.append(flat.reshape(B_HEADS, tq, period - 1)[:, :, :tk])
    return jnp.stack(tabs)


def _attn_b_kernel(q_ref, k_ref, v_ref, b0_ref, b1_ref, b2_ref, o_ref, num_s, mx_s, den_s, *, seq):
    head0 = lax.broadcasted_iota(jnp.int32, (1, LANES), 1) < HEAD_DIM
    pick = lambda a, b: jnp.where(head0, a, b)

    tiles = []
    for g, ((_, dil), bias_ref) in enumerate(zip(DIL_PAIRS, (b0_ref, b1_ref, b2_ref))):
        sub_len = seq // dil
        tq, tk, _ = _branch_tiles(sub_len)
        n_tiles = sub_len // tq
        for r in range(dil):
            for t in range(n_tiles):
                if n_tiles == 1:
                    k0, tab = 0, 0
                else:
                    k0 = min(max(t * tq - B_SIDE, 0), sub_len - tk)
                    tab = 0 if t == 0 else 2 if t == n_tiles - 1 else 1
                tiles.append((g, bias_ref, tab, pl.ds(r + dil * t * tq, tq, stride=dil),
                              pl.ds(r + dil * k0, tk, stride=dil)))

    for g, bias_ref, tab, rows, keys in tiles:
        tq = rows.size
        qt = q_ref[rows, :]
        q2 = jnp.concatenate([pick(qt, 0.0), pick(0.0, qt)], axis=0).astype(BF16)
        s = _dot_nt(q2, k_ref[keys, :].astype(BF16)) + bias_ref[tab]
        m = jnp.max(s, axis=-1, keepdims=True)
        p = jnp.exp2(s - m)
        l = jnp.sum(p, axis=-1, keepdims=True)
        pv = _dot(p.astype(BF16), v_ref[keys, :].astype(BF16))
        num_s.at[g][rows, :] = pick(pv[:tq], pv[tq:])
        mx_s.at[g][rows, :] = pick(m[:tq], m[tq:])
        den_s.at[g][rows, :] = pick(l[:tq], l[tq:])

    def merge(t, carry):
        rows = pl.ds(pl.multiple_of(t * CONV_TILE, CONV_TILE), CONV_TILE)
        ms = [mx_s[g, rows, :] for g in range(3)]
        top = jnp.maximum(jnp.maximum(ms[0], ms[1]), ms[2])
        es = [jnp.exp2(m - top) for m in ms]
        num = sum(e * num_s[g, rows, :] for g, e in enumerate(es))
        den = sum(e * den_s[g, rows, :] for g, e in enumerate(es))
        o_ref[rows, :] = (num / den).astype(BF16)
        return carry

    lax.fori_loop(0, seq // CONV_TILE, merge, 0)


def _attn_b(q, k, v, biases, batch, seq):
    t = q.shape[0]
    pair = pl.BlockSpec((seq, LANES), lambda b, p: (b, p))
    biases = [x.reshape(x.shape[0], B_HEADS // 2, 2 * x.shape[2], x.shape[3]) for x in biases]
    bias_spec = lambda x: pl.BlockSpec((x.shape[0], None) + x.shape[2:], lambda b, p: (0, p, 0, 0))
    return pl.pallas_call(
        functools.partial(_attn_b_kernel, seq=seq),
        out_shape=jax.ShapeDtypeStruct((t, B_W), BF16),
        grid=(batch, B_HEADS // 2),
        in_specs=[pair, pair, pair] + [bias_spec(x) for x in biases],
        out_specs=pair,
        scratch_shapes=[pltpu.VMEM((len(DIL_PAIRS), seq, LANES), F32)] * 3,
        compiler_params=_cparams("parallel", "parallel"),
        name="attn_b",
    )(q, k, v, *biases)


BASE_BLOCK = 8


def _pair_block_diag(x, same_head):
    reps = same_head.shape[0] // x.shape[0]
    return jnp.where(same_head, jnp.concatenate([x] * reps, axis=0), 0.0).astype(BF16)


def _inv_unit_triangular_pairs(ms, same_block, same_head, eye):
    bd = lambda x: _pair_block_diag(x, same_head)
    b16 = lambda x: x.astype(BF16)
    m8 = [b16(jnp.where(same_block[BASE_BLOCK], m, 0.0)) for m in ms]
    p2 = [_dot(a, bd(a.astype(F32))) for a in m8]
    p2h = [b16(p) for p in p2]
    p4 = [_dot(a, bd(p)) for a, p in zip(p2h, p2)]
    p6 = [_dot(a, bd(p)) for a, p in zip(p2h, p4)]
    even = [eye + a + b + c for a, b, c in zip(p2, p4, p6)]
    inv = [e - _dot(a, bd(e)) for a, e in zip(m8, even)]
    size = BASE_BLOCK
    while size < CHUNK:
        joins = same_block[2 * size] & jnp.logical_not(same_block[size])
        t = [_dot(b16(jnp.where(joins, m, 0.0)), bd(x)) for m, x in zip(ms, inv)]
        inv = [x - _dot(b16(x), bd(y)) for x, y in zip(inv, t)]
        size *= 2
    return inv


def _delta_kernel(xq_ref, xk_ref, xv_ref, wq_ref, wk_ref, wv_ref, z_ref, gt_ref,
                  alog_ref, dtb_ref, gain_ref, o_ref,
                  pad_s, q_s, k_s, v_s, beta_s, g_s, of_s, ob_s, st_s,
                  u_s, lq_s, a_s, kw_s, ku_s, sn_s, eg_s, *, seq):
    n_chunks = seq // CHUNK
    pad = 8
    sq_r = lax.broadcasted_iota(jnp.int32, (LANES, LANES), 0)
    sq_c = lax.broadcasted_iota(jnp.int32, (LANES, LANES), 1)
    same_head = (sq_r < HEAD_DIM) == (sq_c < HEAD_DIM)
    head_ones = same_head.astype(BF16)

    zeros = jnp.zeros((pad, LANES), F32)
    for j, (x_ref, w_ref, dst, scale) in enumerate(((xq_ref, wq_ref, q_s, HEAD_DIM ** -0.5),
                                                     (xk_ref, wk_ref, k_s, 1.0),
                                                     (xv_ref, wv_ref, v_s, None))):
        pad_s[j, 0:pad, :] = zeros
        pad_s[j, pad + seq:pad + seq + pad, :] = zeros
        pad_s[j, pad:pad + seq, :] = x_ref[...]
        for r0 in range(0, seq, CONV_TILE):
            acc = jnp.zeros((CONV_TILE, LANES), F32)
            for d in range(CONV_K):
                lo = r0 + pad - CONV_K // 2 + d
                acc = acc + w_ref[d:d + 1, :] * pad_s[j, lo:lo + CONV_TILE, :]
            y = _silu(acc)
            if scale is not None:
                y = y * lax.rsqrt(_split_dot(y * y, head_ones) + NORM_EPS) * scale
            dst[r0:r0 + CONV_TILE, :] = y

    gt = gt_ref[...]
    beta_s[...] = 1.0 / (1.0 + jnp.exp(-gt))
    gx = gt + dtb_ref[0]
    softplus = jnp.maximum(gx, 0.0) + jnp.log(1.0 + jnp.exp(-jnp.abs(gx)))
    g_s[...] = -jnp.exp(alog_ref[0]) * softplus
    st_s[...] = jnp.zeros(st_s.shape, F32)

    wide = 2 * LANES
    lane_w = lax.broadcasted_iota(jnp.int32, (CHUNK, wide), 1)
    rows = lax.broadcasted_iota(jnp.int32, (CHUNK, wide), 0)
    cols = lane_w & (HEAD_DIM - 1)
    fwd = lane_w < LANES
    fwd_row = lax.broadcasted_iota(jnp.int32, (1, wide), 1) < LANES
    head0 = lax.broadcasted_iota(jnp.int32, (CHUNK, LANES), 1) < HEAD_DIM
    eye_b = rows == cols
    rows_p = lax.broadcasted_iota(jnp.int32, (CHUNK, LANES), 0)
    cols_p = lax.broadcasted_iota(jnp.int32, (CHUNK, LANES), 1) & (HEAD_DIM - 1)
    eye = (rows_p == cols_p).astype(F32)
    same_block = {}
    size = BASE_BLOCK
    while size <= CHUNK:
        shift = size.bit_length() - 1
        same_block[size] = (rows_p >> shift) == (cols_p >> shift)
        size *= 2
    r2 = lax.broadcasted_iota(jnp.int32, (2 * CHUNK, CHUNK), 0)
    c2 = lax.broadcasted_iota(jnp.int32, (2 * CHUNK, CHUNK), 1)
    tri2 = (jnp.where(r2 < CHUNK, r2 - c2, c2 - (r2 - CHUNK)) >= 0).astype(BF16)
    below = jnp.where(fwd, rows - cols, cols - rows)
    incl = below >= 0
    strict = below > 0
    bd = lambda x: _pair_block_diag(x, same_head)
    both = lambda x: jnp.concatenate([x, x], axis=-1)
    halves = (slice(0, LANES), slice(LANES, wide))

    def spread(x, c0):
        col = lambda c: jnp.broadcast_to(x[:, c:c + 1], (CHUNK, LANES))
        return jnp.concatenate([jnp.where(head0, col(c0), col(c0 + 1)),
                                jnp.where(head0, col(c0 + 2), col(c0 + 3))], axis=-1)

    def chunk_group(it, carry):
        ns = [it * CHUNK_GROUP + g for g in range(CHUNK_GROUP)]
        sls = [pl.ds(pl.multiple_of(n * CHUNK, CHUNK), CHUNK) for n in ns]
        k2 = [k_s[sl, :] for sl in sls]
        q2 = [q_s[sl, :] for sl in sls]
        gsum = []
        for sl in sls:
            g2 = g_s[sl, :]
            hi = g2.astype(BF16)
            lo = (g2 - hi.astype(F32)).astype(BF16)
            gsum.append(_dot(tri2, jnp.concatenate([hi, lo], axis=-1)))
        kq = [_dot_nt(jnp.concatenate([k, q], axis=0).astype(BF16), bd(k)) for k, q in zip(k2, q2)]
        beta, egc, kd, ms = [], [], [], []
        for g, sl in enumerate(sls):
            gs = gsum[g][:, :LANES] + gsum[g][:, LANES:]
            gcum = jnp.concatenate([gs[:CHUNK], gs[CHUNK:]], axis=-1)
            beta.append(spread(beta_s[sl, :], 0))
            gc = jnp.where(fwd, spread(gcum[:, :LANES], 4), spread(gcum[:, LANES:], 4))
            gc_row = jnp.sum(jnp.where(eye_b, gc, 0.0), axis=0, keepdims=True)
            decay = jnp.where(incl, jnp.exp(gc - gc_row), 0.0)
            ms.append(jnp.where(strict, beta[g] * both(kq[g][:CHUNK]) * decay, 0.0))
            a = (both(kq[g][CHUNK:]) * decay).astype(BF16)
            egc.append(jnp.exp(gc))
            g_last = jnp.where(fwd_row, gc[CHUNK - 1:CHUNK, :], gc[0:1, :])
            kd.append((both(k2[g]) * jnp.exp(g_last - gc)).astype(BF16))
            eg = jnp.exp(g_last)
            rows8 = pl.ds(pl.multiple_of(ns[g] * 8, 8), 8)
            for d in range(2):
                a_s[d, sl, :] = a[:, d * LANES:(d + 1) * LANES]
                eg_s[d, rows8, :] = jnp.broadcast_to(eg[:, d * LANES:(d + 1) * LANES], (8, LANES))
        chains = [(g, d) for g in range(CHUNK_GROUP) for d in range(2)]
        inv = _inv_unit_triangular_pairs([ms[g][:, halves[d]] for g, d in chains], same_block, same_head, eye)
        ys = [jnp.concatenate([bd((k2[g] * (beta[g] * egc[g])[:, halves[d]])),
                               bd(v_s[sls[g], :] * beta[g][:, halves[d]])], axis=-1) for g, d in chains]
        wus = [_dot(x.astype(BF16), y) for x, y in zip(inv, ys)]
        kwu = [_dot_tn(kd[g][:, halves[d]], wu.astype(BF16)) for wu, (g, d) in zip(wus, chains)]
        for wu, x, (g, d) in zip(wus, kwu, chains):
            u_s[d, sls[g], :] = wu[:, LANES:]
            qg = q2[g] * egc[g][:, halves[d]]
            lq_s[d, ns[g]] = jnp.concatenate([wu[:, :LANES], qg], axis=0).astype(BF16)
            kw_s[d, ns[g]] = jnp.where(same_head, x[:, :LANES], 0.0).astype(BF16)
            ku_s[d, ns[g]] = jnp.where(same_head, x[:, LANES:], 0.0)
        return carry

    lax.fori_loop(0, n_chunks // CHUNK_GROUP, chunk_group, 0)

    def scan_step(i, carry):
        ns = (i, n_chunks - 1 - i)
        state = [st_s[d] for d in range(2)]
        s16 = [s.astype(BF16) for s in state]
        drop = [_dot(kw_s[d, ns[d]], s16[d]) for d in range(2)]
        for d in range(2):
            eg = eg_s[d, pl.ds(pl.multiple_of(ns[d] * 8, 8), 8), :][0:1]
            sn_s[d, ns[d]] = s16[d]
            st_s[d] = state[d] * eg - drop[d] + ku_s[d, ns[d]]
        return carry

    lax.fori_loop(0, n_chunks, scan_step, 0)

    def output_group(it, carry):
        todo = [(it * CHUNK_GROUP + g, d) for g in range(CHUNK_GROUP) for d in range(2)]
        sls = [pl.ds(pl.multiple_of(n * CHUNK, CHUNK), CHUNK) for n, _ in todo]
        rs = [_dot(lq_s[d, n], sn_s[d, n]) for n, d in todo]
        v_new = [u_s[d, sl, :] - r[:CHUNK] for r, sl, (_, d) in zip(rs, sls, todo)]
        outs = [r[CHUNK:] + _dot(a_s[d, sl, :], bd(v)) for r, v, sl, (_, d) in zip(rs, v_new, sls, todo)]
        for o, sl, (_, d) in zip(outs, sls, todo):
            (of_s, ob_s)[d][sl, :] = o
        return carry

    lax.fori_loop(0, n_chunks // CHUNK_GROUP, output_group, 0)

    o = of_s[...] + ob_s[...]
    ms = _split_dot(o * o, head_ones) * (1.0 / HEAD_DIM)
    o = o * lax.rsqrt(ms + NORM_EPS) * gain_ref[...]
    o_ref[...] = (o * _silu(z_ref[...].astype(F32))).astype(BF16)


def _split_dot_left(m, x):
    hi = x.astype(BF16)
    lo = (x - hi.astype(F32)).astype(BF16)
    return _dot(m, hi) + _dot(m, lo)


def _delta(cx, cz, gt, conv_w, alog, dtb, gain, batch, seq):
    t = cx.shape[0]
    xspec = lambda part: pl.BlockSpec((seq, LANES), lambda b, p: (b, part * C_PAIRS + p))
    wspec = lambda part: pl.BlockSpec((CONV_K, LANES), lambda b, p: (0, part * C_PAIRS + p))
    pair = pl.BlockSpec((seq, LANES), lambda b, p: (b, p))
    prow = pl.BlockSpec((1, 1, LANES), lambda b, p: (p, 0, 0))
    return pl.pallas_call(
        functools.partial(_delta_kernel, seq=seq),
        out_shape=jax.ShapeDtypeStruct((t, C_W), BF16),
        grid=(batch, C_PAIRS),
        in_specs=[xspec(0), xspec(1), xspec(2), wspec(0), wspec(1), wspec(2),
                  pair, pair, prow, prow,
                  _resident((1, LANES))],
        out_specs=pair,
        scratch_shapes=[pltpu.VMEM((3, seq + 16, LANES), F32),
                        pltpu.VMEM((seq, LANES), F32),
                        pltpu.VMEM((seq, LANES), F32),
                        pltpu.VMEM((seq, LANES), F32),
                        pltpu.VMEM((seq, LANES), F32),
                        pltpu.VMEM((seq, LANES), F32),
                        pltpu.VMEM((seq, LANES), F32),
                        pltpu.VMEM((seq, LANES), F32),
                        pltpu.VMEM((2, LANES, LANES), F32),
                        pltpu.VMEM((2, seq, LANES), F32),
                        pltpu.VMEM((2, seq // CHUNK, LANES, LANES), BF16),
                        pltpu.VMEM((2, seq, LANES), BF16),
                        pltpu.VMEM((2, seq // CHUNK, LANES, LANES), BF16),
                        pltpu.VMEM((2, seq // CHUNK, LANES, LANES), F32),
                        pltpu.VMEM((2, seq // CHUNK, LANES, LANES), BF16),
                        pltpu.VMEM((2, seq // CHUNK * 8, LANES), F32)],
        compiler_params=_cparams("parallel", "parallel"),
        name="delta",
    )(cx, cx, cx, conv_w, conv_w, conv_w, cz, gt, alog, dtb, gain)


def _out_ffn_kernel(h_ref, a_ref, b_ref, c_ref, w_ref, g_ref, wg_ref, wu_ref, wd_ref, o_ref):
    y = _dot(a_ref[...], w_ref[0:A_QW, :])
    y = y + _dot(b_ref[...], w_ref[A_QW:A_QW + B_W, :])
    y = y + _dot(c_ref[...], w_ref[A_QW + B_W:, :])
    o_ref[...] = _swiglu_residual(h_ref[...] + y, g_ref, wg_ref, wu_ref, wd_ref)


def _mix_out_ffn(h, a, b, c, layer, w, g, wg, wu, wd):
    t, d = h.shape
    tm = min(TOKEN_TILE, t)
    row = lambda width: pl.BlockSpec((tm, width), lambda i: (i, 0))
    return pl.pallas_call(
        _out_ffn_kernel,
        out_shape=jax.ShapeDtypeStruct((t, d), F32),
        grid=(t // tm,),
        in_specs=[row(d), row(A_QW), row(B_W), row(C_W)]
                 + [_layer_resident(p, layer) for p in (w, g, wg, wu, wd)],
        out_specs=row(d),
        compiler_params=_cparams("parallel"),
        name="mix_out_ffn",
    )(h, a, b, c, w, g, wg, wu, wd)


def _rope_tables(seq):
    rows = seq // GRID_W
    row = jnp.repeat(jnp.arange(rows), GRID_W).astype(F32)
    col = jnp.tile(jnp.arange(GRID_W), rows).astype(F32)
    n_freq = HEAD_DIM // 4
    inv = ROPE_THETA ** (-jnp.arange(n_freq, dtype=F32) / n_freq)
    ang = jnp.concatenate([row[:, None] * inv, col[:, None] * inv], axis=-1)
    cos, sin = jnp.cos(ang), jnp.sin(ang)
    cos_h = jnp.concatenate([cos, cos], axis=-1)
    sin_h = jnp.concatenate([-sin, sin], axis=-1)
    return jnp.tile(cos_h, (1, A_HEADS)), jnp.tile(sin_h, (1, A_HEADS))


def _pair_columns(x):
    lead = x.shape[:-1]
    x = x.reshape(lead + (2, C_PAIRS, 2))
    return jnp.moveaxis(x, -2, -3).reshape(lead + (C_PAIRS, 4))


def _pad_w_in(w_in):
    lead = w_in.shape[:-1]
    w_in = w_in.astype(BF16)
    aq = [w_in[..., hq * HEAD_DIM:(hq + 1) * HEAD_DIM] for hq in A_Q_HEAD_ORDER]
    cb = _pair_columns(w_in[..., OFF_G:OFF_G + 2 * C_HEADS])
    ca = _pair_columns(w_in[..., OFF_G + 2 * C_HEADS:])
    gates = jnp.concatenate([cb, ca, jnp.zeros(lead + (C_PAIRS, LANES - 8), BF16)], axis=-1)
    return jnp.concatenate(aq + [w_in[..., A_QW:OFF_G], gates.reshape(lead + (GATE_W,))], axis=-1)


def _pair_row(p):
    v = _pair_columns(p.reshape(2 * C_HEADS).astype(F32))
    z4 = jnp.zeros((C_PAIRS, 4), F32)
    return jnp.concatenate([z4, v, jnp.zeros((C_PAIRS, LANES - 8), F32)], axis=-1)[:, None, :]


def kernel(x, rel_bias, ffn1_norm, ffn1_w_gate, ffn1_w_up, ffn1_w_down, mix_norm, w_in, a_q_norm, a_k_norm, b_q_norm, b_k_norm, c_conv, c_A_log, c_dt_bias, c_out_norm, w_out, ffn2_norm, ffn2_w_gate, ffn2_w_up, ffn2_w_down):
    batch, seq, d = x.shape
    depth = w_in.shape[0]
    scale = HEAD_DIM ** -0.5 * LOG2E
    h = x.reshape(batch * seq, d)
    cos, sin = _rope_tables(seq)
    idx = np.arange(MXU_W)
    mblk = jnp.asarray((idx[:, None] // HEAD_DIM == idx[None, :] // HEAD_DIM) / HEAD_DIM, BF16)
    biases = [_branch_bias(rel_bias, dil, seq // dil) for _, dil in DIL_PAIRS]
    row = lambda v: v.reshape(1, -1).astype(F32)
    ffn1 = (ffn1_norm.astype(F32)[:, None, :], ffn1_w_gate.astype(BF16), ffn1_w_up.astype(BF16),
            ffn1_w_down.astype(BF16))
    ffn2 = (ffn2_norm.astype(F32)[:, None, :], ffn2_w_gate.astype(BF16), ffn2_w_up.astype(BF16),
            ffn2_w_down.astype(BF16))
    wo_a = [w_out[:, hq * HEAD_DIM:(hq + 1) * HEAD_DIM] for hq in A_Q_HEAD_ORDER]
    wo = jnp.concatenate(wo_a + [w_out[:, A_QW:]], axis=1).astype(BF16)
    mix_g = mix_norm.astype(F32)[:, None, :]
    w_proj = _pad_w_in(w_in)
    wvt = jnp.swapaxes(w_in[:, :, A_QW + A_KVW:OFF_B], 1, 2).astype(BF16)
    for l in range(depth):
        h = _ffn(h, l, *ffn1)
        aq, ak, avt, bq, bk, bv, cx, cz, gt = _proj(
            h, l, mix_g, w_proj, wvt, cos, sin,
            row(jnp.tile(a_q_norm[l], A_HEADS)) * scale, row(jnp.tile(a_k_norm[l], A_KV_HEADS)),
            row(jnp.tile(b_q_norm[l], B_HEADS)) * scale, row(jnp.tile(b_k_norm[l], B_HEADS)),
            mblk, seq)
        out_a = _attn_a(aq, ak, avt, seq)
        out_b = _attn_b(bq, bk, bv, biases, batch, seq)
        out_c = _delta(cx, cz, gt, c_conv[l].astype(F32), _pair_row(c_A_log[l]), _pair_row(c_dt_bias[l]),
                       row(jnp.tile(c_out_norm[l], 2)), batch, seq)
        h = _mix_out_ffn(h, out_a, out_b, out_c, l, wo, *ffn2)
    return h.reshape(batch, seq, d)
```

```python
import functools
import math

import numpy as np
import jax
import jax.numpy as jnp
from jax import lax
from jax.experimental import pallas as pl
from jax.experimental.pallas import tpu as pltpu

F32 = jnp.float32
BF16 = jnp.bfloat16

D_MODEL = 1024
HEAD_DIM = 64
A_HEADS = 4
A_KV_HEADS = 2
B_HEADS = 6
C_HEADS = 6
D_FF = 2816
GRID_W = 64
ROPE_THETA = 10000.0
DIL_PAIRS = ((128, 1), (512, 4), (2048, 16))
REL_BUCKETS = 32
REL_MAX_DIST = 1024
CONV_K = 5
CHUNK = 64
NORM_EPS = 1e-6
NEG_INF = -1e30
LOG2E = math.log2(math.e)

A_QW = A_HEADS * HEAD_DIM
A_KVW = A_KV_HEADS * HEAD_DIM
B_W = B_HEADS * HEAD_DIM
C_W = C_HEADS * HEAD_DIM
C_PAIRS = C_HEADS // 2
LANES = 128
MXU_W = 256
GATE_W = C_PAIRS * LANES

OFF_A = 0
OFF_B = OFF_A + A_QW + 2 * A_KVW
OFF_C = OFF_B + 3 * B_W
OFF_Z = OFF_C + 3 * C_W
OFF_G = OFF_Z + C_W
N_PROJ = OFF_G + GATE_W

TOKEN_TILE = 512
A_Q_TILE = 256
B_Q_TILE = 128
B_K_TILE = 256
B_SIDE = 64
CONV_TILE = 256
CHUNK_GROUP = 16
VMEM_LIMIT = 56 * 1024 * 1024


def _cparams(*sem):
    return pltpu.CompilerParams(dimension_semantics=sem, vmem_limit_bytes=VMEM_LIMIT)


def _resident(shape):
    return pl.BlockSpec(shape, lambda *_: (0,) * len(shape), pipeline_mode=pl.Buffered(1))


def _dot(a, b):
    return jnp.dot(a, b, preferred_element_type=F32)


def _dot_nt(a, b):
    return lax.dot_general(a, b, (((1,), (1,)), ((), ())), preferred_element_type=F32)


def _dot_tn(a, b):
    return lax.dot_general(a, b, (((0,), (0,)), ((), ())), preferred_element_type=F32)


def _silu(x):
    return x / (1.0 + jnp.exp(-x))


def _split_dot(x, m):
    hi = x.astype(BF16)
    lo = (x - hi.astype(F32)).astype(BF16)
    return _dot(hi, m) + _dot(lo, m)


def _rms_rows(x, g):
    ms = jnp.mean(x * x, axis=-1, keepdims=True)
    return x * lax.rsqrt(ms + NORM_EPS) * g


def _layer_resident(stacked, layer):
    shape = stacked.shape[1:]
    return pl.BlockSpec((None,) + shape, lambda *_: (layer,) + (0,) * len(shape),
                        pipeline_mode=pl.Buffered(1))


def _swiglu_residual(x, g_ref, wg_ref, wu_ref, wd_ref):
    xn = _rms_rows(x, g_ref[...]).astype(BF16)
    gate = _dot(xn, wg_ref[...])
    up = _dot(xn, wu_ref[...])
    act = (_silu(gate) * up).astype(BF16)
    return x + 0.5 * _dot(act, wd_ref[...])


def _ffn_kernel(h_ref, g_ref, wg_ref, wu_ref, wd_ref, o_ref):
    o_ref[...] = _swiglu_residual(h_ref[...], g_ref, wg_ref, wu_ref, wd_ref)


def _ffn(h, layer, g, wg, wu, wd):
    t, d = h.shape
    tm = min(TOKEN_TILE, t)
    return pl.pallas_call(
        _ffn_kernel,
        out_shape=jax.ShapeDtypeStruct((t, d), F32),
        grid=(t // tm,),
        in_specs=[pl.BlockSpec((tm, d), lambda i: (i, 0))]
                 + [_layer_resident(p, layer) for p in (g, wg, wu, wd)],
        out_specs=pl.BlockSpec((tm, d), lambda i: (i, 0)),
        compiler_params=_cparams("parallel"),
        name="ffn",
    )(h, g, wg, wu, wd)


def _rope(x, cos, sin):
    w = x.shape[1]
    lane = lax.broadcasted_iota(jnp.int32, x.shape, 1)
    first = (lane & (HEAD_DIM - 1)) < HEAD_DIM // 2
    half = HEAD_DIM // 2
    partner = jnp.where(first, pltpu.roll(x, w - half, 1), pltpu.roll(x, half, 1))
    return x * cos + partner * sin


def _proj_kernel(h_ref, g_ref, w_ref, wvt_ref, cos_ref, sin_ref, gaq_ref, gak_ref, gbq_ref, gbk_ref,
                 mblk_ref, aq_o, ak_o, avt_o, bq_o, bk_o, bv_o, cx_o, cz_o, gt_o):
    xn = _rms_rows(h_ref[...], g_ref[...]).astype(BF16)
    pr = _dot(xn, w_ref[...])
    avt_o[...] = _dot_nt(wvt_ref[...], xn).astype(BF16)

    def unit_rms(lo, width):
        outs = []
        for c in range(lo, lo + width, MXU_W):
            w = min(MXU_W, lo + width - c)
            x = pr[:, c:c + w]
            outs.append(x * lax.rsqrt(_split_dot(x * x, mblk_ref[:w, :w]) + NORM_EPS))
        return outs[0] if len(outs) == 1 else jnp.concatenate(outs, axis=-1)

    cos = cos_ref[...]
    sin = sin_ref[...]
    aq_o[...] = _rope(unit_rms(OFF_A, A_QW) * gaq_ref[...], cos, sin).astype(BF16)
    ak = unit_rms(OFF_A + A_QW, A_KVW) * gak_ref[...]
    ak_o[...] = _rope(ak, cos[:, :A_KVW], sin[:, :A_KVW]).astype(BF16)
    bqk = unit_rms(OFF_B, 2 * B_W)
    bq_o[...] = bqk[:, :B_W] * gbq_ref[...]
    bk_o[...] = bqk[:, B_W:] * gbk_ref[...]
    bv_o[...] = pr[:, OFF_B + 2 * B_W:OFF_C]
    cx_o[...] = pr[:, OFF_C:OFF_Z]
    cz_o[...] = pr[:, OFF_Z:OFF_G].astype(BF16)
    gt_o[...] = pr[:, OFF_G:]


def _proj(h, layer, g, w, wvt, cos, sin, gaq, gak, gbq, gbk, mblk, seq):
    t, d = h.shape
    tm = min(TOKEN_TILE, seq)
    per_seq = seq // tm
    row = lambda i: (i, 0)
    pos = lambda i: (i % per_seq, 0)
    widths = (A_QW, A_KVW, None, B_W, B_W, B_W, 3 * C_W, C_W, GATE_W)
    dtypes = (BF16, BF16, BF16, F32, F32, F32, F32, BF16, F32)
    shapes = [(t, wd) if wd else (A_KVW, t) for wd in widths]
    specs = [pl.BlockSpec((tm, wd), row) if wd else pl.BlockSpec((A_KVW, tm), lambda i: (0, i))
             for wd in widths]
    return pl.pallas_call(
        _proj_kernel,
        out_shape=[jax.ShapeDtypeStruct(s, dt) for s, dt in zip(shapes, dtypes)],
        grid=(t // tm,),
        in_specs=[pl.BlockSpec((tm, d), row),
                  _layer_resident(g, layer), _layer_resident(w, layer), _layer_resident(wvt, layer),
                  pl.BlockSpec((tm, A_QW), pos),
                  pl.BlockSpec((tm, A_QW), pos),
                  _resident((1, A_QW)), _resident((1, A_KVW)), _resident((1, B_W)),
                  _resident((1, B_W)), _resident((MXU_W, MXU_W))],
        out_specs=specs,
        compiler_params=_cparams("parallel"),
        name="mix_in",
    )(h, g, w, wvt, cos, sin, gaq, gak, gbq, gbk, mblk)


A_Q_HEAD_ORDER = tuple(hk * (A_HEADS // A_KV_HEADS) + g
                       for g in range(A_HEADS // A_KV_HEADS) for hk in range(A_KV_HEADS))


def _attn_a_kernel(q_ref, k_ref, vt_ref, o_ref):
    k2 = k_ref[...]
    lane_hi = lax.broadcasted_iota(jnp.int32, (1, LANES), 1) >= HEAD_DIM
    row_hi = lax.broadcasted_iota(jnp.int32, (LANES, 1), 0) >= HEAD_DIM
    vt = vt_ref[...].astype(F32)
    vt_ones = [jnp.where(row_hi == (hk == 1), vt, 1.0).astype(BF16) for hk in range(A_KV_HEADS)]
    heads = [(g, hk) for g in range(A_HEADS // A_KV_HEADS) for hk in range(A_KV_HEADS)]

    def scores(g, hk):
        qb = q_ref[:, g * LANES:(g + 1) * LANES].astype(F32)
        qm = jnp.where(lane_hi == (hk == 1), qb, 0.0).astype(BF16)
        return _dot_nt(k2, qm)

    st_next = scores(*heads[0])
    halves = []
    for i, (g, hk) in enumerate(heads):
        st = st_next
        if i + 1 < len(heads):
            st_next = scores(*heads[i + 1])
        m = jnp.max(st, axis=0, keepdims=True)
        p = jnp.exp2(st - m).astype(BF16)
        ov = _dot(vt_ones[hk], p)
        den_row = (1 - hk) * HEAD_DIM
        halves.append(ov / ov[den_row:den_row + 1, :])
        if hk == A_KV_HEADS - 1:
            ot = jnp.where(row_hi, halves[1], halves[0])
            o_ref[:, g * LANES:(g + 1) * LANES] = ot.T.astype(BF16)
            halves = []


def _attn_a(q, k, vt, seq):
    t = q.shape[0]
    tq = min(A_Q_TILE, seq)
    per_seq = seq // tq
    return pl.pallas_call(
        _attn_a_kernel,
        out_shape=jax.ShapeDtypeStruct((t, A_QW), BF16),
        grid=(t // seq, per_seq),
        in_specs=[pl.BlockSpec((tq, A_QW), lambda b, i: (b * per_seq + i, 0)),
                  pl.BlockSpec((seq, A_KVW), lambda b, i: (b, 0)),
                  pl.BlockSpec((A_KVW, seq), lambda b, i: (0, b))],
        out_specs=pl.BlockSpec((tq, A_QW), lambda b, i: (b * per_seq + i, 0)),
        compiler_params=_cparams("parallel", "parallel"),
        name="attn_a",
    )(q, k, vt)


def _t5_bucket_np(rel):
    half = REL_BUCKETS // 2
    exact = half // 2
    sign = np.where(rel > 0, half, 0)
    n = np.abs(rel)
    nf = np.maximum(n, 1).astype(np.float32)
    large = exact + (np.log(nf / np.float32(exact)) / np.float32(math.log(REL_MAX_DIST / exact))
                     * np.float32(half - exact)).astype(np.int32)
    large = np.minimum(large, half - 1)
    return sign + np.where(n < exact, n, large)


def _branch_tiles(sub_len):
    if sub_len >= B_K_TILE:
        return B_Q_TILE, B_K_TILE, (0, -B_SIDE, B_Q_TILE - B_K_TILE)
    return sub_len, sub_len, (0,)


def _branch_bias(rel_bias, dil, sub_len):
    tq, tk, offs = _branch_tiles(sub_len)
    period = tq + tk
    slot = np.arange(period)
    tabs = []
    for off in offs:
        delta = np.where(slot < tk, slot, slot - period) + off
        bucket = _t5_bucket_np(delta * dil)
        vec = jnp.where((np.abs(delta) <= B_SIDE)[:, None], rel_bias.astype(F32)[bucket] * LOG2E, NEG_INF).T
        flat = jnp.tile(vec, (1, tq))[:, :tq * (period - 1)]
        tabs.append(flat.reshape(B_HEADS, tq, period - 1)[:, :, :tk])
    return jnp.stack(tabs)


def _attn_b_kernel(q_ref, k_ref, v_ref, b0_ref, b1_ref, b2_ref, o_ref, num_s, mx_s, den_s, *, seq):
    head0 = lax.broadcasted_iota(jnp.int32, (1, LANES), 1) < HEAD_DIM
    pick = lambda a, b: jnp.where(head0, a, b)

    tiles = []
    for g, ((_, dil), bias_ref) in enumerate(zip(DIL_PAIRS, (b0_ref, b1_ref, b2_ref))):
        sub_len = seq // dil
        tq, tk, _ = _branch_tiles(sub_len)
        n_tiles = sub_len // tq
        for r in range(dil):
            for t in range(n_tiles):
                if n_tiles == 1:
                    k0, tab = 0, 0
                else:
                    k0 = min(max(t * tq - B_SIDE, 0), sub_len - tk)
                    tab = 0 if t == 0 else 2 if t == n_tiles - 1 else 1
                tiles.append((g, bias_ref, tab, pl.ds(r + dil * t * tq, tq, stride=dil),
                              pl.ds(r + dil * k0, tk, stride=dil)))

    for g, bias_ref, tab, rows, keys in tiles:
        tq = rows.size
        qt = q_ref[rows, :]
        q2 = jnp.concatenate([pick(qt, 0.0), pick(0.0, qt)], axis=0).astype(BF16)
        s = _dot_nt(q2, k_ref[keys, :].astype(BF16)) + bias_ref[tab]
        m = jnp.max(s, axis=-1, keepdims=True)
        p = jnp.exp2(s - m)
        l = jnp.sum(p, axis=-1, keepdims=True)
        pv = _dot(p.astype(BF16), v_ref[keys, :].astype(BF16))
        num_s.at[g][rows, :] = pick(pv[:tq], pv[tq:])
        mx_s.at[g][rows, :] = pick(m[:tq], m[tq:])
        den_s.at[g][rows, :] = pick(l[:tq], l[tq:])

    def merge(t, carry):
        rows = pl.ds(pl.multiple_of(t * CONV_TILE, CONV_TILE), CONV_TILE)
        ms = [mx_s[g, rows, :] for g in range(3)]
        top = jnp.maximum(jnp.maximum(ms[0], ms[1]), ms[2])
        es = [jnp.exp2(m - top) for m in ms]
        num = sum(e * num_s[g, rows, :] for g, e in enumerate(es))
        den = sum(e * den_s[g, rows, :] for g, e in enumerate(es))
        o_ref[rows, :] = (num / den).astype(BF16)
        return carry

    lax.fori_loop(0, seq // CONV_TILE, merge, 0)


def _attn_b(q, k, v, biases, batch, seq):
    t = q.shape[0]
    pair = pl.BlockSpec((seq, LANES), lambda b, p: (b, p))
    biases = [x.reshape(x.shape[0], B_HEADS // 2, 2 * x.shape[2], x.shape[3]) for x in biases]
    bias_spec = lambda x: pl.BlockSpec((x.shape[0], None) + x.shape[2:], lambda b, p: (0, p, 0, 0))
    return pl.pallas_call(
        functools.partial(_attn_b_kernel, seq=seq),
        out_shape=jax.ShapeDtypeStruct((t, B_W), BF16),
        grid=(batch, B_HEADS // 2),
        in_specs=[pair, pair, pair] + [bias_spec(x) for x in biases],
        out_specs=pair,
        scratch_shapes=[pltpu.VMEM((len(DIL_PAIRS), seq, LANES), F32)] * 3,
        compiler_params=_cparams("parallel", "parallel"),
        name="attn_b",
    )(q, k, v, *biases)


BASE_BLOCK = 8


def _pair_block_diag(x, same_head):
    reps = same_head.shape[0] // x.shape[0]
    return jnp.where(same_head, jnp.concatenate([x] * reps, axis=0), 0.0).astype(BF16)


def _inv_unit_triangular_pairs(ms, same_block, same_head, eye):
    bd = lambda x: _pair_block_diag(x, same_head)
    b16 = lambda x: x.astype(BF16)
    m8 = [b16(jnp.where(same_block[BASE_BLOCK], m, 0.0)) for m in ms]
    p2 = [_dot(a, bd(a.astype(F32))) for a in m8]
    p2h = [b16(p) for p in p2]
    p4 = [_dot(a, bd(p)) for a, p in zip(p2h, p2)]
    p6 = [_dot(a, bd(p)) for a, p in zip(p2h, p4)]
    even = [eye + a + b + c for a, b, c in zip(p2, p4, p6)]
    inv = [e - _dot(a, bd(e)) for a, e in zip(m8, even)]
    size = BASE_BLOCK
    while size < CHUNK:
        joins = same_block[2 * size] & jnp.logical_not(same_block[size])
        t = [_dot(b16(jnp.where(joins, m, 0.0)), bd(x)) for m, x in zip(ms, inv)]
        inv = [x - _dot(b16(x), bd(y)) for x, y in zip(inv, t)]
        size *= 2
    return inv


def _delta_kernel(xq_ref, xk_ref, xv_ref, wq_ref, wk_ref, wv_ref, z_ref, gt_ref,
                  alog_ref, dtb_ref, gain_ref, o_ref,
                  pad_s, q_s, k_s, v_s, beta_s, g_s, st_s,
                  u_s, lq_s, a_s, kw_s, ku_s, sn_s, eg_s, *, seq):
    n_chunks = seq // CHUNK
    pad = 8
    sq_r = lax.broadcasted_iota(jnp.int32, (LANES, LANES), 0)
    sq_c = lax.broadcasted_iota(jnp.int32, (LANES, LANES), 1)
    same_head = (sq_r < HEAD_DIM) == (sq_c < HEAD_DIM)
    head_ones = same_head.astype(BF16)

    zeros = jnp.zeros((pad, LANES), F32)
    for j, (x_ref, w_ref, dst, scale) in enumerate(((xq_ref, wq_ref, q_s, HEAD_DIM ** -0.5),
                                                     (xk_ref, wk_ref, k_s, 1.0),
                                                     (xv_ref, wv_ref, v_s, None))):
        pad_s[j, 0:pad, :] = zeros
        pad_s[j, pad + seq:pad + seq + pad, :] = zeros
        pad_s[j, pad:pad + seq, :] = x_ref[...]
        for r0 in range(0, seq, CONV_TILE):
            acc = jnp.zeros((CONV_TILE, LANES), F32)
            for d in range(CONV_K):
                lo = r0 + pad - CONV_K // 2 + d
                acc = acc + w_ref[d:d + 1, :] * pad_s[j, lo:lo + CONV_TILE, :]
            y = _silu(acc)
            if scale is not None:
                y = y * lax.rsqrt(_split_dot(y * y, head_ones) + NORM_EPS) * scale
            dst[r0:r0 + CONV_TILE, :] = y

    gt = gt_ref[...]
    beta_s[...] = 1.0 / (1.0 + jnp.exp(-gt))
    gx = gt + dtb_ref[0]
    softplus = jnp.maximum(gx, 0.0) + jnp.log(1.0 + jnp.exp(-jnp.abs(gx)))
    g_s[...] = -jnp.exp(alog_ref[0]) * softplus
    st_s[...] = jnp.zeros(st_s.shape, F32)

    wide = 2 * LANES
    lane_w = lax.broadcasted_iota(jnp.int32, (CHUNK, wide), 1)
    rows = lax.broadcasted_iota(jnp.int32, (CHUNK, wide), 0)
    cols = lane_w & (HEAD_DIM - 1)
    fwd = lane_w < LANES
    fwd_row = lax.broadcasted_iota(jnp.int32, (1, wide), 1) < LANES
    head0 = lax.broadcasted_iota(jnp.int32, (CHUNK, LANES), 1) < HEAD_DIM
    eye_b = rows == cols
    rows_p = lax.broadcasted_iota(jnp.int32, (CHUNK, LANES), 0)
    cols_p = lax.broadcasted_iota(jnp.int32, (CHUNK, LANES), 1) & (HEAD_DIM - 1)
    eye = (rows_p == cols_p).astype(F32)
    same_block = {}
    size = BASE_BLOCK
    while size <= CHUNK:
        shift = size.bit_length() - 1
        same_block[size] = (rows_p >> shift) == (cols_p >> shift)
        size *= 2
    r2 = lax.broadcasted_iota(jnp.int32, (2 * CHUNK, CHUNK), 0)
    c2 = lax.broadcasted_iota(jnp.int32, (2 * CHUNK, CHUNK), 1)
    tri2 = (jnp.where(r2 < CHUNK, r2 - c2, c2 - (r2 - CHUNK)) >= 0).astype(BF16)
    below = jnp.where(fwd, rows - cols, cols - rows)
    incl = below >= 0
    strict = below > 0
    bd = lambda x: _pair_block_diag(x, same_head)
    both = lambda x: jnp.concatenate([x, x], axis=-1)
    halves = (slice(0, LANES), slice(LANES, wide))

    def spread(x, c0):
        col = lambda c: jnp.broadcast_to(x[:, c:c + 1], (CHUNK, LANES))
        return jnp.concatenate([jnp.where(head0, col(c0), col(c0 + 1)),
                                jnp.where(head0, col(c0 + 2), col(c0 + 3))], axis=-1)

    def chunk_group(it, carry):
        ns = [it * CHUNK_GROUP + g for g in range(CHUNK_GROUP)]
        sls = [pl.ds(n * CHUNK, CHUNK) for n in ns]
        k2 = [k_s[sl, :] for sl in sls]
        q2 = [q_s[sl, :] for sl in sls]
        gsum = []
        for sl in sls:
            g2 = g_s[sl, :]
            hi = g2.astype(BF16)
            lo = (g2 - hi.astype(F32)).astype(BF16)
            gsum.append(_dot(tri2, jnp.concatenate([hi, lo], axis=-1)))
        kq = [_dot_nt(jnp.concatenate([k, q], axis=0).astype(BF16), bd(k)) for k, q in zip(k2, q2)]
        beta, egc, kd, ms = [], [], [], []
        for g, sl in enumerate(sls):
            gs = gsum[g][:, :LANES] + gsum[g][:, LANES:]
            gcum = jnp.concatenate([gs[:CHUNK], gs[CHUNK:]], axis=-1)
            beta.append(spread(beta_s[sl, :], 0))
            gc = jnp.where(fwd, spread(gcum[:, :LANES], 4), spread(gcum[:, LANES:], 4))
            gc_row = jnp.sum(jnp.where(eye_b, gc, 0.0), axis=0, keepdims=True)
            decay = jnp.where(incl, jnp.exp(gc - gc_row), 0.0)
            ms.append(jnp.where(strict, beta[g] * both(kq[g][:CHUNK]) * decay, 0.0))
            a = (both(kq[g][CHUNK:]) * decay).astype(BF16)
            egc.append(jnp.exp(gc))
            g_last = jnp.where(fwd_row, gc[CHUNK - 1:CHUNK, :], gc[0:1, :])
            kd.append((both(k2[g]) * jnp.exp(g_last - gc)).astype(BF16))
            eg = jnp.exp(g_last)
            rows8 = pl.ds(ns[g] * 8, 8)
            for d in range(2):
                a_s[d, sl, :] = a[:, d * LANES:(d + 1) * LANES]
                eg_s[d, rows8, :] = jnp.broadcast_to(eg[:, d * LANES:(d + 1) * LANES], (8, LANES))
        chains = [(g, d) for g in range(CHUNK_GROUP) for d in range(2)]
        inv = _inv_unit_triangular_pairs([ms[g][:, halves[d]] for g, d in chains], same_block, same_head, eye)
        ys = [jnp.concatenate([bd((k2[g] * (beta[g] * egc[g])[:, halves[d]])),
                               bd(v_s[sls[g], :] * beta[g][:, halves[d]])], axis=-1) for g, d in chains]
        wus = [_dot(x.astype(BF16), y) for x, y in zip(inv, ys)]
        kwu = [_dot_tn(kd[g][:, halves[d]], wu.astype(BF16)) for wu, (g, d) in zip(wus, chains)]
        for wu, x, (g, d) in zip(wus, kwu, chains):
            u_s[d, sls[g], :] = wu[:, LANES:]
            qg = q2[g] * egc[g][:, halves[d]]
            lq_s[d, ns[g]] = jnp.concatenate([wu[:, :LANES], qg], axis=0).astype(BF16)
            kw_s[d, ns[g]] = jnp.where(same_head, x[:, :LANES], 0.0).astype(BF16)
            ku_s[d, ns[g]] = jnp.where(same_head, x[:, LANES:], 0.0)
        return carry

    for it in range(n_chunks // CHUNK_GROUP):
        chunk_group(it, 0)

    def scan_step(i, carry):
        ns = (i, n_chunks - 1 - i)
        state = [st_s[d] for d in range(2)]
        s16 = [s.astype(BF16) for s in state]
        drop = [_dot(kw_s[d, ns[d]], s16[d]) for d in range(2)]
        for d in range(2):
            eg = eg_s[d, pl.ds(pl.multiple_of(ns[d] * 8, 8), 8), :][0:1]
            sn_s[d, ns[d]] = s16[d]
            st_s[d] = state[d] * eg - drop[d] + ku_s[d, ns[d]]
        return carry

    lax.fori_loop(0, n_chunks, scan_step, 0)

    for it in range(n_chunks // CHUNK_GROUP):
        todo = [(it * CHUNK_GROUP + g, d) for g in range(CHUNK_GROUP) for d in range(2)]
        sls = [pl.ds(n * CHUNK, CHUNK) for n, _ in todo]
        rs = [_dot(lq_s[d, n], sn_s[d, n]) for n, d in todo]
        v_new = [u_s[d, sl, :] - r[:CHUNK] for r, sl, (_, d) in zip(rs, sls, todo)]
        outs = [r[CHUNK:] + _dot(a_s[d, sl, :], bd(v)) for r, v, sl, (_, d) in zip(rs, v_new, sls, todo)]
        for g in range(CHUNK_GROUP):
            sl = sls[2 * g]
            o = outs[2 * g] + outs[2 * g + 1]
            ms = _split_dot(o * o, head_ones) * (1.0 / HEAD_DIM)
            o = o * lax.rsqrt(ms + NORM_EPS) * gain_ref[...]
            o_ref[sl, :] = (o * _silu(z_ref[sl, :].astype(F32))).astype(BF16)


def _split_dot_left(m, x):
    hi = x.astype(BF16)
    lo = (x - hi.astype(F32)).astype(BF16)
    return _dot(m, hi) + _dot(m, lo)


def _delta(cx, cz, gt, conv_w, alog, dtb, gain, batch, seq):
    t = cx.shape[0]
    xspec = lambda part: pl.BlockSpec((seq, LANES), lambda b, p: (b, part * C_PAIRS + p))
    wspec = lambda part: pl.BlockSpec((CONV_K, LANES), lambda b, p: (0, part * C_PAIRS + p))
    pair = pl.BlockSpec((seq, LANES), lambda b, p: (b, p))
    prow = pl.BlockSpec((1, 1, LANES), lambda b, p: (p, 0, 0))
    return pl.pallas_call(
        functools.partial(_delta_kernel, seq=seq),
        out_shape=jax.ShapeDtypeStruct((t, C_W), BF16),
        grid=(batch, C_PAIRS),
        in_specs=[xspec(0), xspec(1), xspec(2), wspec(0), wspec(1), wspec(2),
                  pair, pair, prow, prow,
                  _resident((1, LANES))],
        out_specs=pair,
        scratch_shapes=[pltpu.VMEM((3, seq + 16, LANES), F32),
                        pltpu.VMEM((seq, LANES), F32),
                        pltpu.VMEM((seq, LANES), F32),
                        pltpu.VMEM((seq, LANES), F32),
                        pltpu.VMEM((seq, LANES), F32),
                        pltpu.VMEM((seq, LANES), F32),
                        pltpu.VMEM((2, LANES, LANES), F32),
                        pltpu.VMEM((2, seq, LANES), F32),
                        pltpu.VMEM((2, seq // CHUNK, LANES, LANES), BF16),
                        pltpu.VMEM((2, seq, LANES), BF16),
                        pltpu.VMEM((2, seq // CHUNK, LANES, LANES), BF16),
                        pltpu.VMEM((2, seq // CHUNK, LANES, LANES), F32),
                        pltpu.VMEM((2, seq // CHUNK, LANES, LANES), BF16),
                        pltpu.VMEM((2, seq // CHUNK * 8, LANES), F32)],
        compiler_params=_cparams("parallel", "parallel"),
        name="delta",
    )(cx, cx, cx, conv_w, conv_w, conv_w, cz, gt, alog, dtb, gain)


def _out_ffn_kernel(h_ref, a_ref, b_ref, c_ref, w_ref, g_ref, wg_ref, wu_ref, wd_ref, o_ref):
    y = _dot(a_ref[...], w_ref[0:A_QW, :])
    y = y + _dot(b_ref[...], w_ref[A_QW:A_QW + B_W, :])
    y = y + _dot(c_ref[...], w_ref[A_QW + B_W:, :])
    o_ref[...] = _swiglu_residual(h_ref[...] + y, g_ref, wg_ref, wu_ref, wd_ref)


def _mix_out_ffn(h, a, b, c, layer, w, g, wg, wu, wd):
    t, d = h.shape
    tm = min(TOKEN_TILE, t)
    row = lambda width: pl.BlockSpec((tm, width), lambda i: (i, 0))
    return pl.pallas_call(
        _out_ffn_kernel,
        out_shape=jax.ShapeDtypeStruct((t, d), F32),
        grid=(t // tm,),
        in_specs=[row(d), row(A_QW), row(B_W), row(C_W)]
                 + [_layer_resident(p, layer) for p in (w, g, wg, wu, wd)],
        out_specs=row(d),
        compiler_params=_cparams("parallel"),
        name="mix_out_ffn",
    )(h, a, b, c, w, g, wg, wu, wd)


def _rope_tables(seq):
    rows = seq // GRID_W
    row = jnp.repeat(jnp.arange(rows), GRID_W).astype(F32)
    col = jnp.tile(jnp.arange(GRID_W), rows).astype(F32)
    n_freq = HEAD_DIM // 4
    inv = ROPE_THETA ** (-jnp.arange(n_freq, dtype=F32) / n_freq)
    ang = jnp.concatenate([row[:, None] * inv, col[:, None] * inv], axis=-1)
    cos, sin = jnp.cos(ang), jnp.sin(ang)
    cos_h = jnp.concatenate([cos, cos], axis=-1)
    sin_h = jnp.concatenate([-sin, sin], axis=-1)
    return jnp.tile(cos_h, (1, A_HEADS)), jnp.tile(sin_h, (1, A_HEADS))


def _pair_columns(x):
    lead = x.shape[:-1]
    x = x.reshape(lead + (2, C_PAIRS, 2))
    return jnp.moveaxis(x, -2, -3).reshape(lead + (C_PAIRS, 4))


def _pad_w_in(w_in):
    lead = w_in.shape[:-1]
    w_in = w_in.astype(BF16)
    aq = [w_in[..., hq * HEAD_DIM:(hq + 1) * HEAD_DIM] for hq in A_Q_HEAD_ORDER]
    cb = _pair_columns(w_in[..., OFF_G:OFF_G + 2 * C_HEADS])
    ca = _pair_columns(w_in[..., OFF_G + 2 * C_HEADS:])
    gates = jnp.concatenate([cb, ca, jnp.zeros(lead + (C_PAIRS, LANES - 8), BF16)], axis=-1)
    return jnp.concatenate(aq + [w_in[..., A_QW:OFF_G], gates.reshape(lead + (GATE_W,))], axis=-1)


def _pair_row(p):
    v = _pair_columns(p.reshape(2 * C_HEADS).astype(F32))
    z4 = jnp.zeros((C_PAIRS, 4), F32)
    return jnp.concatenate([z4, v, jnp.zeros((C_PAIRS, LANES - 8), F32)], axis=-1)[:, None, :]


def kernel(x, rel_bias, ffn1_norm, ffn1_w_gate, ffn1_w_up, ffn1_w_down, mix_norm, w_in, a_q_norm, a_k_norm, b_q_norm, b_k_norm, c_conv, c_A_log, c_dt_bias, c_out_norm, w_out, ffn2_norm, ffn2_w_gate, ffn2_w_up, ffn2_w_down):
    batch, seq, d = x.shape
    depth = w_in.shape[0]
    scale = HEAD_DIM ** -0.5 * LOG2E
    h = x.reshape(batch * seq, d)
    cos, sin = _rope_tables(seq)
    idx = np.arange(MXU_W)
    mblk = jnp.asarray((idx[:, None] // HEAD_DIM == idx[None, :] // HEAD_DIM) / HEAD_DIM, BF16)
    biases = [_branch_bias(rel_bias, dil, seq // dil) for _, dil in DIL_PAIRS]
    row = lambda v: v.reshape(1, -1).astype(F32)
    ffn1 = (ffn1_norm.astype(F32)[:, None, :], ffn1_w_gate.astype(BF16), ffn1_w_up.astype(BF16),
            ffn1_w_down.astype(BF16))
    ffn2 = (ffn2_norm.astype(F32)[:, None, :], ffn2_w_gate.astype(BF16), ffn2_w_up.astype(BF16),
            ffn2_w_down.astype(BF16))
    wo_a = [w_out[:, hq * HEAD_DIM:(hq + 1) * HEAD_DIM] for hq in A_Q_HEAD_ORDER]
    wo = jnp.concatenate(wo_a + [w_out[:, A_QW:]], axis=1).astype(BF16)
    mix_g = mix_norm.astype(F32)[:, None, :]
    w_proj = _pad_w_in(w_in)
    wvt = jnp.swapaxes(w_in[:, :, A_QW + A_KVW:OFF_B], 1, 2).astype(BF16)
    for l in range(depth):
        h = _ffn(h, l, *ffn1)
        aq, ak, avt, bq, bk, bv, cx, cz, gt = _proj(
            h, l, mix_g, w_proj, wvt, cos, sin,
            row(jnp.tile(a_q_norm[l], A_HEADS)) * scale, row(jnp.tile(a_k_norm[l], A_KV_HEADS)),
            row(jnp.tile(b_q_norm[l], B_HEADS)) * scale, row(jnp.tile(b_k_norm[l], B_HEADS)),
            mblk, seq)
        out_a = _attn_a(aq, ak, avt, seq)
        out_b = _attn_b(bq, bk, bv, biases, batch, seq)
        out_c = _delta(cx, cz, gt, c_conv[l].astype(F32), _pair_row(c_A_log[l]), _pair_row(c_dt_bias[l]),
                       row(jnp.tile(c_out_norm[l], 2)), batch, seq)
        h = _mix_out_ffn(h, out_a, out_b, out_c, l, wo, *ffn2)
    return h.reshape(batch, seq, d)
```

```python
import functools
import math

import numpy as np
import jax
import jax.numpy as jnp
from jax import lax
from jax.experimental import pallas as pl
from jax.experimental.pallas import tpu as pltpu

F32 = jnp.float32
BF16 = jnp.bfloat16

D_MODEL = 1024
HEAD_DIM = 64
A_HEADS = 4
A_KV_HEADS = 2
B_HEADS = 6
C_HEADS = 6
D_FF = 2816
GRID_W = 64
ROPE_THETA = 10000.0
DIL_PAIRS = ((128, 1), (512, 4), (2048, 16))
REL_BUCKETS = 32
REL_MAX_DIST = 1024
CONV_K = 5
CHUNK = 64
NORM_EPS = 1e-6
NEG_INF = -1e30
LOG2E = math.log2(math.e)

A_QW = A_HEADS * HEAD_DIM
A_KVW = A_KV_HEADS * HEAD_DIM
B_W = B_HEADS * HEAD_DIM
C_W = C_HEADS * HEAD_DIM
C_PAIRS = C_HEADS // 2
LANES = 128
MXU_W = 256
GATE_W = C_PAIRS * LANES

OFF_A = 0
OFF_B = OFF_A + A_QW + 2 * A_KVW
OFF_C = OFF_B + 3 * B_W
OFF_Z = OFF_C + 3 * C_W
OFF_G = OFF_Z + C_W
N_PROJ = OFF_G + GATE_W

TOKEN_TILE = 512
A_Q_TILE = 256
A_K_CHUNK = 256
B_Q_TILE = 128
B_K_TILE = 256
B_SIDE = 64
CONV_TILE = 256
CHUNK_GROUP = 16
VMEM_LIMIT = 56 * 1024 * 1024


def _cparams(*sem):
    return pltpu.CompilerParams(dimension_semantics=sem, vmem_limit_bytes=VMEM_LIMIT)


def _resident(shape):
    return pl.BlockSpec(shape, lambda *_: (0,) * len(shape), pipeline_mode=pl.Buffered(1))


def _dot(a, b):
    return jnp.dot(a, b, preferred_element_type=F32)


def _dot_nt(a, b):
    return lax.dot_general(a, b, (((1,), (1,)), ((), ())), preferred_element_type=F32)


def _dot_tn(a, b):
    return lax.dot_general(a, b, (((0,), (0,)), ((), ())), preferred_element_type=F32)


def _silu(x):
    return x / (1.0 + jnp.exp(-x))


def _split_dot(x, m):
    hi = x.astype(BF16)
    lo = (x - hi.astype(F32)).astype(BF16)
    return _dot(hi, m) + _dot(lo, m)


def _rms_rows(x, g):
    ms = jnp.mean(x * x, axis=-1, keepdims=True)
    return x * lax.rsqrt(ms + NORM_EPS) * g


def _layer_resident(stacked, layer):
    shape = stacked.shape[1:]
    return pl.BlockSpec((None,) + shape, lambda *_: (layer,) + (0,) * len(shape),
                        pipeline_mode=pl.Buffered(1))


def _swiglu_residual(x, g_ref, wg_ref, wu_ref, wd_ref):
    xn = _rms_rows(x, g_ref[...]).astype(BF16)
    gate = _dot(xn, wg_ref[...])
    up = _dot(xn, wu_ref[...])
    act = (_silu(gate) * up).astype(BF16)
    return x + 0.5 * _dot(act, wd_ref[...])


def _ffn_kernel(h_ref, g_ref, wg_ref, wu_ref, wd_ref, o_ref):
    o_ref[...] = _swiglu_residual(h_ref[...], g_ref, wg_ref, wu_ref, wd_ref)


def _ffn(h, layer, g, wg, wu, wd):
    t, d = h.shape
    tm = min(TOKEN_TILE, t)
    return pl.pallas_call(
        _ffn_kernel,
        out_shape=jax.ShapeDtypeStruct((t, d), F32),
        grid=(t // tm,),
        in_specs=[pl.BlockSpec((tm, d), lambda i: (i, 0))]
                 + [_layer_resident(p, layer) for p in (g, wg, wu, wd)],
        out_specs=pl.BlockSpec((tm, d), lambda i: (i, 0)),
        compiler_params=_cparams("parallel"),
        name="ffn",
    )(h, g, wg, wu, wd)


def _rope(x, cos, sin):
    w = x.shape[1]
    lane = lax.broadcasted_iota(jnp.int32, x.shape, 1)
    first = (lane & (HEAD_DIM - 1)) < HEAD_DIM // 2
    half = HEAD_DIM // 2
    partner = jnp.where(first, pltpu.roll(x, w - half, 1), pltpu.roll(x, half, 1))
    return x * cos + partner * sin


def _proj_kernel(h_ref, g_ref, w_ref, wvt_ref, cos_ref, sin_ref, gaq_ref, gak_ref, gbq_ref, gbk_ref,
                 mblk_ref, aq_o, ak_o, avt_o, bq_o, bk_o, bv_o, cx_o, cz_o, gt_o):
    xn = _rms_rows(h_ref[...], g_ref[...]).astype(BF16)
    pr = _dot(xn, w_ref[...])
    avt_o[...] = _dot_nt(wvt_ref[...], xn).astype(BF16)

    def unit_rms(lo, width):
        outs = []
        for c in range(lo, lo + width, MXU_W):
            w = min(MXU_W, lo + width - c)
            x = pr[:, c:c + w]
            outs.append(x * lax.rsqrt(_split_dot(x * x, mblk_ref[:w, :w]) + NORM_EPS))
        return outs[0] if len(outs) == 1 else jnp.concatenate(outs, axis=-1)

    cos = cos_ref[...]
    sin = sin_ref[...]
    aq_o[...] = _rope(unit_rms(OFF_A, A_QW) * gaq_ref[...], cos, sin).astype(BF16)
    ak = unit_rms(OFF_A + A_QW, A_KVW) * gak_ref[...]
    ak_o[...] = _rope(ak, cos[:, :A_KVW], sin[:, :A_KVW]).astype(BF16)
    bqk = unit_rms(OFF_B, 2 * B_W)
    bq_o[...] = bqk[:, :B_W] * gbq_ref[...]
    bk_o[...] = bqk[:, B_W:] * gbk_ref[...]
    bv_o[...] = pr[:, OFF_B + 2 * B_W:OFF_C]
    cx_o[...] = pr[:, OFF_C:OFF_Z]
    cz_o[...] = pr[:, OFF_Z:OFF_G].astype(BF16)
    gt_o[...] = pr[:, OFF_G:]


def _proj(h, layer, g, w, wvt, cos, sin, gaq, gak, gbq, gbk, mblk, seq):
    t, d = h.shape
    tm = min(TOKEN_TILE, seq)
    per_seq = seq // tm
    row = lambda i: (i, 0)
    pos = lambda i: (i % per_seq, 0)
    widths = (A_QW, A_KVW, None, B_W, B_W, B_W, 3 * C_W, C_W, GATE_W)
    dtypes = (BF16, BF16, BF16, F32, F32, F32, F32, BF16, F32)
    shapes = [(t, wd) if wd else (A_KVW, t) for wd in widths]
    specs = [pl.BlockSpec((tm, wd), row) if wd else pl.BlockSpec((A_KVW, tm), lambda i: (0, i))
             for wd in widths]
    return pl.pallas_call(
        _proj_kernel,
        out_shape=[jax.ShapeDtypeStruct(s, dt) for s, dt in zip(shapes, dtypes)],
        grid=(t // tm,),
        in_specs=[pl.BlockSpec((tm, d), row),
                  _layer_resident(g, layer), _layer_resident(w, layer), _layer_resident(wvt, layer),
                  pl.BlockSpec((tm, A_QW), pos),
                  pl.BlockSpec((tm, A_QW), pos),
                  _resident((1, A_QW)), _resident((1, A_KVW)), _resident((1, B_W)),
                  _resident((1, B_W)), _resident((MXU_W, MXU_W))],
        out_specs=specs,
        compiler_params=_cparams("parallel"),
        name="mix_in",
    )(h, g, w, wvt, cos, sin, gaq, gak, gbq, gbk, mblk)


A_Q_HEAD_ORDER = tuple(hk * (A_HEADS // A_KV_HEADS) + g
                       for g in range(A_HEADS // A_KV_HEADS) for hk in range(A_KV_HEADS))


def _attn_a_kernel(q_ref, k_ref, vt_ref, o_ref):
    seq = k_ref.shape[0]
    lane_hi = lax.broadcasted_iota(jnp.int32, (1, LANES), 1) >= HEAD_DIM
    row_hi = lax.broadcasted_iota(jnp.int32, (LANES, 1), 0) >= HEAD_DIM
    heads = [(g, hk) for g in range(A_HEADS // A_KV_HEADS) for hk in range(A_KV_HEADS)]
    qm = []
    for g, hk in heads:
        qb = q_ref[:, g * LANES:(g + 1) * LANES].astype(F32)
        qm.append(jnp.where(lane_hi == (hk == 1), qb, 0.0).astype(BF16))
    def scores(c):
        k_c = k_ref[c:c + A_K_CHUNK, :]
        return [_dot_nt(k_c, q) for q in qm]

    m_run = [None] * len(heads)
    acc = [None] * len(heads)
    sts_next = scores(0)
    for c in range(0, seq, A_K_CHUNK):
        sts = sts_next
        if c + A_K_CHUNK < seq:
            sts_next = scores(c + A_K_CHUNK)
        vt_c = vt_ref[:, c:c + A_K_CHUNK].astype(F32)
        vt_ones = [jnp.where(row_hi == (hk == 1), vt_c, 1.0).astype(BF16) for hk in range(A_KV_HEADS)]
        ps, m_news = [], []
        for i, st in enumerate(sts):
            m_new = jnp.max(st, axis=0, keepdims=True)
            if m_run[i] is not None:
                m_new = jnp.maximum(m_run[i], m_new)
            m_news.append(m_new)
            ps.append(jnp.exp2(st - m_new).astype(BF16))
        for i, (g, hk) in enumerate(heads):
            ov = _dot(vt_ones[hk], ps[i])
            acc[i] = ov if acc[i] is None else acc[i] * jnp.exp2(m_run[i] - m_news[i]) + ov
            m_run[i] = m_news[i]
    for g in range(A_HEADS // A_KV_HEADS):
        halves = []
        for hk in range(A_KV_HEADS):
            ov = acc[g * A_KV_HEADS + hk]
            den_row = (1 - hk) * HEAD_DIM
            halves.append(ov / ov[den_row:den_row + 1, :])
        ot = jnp.where(row_hi, halves[1], halves[0])
        o_ref[:, g * LANES:(g + 1) * LANES] = ot.T.astype(BF16)


def _attn_a(q, k, vt, seq):
    t = q.shape[0]
    tq = min(A_Q_TILE, seq)
    per_seq = seq // tq
    return pl.pallas_call(
        _attn_a_kernel,
        out_shape=jax.ShapeDtypeStruct((t, A_QW), BF16),
        grid=(t // seq, per_seq),
        in_specs=[pl.BlockSpec((tq, A_QW), lambda b, i: (b * per_seq + i, 0)),
                  pl.BlockSpec((seq, A_KVW), lambda b, i: (b, 0)),
                  pl.BlockSpec((A_KVW, seq), lambda b, i: (0, b))],
        out_specs=pl.BlockSpec((tq, A_QW), lambda b, i: (b * per_seq + i, 0)),
        compiler_params=_cparams("parallel", "parallel"),
        name="attn_a",
    )(q, k, vt)


def _t5_bucket_np(rel):
    half = REL_BUCKETS // 2
    exact = half // 2
    sign = np.where(rel > 0, half, 0)
    n = np.abs(rel)
    nf = np.maximum(n, 1).astype(np.float32)
    large = exact + (np.log(nf / np.float32(exact)) / np.float32(math.log(REL_MAX_DIST / exact))
                     * np.float32(half - exact)).astype(np.int32)
    large = np.minimum(large, half - 1)
    return sign + np.where(n < exact, n, large)


def _branch_tiles(sub_len):
    if sub_len >= B_K_TILE:
        return B_Q_TILE, B_K_TILE, (0, -B_SIDE, B_Q_TILE - B_K_TILE)
    return sub_len, sub_len, (0,)


def _branch_bias(rel_bias, dil, sub_len):
    tq, tk, offs = _branch_tiles(sub_len)
    period = tq + tk
    slot = np.arange(period)
    tabs = []
    for off in offs:
        delta = np.where(slot < tk, slot, slot - period) + off
        bucket = _t5_bucket_np(delta * dil)
        vec = jnp.where((np.abs(delta) <= B_SIDE)[:, None], rel_bias.astype(F32)[bucket] * LOG2E, NEG_INF).T
        flat = jnp.tile(vec, (1, tq))[:, :tq * (period - 1)]
        tabs.append(flat.reshape(B_HEADS, tq, period - 1)[:, :, :tk])
    return jnp.stack(tabs)


def _attn_b_kernel(q_ref, k_ref, v_ref, b0_ref, b1_ref, b2_ref, o_ref, num_s, mx_s, den_s, *, seq):
    head0 = lax.broadcasted_iota(jnp.int32, (1, LANES), 1) < HEAD_DIM
    pick = lambda a, b: jnp.where(head0, a, b)

    tiles = []
    for g, ((_, dil), bias_ref) in enumerate(zip(DIL_PAIRS, (b0_ref, b1_ref, b2_ref))):
        sub_len = seq // dil
        tq, tk, _ = _branch_tiles(sub_len)
        n_tiles = sub_len // tq
        for r in range(dil):
            for t in range(n_tiles):
                if n_tiles == 1:
                    k0, tab = 0, 0
                else:
                    k0 = min(max(t * tq - B_SIDE, 0), sub_len - tk)
                    tab = 0 if t == 0 else 2 if t == n_tiles - 1 else 1
                tiles.append((g, bias_ref, tab, pl.ds(r + dil * t * tq, tq, stride=dil),
                              pl.ds(r + dil * k0, tk, stride=dil)))

    for g, bias_ref, tab, rows, keys in tiles:
        tq = rows.size
        qt = q_ref[rows, :]
        q2 = jnp.concatenate([pick(qt, 0.0), pick(0.0, qt)], axis=0).astype(BF16)
        s = _dot_nt(q2, k_ref[keys, :].astype(BF16)) + bias_ref[tab]
        m = jnp.max(s, axis=-1, keepdims=True)
        p = jnp.exp2(s - m)
        l = jnp.sum(p, axis=-1, keepdims=True)
        pv = _dot(p.astype(BF16), v_ref[keys, :].astype(BF16))
        num_s.at[g][rows, :] = pick(pv[:tq], pv[tq:])
        mx_s.at[g][rows, :] = pick(m[:tq], m[tq:])
        den_s.at[g][rows, :] = pick(l[:tq], l[tq:])

    def merge(t, carry):
        rows = pl.ds(pl.multiple_of(t * CONV_TILE, CONV_TILE), CONV_TILE)
        ms = [mx_s[g, rows, :] for g in range(3)]
        top = jnp.maximum(jnp.maximum(ms[0], ms[1]), ms[2])
        es = [jnp.exp2(m - top) for m in ms]
        num = sum(e * num_s[g, rows, :] for g, e in enumerate(es))
        den = sum(e * den_s[g, rows, :] for g, e in enumerate(es))
        o_ref[rows, :] = (num / den).astype(BF16)
        return carry

    lax.fori_loop(0, seq // CONV_TILE, merge, 0)


def _attn_b(q, k, v, biases, batch, seq):
    t = q.shape[0]
    pair = pl.BlockSpec((seq, LANES), lambda b, p: (b, p))
    biases = [x.reshape(x.shape[0], B_HEADS // 2, 2 * x.shape[2], x.shape[3]) for x in biases]
    bias_spec = lambda x: pl.BlockSpec((x.shape[0], None) + x.shape[2:], lambda b, p: (0, p, 0, 0))
    return pl.pallas_call(
        functools.partial(_attn_b_kernel, seq=seq),
        out_shape=jax.ShapeDtypeStruct((t, B_W), BF16),
        grid=(batch, B_HEADS // 2),
        in_specs=[pair, pair, pair] + [bias_spec(x) for x in biases],
        out_specs=pair,
        scratch_shapes=[pltpu.VMEM((len(DIL_PAIRS), seq, LANES), F32)] * 3,
        compiler_params=_cparams("parallel", "parallel"),
        name="attn_b",
    )(q, k, v, *biases)


BASE_BLOCK = 8


def _pair_block_diag(x, same_head):
    reps = same_head.shape[0] // x.shape[0]
    return jnp.where(same_head, jnp.concatenate([x] * reps, axis=0), 0.0).astype(BF16)


def _inv_unit_triangular_pairs(ms, same_block, same_head, eye):
    bd = lambda x: _pair_block_diag(x, same_head)
    b16 = lambda x: x.astype(BF16)
    m8 = [b16(jnp.where(same_block[BASE_BLOCK], m, 0.0)) for m in ms]
    p2 = [_dot(a, bd(a.astype(F32))) for a in m8]
    p2h = [b16(p) for p in p2]
    p4 = [_dot(a, bd(p)) for a, p in zip(p2h, p2)]
    p6 = [_dot(a, bd(p)) for a, p in zip(p2h, p4)]
    even = [eye + a + b + c for a, b, c in zip(p2, p4, p6)]
    inv = [e - _dot(a, bd(e)) for a, e in zip(m8, even)]
    size = BASE_BLOCK
    while size < CHUNK:
        joins = same_block[2 * size] & jnp.logical_not(same_block[size])
        t = [_dot(b16(jnp.where(joins, m, 0.0)), bd(x)) for m, x in zip(ms, inv)]
        inv = [x - _dot(b16(x), bd(y)) for x, y in zip(inv, t)]
        size *= 2
    return inv


def _delta_kernel(xq_ref, xk_ref, xv_ref, wq_ref, wk_ref, wv_ref, z_ref, gt_ref,
                  alog_ref, dtb_ref, gain_ref, o_ref,
                  pad_s, q_s, k_s, v_s, beta_s, g_s, st_s,
                  u_s, lq_s, a_s, kw_s, ku_s, sn_s, eg_s, *, seq):
    n_chunks = seq // CHUNK
    pad = 8
    sq_r = lax.broadcasted_iota(jnp.int32, (LANES, LANES), 0)
    sq_c = lax.broadcasted_iota(jnp.int32, (LANES, LANES), 1)
    same_head = (sq_r < HEAD_DIM) == (sq_c < HEAD_DIM)
    head_ones = same_head.astype(BF16)

    zeros = jnp.zeros((pad, LANES), F32)
    for j, (x_ref, w_ref, dst, scale) in enumerate(((xq_ref, wq_ref, q_s, HEAD_DIM ** -0.5),
                                                     (xk_ref, wk_ref, k_s, 1.0),
                                                     (xv_ref, wv_ref, v_s, None))):
        pad_s[j, 0:pad, :] = zeros
        pad_s[j, pad + seq:pad + seq + pad, :] = zeros
        pad_s[j, pad:pad + seq, :] = x_ref[...]
        for r0 in range(0, seq, CONV_TILE):
            acc = jnp.zeros((CONV_TILE, LANES), F32)
            for d in range(CONV_K):
                lo = r0 + pad - CONV_K // 2 + d
                acc = acc + w_ref[d:d + 1, :] * pad_s[j, lo:lo + CONV_TILE, :]
            y = _silu(acc)
            if scale is not None:
                y = y * lax.rsqrt(_split_dot(y * y, head_ones) + NORM_EPS) * scale
            dst[r0:r0 + CONV_TILE, :] = y

    gt = gt_ref[...]
    beta_s[...] = 1.0 / (1.0 + jnp.exp(-gt))
    gx = gt + dtb_ref[0]
    softplus = jnp.maximum(gx, 0.0) + jnp.log(1.0 + jnp.exp(-jnp.abs(gx)))
    g_s[...] = -jnp.exp(alog_ref[0]) * softplus
    st_s[...] = jnp.zeros(st_s.shape, F32)

    wide = 2 * LANES
    lane_w = lax.broadcasted_iota(jnp.int32, (CHUNK, wide), 1)
    rows = lax.broadcasted_iota(jnp.int32, (CHUNK, wide), 0)
    cols = lane_w & (HEAD_DIM - 1)
    fwd = lane_w < LANES
    fwd_row = lax.broadcasted_iota(jnp.int32, (1, wide), 1) < LANES
    head0 = lax.broadcasted_iota(jnp.int32, (CHUNK, LANES), 1) < HEAD_DIM
    eye_b = rows == cols
    rows_p = lax.broadcasted_iota(jnp.int32, (CHUNK, LANES), 0)
    cols_p = lax.broadcasted_iota(jnp.int32, (CHUNK, LANES), 1) & (HEAD_DIM - 1)
    eye = (rows_p == cols_p).astype(F32)
    same_block = {}
    size = BASE_BLOCK
    while size <= CHUNK:
        shift = size.bit_length() - 1
        same_block[size] = (rows_p >> shift) == (cols_p >> shift)
        size *= 2
    r2 = lax.broadcasted_iota(jnp.int32, (2 * CHUNK, CHUNK), 0)
    c2 = lax.broadcasted_iota(jnp.int32, (2 * CHUNK, CHUNK), 1)
    tri2 = (jnp.where(r2 < CHUNK, r2 - c2, c2 - (r2 - CHUNK)) >= 0).astype(BF16)
    below = jnp.where(fwd, rows - cols, cols - rows)
    incl = below >= 0
    strict = below > 0
    bd = lambda x: _pair_block_diag(x, same_head)
    both = lambda x: jnp.concatenate([x, x], axis=-1)
    halves = (slice(0, LANES), slice(LANES, wide))

    def spread(x, c0):
        col = lambda c: jnp.broadcast_to(x[:, c:c + 1], (CHUNK, LANES))
        return jnp.concatenate([jnp.where(head0, col(c0), col(c0 + 1)),
                                jnp.where(head0, col(c0 + 2), col(c0 + 3))], axis=-1)

    def chunk_group(it, carry):
        ns = [it * CHUNK_GROUP + g for g in range(CHUNK_GROUP)]
        sls = [pl.ds(n * CHUNK, CHUNK) for n in ns]
        k2 = [k_s[sl, :] for sl in sls]
        q2 = [q_s[sl, :] for sl in sls]
        gsum = []
        for sl in sls:
            g2 = g_s[sl, :]
            hi = g2.astype(BF16)
            lo = (g2 - hi.astype(F32)).astype(BF16)
            gsum.append(_dot(tri2, jnp.concatenate([hi, lo], axis=-1)))
        kq = [_dot_nt(jnp.concatenate([k, q], axis=0).astype(BF16), bd(k)) for k, q in zip(k2, q2)]
        beta, egc, kd, ms = [], [], [], []
        for g, sl in enumerate(sls):
            gs = gsum[g][:, :LANES] + gsum[g][:, LANES:]
            gcum = jnp.concatenate([gs[:CHUNK], gs[CHUNK:]], axis=-1)
            beta.append(spread(beta_s[sl, :], 0))
            gc = jnp.where(fwd, spread(gcum[:, :LANES], 4), spread(gcum[:, LANES:], 4))
            gc_row = jnp.sum(jnp.where(eye_b, gc, 0.0), axis=0, keepdims=True)
            decay = jnp.where(incl, jnp.exp(gc - gc_row), 0.0)
            ms.append(jnp.where(strict, beta[g] * both(kq[g][:CHUNK]) * decay, 0.0))
            a = (both(kq[g][CHUNK:]) * decay).astype(BF16)
            egc.append(jnp.exp(gc))
            g_last = jnp.where(fwd_row, gc[CHUNK - 1:CHUNK, :], gc[0:1, :])
            kd.append((both(k2[g]) * jnp.exp(g_last - gc)).astype(BF16))
            eg = jnp.exp(g_last)
            rows8 = pl.ds(ns[g] * 8, 8)
            for d in range(2):
                a_s[d, sl, :] = a[:, d * LANES:(d + 1) * LANES]
                eg_s[d, rows8, :] = jnp.broadcast_to(eg[:, d * LANES:(d + 1) * LANES], (8, LANES))
        chains = [(g, d) for g in range(CHUNK_GROUP) for d in range(2)]
        inv = _inv_unit_triangular_pairs([ms[g][:, halves[d]] for g, d in chains], same_block, same_head, eye)
        ys = [jnp.concatenate([bd((k2[g] * (beta[g] * egc[g])[:, halves[d]])),
                               bd(v_s[sls[g], :] * beta[g][:, halves[d]])], axis=-1) for g, d in chains]
        wus = [_dot(x.astype(BF16), y) for x, y in zip(inv, ys)]
        kwu = [_dot_tn(kd[g][:, halves[d]], wu.astype(BF16)) for wu, (g, d) in zip(wus, chains)]
        for wu, x, (g, d) in zip(wus, kwu, chains):
            u_s[d, sls[g], :] = wu[:, LANES:]
            qg = q2[g] * egc[g][:, halves[d]]
            lq_s[d, ns[g]] = jnp.concatenate([wu[:, :LANES], qg], axis=0).astype(BF16)
            kw_s[d, ns[g]] = jnp.where(same_head, x[:, :LANES], 0.0).astype(BF16)
            ku_s[d, ns[g]] = jnp.where(same_head, x[:, LANES:], 0.0)
        return carry

    for it in range(n_chunks // CHUNK_GROUP):
        chunk_group(it, 0)

    def scan_step(i, carry):
        ns = (i, n_chunks - 1 - i)
        state = [st_s[d] for d in range(2)]
        s16 = [s.astype(BF16) for s in state]
        drop = [_dot(kw_s[d, ns[d]], s16[d]) for d in range(2)]
        for d in range(2):
            eg = eg_s[d, pl.ds(pl.multiple_of(ns[d] * 8, 8), 8), :][0:1]
            sn_s[d, ns[d]] = s16[d]
            st_s[d] = state[d] * eg - drop[d] + ku_s[d, ns[d]]
        return carry

    lax.fori_loop(0, n_chunks, scan_step, 0)

    for it in range(n_chunks // CHUNK_GROUP):
        todo = [(it * CHUNK_GROUP + g, d) for g in range(CHUNK_GROUP) for d in range(2)]
        sls = [pl.ds(n * CHUNK, CHUNK) for n, _ in todo]
        rs = [_dot(lq_s[d, n], sn_s[d, n]) for n, d in todo]
        v_new = [u_s[d, sl, :] - r[:CHUNK] for r, sl, (_, d) in zip(rs, sls, todo)]
        outs = [r[CHUNK:] + _dot(a_s[d, sl, :], bd(v)) for r, v, sl, (_, d) in zip(rs, v_new, sls, todo)]
        for g in range(CHUNK_GROUP):
            sl = sls[2 * g]
            o = outs[2 * g] + outs[2 * g + 1]
            ms = _split_dot(o * o, head_ones) * (1.0 / HEAD_DIM)
            o = o * lax.rsqrt(ms + NORM_EPS) * gain_ref[...]
            o_ref[sl, :] = (o * _silu(z_ref[sl, :].astype(F32))).astype(BF16)


def _split_dot_left(m, x):
    hi = x.astype(BF16)
    lo = (x - hi.astype(F32)).astype(BF16)
    return _dot(m, hi) + _dot(m, lo)


def _delta(cx, cz, gt, conv_w, alog, dtb, gain, batch, seq):
    t = cx.shape[0]
    xspec = lambda part: pl.BlockSpec((seq, LANES), lambda b, p: (b, part * C_PAIRS + p))
    wspec = lambda part: pl.BlockSpec((CONV_K, LANES), lambda b, p: (0, part * C_PAIRS + p))
    pair = pl.BlockSpec((seq, LANES), lambda b, p: (b, p))
    prow = pl.BlockSpec((1, 1, LANES), lambda b, p: (p, 0, 0))
    return pl.pallas_call(
        functools.partial(_delta_kernel, seq=seq),
        out_shape=jax.ShapeDtypeStruct((t, C_W), BF16),
        grid=(batch, C_PAIRS),
        in_specs=[xspec(0), xspec(1), xspec(2), wspec(0), wspec(1), wspec(2),
                  pair, pair, prow, prow,
                  _resident((1, LANES))],
        out_specs=pair,
        scratch_shapes=[pltpu.VMEM((3, seq + 16, LANES), F32),
                        pltpu.VMEM((seq, LANES), F32),
                        pltpu.VMEM((seq, LANES), F32),
                        pltpu.VMEM((seq, LANES), F32),
                        pltpu.VMEM((seq, LANES), F32),
                        pltpu.VMEM((seq, LANES), F32),
                        pltpu.VMEM((2, LANES, LANES), F32),
                        pltpu.VMEM((2, seq, LANES), F32),
                        pltpu.VMEM((2, seq // CHUNK, LANES, LANES), BF16),
                        pltpu.VMEM((2, seq, LANES), BF16),
                        pltpu.VMEM((2, seq // CHUNK, LANES, LANES), BF16),
                        pltpu.VMEM((2, seq // CHUNK, LANES, LANES), F32),
                        pltpu.VMEM((2, seq // CHUNK, LANES, LANES), BF16),
                        pltpu.VMEM((2, seq // CHUNK * 8, LANES), F32)],
        compiler_params=_cparams("parallel", "parallel"),
        name="delta",
    )(cx, cx, cx, conv_w, conv_w, conv_w, cz, gt, alog, dtb, gain)


def _out_ffn_kernel(h_ref, a_ref, b_ref, c_ref, w_ref, g_ref, wg_ref, wu_ref, wd_ref, o_ref):
    y = _dot(a_ref[...], w_ref[0:A_QW, :])
    y = y + _dot(b_ref[...], w_ref[A_QW:A_QW + B_W, :])
    y = y + _dot(c_ref[...], w_ref[A_QW + B_W:, :])
    o_ref[...] = _swiglu_residual(h_ref[...] + y, g_ref, wg_ref, wu_ref, wd_ref)


def _mix_out_ffn(h, a, b, c, layer, w, g, wg, wu, wd):
    t, d = h.shape
    tm = min(TOKEN_TILE, t)
    row = lambda width: pl.BlockSpec((tm, width), lambda i: (i, 0))
    return pl.pallas_call(
        _out_ffn_kernel,
        out_shape=jax.ShapeDtypeStruct((t, d), F32),
        grid=(t // tm,),
        in_specs=[row(d), row(A_QW), row(B_W), row(C_W)]
                 + [_layer_resident(p, layer) for p in (w, g, wg, wu, wd)],
        out_specs=row(d),
        compiler_params=_cparams("parallel"),
        name="mix_out_ffn",
    )(h, a, b, c, w, g, wg, wu, wd)


def _rope_tables(seq):
    rows = seq // GRID_W
    row = jnp.repeat(jnp.arange(rows), GRID_W).astype(F32)
    col = jnp.tile(jnp.arange(GRID_W), rows).astype(F32)
    n_freq = HEAD_DIM // 4
    inv = ROPE_THETA ** (-jnp.arange(n_freq, dtype=F32) / n_freq)
    ang = jnp.concatenate([row[:, None] * inv, col[:, None] * inv], axis=-1)
    cos, sin = jnp.cos(ang), jnp.sin(ang)
    cos_h = jnp.concatenate([cos, cos], axis=-1)
    sin_h = jnp.concatenate([-sin, sin], axis=-1)
    return jnp.tile(cos_h, (1, A_HEADS)), jnp.tile(sin_h, (1, A_HEADS))


def _pair_columns(x):
    lead = x.shape[:-1]
    x = x.reshape(lead + (2, C_PAIRS, 2))
    return jnp.moveaxis(x, -2, -3).reshape(lead + (C_PAIRS, 4))


def _pad_w_in(w_in):
    lead = w_in.shape[:-1]
    w_in = w_in.astype(BF16)
    aq = [w_in[..., hq * HEAD_DIM:(hq + 1) * HEAD_DIM] for hq in A_Q_HEAD_ORDER]
    cb = _pair_columns(w_in[..., OFF_G:OFF_G + 2 * C_HEADS])
    ca = _pair_columns(w_in[..., OFF_G + 2 * C_HEADS:])
    gates = jnp.concatenate([cb, ca, jnp.zeros(lead + (C_PAIRS, LANES - 8), BF16)], axis=-1)
    return jnp.concatenate(aq + [w_in[..., A_QW:OFF_G], gates.reshape(lead + (GATE_W,))], axis=-1)


def _pair_row(p):
    v = _pair_columns(p.reshape(2 * C_HEADS).astype(F32))
    z4 = jnp.zeros((C_PAIRS, 4), F32)
    return jnp.concatenate([z4, v, jnp.zeros((C_PAIRS, LANES - 8), F32)], axis=-1)[:, None, :]


def kernel(x, rel_bias, ffn1_norm, ffn1_w_gate, ffn1_w_up, ffn1_w_down, mix_norm, w_in, a_q_norm, a_k_norm, b_q_norm, b_k_norm, c_conv, c_A_log, c_dt_bias, c_out_norm, w_out, ffn2_norm, ffn2_w_gate, ffn2_w_up, ffn2_w_down):
    batch, seq, d = x.shape
    depth = w_in.shape[0]
    scale = HEAD_DIM ** -0.5 * LOG2E
    h = x.reshape(batch * seq, d)
    cos, sin = _rope_tables(seq)
    idx = np.arange(MXU_W)
    mblk = jnp.asarray((idx[:, None] // HEAD_DIM == idx[None, :] // HEAD_DIM) / HEAD_DIM, BF16)
    biases = [_branch_bias(rel_bias, dil, seq // dil) for _, dil in DIL_PAIRS]
    row = lambda v: v.reshape(1, -1).astype(F32)
    ffn1 = (ffn1_norm.astype(F32)[:, None, :], ffn1_w_gate.astype(BF16), ffn1_w_up.astype(BF16),
            ffn1_w_down.astype(BF16))
    ffn2 = (ffn2_norm.astype(F32)[:, None, :], ffn2_w_gate.astype(BF16), ffn2_w_up.astype(BF16),
            ffn2_w_down.astype(BF16))
    wo_a = [w_out[:, hq * HEAD_DIM:(hq + 1) * HEAD_DIM] for hq in A_Q_HEAD_ORDER]
    wo = jnp.concatenate(wo_a + [w_out[:, A_QW:]], axis=1).astype(BF16)
    mix_g = mix_norm.astype(F32)[:, None, :]
    w_proj = _pad_w_in(w_in)
    wvt = jnp.swapaxes(w_in[:, :, A_QW + A_KVW:OFF_B], 1, 2).astype(BF16)
    for l in range(depth):
        h = _ffn(h, l, *ffn1)
        aq, ak, avt, bq, bk, bv, cx, cz, gt = _proj(
            h, l, mix_g, w_proj, wvt, cos, sin,
            row(jnp.tile(a_q_norm[l], A_HEADS)) * scale, row(jnp.tile(a_k_norm[l], A_KV_HEADS)),
            row(jnp.tile(b_q_norm[l], B_HEADS)) * scale, row(jnp.tile(b_k_norm[l], B_HEADS)),
            mblk, seq)
        out_a = _attn_a(aq, ak, avt, seq)
        out_b = _attn_b(bq, bk, bv, biases, batch, seq)
        out_c = _delta(cx, cz, gt, c_conv[l].astype(F32), _pair_row(c_A_log[l]), _pair_row(c_dt_bias[l]),
                       row(jnp.tile(c_out_norm[l], 2)), batch, seq)
        h = _mix_out_ffn(h, out_a, out_b, out_c, l, wo, *ffn2)
    return h.reshape(batch, seq, d)
```

```python
import functools
import math

import numpy as np
import jax
import jax.numpy as jnp
from jax import lax
from jax.experimental import pallas as pl
from jax.experimental.pallas import tpu as pltpu

F32 = jnp.float32
BF16 = jnp.bfloat16

D_MODEL = 1024
HEAD_DIM = 64
A_HEADS = 4
A_KV_HEADS = 2
B_HEADS = 6
C_HEADS = 6
D_FF = 2816
GRID_W = 64
ROPE_THETA = 10000.0
DIL_PAIRS = ((128, 1), (512, 4), (2048, 16))
REL_BUCKETS = 32
REL_MAX_DIST = 1024
CONV_K = 5
CHUNK = 64
NORM_EPS = 1e-6
NEG_INF = -1e30
LOG2E = math.log2(math.e)

A_QW = A_HEADS * HEAD_DIM
A_KVW = A_KV_HEADS * HEAD_DIM
B_W = B_HEADS * HEAD_DIM
C_W = C_HEADS * HEAD_DIM
C_PAIRS = C_HEADS // 2
LANES = 128
MXU_W = 256
GATE_W = C_PAIRS * LANES

OFF_A = 0
OFF_B = OFF_A + A_QW + 2 * A_KVW
OFF_C = OFF_B + 3 * B_W
OFF_Z = OFF_C + 3 * C_W
OFF_G = OFF_Z + C_W
N_PROJ = OFF_G + GATE_W

TOKEN_TILE = 512
A_Q_TILE = 256
A_K_CHUNK = 256
B_Q_TILE = 128
B_K_TILE = 256
B_SIDE = 64
CONV_TILE = 256
CHUNK_GROUP = 16
VMEM_LIMIT = 56 * 1024 * 1024


def _cparams(*sem):
    return pltpu.CompilerParams(dimension_semantics=sem, vmem_limit_bytes=VMEM_LIMIT)


def _resident(shape):
    return pl.BlockSpec(shape, lambda *_: (0,) * len(shape), pipeline_mode=pl.Buffered(1))


def _dot(a, b):
    return jnp.dot(a, b, preferred_element_type=F32)


def _dot_nt(a, b):
    return lax.dot_general(a, b, (((1,), (1,)), ((), ())), preferred_element_type=F32)


def _dot_tn(a, b):
    return lax.dot_general(a, b, (((0,), (0,)), ((), ())), preferred_element_type=F32)


def _silu(x):
    return x / (1.0 + jnp.exp(-x))


def _split_dot(x, m):
    hi = x.astype(BF16)
    lo = (x - hi.astype(F32)).astype(BF16)
    return _dot(hi, m) + _dot(lo, m)


def _rms_rows(x, g):
    ms = jnp.mean(x * x, axis=-1, keepdims=True)
    return x * lax.rsqrt(ms + NORM_EPS) * g


def _layer_resident(stacked, layer):
    shape = stacked.shape[1:]
    return pl.BlockSpec((None,) + shape, lambda *_: (layer,) + (0,) * len(shape),
                        pipeline_mode=pl.Buffered(1))


def _swiglu_residual(x, g_ref, wg_ref, wu_ref, wd_ref):
    xn = _rms_rows(x, g_ref[...]).astype(BF16)
    gate = _dot(xn, wg_ref[...])
    up = _dot(xn, wu_ref[...])
    act = (_silu(gate) * up).astype(BF16)
    return x + 0.5 * _dot(act, wd_ref[...])


def _ffn_kernel(h_ref, g_ref, wg_ref, wu_ref, wd_ref, o_ref):
    o_ref[...] = _swiglu_residual(h_ref[...], g_ref, wg_ref, wu_ref, wd_ref)


def _ffn(h, layer, g, wg, wu, wd):
    t, d = h.shape
    tm = min(TOKEN_TILE, t)
    return pl.pallas_call(
        _ffn_kernel,
        out_shape=jax.ShapeDtypeStruct((t, d), F32),
        grid=(t // tm,),
        in_specs=[pl.BlockSpec((tm, d), lambda i: (i, 0))]
                 + [_layer_resident(p, layer) for p in (g, wg, wu, wd)],
        out_specs=pl.BlockSpec((tm, d), lambda i: (i, 0)),
        compiler_params=_cparams("parallel"),
        name="ffn",
    )(h, g, wg, wu, wd)


def _rope(x, cos, sin):
    w = x.shape[1]
    lane = lax.broadcasted_iota(jnp.int32, x.shape, 1)
    first = (lane & (HEAD_DIM - 1)) < HEAD_DIM // 2
    half = HEAD_DIM // 2
    partner = jnp.where(first, pltpu.roll(x, w - half, 1), pltpu.roll(x, half, 1))
    return x * cos + partner * sin


def _proj_kernel(h_ref, g_ref, w_ref, wvt_ref, cos_ref, sin_ref, gaq_ref, gak_ref, gbq_ref, gbk_ref,
                 mblk_ref, aq_o, ak_o, avt_o, bq_o, bk_o, bv_o, cx_o, cz_o, gt_o):
    xn = _rms_rows(h_ref[...], g_ref[...]).astype(BF16)
    pr = _dot(xn, w_ref[...])
    avt_o[...] = _dot_nt(wvt_ref[...], xn).astype(BF16)

    def unit_rms(lo, width):
        outs = []
        for c in range(lo, lo + width, MXU_W):
            w = min(MXU_W, lo + width - c)
            x = pr[:, c:c + w]
            outs.append(x * lax.rsqrt(_split_dot(x * x, mblk_ref[:w, :w]) + NORM_EPS))
        return outs[0] if len(outs) == 1 else jnp.concatenate(outs, axis=-1)

    cos = cos_ref[...]
    sin = sin_ref[...]
    aq_o[...] = _rope(unit_rms(OFF_A, A_QW) * gaq_ref[...], cos, sin).astype(BF16)
    ak = unit_rms(OFF_A + A_QW, A_KVW) * gak_ref[...]
    ak_o[...] = _rope(ak, cos[:, :A_KVW], sin[:, :A_KVW]).astype(BF16)
    bqk = unit_rms(OFF_B, 2 * B_W)
    bq_o[...] = bqk[:, :B_W] * gbq_ref[...]
    bk_o[...] = bqk[:, B_W:] * gbk_ref[...]
    bv_o[...] = pr[:, OFF_B + 2 * B_W:OFF_C]
    cx_o[...] = pr[:, OFF_C:OFF_Z]
    cz_o[...] = pr[:, OFF_Z:OFF_G].astype(BF16)
    gt_o[...] = pr[:, OFF_G:]


def _proj(h, layer, g, w, wvt, cos, sin, gaq, gak, gbq, gbk, mblk, seq):
    t, d = h.shape
    tm = min(TOKEN_TILE, seq)
    per_seq = seq // tm
    row = lambda i: (i, 0)
    pos = lambda i: (i % per_seq, 0)
    widths = (A_QW, A_KVW, None, B_W, B_W, B_W, 3 * C_W, C_W, GATE_W)
    dtypes = (BF16, BF16, BF16, F32, F32, F32, F32, BF16, F32)
    shapes = [(t, wd) if wd else (A_KVW, t) for wd in widths]
    specs = [pl.BlockSpec((tm, wd), row) if wd else pl.BlockSpec((A_KVW, tm), lambda i: (0, i))
             for wd in widths]
    return pl.pallas_call(
        _proj_kernel,
        out_shape=[jax.ShapeDtypeStruct(s, dt) for s, dt in zip(shapes, dtypes)],
        grid=(t // tm,),
        in_specs=[pl.BlockSpec((tm, d), row),
                  _layer_resident(g, layer), _layer_resident(w, layer), _layer_resident(wvt, layer),
                  pl.BlockSpec((tm, A_QW), pos),
                  pl.BlockSpec((tm, A_QW), pos),
                  _resident((1, A_QW)), _resident((1, A_KVW)), _resident((1, B_W)),
                  _resident((1, B_W)), _resident((MXU_W, MXU_W))],
        out_specs=specs,
        compiler_params=_cparams("parallel"),
        name="mix_in",
    )(h, g, w, wvt, cos, sin, gaq, gak, gbq, gbk, mblk)


A_Q_HEAD_ORDER = tuple(hk * (A_HEADS // A_KV_HEADS) + g
                       for g in range(A_HEADS // A_KV_HEADS) for hk in range(A_KV_HEADS))


def _attn_a_kernel(q_ref, k_ref, vt_ref, o_ref):
    seq = k_ref.shape[0]
    lane_hi = lax.broadcasted_iota(jnp.int32, (1, LANES), 1) >= HEAD_DIM
    row_hi = lax.broadcasted_iota(jnp.int32, (LANES, 1), 0) >= HEAD_DIM
    heads = [(g, hk) for g in range(A_HEADS // A_KV_HEADS) for hk in range(A_KV_HEADS)]
    qm = []
    for g, hk in heads:
        qb = q_ref[:, g * LANES:(g + 1) * LANES].astype(F32)
        qm.append(jnp.where(lane_hi == (hk == 1), qb, 0.0).astype(BF16))
    def scores(c):
        k_c = k_ref[c:c + A_K_CHUNK, :]
        return [_dot_nt(k_c, q) for q in qm]

    m_run = [None] * len(heads)
    acc = [None] * len(heads)
    sts_next = scores(0)
    for c in range(0, seq, A_K_CHUNK):
        sts = sts_next
        if c + A_K_CHUNK < seq:
            sts_next = scores(c + A_K_CHUNK)
        vt_c = vt_ref[:, c:c + A_K_CHUNK].astype(F32)
        vt_ones = [jnp.where(row_hi == (hk == 1), vt_c, 1.0).astype(BF16) for hk in range(A_KV_HEADS)]
        ps, m_news = [], []
        for i, st in enumerate(sts):
            m_new = jnp.max(st, axis=0, keepdims=True)
            if m_run[i] is not None:
                m_new = jnp.maximum(m_run[i], m_new)
            m_news.append(m_new)
            ps.append(jnp.exp2(st - m_new).astype(BF16))
        for i, (g, hk) in enumerate(heads):
            ov = _dot(vt_ones[hk], ps[i])
            acc[i] = ov if acc[i] is None else acc[i] * jnp.exp2(m_run[i] - m_news[i]) + ov
            m_run[i] = m_news[i]
    for g in range(A_HEADS // A_KV_HEADS):
        halves = []
        for hk in range(A_KV_HEADS):
            ov = acc[g * A_KV_HEADS + hk]
            den_row = (1 - hk) * HEAD_DIM
            halves.append(ov / ov[den_row:den_row + 1, :])
        ot = jnp.where(row_hi, halves[1], halves[0])
        o_ref[:, g * LANES:(g + 1) * LANES] = ot.T.astype(BF16)


def _attn_a(q, k, vt, seq):
    t = q.shape[0]
    tq = min(A_Q_TILE, seq)
    per_seq = seq // tq
    return pl.pallas_call(
        _attn_a_kernel,
        out_shape=jax.ShapeDtypeStruct((t, A_QW), BF16),
        grid=(t // seq, per_seq),
        in_specs=[pl.BlockSpec((tq, A_QW), lambda b, i: (b * per_seq + i, 0)),
                  pl.BlockSpec((seq, A_KVW), lambda b, i: (b, 0)),
                  pl.BlockSpec((A_KVW, seq), lambda b, i: (0, b))],
        out_specs=pl.BlockSpec((tq, A_QW), lambda b, i: (b * per_seq + i, 0)),
        compiler_params=_cparams("parallel", "parallel"),
        name="attn_a",
    )(q, k, vt)


def _t5_bucket_np(rel):
    half = REL_BUCKETS // 2
    exact = half // 2
    sign = np.where(rel > 0, half, 0)
    n = np.abs(rel)
    nf = np.maximum(n, 1).astype(np.float32)
    large = exact + (np.log(nf / np.float32(exact)) / np.float32(math.log(REL_MAX_DIST / exact))
                     * np.float32(half - exact)).astype(np.int32)
    large = np.minimum(large, half - 1)
    return sign + np.where(n < exact, n, large)


def _branch_tiles(sub_len):
    if sub_len >= B_K_TILE:
        return B_Q_TILE, B_K_TILE, (0, -B_SIDE, B_Q_TILE - B_K_TILE)
    return sub_len, sub_len, (0,)


def _branch_bias(rel_bias, dil, sub_len):
    tq, tk, offs = _branch_tiles(sub_len)
    period = tq + tk
    slot = np.arange(period)
    tabs = []
    for off in offs:
        delta = np.where(slot < tk, slot, slot - period) + off
        bucket = _t5_bucket_np(delta * dil)
        vec = jnp.where((np.abs(delta) <= B_SIDE)[:, None], rel_bias.astype(F32)[bucket] * LOG2E, NEG_INF).T
        flat = jnp.tile(vec, (1, tq))[:, :tq * (period - 1)]
        tabs.append(flat.reshape(B_HEADS, tq, period - 1)[:, :, :tk])
    return jnp.stack(tabs)


def _attn_b_kernel(q_ref, k_ref, v_ref, b0_ref, b1_ref, b2_ref, o_ref, num_s, mx_s, den_s, *, seq):
    head0 = lax.broadcasted_iota(jnp.int32, (1, LANES), 1) < HEAD_DIM
    pick = lambda a, b: jnp.where(head0, a, b)

    tiles = []
    for g, ((_, dil), bias_ref) in enumerate(zip(DIL_PAIRS, (b0_ref, b1_ref, b2_ref))):
        sub_len = seq // dil
        tq, tk, _ = _branch_tiles(sub_len)
        n_tiles = sub_len // tq
        for r in range(dil):
            for t in range(n_tiles):
                if n_tiles == 1:
                    k0, tab = 0, 0
                else:
                    k0 = min(max(t * tq - B_SIDE, 0), sub_len - tk)
                    tab = 0 if t == 0 else 2 if t == n_tiles - 1 else 1
                tiles.append((g, bias_ref, tab, pl.ds(r + dil * t * tq, tq, stride=dil),
                              pl.ds(r + dil * k0, tk, stride=dil)))

    for g, bias_ref, tab, rows, keys in tiles:
        tq = rows.size
        qt = q_ref[rows, :]
        q2 = jnp.concatenate([pick(qt, 0.0), pick(0.0, qt)], axis=0).astype(BF16)
        s = _dot_nt(q2, k_ref[keys, :].astype(BF16)) + bias_ref[tab]
        m = jnp.max(s, axis=-1, keepdims=True)
        p = jnp.exp2(s - m)
        l = jnp.sum(p, axis=-1, keepdims=True)
        pv = _dot(p.astype(BF16), v_ref[keys, :].astype(BF16))
        num_s.at[g][rows, :] = pick(pv[:tq], pv[tq:])
        mx_s.at[g][rows, :] = pick(m[:tq], m[tq:])
        den_s.at[g][rows, :] = pick(l[:tq], l[tq:])

    def merge(t, carry):
        rows = pl.ds(pl.multiple_of(t * CONV_TILE, CONV_TILE), CONV_TILE)
        ms = [mx_s[g, rows, :] for g in range(3)]
        top = jnp.maximum(jnp.maximum(ms[0], ms[1]), ms[2])
        es = [jnp.exp2(m - top) for m in ms]
        num = sum(e * num_s[g, rows, :] for g, e in enumerate(es))
        den = sum(e * den_s[g, rows, :] for g, e in enumerate(es))
        o_ref[rows, :] = (num / den).astype(BF16)
        return carry

    lax.fori_loop(0, seq // CONV_TILE, merge, 0)


def _attn_b(q, k, v, biases, batch, seq):
    t = q.shape[0]
    pair = pl.BlockSpec((seq, LANES), lambda b, p: (b, p))
    biases = [x.reshape(x.shape[0], B_HEADS // 2, 2 * x.shape[2], x.shape[3]) for x in biases]
    bias_spec = lambda x: pl.BlockSpec((x.shape[0], None) + x.shape[2:], lambda b, p: (0, p, 0, 0))
    return pl.pallas_call(
        functools.partial(_attn_b_kernel, seq=seq),
        out_shape=jax.ShapeDtypeStruct((t, B_W), BF16),
        grid=(batch, B_HEADS // 2),
        in_specs=[pair, pair, pair] + [bias_spec(x) for x in biases],
        out_specs=pair,
        scratch_shapes=[pltpu.VMEM((len(DIL_PAIRS), seq, LANES), F32)] * 3,
        compiler_params=_cparams("parallel", "parallel"),
        name="attn_b",
    )(q, k, v, *biases)


BASE_BLOCK = 8


def _pair_block_diag(x, same_head):
    reps = same_head.shape[0] // x.shape[0]
    return jnp.where(same_head, jnp.concatenate([x] * reps, axis=0), 0.0).astype(BF16)


def _inv_unit_triangular_pairs(ms, same_block, same_head, eye):
    bd = lambda x: _pair_block_diag(x, same_head)
    b16 = lambda x: x.astype(BF16)
    m8 = [b16(jnp.where(same_block[BASE_BLOCK], m, 0.0)) for m in ms]
    p2 = [_dot(a, bd(a.astype(F32))) for a in m8]
    p2h = [b16(p) for p in p2]
    p4 = [_dot(a, bd(p)) for a, p in zip(p2h, p2)]
    p6 = [_dot(a, bd(p)) for a, p in zip(p2h, p4)]
    even = [eye + a + b + c for a, b, c in zip(p2, p4, p6)]
    inv = [e - _dot(a, bd(e)) for a, e in zip(m8, even)]
    size = BASE_BLOCK
    while size < CHUNK:
        joins = same_block[2 * size] & jnp.logical_not(same_block[size])
        t = [_dot(b16(jnp.where(joins, m, 0.0)), bd(x)) for m, x in zip(ms, inv)]
        inv = [x - _dot(b16(x), bd(y)) for x, y in zip(inv, t)]
        size *= 2
    return inv


def _delta_kernel(xq_ref, xk_ref, xv_ref, wq_ref, wk_ref, wv_ref, z_ref, gt_ref,
                  alog_ref, dtb_ref, gain_ref, o_ref,
                  pad_s, q_s, k_s, v_s, beta_s, g_s,
                  u_s, lq_s, a_s, kw_s, ku_s, eg_s, *, seq):
    n_chunks = seq // CHUNK
    pad = 8
    sq_r = lax.broadcasted_iota(jnp.int32, (LANES, LANES), 0)
    sq_c = lax.broadcasted_iota(jnp.int32, (LANES, LANES), 1)
    same_head = (sq_r < HEAD_DIM) == (sq_c < HEAD_DIM)
    head_ones = same_head.astype(BF16)

    zeros = jnp.zeros((pad, LANES), F32)
    for j, (x_ref, w_ref, dst, scale) in enumerate(((xq_ref, wq_ref, q_s, HEAD_DIM ** -0.5),
                                                     (xk_ref, wk_ref, k_s, 1.0),
                                                     (xv_ref, wv_ref, v_s, None))):
        pad_s[j, 0:pad, :] = zeros
        pad_s[j, pad + seq:pad + seq + pad, :] = zeros
        pad_s[j, pad:pad + seq, :] = x_ref[...]
        for r0 in range(0, seq, CONV_TILE):
            acc = jnp.zeros((CONV_TILE, LANES), F32)
            for d in range(CONV_K):
                lo = r0 + pad - CONV_K // 2 + d
                acc = acc + w_ref[d:d + 1, :] * pad_s[j, lo:lo + CONV_TILE, :]
            y = _silu(acc)
            if scale is not None:
                y = y * lax.rsqrt(_split_dot(y * y, head_ones) + NORM_EPS) * scale
            dst[r0:r0 + CONV_TILE, :] = y

    gt = gt_ref[...]
    beta_s[...] = 1.0 / (1.0 + jnp.exp(-gt))
    gx = gt + dtb_ref[0]
    softplus = jnp.maximum(gx, 0.0) + jnp.log(1.0 + jnp.exp(-jnp.abs(gx)))
    g_s[...] = -jnp.exp(alog_ref[0]) * softplus

    wide = 2 * LANES
    lane_w = lax.broadcasted_iota(jnp.int32, (CHUNK, wide), 1)
    rows = lax.broadcasted_iota(jnp.int32, (CHUNK, wide), 0)
    cols = lane_w & (HEAD_DIM - 1)
    fwd = lane_w < LANES
    fwd_row = lax.broadcasted_iota(jnp.int32, (1, wide), 1) < LANES
    head0 = lax.broadcasted_iota(jnp.int32, (CHUNK, LANES), 1) < HEAD_DIM
    eye_b = rows == cols
    rows_p = lax.broadcasted_iota(jnp.int32, (CHUNK, LANES), 0)
    cols_p = lax.broadcasted_iota(jnp.int32, (CHUNK, LANES), 1) & (HEAD_DIM - 1)
    eye = (rows_p == cols_p).astype(F32)
    same_block = {}
    size = BASE_BLOCK
    while size <= CHUNK:
        shift = size.bit_length() - 1
        same_block[size] = (rows_p >> shift) == (cols_p >> shift)
        size *= 2
    r2 = lax.broadcasted_iota(jnp.int32, (2 * CHUNK, CHUNK), 0)
    c2 = lax.broadcasted_iota(jnp.int32, (2 * CHUNK, CHUNK), 1)
    tri2 = (jnp.where(r2 < CHUNK, r2 - c2, c2 - (r2 - CHUNK)) >= 0).astype(BF16)
    below = jnp.where(fwd, rows - cols, cols - rows)
    incl = below >= 0
    strict = below > 0
    bd = lambda x: _pair_block_diag(x, same_head)
    both = lambda x: jnp.concatenate([x, x], axis=-1)
    halves = (slice(0, LANES), slice(LANES, wide))

    def spread(x, c0):
        col = lambda c: jnp.broadcast_to(x[:, c:c + 1], (CHUNK, LANES))
        return jnp.concatenate([jnp.where(head0, col(c0), col(c0 + 1)),
                                jnp.where(head0, col(c0 + 2), col(c0 + 3))], axis=-1)

    def chunk_group(it, carry):
        ns = [it * CHUNK_GROUP + g for g in range(CHUNK_GROUP)]
        sls = [pl.ds(n * CHUNK, CHUNK) for n in ns]
        k2 = [k_s[sl, :] for sl in sls]
        q2 = [q_s[sl, :] for sl in sls]
        gsum = []
        for sl in sls:
            g2 = g_s[sl, :]
            hi = g2.astype(BF16)
            lo = (g2 - hi.astype(F32)).astype(BF16)
            gsum.append(_dot(tri2, jnp.concatenate([hi, lo], axis=-1)))
        kq = [_dot_nt(jnp.concatenate([k, q], axis=0).astype(BF16), bd(k)) for k, q in zip(k2, q2)]
        beta, egc, kd, ms = [], [], [], []
        for g, sl in enumerate(sls):
            gs = gsum[g][:, :LANES] + gsum[g][:, LANES:]
            gcum = jnp.concatenate([gs[:CHUNK], gs[CHUNK:]], axis=-1)
            beta.append(spread(beta_s[sl, :], 0))
            gc = jnp.where(fwd, spread(gcum[:, :LANES], 4), spread(gcum[:, LANES:], 4))
            gc_row = jnp.sum(jnp.where(eye_b, gc, 0.0), axis=0, keepdims=True)
            decay = jnp.where(incl, jnp.exp(gc - gc_row), 0.0)
            ms.append(jnp.where(strict, beta[g] * both(kq[g][:CHUNK]) * decay, 0.0))
            a = (both(kq[g][CHUNK:]) * decay).astype(BF16)
            egc.append(jnp.exp(gc))
            g_last = jnp.where(fwd_row, gc[CHUNK - 1:CHUNK, :], gc[0:1, :])
            kd.append((both(k2[g]) * jnp.exp(g_last - gc)).astype(BF16))
            eg = jnp.exp(g_last)
            rows8 = pl.ds(ns[g] * 8, 8)
            for d in range(2):
                a_s[d, sl, :] = a[:, d * LANES:(d + 1) * LANES]
                eg_s[d, rows8, :] = jnp.broadcast_to(eg[:, d * LANES:(d + 1) * LANES], (8, LANES))
        chains = [(g, d) for g in range(CHUNK_GROUP) for d in range(2)]
        inv = _inv_unit_triangular_pairs([ms[g][:, halves[d]] for g, d in chains], same_block, same_head, eye)
        ys = [jnp.concatenate([bd((k2[g] * (beta[g] * egc[g])[:, halves[d]])),
                               bd(v_s[sls[g], :] * beta[g][:, halves[d]])], axis=-1) for g, d in chains]
        wus = [_dot(x.astype(BF16), y) for x, y in zip(inv, ys)]
        kwu = [_dot_tn(kd[g][:, halves[d]], wu.astype(BF16)) for wu, (g, d) in zip(wus, chains)]
        for wu, x, (g, d) in zip(wus, kwu, chains):
            u_s[d, sls[g], :] = wu[:, LANES:]
            qg = q2[g] * egc[g][:, halves[d]]
            lq_s[d, ns[g]] = jnp.concatenate([wu[:, :LANES], qg], axis=0).astype(BF16)
            kw_s[d, ns[g]] = jnp.where(same_head, x[:, :LANES], 0.0).astype(BF16)
            ku_s[d, ns[g]] = jnp.where(same_head, x[:, LANES:], 0.0)
        return carry

    for it in range(n_chunks // CHUNK_GROUP):
        chunk_group(it, 0)

    state = [jnp.zeros((LANES, LANES), F32) for _ in range(2)]
    outs = [[None] * n_chunks for _ in range(2)]
    res_prev = None
    for i in range(n_chunks + 2):
        if i < n_chunks:
            ns = (i, n_chunks - 1 - i)
            s16 = [s.astype(BF16) for s in state]
            res = [_dot(jnp.concatenate([kw_s[d, n], lq_s[d, n]], axis=0), s16[d])
                   for d, n in enumerate(ns)]
            for d, n in enumerate(ns):
                state[d] = state[d] * eg_s[d, n * 8:n * 8 + 1, :] - res[d][:LANES] + ku_s[d, n]
        if i >= 2 and i - 2 >= n_chunks // 2:
            for n in (i - 2, n_chunks - 1 - (i - 2)):
                sl = pl.ds(n * CHUNK, CHUNK)
                o = outs[0][n] + outs[1][n]
                ms = _split_dot(o * o, head_ones) * (1.0 / HEAD_DIM)
                o = o * lax.rsqrt(ms + NORM_EPS) * gain_ref[...]
                o_ref[sl, :] = (o * _silu(z_ref[sl, :].astype(F32))).astype(BF16)
        if 1 <= i <= n_chunks:
            for d, n in enumerate((i - 1, n_chunks - i)):
                sl = pl.ds(n * CHUNK, CHUNK)
                v_new = u_s[d, sl, :] - res_prev[d][LANES:LANES + CHUNK]
                outs[d][n] = res_prev[d][LANES + CHUNK:] + _dot(a_s[d, sl, :], bd(v_new))
        res_prev = res


def _split_dot_left(m, x):
    hi = x.astype(BF16)
    lo = (x - hi.astype(F32)).astype(BF16)
    return _dot(m, hi) + _dot(m, lo)


def _delta(cx, cz, gt, conv_w, alog, dtb, gain, batch, seq):
    t = cx.shape[0]
    xspec = lambda part: pl.BlockSpec((seq, LANES), lambda b, p: (b, part * C_PAIRS + p))
    wspec = lambda part: pl.BlockSpec((CONV_K, LANES), lambda b, p: (0, part * C_PAIRS + p))
    pair = pl.BlockSpec((seq, LANES), lambda b, p: (b, p))
    prow = pl.BlockSpec((1, 1, LANES), lambda b, p: (p, 0, 0))
    return pl.pallas_call(
        functools.partial(_delta_kernel, seq=seq),
        out_shape=jax.ShapeDtypeStruct((t, C_W), BF16),
        grid=(batch, C_PAIRS),
        in_specs=[xspec(0), xspec(1), xspec(2), wspec(0), wspec(1), wspec(2),
                  pair, pair, prow, prow,
                  _resident((1, LANES))],
        out_specs=pair,
        scratch_shapes=[pltpu.VMEM((3, seq + 16, LANES), F32),
                        pltpu.VMEM((seq, LANES), F32),
                        pltpu.VMEM((seq, LANES), F32),
                        pltpu.VMEM((seq, LANES), F32),
                        pltpu.VMEM((seq, LANES), F32),
                        pltpu.VMEM((seq, LANES), F32),
                        pltpu.VMEM((2, seq, LANES), F32),
                        pltpu.VMEM((2, seq // CHUNK, LANES, LANES), BF16),
                        pltpu.VMEM((2, seq, LANES), BF16),
                        pltpu.VMEM((2, seq // CHUNK, LANES, LANES), BF16),
                        pltpu.VMEM((2, seq // CHUNK, LANES, LANES), F32),
                        pltpu.VMEM((2, seq // CHUNK * 8, LANES), F32)],
        compiler_params=_cparams("parallel", "parallel"),
        name="delta",
    )(cx, cx, cx, conv_w, conv_w, conv_w, cz, gt, alog, dtb, gain)


def _out_ffn_kernel(h_ref, a_ref, b_ref, c_ref, w_ref, g_ref, wg_ref, wu_ref, wd_ref, o_ref):
    y = _dot(a_ref[...], w_ref[0:A_QW, :])
    y = y + _dot(b_ref[...], w_ref[A_QW:A_QW + B_W, :])
    y = y + _dot(c_ref[...], w_ref[A_QW + B_W:, :])
    o_ref[...] = _swiglu_residual(h_ref[...] + y, g_ref, wg_ref, wu_ref, wd_ref)


def _mix_out_ffn(h, a, b, c, layer, w, g, wg, wu, wd):
    t, d = h.shape
    tm = min(TOKEN_TILE, t)
    row = lambda width: pl.BlockSpec((tm, width), lambda i: (i, 0))
    return pl.pallas_call(
        _out_ffn_kernel,
        out_shape=jax.ShapeDtypeStruct((t, d), F32),
        grid=(t // tm,),
        in_specs=[row(d), row(A_QW), row(B_W), row(C_W)]
                 + [_layer_resident(p, layer) for p in (w, g, wg, wu, wd)],
        out_specs=row(d),
        compiler_params=_cparams("parallel"),
        name="mix_out_ffn",
    )(h, a, b, c, w, g, wg, wu, wd)


def _rope_tables(seq):
    rows = seq // GRID_W
    row = jnp.repeat(jnp.arange(rows), GRID_W).astype(F32)
    col = jnp.tile(jnp.arange(GRID_W), rows).astype(F32)
    n_freq = HEAD_DIM // 4
    inv = ROPE_THETA ** (-jnp.arange(n_freq, dtype=F32) / n_freq)
    ang = jnp.concatenate([row[:, None] * inv, col[:, None] * inv], axis=-1)
    cos, sin = jnp.cos(ang), jnp.sin(ang)
    cos_h = jnp.concatenate([cos, cos], axis=-1)
    sin_h = jnp.concatenate([-sin, sin], axis=-1)
    return jnp.tile(cos_h, (1, A_HEADS)), jnp.tile(sin_h, (1, A_HEADS))


def _pair_columns(x):
    lead = x.shape[:-1]
    x = x.reshape(lead + (2, C_PAIRS, 2))
    return jnp.moveaxis(x, -2, -3).reshape(lead + (C_PAIRS, 4))


def _pad_w_in(w_in):
    lead = w_in.shape[:-1]
    w_in = w_in.astype(BF16)
    aq = [w_in[..., hq * HEAD_DIM:(hq + 1) * HEAD_DIM] for hq in A_Q_HEAD_ORDER]
    cb = _pair_columns(w_in[..., OFF_G:OFF_G + 2 * C_HEADS])
    ca = _pair_columns(w_in[..., OFF_G + 2 * C_HEADS:])
    gates = jnp.concatenate([cb, ca, jnp.zeros(lead + (C_PAIRS, LANES - 8), BF16)], axis=-1)
    return jnp.concatenate(aq + [w_in[..., A_QW:OFF_G], gates.reshape(lead + (GATE_W,))], axis=-1)


def _pair_row(p):
    v = _pair_columns(p.reshape(2 * C_HEADS).astype(F32))
    z4 = jnp.zeros((C_PAIRS, 4), F32)
    return jnp.concatenate([z4, v, jnp.zeros((C_PAIRS, LANES - 8), F32)], axis=-1)[:, None, :]


def kernel(x, rel_bias, ffn1_norm, ffn1_w_gate, ffn1_w_up, ffn1_w_down, mix_norm, w_in, a_q_norm, a_k_norm, b_q_norm, b_k_norm, c_conv, c_A_log, c_dt_bias, c_out_norm, w_out, ffn2_norm, ffn2_w_gate, ffn2_w_up, ffn2_w_down):
    batch, seq, d = x.shape
    depth = w_in.shape[0]
    scale = HEAD_DIM ** -0.5 * LOG2E
    h = x.reshape(batch * seq, d)
    cos, sin = _rope_tables(seq)
    idx = np.arange(MXU_W)
    mblk = jnp.asarray((idx[:, None] // HEAD_DIM == idx[None, :] // HEAD_DIM) / HEAD_DIM, BF16)
    biases = [_branch_bias(rel_bias, dil, seq // dil) for _, dil in DIL_PAIRS]
    row = lambda v: v.reshape(1, -1).astype(F32)
    ffn1 = (ffn1_norm.astype(F32)[:, None, :], ffn1_w_gate.astype(BF16), ffn1_w_up.astype(BF16),
            ffn1_w_down.astype(BF16))
    ffn2 = (ffn2_norm.astype(F32)[:, None, :], ffn2_w_gate.astype(BF16), ffn2_w_up.astype(BF16),
            ffn2_w_down.astype(BF16))
    wo_a = [w_out[:, hq * HEAD_DIM:(hq + 1) * HEAD_DIM] for hq in A_Q_HEAD_ORDER]
    wo = jnp.concatenate(wo_a + [w_out[:, A_QW:]], axis=1).astype(BF16)
    mix_g = mix_norm.astype(F32)[:, None, :]
    w_proj = _pad_w_in(w_in)
    wvt = jnp.swapaxes(w_in[:, :, A_QW + A_KVW:OFF_B], 1, 2).astype(BF16)
    for l in range(depth):
        h = _ffn(h, l, *ffn1)
        aq, ak, avt, bq, bk, bv, cx, cz, gt = _proj(
            h, l, mix_g, w_proj, wvt, cos, sin,
            row(jnp.tile(a_q_norm[l], A_HEADS)) * scale, row(jnp.tile(a_k_norm[l], A_KV_HEADS)),
            row(jnp.tile(b_q_norm[l], B_HEADS)) * scale, row(jnp.tile(b_k_norm[l], B_HEADS)),
            mblk, seq)
        out_a = _attn_a(aq, ak, avt, seq)
        out_b = _attn_b(bq, bk, bv, biases, batch, seq)
        out_c = _delta(cx, cz, gt, c_conv[l].astype(F32), _pair_row(c_A_log[l]), _pair_row(c_dt_bias[l]),
                       row(jnp.tile(c_out_norm[l], 2)), batch, seq)
        h = _mix_out_ffn(h, out_a, out_b, out_c, l, wo, *ffn2)
    return h.reshape(batch, seq, d)
```

```python
import functools
import math

import numpy as np
import jax
import jax.numpy as jnp
from jax import lax
from jax.experimental import pallas as pl
from jax.experimental.pallas import tpu as pltpu

F32 = jnp.float32
BF16 = jnp.bfloat16

D_MODEL = 1024
HEAD_DIM = 64
A_HEADS = 4
A_KV_HEADS = 2
B_HEADS = 6
C_HEADS = 6
D_FF = 2816
GRID_W = 64
ROPE_THETA = 10000.0
DIL_PAIRS = ((128, 1), (512, 4), (2048, 16))
REL_BUCKETS = 32
REL_MAX_DIST = 1024
CONV_K = 5
CHUNK = 64
NORM_EPS = 1e-6
NEG_INF = -1e30
LOG2E = math.log2(math.e)

A_QW = A_HEADS * HEAD_DIM
A_KVW = A_KV_HEADS * HEAD_DIM
B_W = B_HEADS * HEAD_DIM
C_W = C_HEADS * HEAD_DIM
C_PAIRS = C_HEADS // 2
LANES = 128
MXU_W = 256
GATE_W = C_PAIRS * LANES

OFF_A = 0
OFF_B = OFF_A + A_QW + 2 * A_KVW
OFF_C = OFF_B + 3 * B_W
OFF_Z = OFF_C + 3 * C_W
OFF_G = OFF_Z + C_W
N_PROJ = OFF_G + GATE_W

TOKEN_TILE = 512
A_Q_TILE = 512
A_K_CHUNK = 256
B_Q_TILE = 128
B_K_TILE = 256
B_SIDE = 64
CONV_TILE = 256
CHUNK_GROUP = 16
VMEM_LIMIT = 56 * 1024 * 1024


def _cparams(*sem):
    return pltpu.CompilerParams(dimension_semantics=sem, vmem_limit_bytes=VMEM_LIMIT)


def _resident(shape):
    return pl.BlockSpec(shape, lambda *_: (0,) * len(shape), pipeline_mode=pl.Buffered(1))


def _dot(a, b):
    return jnp.dot(a, b, preferred_element_type=F32)


def _dot_nt(a, b):
    return lax.dot_general(a, b, (((1,), (1,)), ((), ())), preferred_element_type=F32)


def _dot_tn(a, b):
    return lax.dot_general(a, b, (((0,), (0,)), ((), ())), preferred_element_type=F32)


def _silu(x):
    return x / (1.0 + jnp.exp(-x))


def _split_dot(x, m):
    hi = x.astype(BF16)
    lo = (x - hi.astype(F32)).astype(BF16)
    return _dot(hi, m) + _dot(lo, m)


def _rms_rows(x, g):
    ms = jnp.mean(x * x, axis=-1, keepdims=True)
    return x * lax.rsqrt(ms + NORM_EPS) * g


def _layer_resident(stacked, layer):
    shape = stacked.shape[1:]
    return pl.BlockSpec((None,) + shape, lambda *_: (layer,) + (0,) * len(shape),
                        pipeline_mode=pl.Buffered(1))


def _swiglu_residual(x, g_ref, wg_ref, wu_ref, wd_ref):
    xn = _rms_rows(x, g_ref[...]).astype(BF16)
    gate = _dot(xn, wg_ref[...])
    up = _dot(xn, wu_ref[...])
    act = (_silu(gate) * up).astype(BF16)
    return x + 0.5 * _dot(act, wd_ref[...])


def _ffn_kernel(h_ref, g_ref, wg_ref, wu_ref, wd_ref, o_ref):
    o_ref[...] = _swiglu_residual(h_ref[...], g_ref, wg_ref, wu_ref, wd_ref)


def _ffn(h, layer, g, wg, wu, wd):
    t, d = h.shape
    tm = min(TOKEN_TILE, t)
    return pl.pallas_call(
        _ffn_kernel,
        out_shape=jax.ShapeDtypeStruct((t, d), F32),
        grid=(t // tm,),
        in_specs=[pl.BlockSpec((tm, d), lambda i: (i, 0))]
                 + [_layer_resident(p, layer) for p in (g, wg, wu, wd)],
        out_specs=pl.BlockSpec((tm, d), lambda i: (i, 0)),
        compiler_params=_cparams("parallel"),
        name="ffn",
    )(h, g, wg, wu, wd)


def _rope(x, cos, sin):
    w = x.shape[1]
    lane = lax.broadcasted_iota(jnp.int32, x.shape, 1)
    first = (lane & (HEAD_DIM - 1)) < HEAD_DIM // 2
    half = HEAD_DIM // 2
    partner = jnp.where(first, pltpu.roll(x, w - half, 1), pltpu.roll(x, half, 1))
    return x * cos + partner * sin


def _proj_kernel(h_ref, g_ref, w_ref, cos_ref, sin_ref, gaq_ref, gak_ref, gbq_ref, gbk_ref,
                 mblk_ref, aq_o, ak_o, avt_o, bq_o, bk_o, bv_o, cx_o, cz_o, gt_o):
    xn = _rms_rows(h_ref[...], g_ref[...]).astype(BF16)
    pr = _dot(xn, w_ref[...])
    avt_o[...] = pr[:, OFF_A + A_QW + A_KVW:OFF_B].T.astype(BF16)

    def unit_rms(lo, width):
        outs = []
        for c in range(lo, lo + width, MXU_W):
            w = min(MXU_W, lo + width - c)
            x = pr[:, c:c + w]
            outs.append(x * lax.rsqrt(_split_dot(x * x, mblk_ref[:w, :w]) + NORM_EPS))
        return outs[0] if len(outs) == 1 else jnp.concatenate(outs, axis=-1)

    cos = cos_ref[...]
    sin = sin_ref[...]
    aq_o[...] = _rope(unit_rms(OFF_A, A_QW) * gaq_ref[...], cos, sin).astype(BF16)
    ak = unit_rms(OFF_A + A_QW, A_KVW) * gak_ref[...]
    ak_o[...] = _rope(ak, cos[:, :A_KVW], sin[:, :A_KVW]).astype(BF16)
    bqk = unit_rms(OFF_B, 2 * B_W)
    bq_o[...] = bqk[:, :B_W] * gbq_ref[...]
    bk_o[...] = bqk[:, B_W:] * gbk_ref[...]
    bv_o[...] = pr[:, OFF_B + 2 * B_W:OFF_C]
    cx_o[...] = pr[:, OFF_C:OFF_Z]
    cz_o[...] = pr[:, OFF_Z:OFF_G].astype(BF16)
    gt_o[...] = pr[:, OFF_G:]


def _proj(h, layer, g, w, cos, sin, gaq, gak, gbq, gbk, mblk, seq):
    t, d = h.shape
    tm = min(TOKEN_TILE, seq)
    per_seq = seq // tm
    row = lambda i: (i, 0)
    pos = lambda i: (i % per_seq, 0)
    widths = (A_QW, A_KVW, None, B_W, B_W, B_W, 3 * C_W, C_W, GATE_W)
    dtypes = (BF16, BF16, BF16, F32, F32, F32, F32, BF16, F32)
    shapes = [(t, wd) if wd else (A_KVW, t) for wd in widths]
    specs = [pl.BlockSpec((tm, wd), row) if wd else pl.BlockSpec((A_KVW, tm), lambda i: (0, i))
             for wd in widths]
    return pl.pallas_call(
        _proj_kernel,
        out_shape=[jax.ShapeDtypeStruct(s, dt) for s, dt in zip(shapes, dtypes)],
        grid=(t // tm,),
        in_specs=[pl.BlockSpec((tm, d), row),
                  _layer_resident(g, layer), _layer_resident(w, layer),
                  pl.BlockSpec((tm, A_QW), pos),
                  pl.BlockSpec((tm, A_QW), pos),
                  _resident((1, A_QW)), _resident((1, A_KVW)), _resident((1, B_W)),
                  _resident((1, B_W)), _resident((MXU_W, MXU_W))],
        out_specs=specs,
        compiler_params=_cparams("parallel"),
        name="mix_in",
    )(h, g, w, cos, sin, gaq, gak, gbq, gbk, mblk)


A_Q_HEAD_ORDER = tuple(hk * (A_HEADS // A_KV_HEADS) + g
                       for g in range(A_HEADS // A_KV_HEADS) for hk in range(A_KV_HEADS))


def _attn_a_kernel(q_ref, k_ref, vt_ref, o_ref):
    seq = k_ref.shape[0]
    lane_hi = lax.broadcasted_iota(jnp.int32, (1, LANES), 1) >= HEAD_DIM
    row_hi = lax.broadcasted_iota(jnp.int32, (LANES, 1), 0) >= HEAD_DIM
    heads = [(g, hk) for g in range(A_HEADS // A_KV_HEADS) for hk in range(A_KV_HEADS)]
    qm = []
    for g, hk in heads:
        qb = q_ref[:, g * LANES:(g + 1) * LANES].astype(F32)
        qm.append(jnp.where(lane_hi == (hk == 1), qb, 0.0).astype(BF16))
    def scores(c):
        k_c = k_ref[c:c + A_K_CHUNK, :]
        return [_dot_nt(k_c, q) for q in qm]

    m_run = [None] * len(heads)
    acc = [None] * len(heads)
    sts_next = scores(0)
    for c in range(0, seq, A_K_CHUNK):
        sts = sts_next
        if c + A_K_CHUNK < seq:
            sts_next = scores(c + A_K_CHUNK)
        vt_c = vt_ref[:, c:c + A_K_CHUNK].astype(F32)
        vt_ones = [jnp.where(row_hi == (hk == 1), vt_c, 1.0).astype(BF16) for hk in range(A_KV_HEADS)]
        ps, m_news = [], []
        for i, st in enumerate(sts):
            m_new = jnp.max(st, axis=0, keepdims=True)
            if m_run[i] is not None:
                m_new = jnp.maximum(m_run[i], m_new)
            m_news.append(m_new)
            ps.append(jnp.exp2(st - m_new).astype(BF16))
        for i, (g, hk) in enumerate(heads):
            ov = _dot(vt_ones[hk], ps[i])
            acc[i] = ov if acc[i] is None else acc[i] * jnp.exp2(m_run[i] - m_news[i]) + ov
            m_run[i] = m_news[i]
    for g in range(A_HEADS // A_KV_HEADS):
        halves = []
        for hk in range(A_KV_HEADS):
            ov = acc[g * A_KV_HEADS + hk]
            den_row = (1 - hk) * HEAD_DIM
            halves.append(ov / ov[den_row:den_row + 1, :])
        ot = jnp.where(row_hi, halves[1], halves[0])
        o_ref[:, g * LANES:(g + 1) * LANES] = ot.T.astype(BF16)


def _attn_a(q, k, vt, seq):
    t = q.shape[0]
    tq = min(A_Q_TILE, seq)
    per_seq = seq // tq
    return pl.pallas_call(
        _attn_a_kernel,
        out_shape=jax.ShapeDtypeStruct((t, A_QW), BF16),
        grid=(t // seq, per_seq),
        in_specs=[pl.BlockSpec((tq, A_QW), lambda b, i: (b * per_seq + i, 0)),
                  pl.BlockSpec((seq, A_KVW), lambda b, i: (b, 0)),
                  pl.BlockSpec((A_KVW, seq), lambda b, i: (0, b))],
        out_specs=pl.BlockSpec((tq, A_QW), lambda b, i: (b * per_seq + i, 0)),
        compiler_params=_cparams("parallel", "parallel"),
        name="attn_a",
    )(q, k, vt)


def _t5_bucket_np(rel):
    half = REL_BUCKETS // 2
    exact = half // 2
    sign = np.where(rel > 0, half, 0)
    n = np.abs(rel)
    nf = np.maximum(n, 1).astype(np.float32)
    large = exact + (np.log(nf / np.float32(exact)) / np.float32(math.log(REL_MAX_DIST / exact))
                     * np.float32(half - exact)).astype(np.int32)
    large = np.minimum(large, half - 1)
    return sign + np.where(n < exact, n, large)


def _branch_tiles(sub_len):
    if sub_len >= B_K_TILE:
        return B_Q_TILE, B_K_TILE, (0, -B_SIDE, B_Q_TILE - B_K_TILE)
    return sub_len, sub_len, (0,)


def _branch_bias(rel_bias, dil, sub_len):
    tq, tk, offs = _branch_tiles(sub_len)
    period = tq + tk
    slot = np.arange(period)
    tabs = []
    for off in offs:
        delta = np.where(slot < tk, slot, slot - period) + off
        bucket = _t5_bucket_np(delta * dil)
        vec = jnp.where((np.abs(delta) <= B_SIDE)[:, None], rel_bias.astype(F32)[bucket] * LOG2E, NEG_INF).T
        flat = jnp.tile(vec, (1, tq))[:, :tq * (period - 1)]
        tabs.append(flat.reshape(B_HEADS, tq, period - 1)[:, :, :tk])
    return jnp.stack(tabs)


def _attn_b_kernel(q_ref, k_ref, v_ref, b0_ref, b1_ref, b2_ref, o_ref, num_s, mx_s, den_s, *, seq):
    head0 = lax.broadcasted_iota(jnp.int32, (1, LANES), 1) < HEAD_DIM
    pick = lambda a, b: jnp.where(head0, a, b)

    tiles = []
    for g, ((_, dil), bias_ref) in enumerate(zip(DIL_PAIRS, (b0_ref, b1_ref, b2_ref))):
        sub_len = seq // dil
        tq, tk, _ = _branch_tiles(sub_len)
        n_tiles = sub_len // tq
        for r in range(dil):
            for t in range(n_tiles):
                if n_tiles == 1:
                    k0, tab = 0, 0
                else:
                    k0 = min(max(t * tq - B_SIDE, 0), sub_len - tk)
                    tab = 0 if t == 0 else 2 if t == n_tiles - 1 else 1
                tiles.append((g, bias_ref, tab, pl.ds(r + dil * t * tq, tq, stride=dil),
                              pl.ds(r + dil * k0, tk, stride=dil)))

    for g, bias_ref, tab, rows, keys in tiles:
        tq = rows.size
        qt = q_ref[rows, :]
        q2 = jnp.concatenate([pick(qt, 0.0), pick(0.0, qt)], axis=0).astype(BF16)
        s = _dot_nt(q2, k_ref[keys, :].astype(BF16)) + bias_ref[tab]
        m = jnp.max(s, axis=-1, keepdims=True)
        p = jnp.exp2(s - m)
        l = jnp.sum(p, axis=-1, keepdims=True)
        pv = _dot(p.astype(BF16), v_ref[keys, :].astype(BF16))
        num_s.at[g][rows, :] = pick(pv[:tq], pv[tq:])
        mx_s.at[g][rows, :] = pick(m[:tq], m[tq:])
        den_s.at[g][rows, :] = pick(l[:tq], l[tq:])

    def merge(t, carry):
        rows = pl.ds(pl.multiple_of(t * CONV_TILE, CONV_TILE), CONV_TILE)
        ms = [mx_s[g, rows, :] for g in range(3)]
        top = jnp.maximum(jnp.maximum(ms[0], ms[1]), ms[2])
        es = [jnp.exp2(m - top) for m in ms]
        num = sum(e * num_s[g, rows, :] for g, e in enumerate(es))
        den = sum(e * den_s[g, rows, :] for g, e in enumerate(es))
        o_ref[rows, :] = (num / den).astype(BF16)
        return carry

    lax.fori_loop(0, seq // CONV_TILE, merge, 0)


def _attn_b(q, k, v, biases, batch, seq):
    t = q.shape[0]
    pair = pl.BlockSpec((seq, LANES), lambda b, p: (b, p))
    biases = [x.reshape(x.shape[0], B_HEADS // 2, 2 * x.shape[2], x.shape[3]) for x in biases]
    bias_spec = lambda x: pl.BlockSpec((x.shape[0], None) + x.shape[2:], lambda b, p: (0, p, 0, 0))
    return pl.pallas_call(
        functools.partial(_attn_b_kernel, seq=seq),
        out_shape=jax.ShapeDtypeStruct((t, B_W), BF16),
        grid=(batch, B_HEADS // 2),
        in_specs=[pair, pair, pair] + [bias_spec(x) for x in biases],
        out_specs=pair,
        scratch_shapes=[pltpu.VMEM((len(DIL_PAIRS), seq, LANES), F32)] * 3,
        compiler_params=_cparams("parallel", "parallel"),
        name="attn_b",
    )(q, k, v, *biases)


BASE_BLOCK = 8


def _pair_block_diag(x, same_head):
    reps = same_head.shape[0] // x.shape[0]
    return jnp.where(same_head, jnp.concatenate([x] * reps, axis=0), 0.0).astype(BF16)


def _inv_unit_triangular_pairs(ms, same_block, same_head, eye):
    bd = lambda x: _pair_block_diag(x, same_head)
    b16 = lambda x: x.astype(BF16)
    m8 = [b16(jnp.where(same_block[BASE_BLOCK], m, 0.0)) for m in ms]
    p2 = [_dot(a, bd(a.astype(F32))) for a in m8]
    p2h = [b16(p) for p in p2]
    p4 = [_dot(a, bd(p)) for a, p in zip(p2h, p2)]
    p6 = [_dot(a, bd(p)) for a, p in zip(p2h, p4)]
    even = [eye + a + b + c for a, b, c in zip(p2, p4, p6)]
    inv = [e - _dot(a, bd(e)) for a, e in zip(m8, even)]
    size = BASE_BLOCK
    while size < CHUNK:
        joins = same_block[2 * size] & jnp.logical_not(same_block[size])
        t = [_dot(b16(jnp.where(joins, m, 0.0)), bd(x)) for m, x in zip(ms, inv)]
        inv = [x - _dot(b16(x), bd(y)) for x, y in zip(inv, t)]
        size *= 2
    return inv


def _delta_kernel(xq_ref, xk_ref, xv_ref, wq_ref, wk_ref, wv_ref, z_ref, gt_ref,
                  alog_ref, dtb_ref, gain_ref, o_ref,
                  pad_s, q_s, k_s, v_s, beta_s, g_s,
                  u_s, lq_s, a_s, kw_s, ku_s, eg_s, *, seq):
    n_chunks = seq // CHUNK
    pad = 8
    sq_r = lax.broadcasted_iota(jnp.int32, (LANES, LANES), 0)
    sq_c = lax.broadcasted_iota(jnp.int32, (LANES, LANES), 1)
    same_head = (sq_r < HEAD_DIM) == (sq_c < HEAD_DIM)
    head_ones = same_head.astype(BF16)

    zeros = jnp.zeros((pad, LANES), F32)
    for j, (x_ref, w_ref, dst, scale) in enumerate(((xq_ref, wq_ref, q_s, HEAD_DIM ** -0.5),
                                                     (xk_ref, wk_ref, k_s, 1.0),
                                                     (xv_ref, wv_ref, v_s, None))):
        pad_s[j, 0:pad, :] = zeros
        pad_s[j, pad + seq:pad + seq + pad, :] = zeros
        pad_s[j, pad:pad + seq, :] = x_ref[...]
        for r0 in range(0, seq, CONV_TILE):
            acc = jnp.zeros((CONV_TILE, LANES), F32)
            for d in range(CONV_K):
                lo = r0 + pad - CONV_K // 2 + d
                acc = acc + w_ref[d:d + 1, :] * pad_s[j, lo:lo + CONV_TILE, :]
            y = _silu(acc)
            if scale is not None:
                y = y * lax.rsqrt(_split_dot(y * y, head_ones) + NORM_EPS) * scale
            dst[r0:r0 + CONV_TILE, :] = y

    gt = gt_ref[...]
    beta_s[...] = 1.0 / (1.0 + jnp.exp(-gt))
    gx = gt + dtb_ref[0]
    softplus = jnp.maximum(gx, 0.0) + jnp.log(1.0 + jnp.exp(-jnp.abs(gx)))
    g_s[...] = -jnp.exp(alog_ref[0]) * softplus

    wide = 2 * LANES
    lane_w = lax.broadcasted_iota(jnp.int32, (CHUNK, wide), 1)
    rows = lax.broadcasted_iota(jnp.int32, (CHUNK, wide), 0)
    cols = lane_w & (HEAD_DIM - 1)
    fwd = lane_w < LANES
    fwd_row = lax.broadcasted_iota(jnp.int32, (1, wide), 1) < LANES
    head0 = lax.broadcasted_iota(jnp.int32, (CHUNK, LANES), 1) < HEAD_DIM
    eye_b = rows == cols
    rows_p = lax.broadcasted_iota(jnp.int32, (CHUNK, LANES), 0)
    cols_p = lax.broadcasted_iota(jnp.int32, (CHUNK, LANES), 1) & (HEAD_DIM - 1)
    eye = (rows_p == cols_p).astype(F32)
    same_block = {}
    size = BASE_BLOCK
    while size <= CHUNK:
        shift = size.bit_length() - 1
        same_block[size] = (rows_p >> shift) == (cols_p >> shift)
        size *= 2
    r2 = lax.broadcasted_iota(jnp.int32, (2 * CHUNK, CHUNK), 0)
    c2 = lax.broadcasted_iota(jnp.int32, (2 * CHUNK, CHUNK), 1)
    tri2 = (jnp.where(r2 < CHUNK, r2 - c2, c2 - (r2 - CHUNK)) >= 0).astype(BF16)
    below = jnp.where(fwd, rows - cols, cols - rows)
    incl = below >= 0
    strict = below > 0
    bd = lambda x: _pair_block_diag(x, same_head)
    both = lambda x: jnp.concatenate([x, x], axis=-1)
    halves = (slice(0, LANES), slice(LANES, wide))

    def spread(x, c0):
        col = lambda c: jnp.broadcast_to(x[:, c:c + 1], (CHUNK, LANES))
        return jnp.concatenate([jnp.where(head0, col(c0), col(c0 + 1)),
                                jnp.where(head0, col(c0 + 2), col(c0 + 3))], axis=-1)

    def chunk_group(it, carry):
        ns = [it * CHUNK_GROUP + g for g in range(CHUNK_GROUP)]
        sls = [pl.ds(n * CHUNK, CHUNK) for n in ns]
        k2 = [k_s[sl, :] for sl in sls]
        q2 = [q_s[sl, :] for sl in sls]
        gsum = []
        for sl in sls:
            g2 = g_s[sl, :]
            hi = g2.astype(BF16)
            lo = (g2 - hi.astype(F32)).astype(BF16)
            gsum.append(_dot(tri2, jnp.concatenate([hi, lo], axis=-1)))
        kq = [_dot_nt(jnp.concatenate([k, q], axis=0).astype(BF16), bd(k)) for k, q in zip(k2, q2)]
        beta, egc, kd, ms = [], [], [], []
        for g, sl in enumerate(sls):
            gs = gsum[g][:, :LANES] + gsum[g][:, LANES:]
            gcum = jnp.concatenate([gs[:CHUNK], gs[CHUNK:]], axis=-1)
            beta.append(spread(beta_s[sl, :], 0))
            gc = jnp.where(fwd, spread(gcum[:, :LANES], 4), spread(gcum[:, LANES:], 4))
            gc_row = jnp.sum(jnp.where(eye_b, gc, 0.0), axis=0, keepdims=True)
            decay = jnp.where(incl, jnp.exp(gc - gc_row), 0.0)
            ms.append(jnp.where(strict, beta[g] * both(kq[g][:CHUNK]) * decay, 0.0))
            a = (both(kq[g][CHUNK:]) * decay).astype(BF16)
            egc.append(jnp.exp(gc))
            g_last = jnp.where(fwd_row, gc[CHUNK - 1:CHUNK, :], gc[0:1, :])
            kd.append((both(k2[g]) * jnp.exp(g_last - gc)).astype(BF16))
            eg = jnp.exp(g_last)
            rows8 = pl.ds(ns[g] * 8, 8)
            for d in range(2):
                a_s[d, sl, :] = a[:, d * LANES:(d + 1) * LANES]
                eg_s[d, rows8, :] = jnp.broadcast_to(eg[:, d * LANES:(d + 1) * LANES], (8, LANES))
        chains = [(g, d) for g in range(CHUNK_GROUP) for d in range(2)]
        inv = _inv_unit_triangular_pairs([ms[g][:, halves[d]] for g, d in chains], same_block, same_head, eye)
        ys = [jnp.concatenate([bd((k2[g] * (beta[g] * egc[g])[:, halves[d]])),
                               bd(v_s[sls[g], :] * beta[g][:, halves[d]])], axis=-1) for g, d in chains]
        wus = [_dot(x.astype(BF16), y) for x, y in zip(inv, ys)]
        kwu = [_dot_tn(kd[g][:, halves[d]], wu.astype(BF16)) for wu, (g, d) in zip(wus, chains)]
        for wu, x, (g, d) in zip(wus, kwu, chains):
            u_s[d, sls[g], :] = wu[:, LANES:]
            qg = q2[g] * egc[g][:, halves[d]]
            lq_s[d, ns[g]] = jnp.concatenate([wu[:, :LANES], qg], axis=0).astype(BF16)
            kw_s[d, ns[g]] = jnp.where(same_head, x[:, :LANES], 0.0).astype(BF16)
            ku_s[d, ns[g]] = jnp.where(same_head, x[:, LANES:], 0.0)
        return carry

    for it in range(n_chunks // CHUNK_GROUP):
        chunk_group(it, 0)

    state = [jnp.zeros((LANES, LANES), F32) for _ in range(2)]
    outs = [[None] * n_chunks for _ in range(2)]
    res_prev = None
    for i in range(n_chunks + 2):
        if i < n_chunks:
            ns = (i, n_chunks - 1 - i)
            s16 = [s.astype(BF16) for s in state]
            res = [_dot(jnp.concatenate([kw_s[d, n], lq_s[d, n]], axis=0), s16[d])
                   for d, n in enumerate(ns)]
            for d, n in enumerate(ns):
                state[d] = state[d] * eg_s[d, n * 8:n * 8 + 1, :] - res[d][:LANES] + ku_s[d, n]
        if i >= 2 and i - 2 >= n_chunks // 2:
            for n in (i - 2, n_chunks - 1 - (i - 2)):
                sl = pl.ds(n * CHUNK, CHUNK)
                o = outs[0][n] + outs[1][n]
                ms = _split_dot(o * o, head_ones) * (1.0 / HEAD_DIM)
                o = o * lax.rsqrt(ms + NORM_EPS) * gain_ref[...]
                o_ref[sl, :] = (o * _silu(z_ref[sl, :].astype(F32))).astype(BF16)
        if 1 <= i <= n_chunks:
            for d, n in enumerate((i - 1, n_chunks - i)):
                sl = pl.ds(n * CHUNK, CHUNK)
                v_new = u_s[d, sl, :] - res_prev[d][LANES:LANES + CHUNK]
                outs[d][n] = res_prev[d][LANES + CHUNK:] + _dot(a_s[d, sl, :], bd(v_new))
        res_prev = res


def _split_dot_left(m, x):
    hi = x.astype(BF16)
    lo = (x - hi.astype(F32)).astype(BF16)
    return _dot(m, hi) + _dot(m, lo)


def _delta(cx, cz, gt, conv_w, alog, dtb, gain, batch, seq):
    t = cx.shape[0]
    xspec = lambda part: pl.BlockSpec((seq, LANES), lambda b, p: (b, part * C_PAIRS + p))
    wspec = lambda part: pl.BlockSpec((CONV_K, LANES), lambda b, p: (0, part * C_PAIRS + p))
    pair = pl.BlockSpec((seq, LANES), lambda b, p: (b, p))
    prow = pl.BlockSpec((1, 1, LANES), lambda b, p: (p, 0, 0))
    return pl.pallas_call(
        functools.partial(_delta_kernel, seq=seq),
        out_shape=jax.ShapeDtypeStruct((t, C_W), BF16),
        grid=(batch, C_PAIRS),
        in_specs=[xspec(0), xspec(1), xspec(2), wspec(0), wspec(1), wspec(2),
                  pair, pair, prow, prow,
                  _resident((1, LANES))],
        out_specs=pair,
        scratch_shapes=[pltpu.VMEM((3, seq + 16, LANES), F32),
                        pltpu.VMEM((seq, LANES), F32),
                        pltpu.VMEM((seq, LANES), F32),
                        pltpu.VMEM((seq, LANES), F32),
                        pltpu.VMEM((seq, LANES), F32),
                        pltpu.VMEM((seq, LANES), F32),
                        pltpu.VMEM((2, seq, LANES), F32),
                        pltpu.VMEM((2, seq // CHUNK, LANES, LANES), BF16),
                        pltpu.VMEM((2, seq, LANES), BF16),
                        pltpu.VMEM((2, seq // CHUNK, LANES, LANES), BF16),
                        pltpu.VMEM((2, seq // CHUNK, LANES, LANES), F32),
                        pltpu.VMEM((2, seq // CHUNK * 8, LANES), F32)],
        compiler_params=_cparams("parallel", "parallel"),
        name="delta",
    )(cx, cx, cx, conv_w, conv_w, conv_w, cz, gt, alog, dtb, gain)


def _out_ffn_kernel(h_ref, a_ref, b_ref, c_ref, w_ref, g_ref, wg_ref, wu_ref, wd_ref, o_ref):
    y = _dot(a_ref[...], w_ref[0:A_QW, :])
    y = y + _dot(b_ref[...], w_ref[A_QW:A_QW + B_W, :])
    y = y + _dot(c_ref[...], w_ref[A_QW + B_W:, :])
    o_ref[...] = _swiglu_residual(h_ref[...] + y, g_ref, wg_ref, wu_ref, wd_ref)


def _mix_out_ffn(h, a, b, c, layer, w, g, wg, wu, wd):
    t, d = h.shape
    tm = min(TOKEN_TILE, t)
    row = lambda width: pl.BlockSpec((tm, width), lambda i: (i, 0))
    return pl.pallas_call(
        _out_ffn_kernel,
        out_shape=jax.ShapeDtypeStruct((t, d), F32),
        grid=(t // tm,),
        in_specs=[row(d), row(A_QW), row(B_W), row(C_W)]
                 + [_layer_resident(p, layer) for p in (w, g, wg, wu, wd)],
        out_specs=row(d),
        compiler_params=_cparams("parallel"),
        name="mix_out_ffn",
    )(h, a, b, c, w, g, wg, wu, wd)


def _rope_tables(seq):
    rows = seq // GRID_W
    row = jnp.repeat(jnp.arange(rows), GRID_W).astype(F32)
    col = jnp.tile(jnp.arange(GRID_W), rows).astype(F32)
    n_freq = HEAD_DIM // 4
    inv = ROPE_THETA ** (-jnp.arange(n_freq, dtype=F32) / n_freq)
    ang = jnp.concatenate([row[:, None] * inv, col[:, None] * inv], axis=-1)
    cos, sin = jnp.cos(ang), jnp.sin(ang)
    cos_h = jnp.concatenate([cos, cos], axis=-1)
    sin_h = jnp.concatenate([-sin, sin], axis=-1)
    return jnp.tile(cos_h, (1, A_HEADS)), jnp.tile(sin_h, (1, A_HEADS))


def _pair_columns(x):
    lead = x.shape[:-1]
    x = x.reshape(lead + (2, C_PAIRS, 2))
    return jnp.moveaxis(x, -2, -3).reshape(lead + (C_PAIRS, 4))


def _pad_w_in(w_in):
    lead = w_in.shape[:-1]
    w_in = w_in.astype(BF16)
    aq = [w_in[..., hq * HEAD_DIM:(hq + 1) * HEAD_DIM] for hq in A_Q_HEAD_ORDER]
    cb = _pair_columns(w_in[..., OFF_G:OFF_G + 2 * C_HEADS])
    ca = _pair_columns(w_in[..., OFF_G + 2 * C_HEADS:])
    gates = jnp.concatenate([cb, ca, jnp.zeros(lead + (C_PAIRS, LANES - 8), BF16)], axis=-1)
    return jnp.concatenate(aq + [w_in[..., A_QW:OFF_G], gates.reshape(lead + (GATE_W,))], axis=-1)


def _pair_row(p):
    v = _pair_columns(p.reshape(2 * C_HEADS).astype(F32))
    z4 = jnp.zeros((C_PAIRS, 4), F32)
    return jnp.concatenate([z4, v, jnp.zeros((C_PAIRS, LANES - 8), F32)], axis=-1)[:, None, :]


def kernel(x, rel_bias, ffn1_norm, ffn1_w_gate, ffn1_w_up, ffn1_w_down, mix_norm, w_in, a_q_norm, a_k_norm, b_q_norm, b_k_norm, c_conv, c_A_log, c_dt_bias, c_out_norm, w_out, ffn2_norm, ffn2_w_gate, ffn2_w_up, ffn2_w_down):
    batch, seq, d = x.shape
    depth = w_in.shape[0]
    scale = HEAD_DIM ** -0.5 * LOG2E
    h = x.reshape(batch * seq, d)
    cos, sin = _rope_tables(seq)
    idx = np.arange(MXU_W)
    mblk = jnp.asarray((idx[:, None] // HEAD_DIM == idx[None, :] // HEAD_DIM) / HEAD_DIM, BF16)
    biases = [_branch_bias(rel_bias, dil, seq // dil) for _, dil in DIL_PAIRS]
    row = lambda v: v.reshape(1, -1).astype(F32)
    ffn1 = (ffn1_norm.astype(F32)[:, None, :], ffn1_w_gate.astype(BF16), ffn1_w_up.astype(BF16),
            ffn1_w_down.astype(BF16))
    ffn2 = (ffn2_norm.astype(F32)[:, None, :], ffn2_w_gate.astype(BF16), ffn2_w_up.astype(BF16),
            ffn2_w_down.astype(BF16))
    wo_a = [w_out[:, hq * HEAD_DIM:(hq + 1) * HEAD_DIM] for hq in A_Q_HEAD_ORDER]
    wo = jnp.concatenate(wo_a + [w_out[:, A_QW:]], axis=1).astype(BF16)
    mix_g = mix_norm.astype(F32)[:, None, :]
    w_proj = _pad_w_in(w_in)
    for l in range(depth):
        h = _ffn(h, l, *ffn1)
        aq, ak, avt, bq, bk, bv, cx, cz, gt = _proj(
            h, l, mix_g, w_proj, cos, sin,
            row(jnp.tile(a_q_norm[l], A_HEADS)) * scale, row(jnp.tile(a_k_norm[l], A_KV_HEADS)),
            row(jnp.tile(b_q_norm[l], B_HEADS)) * scale, row(jnp.tile(b_k_norm[l], B_HEADS)),
            mblk, seq)
        out_a = _attn_a(aq, ak, avt, seq)
        out_b = _attn_b(bq, bk, bv, biases, batch, seq)
        out_c = _delta(cx, cz, gt, c_conv[l].astype(F32), _pair_row(c_A_log[l]), _pair_row(c_dt_bias[l]),
                       row(jnp.tile(c_out_norm[l], 2)), batch, seq)
        h = _mix_out_ffn(h, out_a, out_b, out_c, l, wo, *ffn2)
    return h.reshape(batch, seq, d)
```

```python
import functools
import math

import numpy as np
import jax
import jax.numpy as jnp
from jax import lax
from jax.experimental import pallas as pl
from jax.experimental.pallas import tpu as pltpu

F32 = jnp.float32
BF16 = jnp.bfloat16

D_MODEL = 1024
HEAD_DIM = 64
A_HEADS = 4
A_KV_HEADS = 2
B_HEADS = 6
C_HEADS = 6
D_FF = 2816
GRID_W = 64
ROPE_THETA = 10000.0
DIL_PAIRS = ((128, 1), (512, 4), (2048, 16))
REL_BUCKETS = 32
REL_MAX_DIST = 1024
CONV_K = 5
CHUNK = 64
NORM_EPS = 1e-6
NEG_INF = -1e30
LOG2E = math.log2(math.e)

A_QW = A_HEADS * HEAD_DIM
A_KVW = A_KV_HEADS * HEAD_DIM
B_W = B_HEADS * HEAD_DIM
C_W = C_HEADS * HEAD_DIM
C_PAIRS = C_HEADS // 2
LANES = 128
MXU_W = 256
GATE_W = C_PAIRS * LANES

OFF_A = 0
OFF_B = OFF_A + A_QW + 2 * A_KVW
OFF_C = OFF_B + 3 * B_W
OFF_Z = OFF_C + 3 * C_W
OFF_G = OFF_Z + C_W
N_PROJ = OFF_G + GATE_W

TOKEN_TILE = 512
FFN_TILE = 1024
FFN_SPLIT = 2
A_Q_TILE = 512
A_K_CHUNK = 256
B_Q_TILE = 128
B_K_TILE = 256
B_SIDE = 64
CONV_TILE = 256
CHUNK_GROUP = 16
VMEM_LIMIT = 56 * 1024 * 1024


def _cparams(*sem):
    return pltpu.CompilerParams(dimension_semantics=sem, vmem_limit_bytes=VMEM_LIMIT)


def _resident(shape):
    return pl.BlockSpec(shape, lambda *_: (0,) * len(shape), pipeline_mode=pl.Buffered(1))


def _dot(a, b):
    return jnp.dot(a, b, preferred_element_type=F32)


def _dot_nt(a, b):
    return lax.dot_general(a, b, (((1,), (1,)), ((), ())), preferred_element_type=F32)


def _dot_tn(a, b):
    return lax.dot_general(a, b, (((0,), (0,)), ((), ())), preferred_element_type=F32)


def _silu(x):
    return x / (1.0 + jnp.exp(-x))


def _split_dot(x, m):
    hi = x.astype(BF16)
    lo = (x - hi.astype(F32)).astype(BF16)
    return _dot(hi, m) + _dot(lo, m)


def _rms_rows(x, g):
    ms = jnp.mean(x * x, axis=-1, keepdims=True)
    return x * lax.rsqrt(ms + NORM_EPS) * g


def _layer_resident(stacked, layer):
    shape = stacked.shape[1:]
    return pl.BlockSpec((None,) + shape, lambda *_: (layer,) + (0,) * len(shape),
                        pipeline_mode=pl.Buffered(1))


def _swiglu_residual(x, g_ref, wg_ref, wu_ref, wd_ref):
    xn = _rms_rows(x, g_ref[...]).astype(BF16)
    f = wg_ref.shape[1]
    step = -(-f // (FFN_SPLIT * MXU_W)) * MXU_W
    y = None
    for lo in range(0, f, step):
        hi = min(lo + step, f)
        gate = _dot(xn, wg_ref[:, lo:hi])
        up = _dot(xn, wu_ref[:, lo:hi])
        part = _dot((_silu(gate) * up).astype(BF16), wd_ref[lo:hi, :])
        y = part if y is None else y + part
    return x + 0.5 * y


def _ffn_kernel(h_ref, g_ref, wg_ref, wu_ref, wd_ref, o_ref):
    o_ref[...] = _swiglu_residual(h_ref[...], g_ref, wg_ref, wu_ref, wd_ref)


def _ffn(h, layer, g, wg, wu, wd):
    t, d = h.shape
    tm = min(FFN_TILE, t)
    return pl.pallas_call(
        _ffn_kernel,
        out_shape=jax.ShapeDtypeStruct((t, d), F32),
        grid=(t // tm,),
        in_specs=[pl.BlockSpec((tm, d), lambda i: (i, 0))]
                 + [_layer_resident(p, layer) for p in (g, wg, wu, wd)],
        out_specs=pl.BlockSpec((tm, d), lambda i: (i, 0)),
        compiler_params=_cparams("parallel"),
        name="ffn",
    )(h, g, wg, wu, wd)


def _rope(x, cos, sin):
    w = x.shape[1]
    lane = lax.broadcasted_iota(jnp.int32, x.shape, 1)
    first = (lane & (HEAD_DIM - 1)) < HEAD_DIM // 2
    half = HEAD_DIM // 2
    partner = jnp.where(first, pltpu.roll(x, w - half, 1), pltpu.roll(x, half, 1))
    return x * cos + partner * sin


def _proj_kernel(h_ref, g_ref, w_ref, cos_ref, sin_ref, gaq_ref, gak_ref, gbq_ref, gbk_ref,
                 mblk_ref, aq_o, ak_o, avt_o, bq_o, bk_o, bv_o, cx_o, cz_o, gt_o):
    xn = _rms_rows(h_ref[...], g_ref[...]).astype(BF16)
    pr = _dot(xn, w_ref[...])
    avt_o[...] = pr[:, OFF_A + A_QW + A_KVW:OFF_B].T.astype(BF16)

    def unit_rms(lo, width):
        outs = []
        for c in range(lo, lo + width, MXU_W):
            w = min(MXU_W, lo + width - c)
            x = pr[:, c:c + w]
            outs.append(x * lax.rsqrt(_split_dot(x * x, mblk_ref[:w, :w]) + NORM_EPS))
        return outs[0] if len(outs) == 1 else jnp.concatenate(outs, axis=-1)

    cos = cos_ref[...]
    sin = sin_ref[...]
    aq_o[...] = _rope(unit_rms(OFF_A, A_QW) * gaq_ref[...], cos, sin).astype(BF16)
    ak = unit_rms(OFF_A + A_QW, A_KVW) * gak_ref[...]
    ak_o[...] = _rope(ak, cos[:, :A_KVW], sin[:, :A_KVW]).astype(BF16)
    bqk = unit_rms(OFF_B, 2 * B_W)
    bq_o[...] = bqk[:, :B_W] * gbq_ref[...]
    bk_o[...] = bqk[:, B_W:] * gbk_ref[...]
    bv_o[...] = pr[:, OFF_B + 2 * B_W:OFF_C]
    cx_o[...] = pr[:, OFF_C:OFF_Z]
    cz_o[...] = pr[:, OFF_Z:OFF_G].astype(BF16)
    gt_o[...] = pr[:, OFF_G:]


def _proj(h, layer, g, w, cos, sin, gaq, gak, gbq, gbk, mblk, seq):
    t, d = h.shape
    tm = min(TOKEN_TILE, seq)
    per_seq = seq // tm
    row = lambda i: (i, 0)
    pos = lambda i: (i % per_seq, 0)
    widths = (A_QW, A_KVW, None, B_W, B_W, B_W, 3 * C_W, C_W, GATE_W)
    dtypes = (BF16, BF16, BF16, F32, F32, F32, F32, BF16, F32)
    shapes = [(t, wd) if wd else (A_KVW, t) for wd in widths]
    specs = [pl.BlockSpec((tm, wd), row) if wd else pl.BlockSpec((A_KVW, tm), lambda i: (0, i))
             for wd in widths]
    return pl.pallas_call(
        _proj_kernel,
        out_shape=[jax.ShapeDtypeStruct(s, dt) for s, dt in zip(shapes, dtypes)],
        grid=(t // tm,),
        in_specs=[pl.BlockSpec((tm, d), row),
                  _layer_resident(g, layer), _layer_resident(w, layer),
                  pl.BlockSpec((tm, A_QW), pos),
                  pl.BlockSpec((tm, A_QW), pos),
                  _resident((1, A_QW)), _resident((1, A_KVW)), _resident((1, B_W)),
                  _resident((1, B_W)), _resident((MXU_W, MXU_W))],
        out_specs=specs,
        compiler_params=_cparams("parallel"),
        name="mix_in",
    )(h, g, w, cos, sin, gaq, gak, gbq, gbk, mblk)


A_Q_HEAD_ORDER = tuple(hk * (A_HEADS // A_KV_HEADS) + g
                       for g in range(A_HEADS // A_KV_HEADS) for hk in range(A_KV_HEADS))


def _attn_a_kernel(q_ref, k_ref, vt_ref, o_ref):
    seq = k_ref.shape[0]
    lane_hi = lax.broadcasted_iota(jnp.int32, (1, LANES), 1) >= HEAD_DIM
    row_hi = lax.broadcasted_iota(jnp.int32, (LANES, 1), 0) >= HEAD_DIM
    heads = [(g, hk) for g in range(A_HEADS // A_KV_HEADS) for hk in range(A_KV_HEADS)]
    qm = []
    for g, hk in heads:
        qb = q_ref[:, g * LANES:(g + 1) * LANES].astype(F32)
        qm.append(jnp.where(lane_hi == (hk == 1), qb, 0.0).astype(BF16))
    def scores(c):
        k_c = k_ref[c:c + A_K_CHUNK, :]
        return [_dot_nt(k_c, q) for q in qm]

    m_run = [None] * len(heads)
    acc = [None] * len(heads)
    sts_next = scores(0)
    for c in range(0, seq, A_K_CHUNK):
        sts = sts_next
        if c + A_K_CHUNK < seq:
            sts_next = scores(c + A_K_CHUNK)
        vt_c = vt_ref[:, c:c + A_K_CHUNK].astype(F32)
        vt_ones = [jnp.where(row_hi == (hk == 1), vt_c, 1.0).astype(BF16) for hk in range(A_KV_HEADS)]
        ps, m_news = [], []
        for i, st in enumerate(sts):
            m_new = jnp.max(st, axis=0, keepdims=True)
            if m_run[i] is not None:
                m_new = jnp.maximum(m_run[i], m_new)
            m_news.append(m_new)
            ps.append(jnp.exp2(st - m_new).astype(BF16))
        for i, (g, hk) in enumerate(heads):
            ov = _dot(vt_ones[hk], ps[i])
            acc[i] = ov if acc[i] is None else acc[i] * jnp.exp2(m_run[i] - m_news[i]) + ov
            m_run[i] = m_news[i]
    for g in range(A_HEADS // A_KV_HEADS):
        halves = []
        for hk in range(A_KV_HEADS):
            ov = acc[g * A_KV_HEADS + hk]
            den_row = (1 - hk) * HEAD_DIM
            halves.append(ov / ov[den_row:den_row + 1, :])
        ot = jnp.where(row_hi, halves[1], halves[0])
        o_ref[:, g * LANES:(g + 1) * LANES] = ot.T.astype(BF16)


def _attn_a(q, k, vt, seq):
    t = q.shape[0]
    tq = min(A_Q_TILE, seq)
    per_seq = seq // tq
    return pl.pallas_call(
        _attn_a_kernel,
        out_shape=jax.ShapeDtypeStruct((t, A_QW), BF16),
        grid=(t // seq, per_seq),
        in_specs=[pl.BlockSpec((tq, A_QW), lambda b, i: (b * per_seq + i, 0)),
                  pl.BlockSpec((seq, A_KVW), lambda b, i: (b, 0)),
                  pl.BlockSpec((A_KVW, seq), lambda b, i: (0, b))],
        out_specs=pl.BlockSpec((tq, A_QW), lambda b, i: (b * per_seq + i, 0)),
        compiler_params=_cparams("parallel", "parallel"),
        name="attn_a",
    )(q, k, vt)


def _t5_bucket_np(rel):
    half = REL_BUCKETS // 2
    exact = half // 2
    sign = np.where(rel > 0, half, 0)
    n = np.abs(rel)
    nf = np.maximum(n, 1).astype(np.float32)
    large = exact + (np.log(nf / np.float32(exact)) / np.float32(math.log(REL_MAX_DIST / exact))
                     * np.float32(half - exact)).astype(np.int32)
    large = np.minimum(large, half - 1)
    return sign + np.where(n < exact, n, large)


def _branch_tiles(sub_len):
    if sub_len >= B_K_TILE:
        return B_Q_TILE, B_K_TILE, (0, -B_SIDE, B_Q_TILE - B_K_TILE)
    return sub_len, sub_len, (0,)


def _branch_bias(rel_bias, dil, sub_len):
    tq, tk, offs = _branch_tiles(sub_len)
    period = tq + tk
    slot = np.arange(period)
    tabs = []
    for off in offs:
        delta = np.where(slot < tk, slot, slot - period) + off
        bucket = _t5_bucket_np(delta * dil)
        vec = jnp.where((np.abs(delta) <= B_SIDE)[:, None], rel_bias.astype(F32)[bucket] * LOG2E, NEG_INF).T
        flat = jnp.tile(vec, (1, tq))[:, :tq * (period - 1)]
        tabs.append(flat.reshape(B_HEADS, tq, period - 1)[:, :, :tk])
    return jnp.stack(tabs)


def _attn_b_kernel(q_ref, k_ref, v_ref, b0_ref, b1_ref, b2_ref, o_ref, num_s, mx_s, den_s, *, seq):
    head0 = lax.broadcasted_iota(jnp.int32, (1, LANES), 1) < HEAD_DIM
    pick = lambda a, b: jnp.where(head0, a, b)

    tiles = []
    for g, ((_, dil), bias_ref) in enumerate(zip(DIL_PAIRS, (b0_ref, b1_ref, b2_ref))):
        sub_len = seq // dil
        tq, tk, _ = _branch_tiles(sub_len)
        n_tiles = sub_len // tq
        for r in range(dil):
            for t in range(n_tiles):
                if n_tiles == 1:
                    k0, tab = 0, 0
                else:
                    k0 = min(max(t * tq - B_SIDE, 0), sub_len - tk)
                    tab = 0 if t == 0 else 2 if t == n_tiles - 1 else 1
                tiles.append((g, bias_ref, tab, pl.ds(r + dil * t * tq, tq, stride=dil),
                              pl.ds(r + dil * k0, tk, stride=dil)))

    for g, bias_ref, tab, rows, keys in tiles:
        tq = rows.size
        qt = q_ref[rows, :]
        q2 = jnp.concatenate([pick(qt, 0.0), pick(0.0, qt)], axis=0).astype(BF16)
        s = _dot_nt(q2, k_ref[keys, :].astype(BF16)) + bias_ref[tab]
        m = jnp.max(s, axis=-1, keepdims=True)
        p = jnp.exp2(s - m)
        l = jnp.sum(p, axis=-1, keepdims=True)
        pv = _dot(p.astype(BF16), v_ref[keys, :].astype(BF16))
        num_s.at[g][rows, :] = pick(pv[:tq], pv[tq:])
        mx_s.at[g][rows, :] = pick(m[:tq], m[tq:])
        den_s.at[g][rows, :] = pick(l[:tq], l[tq:])

    def merge(t, carry):
        rows = pl.ds(pl.multiple_of(t * CONV_TILE, CONV_TILE), CONV_TILE)
        ms = [mx_s[g, rows, :] for g in range(3)]
        top = jnp.maximum(jnp.maximum(ms[0], ms[1]), ms[2])
        es = [jnp.exp2(m - top) for m in ms]
        num = sum(e * num_s[g, rows, :] for g, e in enumerate(es))
        den = sum(e * den_s[g, rows, :] for g, e in enumerate(es))
        o_ref[rows, :] = (num / den).astype(BF16)
        return carry

    lax.fori_loop(0, seq // CONV_TILE, merge, 0)


def _attn_b(q, k, v, biases, batch, seq):
    t = q.shape[0]
    pair = pl.BlockSpec((seq, LANES), lambda b, p: (b, p))
    biases = [x.reshape(x.shape[0], B_HEADS // 2, 2 * x.shape[2], x.shape[3]) for x in biases]
    bias_spec = lambda x: pl.BlockSpec((x.shape[0], None) + x.shape[2:], lambda b, p: (0, p, 0, 0))
    return pl.pallas_call(
        functools.partial(_attn_b_kernel, seq=seq),
        out_shape=jax.ShapeDtypeStruct((t, B_W), BF16),
        grid=(batch, B_HEADS // 2),
        in_specs=[pair, pair, pair] + [bias_spec(x) for x in biases],
        out_specs=pair,
        scratch_shapes=[pltpu.VMEM((len(DIL_PAIRS), seq, LANES), F32)] * 3,
        compiler_params=_cparams("parallel", "parallel"),
        name="attn_b",
    )(q, k, v, *biases)


BASE_BLOCK = 8


def _pair_block_diag(x, same_head):
    reps = same_head.shape[0] // x.shape[0]
    return jnp.where(same_head, jnp.concatenate([x] * reps, axis=0), 0.0).astype(BF16)


def _inv_unit_triangular_pairs(ms, same_block, same_head, eye):
    bd = lambda x: _pair_block_diag(x, same_head)
    b16 = lambda x: x.astype(BF16)
    m8 = [b16(jnp.where(same_block[BASE_BLOCK], m, 0.0)) for m in ms]
    p2 = [_dot(a, bd(a.astype(F32))) for a in m8]
    p2h = [b16(p) for p in p2]
    p4 = [_dot(a, bd(p)) for a, p in zip(p2h, p2)]
    p6 = [_dot(a, bd(p)) for a, p in zip(p2h, p4)]
    even = [eye + a + b + c for a, b, c in zip(p2, p4, p6)]
    inv = [e - _dot(a, bd(e)) for a, e in zip(m8, even)]
    size = BASE_BLOCK
    while size < CHUNK:
        joins = same_block[2 * size] & jnp.logical_not(same_block[size])
        t = [_dot(b16(jnp.where(joins, m, 0.0)), bd(x)) for m, x in zip(ms, inv)]
        inv = [x - _dot(b16(x), bd(y)) for x, y in zip(inv, t)]
        size *= 2
    return inv


def _delta_kernel(xq_ref, xk_ref, xv_ref, wq_ref, wk_ref, wv_ref, z_ref, gt_ref,
                  alog_ref, dtb_ref, gain_ref, o_ref,
                  pad_s, q_s, k_s, v_s, beta_s, g_s,
                  u_s, lq_s, a_s, kw_s, ku_s, eg_s, *, seq):
    n_chunks = seq // CHUNK
    pad = 8
    sq_r = lax.broadcasted_iota(jnp.int32, (LANES, LANES), 0)
    sq_c = lax.broadcasted_iota(jnp.int32, (LANES, LANES), 1)
    same_head = (sq_r < HEAD_DIM) == (sq_c < HEAD_DIM)
    head_ones = same_head.astype(BF16)

    zeros = jnp.zeros((pad, LANES), F32)
    for j, (x_ref, w_ref, dst, scale) in enumerate(((xq_ref, wq_ref, q_s, HEAD_DIM ** -0.5),
                                                     (xk_ref, wk_ref, k_s, 1.0),
                                                     (xv_ref, wv_ref, v_s, None))):
        pad_s[j, 0:pad, :] = zeros
        pad_s[j, pad + seq:pad + seq + pad, :] = zeros
        pad_s[j, pad:pad + seq, :] = x_ref[...]
        for r0 in range(0, seq, CONV_TILE):
            acc = jnp.zeros((CONV_TILE, LANES), F32)
            for d in range(CONV_K):
                lo = r0 + pad - CONV_K // 2 + d
                acc = acc + w_ref[d:d + 1, :] * pad_s[j, lo:lo + CONV_TILE, :]
            y = _silu(acc)
            if scale is not None:
                y = y * lax.rsqrt(_split_dot(y * y, head_ones) + NORM_EPS) * scale
            dst[r0:r0 + CONV_TILE, :] = y

    gt = gt_ref[...]
    beta_s[...] = 1.0 / (1.0 + jnp.exp(-gt))
    gx = gt + dtb_ref[0]
    softplus = jnp.maximum(gx, 0.0) + jnp.log(1.0 + jnp.exp(-jnp.abs(gx)))
    g_s[...] = -jnp.exp(alog_ref[0]) * softplus

    wide = 2 * LANES
    lane_w = lax.broadcasted_iota(jnp.int32, (CHUNK, wide), 1)
    rows = lax.broadcasted_iota(jnp.int32, (CHUNK, wide), 0)
    cols = lane_w & (HEAD_DIM - 1)
    fwd = lane_w < LANES
    fwd_row = lax.broadcasted_iota(jnp.int32, (1, wide), 1) < LANES
    head0 = lax.broadcasted_iota(jnp.int32, (CHUNK, LANES), 1) < HEAD_DIM
    eye_b = rows == cols
    rows_p = lax.broadcasted_iota(jnp.int32, (CHUNK, LANES), 0)
    cols_p = lax.broadcasted_iota(jnp.int32, (CHUNK, LANES), 1) & (HEAD_DIM - 1)
    eye = (rows_p == cols_p).astype(F32)
    same_block = {}
    size = BASE_BLOCK
    while size <= CHUNK:
        shift = size.bit_length() - 1
        same_block[size] = (rows_p >> shift) == (cols_p >> shift)
        size *= 2
    r2 = lax.broadcasted_iota(jnp.int32, (2 * CHUNK, CHUNK), 0)
    c2 = lax.broadcasted_iota(jnp.int32, (2 * CHUNK, CHUNK), 1)
    tri2 = (jnp.where(r2 < CHUNK, r2 - c2, c2 - (r2 - CHUNK)) >= 0).astype(BF16)
    below = jnp.where(fwd, rows - cols, cols - rows)
    incl = below >= 0
    strict = below > 0
    bd = lambda x: _pair_block_diag(x, same_head)
    both = lambda x: jnp.concatenate([x, x], axis=-1)
    halves = (slice(0, LANES), slice(LANES, wide))

    def spread(x, c0):
        col = lambda c: jnp.broadcast_to(x[:, c:c + 1], (CHUNK, LANES))
        return jnp.concatenate([jnp.where(head0, col(c0), col(c0 + 1)),
                                jnp.where(head0, col(c0 + 2), col(c0 + 3))], axis=-1)

    def chunk_group(it, carry):
        ns = [it * CHUNK_GROUP + g for g in range(CHUNK_GROUP)]
        sls = [pl.ds(n * CHUNK, CHUNK) for n in ns]
        k2 = [k_s[sl, :] for sl in sls]
        q2 = [q_s[sl, :] for sl in sls]
        gsum = []
        for sl in sls:
            g2 = g_s[sl, :]
            hi = g2.astype(BF16)
            lo = (g2 - hi.astype(F32)).astype(BF16)
            gsum.append(_dot(tri2, jnp.concatenate([hi, lo], axis=-1)))
        kq = [_dot_nt(jnp.concatenate([k, q], axis=0).astype(BF16), bd(k)) for k, q in zip(k2, q2)]
        beta, egc, kd, ms = [], [], [], []
        for g, sl in enumerate(sls):
            gs = gsum[g][:, :LANES] + gsum[g][:, LANES:]
            gcum = jnp.concatenate([gs[:CHUNK], gs[CHUNK:]], axis=-1)
            beta.append(spread(beta_s[sl, :], 0))
            gc = jnp.where(fwd, spread(gcum[:, :LANES], 4), spread(gcum[:, LANES:], 4))
            gc_row = jnp.sum(jnp.where(eye_b, gc, 0.0), axis=0, keepdims=True)
            decay = jnp.where(incl, jnp.exp(gc - gc_row), 0.0)
            ms.append(jnp.where(strict, beta[g] * both(kq[g][:CHUNK]) * decay, 0.0))
            a = (both(kq[g][CHUNK:]) * decay).astype(BF16)
            egc.append(jnp.exp(gc))
            g_last = jnp.where(fwd_row, gc[CHUNK - 1:CHUNK, :], gc[0:1, :])
            kd.append((both(k2[g]) * jnp.exp(g_last - gc)).astype(BF16))
            eg = jnp.exp(g_last)
            rows8 = pl.ds(ns[g] * 8, 8)
            for d in range(2):
                a_s[d, sl, :] = a[:, d * LANES:(d + 1) * LANES]
                eg_s[d, rows8, :] = jnp.broadcast_to(eg[:, d * LANES:(d + 1) * LANES], (8, LANES))
        chains = [(g, d) for g in range(CHUNK_GROUP) for d in range(2)]
        inv = _inv_unit_triangular_pairs([ms[g][:, halves[d]] for g, d in chains], same_block, same_head, eye)
        ys = [jnp.concatenate([bd((k2[g] * (beta[g] * egc[g])[:, halves[d]])),
                               bd(v_s[sls[g], :] * beta[g][:, halves[d]])], axis=-1) for g, d in chains]
        wus = [_dot(x.astype(BF16), y) for x, y in zip(inv, ys)]
        kwu = [_dot_tn(kd[g][:, halves[d]], wu.astype(BF16)) for wu, (g, d) in zip(wus, chains)]
        for wu, x, (g, d) in zip(wus, kwu, chains):
            u_s[d, sls[g], :] = wu[:, LANES:]
            qg = q2[g] * egc[g][:, halves[d]]
            lq_s[d, ns[g]] = jnp.concatenate([wu[:, :LANES], qg], axis=0).astype(BF16)
            kw_s[d, ns[g]] = jnp.where(same_head, x[:, :LANES], 0.0).astype(BF16)
            ku_s[d, ns[g]] = jnp.where(same_head, x[:, LANES:], 0.0)
        return carry

    for it in range(n_chunks // CHUNK_GROUP):
        chunk_group(it, 0)

    state = [jnp.zeros((LANES, LANES), F32) for _ in range(2)]
    outs = [[None] * n_chunks for _ in range(2)]
    res_prev = None
    for i in range(n_chunks + 2):
        if i < n_chunks:
            ns = (i, n_chunks - 1 - i)
            s16 = [s.astype(BF16) for s in state]
            res = [_dot(jnp.concatenate([kw_s[d, n], lq_s[d, n]], axis=0), s16[d])
                   for d, n in enumerate(ns)]
            for d, n in enumerate(ns):
                state[d] = state[d] * eg_s[d, n * 8:n * 8 + 1, :] - res[d][:LANES] + ku_s[d, n]
        if i >= 2 and i - 2 >= n_chunks // 2:
            for n in (i - 2, n_chunks - 1 - (i - 2)):
                sl = pl.ds(n * CHUNK, CHUNK)
                o = outs[0][n] + outs[1][n]
                ms = _split_dot(o * o, head_ones) * (1.0 / HEAD_DIM)
                o = o * lax.rsqrt(ms + NORM_EPS) * gain_ref[...]
                o_ref[sl, :] = (o * _silu(z_ref[sl, :].astype(F32))).astype(BF16)
        if 1 <= i <= n_chunks:
            for d, n in enumerate((i - 1, n_chunks - i)):
                sl = pl.ds(n * CHUNK, CHUNK)
                v_new = u_s[d, sl, :] - res_prev[d][LANES:LANES + CHUNK]
                outs[d][n] = res_prev[d][LANES + CHUNK:] + _dot(a_s[d, sl, :], bd(v_new))
        res_prev = res


def _split_dot_left(m, x):
    hi = x.astype(BF16)
    lo = (x - hi.astype(F32)).astype(BF16)
    return _dot(m, hi) + _dot(m, lo)


def _delta(cx, cz, gt, conv_w, alog, dtb, gain, batch, seq):
    t = cx.shape[0]
    xspec = lambda part: pl.BlockSpec((seq, LANES), lambda b, p: (b, part * C_PAIRS + p))
    wspec = lambda part: pl.BlockSpec((CONV_K, LANES), lambda b, p: (0, part * C_PAIRS + p))
    pair = pl.BlockSpec((seq, LANES), lambda b, p: (b, p))
    prow = pl.BlockSpec((1, 1, LANES), lambda b, p: (p, 0, 0))
    return pl.pallas_call(
        functools.partial(_delta_kernel, seq=seq),
        out_shape=jax.ShapeDtypeStruct((t, C_W), BF16),
        grid=(batch, C_PAIRS),
        in_specs=[xspec(0), xspec(1), xspec(2), wspec(0), wspec(1), wspec(2),
                  pair, pair, prow, prow,
                  _resident((1, LANES))],
        out_specs=pair,
        scratch_shapes=[pltpu.VMEM((3, seq + 16, LANES), F32),
                        pltpu.VMEM((seq, LANES), F32),
                        pltpu.VMEM((seq, LANES), F32),
                        pltpu.VMEM((seq, LANES), F32),
                        pltpu.VMEM((seq, LANES), F32),
                        pltpu.VMEM((seq, LANES), F32),
                        pltpu.VMEM((2, seq, LANES), F32),
                        pltpu.VMEM((2, seq // CHUNK, LANES, LANES), BF16),
                        pltpu.VMEM((2, seq, LANES), BF16),
                        pltpu.VMEM((2, seq // CHUNK, LANES, LANES), BF16),
                        pltpu.VMEM((2, seq // CHUNK, LANES, LANES), F32),
                        pltpu.VMEM((2, seq // CHUNK * 8, LANES), F32)],
        compiler_params=_cparams("parallel", "parallel"),
        name="delta",
    )(cx, cx, cx, conv_w, conv_w, conv_w, cz, gt, alog, dtb, gain)


def _out_ffn_kernel(h_ref, a_ref, b_ref, c_ref, w_ref, g_ref, wg_ref, wu_ref, wd_ref, o_ref):
    y = _dot(a_ref[...], w_ref[0:A_QW, :])
    y = y + _dot(b_ref[...], w_ref[A_QW:A_QW + B_W, :])
    y = y + _dot(c_ref[...], w_ref[A_QW + B_W:, :])
    o_ref[...] = _swiglu_residual(h_ref[...] + y, g_ref, wg_ref, wu_ref, wd_ref)


def _mix_out_ffn(h, a, b, c, layer, w, g, wg, wu, wd):
    t, d = h.shape
    tm = min(FFN_TILE, t)
    row = lambda width: pl.BlockSpec((tm, width), lambda i: (i, 0))
    return pl.pallas_call(
        _out_ffn_kernel,
        out_shape=jax.ShapeDtypeStruct((t, d), F32),
        grid=(t // tm,),
        in_specs=[row(d), row(A_QW), row(B_W), row(C_W)]
                 + [_layer_resident(p, layer) for p in (w, g, wg, wu, wd)],
        out_specs=row(d),
        compiler_params=_cparams("parallel"),
        name="mix_out_ffn",
    )(h, a, b, c, w, g, wg, wu, wd)


def _rope_tables(seq):
    rows = seq // GRID_W
    row = jnp.repeat(jnp.arange(rows), GRID_W).astype(F32)
    col = jnp.tile(jnp.arange(GRID_W), rows).astype(F32)
    n_freq = HEAD_DIM // 4
    inv = ROPE_THETA ** (-jnp.arange(n_freq, dtype=F32) / n_freq)
    ang = jnp.concatenate([row[:, None] * inv, col[:, None] * inv], axis=-1)
    cos, sin = jnp.cos(ang), jnp.sin(ang)
    cos_h = jnp.concatenate([cos, cos], axis=-1)
    sin_h = jnp.concatenate([-sin, sin], axis=-1)
    return jnp.tile(cos_h, (1, A_HEADS)), jnp.tile(sin_h, (1, A_HEADS))


def _pair_columns(x):
    lead = x.shape[:-1]
    x = x.reshape(lead + (2, C_PAIRS, 2))
    return jnp.moveaxis(x, -2, -3).reshape(lead + (C_PAIRS, 4))


def _pad_w_in(w_in):
    lead = w_in.shape[:-1]
    w_in = w_in.astype(BF16)
    aq = [w_in[..., hq * HEAD_DIM:(hq + 1) * HEAD_DIM] for hq in A_Q_HEAD_ORDER]
    cb = _pair_columns(w_in[..., OFF_G:OFF_G + 2 * C_HEADS])
    ca = _pair_columns(w_in[..., OFF_G + 2 * C_HEADS:])
    gates = jnp.concatenate([cb, ca, jnp.zeros(lead + (C_PAIRS, LANES - 8), BF16)], axis=-1)
    return jnp.concatenate(aq + [w_in[..., A_QW:OFF_G], gates.reshape(lead + (GATE_W,))], axis=-1)


def _pair_row(p):
    v = _pair_columns(p.reshape(2 * C_HEADS).astype(F32))
    z4 = jnp.zeros((C_PAIRS, 4), F32)
    return jnp.concatenate([z4, v, jnp.zeros((C_PAIRS, LANES - 8), F32)], axis=-1)[:, None, :]


def kernel(x, rel_bias, ffn1_norm, ffn1_w_gate, ffn1_w_up, ffn1_w_down, mix_norm, w_in, a_q_norm, a_k_norm, b_q_norm, b_k_norm, c_conv, c_A_log, c_dt_bias, c_out_norm, w_out, ffn2_norm, ffn2_w_gate, ffn2_w_up, ffn2_w_down):
    batch, seq, d = x.shape
    depth = w_in.shape[0]
    scale = HEAD_DIM ** -0.5 * LOG2E
    h = x.reshape(batch * seq, d)
    cos, sin = _rope_tables(seq)
    idx = np.arange(MXU_W)
    mblk = jnp.asarray((idx[:, None] // HEAD_DIM == idx[None, :] // HEAD_DIM) / HEAD_DIM, BF16)
    biases = [_branch_bias(rel_bias, dil, seq // dil) for _, dil in DIL_PAIRS]
    row = lambda v: v.reshape(1, -1).astype(F32)
    ffn1 = (ffn1_norm.astype(F32)[:, None, :], ffn1_w_gate.astype(BF16), ffn1_w_up.astype(BF16),
            ffn1_w_down.astype(BF16))
    ffn2 = (ffn2_norm.astype(F32)[:, None, :], ffn2_w_gate.astype(BF16), ffn2_w_up.astype(BF16),
            ffn2_w_down.astype(BF16))
    wo_a = [w_out[:, hq * HEAD_DIM:(hq + 1) * HEAD_DIM] for hq in A_Q_HEAD_ORDER]
    wo = jnp.concatenate(wo_a + [w_out[:, A_QW:]], axis=1).astype(BF16)
    mix_g = mix_norm.astype(F32)[:, None, :]
    w_proj = _pad_w_in(w_in)
    for l in range(depth):
        h = _ffn(h, l, *ffn1)
        aq, ak, avt, bq, bk, bv, cx, cz, gt = _proj(
            h, l, mix_g, w_proj, cos, sin,
            row(jnp.tile(a_q_norm[l], A_HEADS)) * scale, row(jnp.tile(a_k_norm[l], A_KV_HEADS)),
            row(jnp.tile(b_q_norm[l], B_HEADS)) * scale, row(jnp.tile(b_k_norm[l], B_HEADS)),
            mblk, seq)
        out_a = _attn_a(aq, ak, avt, seq)
        out_b = _attn_b(bq, bk, bv, biases, batch, seq)
        out_c = _delta(cx, cz, gt, c_conv[l].astype(F32), _pair_row(c_A_log[l]), _pair_row(c_dt_bias[l]),
                       row(jnp.tile(c_out_norm[l], 2)), batch, seq)
        h = _mix_out_ffn(h, out_a, out_b, out_c, l, wo, *ffn2)
    return h.reshape(batch, seq, d)
```

```python
import functools
import math

import numpy as np
import jax
import jax.numpy as jnp
from jax import lax
from jax.experimental import pallas as pl
from jax.experimental.pallas import tpu as pltpu

F32 = jnp.float32
BF16 = jnp.bfloat16

HEAD_DIM = 64
A_HEADS = 4
A_KV_HEADS = 2
B_HEADS = 6
C_HEADS = 6
GRID_W = 64
ROPE_THETA = 10000.0
DIL_PAIRS = ((128, 1), (512, 4), (2048, 16))
REL_BUCKETS = 32
REL_MAX_DIST = 1024
CONV_K = 5
CHUNK = 64
NORM_EPS = 1e-6
NEG_INF = -1e30
LOG2E = math.log2(math.e)

A_QW = A_HEADS * HEAD_DIM
A_KVW = A_KV_HEADS * HEAD_DIM
B_W = B_HEADS * HEAD_DIM
C_W = C_HEADS * HEAD_DIM
C_PAIRS = C_HEADS // 2
LANES = 128
MXU_W = 256
GATE_W = C_PAIRS * LANES

OFF_A = 0
OFF_B = OFF_A + A_QW + 2 * A_KVW
OFF_C = OFF_B + 3 * B_W
OFF_Z = OFF_C + 3 * C_W
OFF_G = OFF_Z + C_W
N_PROJ = OFF_G + GATE_W

TOKEN_TILE = 512
FFN_TILE = 1024
FFN_SPLIT = 2
A_Q_TILE = 512
A_K_CHUNK = 256
B_Q_TILE = 128
B_K_TILE = 256
B_SIDE = 64
CONV_TILE = 256
CHUNK_GROUP = 8
VMEM_LIMIT = 56 * 1024 * 1024

assert A_KV_HEADS * HEAD_DIM == LANES and A_HEADS == 2 * A_KV_HEADS
assert B_HEADS % 2 == 0 and C_HEADS % 2 == 0 and 2 * HEAD_DIM == LANES
assert all(window == 2 * B_SIDE * dil for window, dil in DIL_PAIRS)


def _cparams(*sem):
    return pltpu.CompilerParams(dimension_semantics=sem, vmem_limit_bytes=VMEM_LIMIT)


def _resident(shape):
    return pl.BlockSpec(shape, lambda *_: (0,) * len(shape), pipeline_mode=pl.Buffered(1))


def _dot(a, b):
    return jnp.dot(a, b, preferred_element_type=F32)


def _dot_nt(a, b):
    return lax.dot_general(a, b, (((1,), (1,)), ((), ())), preferred_element_type=F32)


def _dot_tn(a, b):
    return lax.dot_general(a, b, (((0,), (0,)), ((), ())), preferred_element_type=F32)


def _silu(x):
    return x / (1.0 + jnp.exp(-x))


def _split_dot(x, m):
    hi = x.astype(BF16)
    lo = (x - hi.astype(F32)).astype(BF16)
    return _dot(hi, m) + _dot(lo, m)


def _rms_rows(x, g):
    ms = jnp.mean(x * x, axis=-1, keepdims=True)
    return x * lax.rsqrt(ms + NORM_EPS) * g


def _layer_resident(stacked, layer):
    shape = stacked.shape[1:]
    return pl.BlockSpec((None,) + shape, lambda *_: (layer,) + (0,) * len(shape),
                        pipeline_mode=pl.Buffered(1))


def _swiglu_residual(x, g_ref, wg_ref, wu_ref, wd_ref):
    xn = _rms_rows(x, g_ref[...]).astype(BF16)
    f = wg_ref.shape[1]
    step = -(-f // (FFN_SPLIT * MXU_W)) * MXU_W
    y = None
    for lo in range(0, f, step):
        hi = min(lo + step, f)
        gate = _dot(xn, wg_ref[:, lo:hi])
        up = _dot(xn, wu_ref[:, lo:hi])
        part = _dot((_silu(gate) * up).astype(BF16), wd_ref[lo:hi, :])
        y = part if y is None else y + part
    return x + 0.5 * y


def _ffn_kernel(h_ref, g_ref, wg_ref, wu_ref, wd_ref, o_ref):
    o_ref[...] = _swiglu_residual(h_ref[...], g_ref, wg_ref, wu_ref, wd_ref)


def _ffn(h, layer, g, wg, wu, wd):
    t, d = h.shape
    tm = min(FFN_TILE, t)
    return pl.pallas_call(
        _ffn_kernel,
        out_shape=jax.ShapeDtypeStruct((t, d), F32),
        grid=(t // tm,),
        in_specs=[pl.BlockSpec((tm, d), lambda i: (i, 0))]
                 + [_layer_resident(p, layer) for p in (g, wg, wu, wd)],
        out_specs=pl.BlockSpec((tm, d), lambda i: (i, 0)),
        compiler_params=_cparams("parallel"),
        name="ffn",
    )(h, g, wg, wu, wd)


def _rope(x, cos, sin):
    w = x.shape[1]
    lane = lax.broadcasted_iota(jnp.int32, x.shape, 1)
    first = (lane & (HEAD_DIM - 1)) < HEAD_DIM // 2
    half = HEAD_DIM // 2
    partner = jnp.where(first, pltpu.roll(x, w - half, 1), pltpu.roll(x, half, 1))
    return x * cos + partner * sin


def _proj_kernel(h_ref, g_ref, w_ref, cos_ref, sin_ref, gaq_ref, gak_ref, gbq_ref, gbk_ref,
                 mblk_ref, aq_o, ak_o, avt_o, bq_o, bk_o, bv_o, cx_o, cz_o, gt_o):
    xn = _rms_rows(h_ref[...], g_ref[...]).astype(BF16)
    pr = _dot(xn, w_ref[...])
    avt_o[...] = pr[:, OFF_A + A_QW + A_KVW:OFF_B].T.astype(BF16)

    def unit_rms(lo, width):
        outs = []
        for c in range(lo, lo + width, MXU_W):
            w = min(MXU_W, lo + width - c)
            x = pr[:, c:c + w]
            outs.append(x * lax.rsqrt(_split_dot(x * x, mblk_ref[:w, :w]) + NORM_EPS))
        return outs[0] if len(outs) == 1 else jnp.concatenate(outs, axis=-1)

    cos = cos_ref[...]
    sin = sin_ref[...]
    aq_o[...] = _rope(unit_rms(OFF_A, A_QW) * gaq_ref[...], cos, sin).astype(BF16)
    ak = unit_rms(OFF_A + A_QW, A_KVW) * gak_ref[...]
    ak_o[...] = _rope(ak, cos[:, :A_KVW], sin[:, :A_KVW]).astype(BF16)
    bqk = unit_rms(OFF_B, 2 * B_W)
    bq_o[...] = bqk[:, :B_W] * gbq_ref[...]
    bk_o[...] = bqk[:, B_W:] * gbk_ref[...]
    bv_o[...] = pr[:, OFF_B + 2 * B_W:OFF_C]
    cx_o[...] = pr[:, OFF_C:OFF_Z]
    cz_o[...] = pr[:, OFF_Z:OFF_G].astype(BF16)
    gt_o[...] = pr[:, OFF_G:]


def _proj(h, layer, g, w, cos, sin, gaq, gak, gbq, gbk, mblk, seq):
    t, d = h.shape
    tm = min(TOKEN_TILE, seq)
    per_seq = seq // tm
    row = lambda i: (i, 0)
    pos = lambda i: (i % per_seq, 0)
    widths = (A_QW, A_KVW, None, B_W, B_W, B_W, 3 * C_W, C_W, GATE_W)
    dtypes = (BF16, BF16, BF16, F32, F32, F32, F32, BF16, F32)
    shapes = [(t, wd) if wd else (A_KVW, t) for wd in widths]
    specs = [pl.BlockSpec((tm, wd), row) if wd else pl.BlockSpec((A_KVW, tm), lambda i: (0, i))
             for wd in widths]
    return pl.pallas_call(
        _proj_kernel,
        out_shape=[jax.ShapeDtypeStruct(s, dt) for s, dt in zip(shapes, dtypes)],
        grid=(t // tm,),
        in_specs=[pl.BlockSpec((tm, d), row),
                  _layer_resident(g, layer), _layer_resident(w, layer),
                  pl.BlockSpec((tm, A_QW), pos),
                  pl.BlockSpec((tm, A_QW), pos),
                  _resident((1, A_QW)), _resident((1, A_KVW)), _resident((1, B_W)),
                  _resident((1, B_W)), _resident((MXU_W, MXU_W))],
        out_specs=specs,
        compiler_params=_cparams("parallel"),
        name="mix_in",
    )(h, g, w, cos, sin, gaq, gak, gbq, gbk, mblk)


A_Q_HEAD_ORDER = tuple(hk * (A_HEADS // A_KV_HEADS) + g
                       for g in range(A_HEADS // A_KV_HEADS) for hk in range(A_KV_HEADS))


def _attn_a_kernel(q_ref, k_ref, vt_ref, o_ref):
    seq = k_ref.shape[0]
    lane_hi = lax.broadcasted_iota(jnp.int32, (1, LANES), 1) >= HEAD_DIM
    row_hi = lax.broadcasted_iota(jnp.int32, (LANES, 1), 0) >= HEAD_DIM
    heads = [(g, hk) for g in range(A_HEADS // A_KV_HEADS) for hk in range(A_KV_HEADS)]
    qm = []
    for g, hk in heads:
        qb = q_ref[:, g * LANES:(g + 1) * LANES].astype(F32)
        qm.append(jnp.where(lane_hi == (hk == 1), qb, 0.0).astype(BF16))
    def scores(c):
        k_c = k_ref[c:c + A_K_CHUNK, :]
        return [_dot_nt(k_c, q) for q in qm]

    m_run = [None] * len(heads)
    acc = [None] * len(heads)
    sts_next = scores(0)
    for c in range(0, seq, A_K_CHUNK):
        sts = sts_next
        if c + A_K_CHUNK < seq:
            sts_next = scores(c + A_K_CHUNK)
        vt_c = vt_ref[:, c:c + A_K_CHUNK].astype(F32)
        vt_ones = [jnp.where(row_hi == (hk == 1), vt_c, 1.0).astype(BF16) for hk in range(A_KV_HEADS)]
        ps, m_news = [], []
        for i, st in enumerate(sts):
            m_new = jnp.max(st, axis=0, keepdims=True)
            if m_run[i] is not None:
                m_new = jnp.maximum(m_run[i], m_new)
            m_news.append(m_new)
            ps.append(jnp.exp2(st - m_new).astype(BF16))
        for i, (g, hk) in enumerate(heads):
            ov = _dot(vt_ones[hk], ps[i])
            acc[i] = ov if acc[i] is None else acc[i] * jnp.exp2(m_run[i] - m_news[i]) + ov
            m_run[i] = m_news[i]
    for g in range(A_HEADS // A_KV_HEADS):
        halves = []
        for hk in range(A_KV_HEADS):
            ov = acc[g * A_KV_HEADS + hk]
            den_row = (1 - hk) * HEAD_DIM
            halves.append(ov / ov[den_row:den_row + 1, :])
        ot = jnp.where(row_hi, halves[1], halves[0])
        o_ref[:, g * LANES:(g + 1) * LANES] = ot.T.astype(BF16)


def _attn_a(q, k, vt, seq):
    t = q.shape[0]
    tq = min(A_Q_TILE, seq)
    per_seq = seq // tq
    return pl.pallas_call(
        _attn_a_kernel,
        out_shape=jax.ShapeDtypeStruct((t, A_QW), BF16),
        grid=(t // seq, per_seq),
        in_specs=[pl.BlockSpec((tq, A_QW), lambda b, i: (b * per_seq + i, 0)),
                  pl.BlockSpec((seq, A_KVW), lambda b, i: (b, 0)),
                  pl.BlockSpec((A_KVW, seq), lambda b, i: (0, b))],
        out_specs=pl.BlockSpec((tq, A_QW), lambda b, i: (b * per_seq + i, 0)),
        compiler_params=_cparams("parallel", "parallel"),
        name="attn_a",
    )(q, k, vt)


def _t5_bucket_np(rel):
    half = REL_BUCKETS // 2
    exact = half // 2
    sign = np.where(rel > 0, half, 0)
    n = np.abs(rel)
    nf = np.maximum(n, 1).astype(np.float32)
    large = exact + (np.log(nf / np.float32(exact)) / np.float32(math.log(REL_MAX_DIST / exact))
                     * np.float32(half - exact)).astype(np.int32)
    large = np.minimum(large, half - 1)
    return sign + np.where(n < exact, n, large)


def _branch_tiles(sub_len):
    if sub_len >= B_K_TILE:
        return B_Q_TILE, B_K_TILE, (0, -B_SIDE, B_Q_TILE - B_K_TILE)
    return sub_len, sub_len, (0,)


def _branch_bias(rel_bias, dil, sub_len):
    tq, tk, offs = _branch_tiles(sub_len)
    period = tq + tk
    slot = np.arange(period)
    tabs = []
    for off in offs:
        delta = np.where(slot < tk, slot, slot - period) + off
        bucket = _t5_bucket_np(delta * dil)
        vec = jnp.where((np.abs(delta) <= B_SIDE)[:, None], rel_bias.astype(F32)[bucket] * LOG2E, NEG_INF).T
        flat = jnp.tile(vec, (1, tq))[:, :tq * (period - 1)]
        tabs.append(flat.reshape(B_HEADS, tq, period - 1)[:, :, :tk])
    return jnp.stack(tabs)


def _attn_b_kernel(q_ref, k_ref, v_ref, b0_ref, b1_ref, b2_ref, o_ref, num_s, mx_s, den_s, *, seq):
    head0 = lax.broadcasted_iota(jnp.int32, (1, LANES), 1) < HEAD_DIM
    pick = lambda a, b: jnp.where(head0, a, b)

    tiles = []
    for g, ((_, dil), bias_ref) in enumerate(zip(DIL_PAIRS, (b0_ref, b1_ref, b2_ref))):
        sub_len = seq // dil
        tq, tk, _ = _branch_tiles(sub_len)
        n_tiles = sub_len // tq
        for r in range(dil):
            for t in range(n_tiles):
                if n_tiles == 1:
                    k0, tab = 0, 0
                else:
                    k0 = min(max(t * tq - B_SIDE, 0), sub_len - tk)
                    tab = 0 if t == 0 else 2 if t == n_tiles - 1 else 1
                tiles.append((g, bias_ref, tab, pl.ds(r + dil * t * tq, tq, stride=dil),
                              pl.ds(r + dil * k0, tk, stride=dil)))

    for g, bias_ref, tab, rows, keys in tiles:
        tq = rows.size
        qt = q_ref[rows, :]
        q2 = jnp.concatenate([pick(qt, 0.0), pick(0.0, qt)], axis=0).astype(BF16)
        s = _dot_nt(q2, k_ref[keys, :].astype(BF16)) + bias_ref[tab]
        m = jnp.max(s, axis=-1, keepdims=True)
        p = jnp.exp2(s - m)
        l = jnp.sum(p, axis=-1, keepdims=True)
        pv = _dot(p.astype(BF16), v_ref[keys, :].astype(BF16))
        num_s.at[g][rows, :] = pick(pv[:tq], pv[tq:])
        mx_s.at[g][rows, :] = pick(m[:tq], m[tq:])
        den_s.at[g][rows, :] = pick(l[:tq], l[tq:])

    def merge(t, carry):
        rows = pl.ds(pl.multiple_of(t * CONV_TILE, CONV_TILE), CONV_TILE)
        ms = [mx_s[g, rows, :] for g in range(3)]
        top = jnp.maximum(jnp.maximum(ms[0], ms[1]), ms[2])
        es = [jnp.exp2(m - top) for m in ms]
        num = sum(e * num_s[g, rows, :] for g, e in enumerate(es))
        den = sum(e * den_s[g, rows, :] for g, e in enumerate(es))
        o_ref[rows, :] = (num / den).astype(BF16)
        return carry

    lax.fori_loop(0, seq // CONV_TILE, merge, 0)


def _attn_b(q, k, v, biases, batch, seq):
    t = q.shape[0]
    pair = pl.BlockSpec((seq, LANES), lambda b, p: (b, p))
    biases = [x.reshape(x.shape[0], B_HEADS // 2, 2 * x.shape[2], x.shape[3]) for x in biases]
    bias_spec = lambda x: pl.BlockSpec((x.shape[0], None) + x.shape[2:], lambda b, p: (0, p, 0, 0))
    return pl.pallas_call(
        functools.partial(_attn_b_kernel, seq=seq),
        out_shape=jax.ShapeDtypeStruct((t, B_W), BF16),
        grid=(batch, B_HEADS // 2),
        in_specs=[pair, pair, pair] + [bias_spec(x) for x in biases],
        out_specs=pair,
        scratch_shapes=[pltpu.VMEM((len(DIL_PAIRS), seq, LANES), F32)] * 3,
        compiler_params=_cparams("parallel", "parallel"),
        name="attn_b",
    )(q, k, v, *biases)


BASE_BLOCK = 8


def _pair_block_diag(x, same_head):
    reps = same_head.shape[0] // x.shape[0]
    return jnp.where(same_head, jnp.concatenate([x] * reps, axis=0), 0.0).astype(BF16)


def _inv_unit_triangular_pairs(ms, same_block, same_head, eye):
    bd = lambda x: _pair_block_diag(x, same_head)
    b16 = lambda x: x.astype(BF16)
    m8 = [b16(jnp.where(same_block[BASE_BLOCK], m, 0.0)) for m in ms]
    p2 = [_dot(a, bd(a.astype(F32))) for a in m8]
    yield
    p2h = [b16(p) for p in p2]
    p4 = [_dot(a, bd(p)) for a, p in zip(p2h, p2)]
    yield
    p6 = [_dot(a, bd(p)) for a, p in zip(p2h, p4)]
    yield
    even = [eye + a + b + c for a, b, c in zip(p2, p4, p6)]
    inv = [e - _dot(a, bd(e)) for a, e in zip(m8, even)]
    yield
    size = BASE_BLOCK
    while size < CHUNK:
        joins = same_block[2 * size] & jnp.logical_not(same_block[size])
        t = [_dot(b16(jnp.where(joins, m, 0.0)), bd(x)) for m, x in zip(ms, inv)]
        yield
        inv = [x - _dot(b16(x), bd(y)) for x, y in zip(inv, t)]
        yield
        size *= 2
    return inv


def _delta_kernel(xq_ref, xk_ref, xv_ref, wq_ref, wk_ref, wv_ref, z_ref, gt_ref,
                  alog_ref, dtb_ref, gain_ref, o_ref,
                  pad_s, q_s, k_s, v_s, beta_s, g_s,
                  u_s, lq_s, a_s, kw_s, ku_s, eg_s, *, seq):
    n_chunks = seq // CHUNK
    pad = 8
    sq_r = lax.broadcasted_iota(jnp.int32, (LANES, LANES), 0)
    sq_c = lax.broadcasted_iota(jnp.int32, (LANES, LANES), 1)
    same_head = (sq_r < HEAD_DIM) == (sq_c < HEAD_DIM)
    head_ones = same_head.astype(BF16)

    zeros = jnp.zeros((pad, LANES), F32)
    for j, (x_ref, w_ref, dst, scale) in enumerate(((xq_ref, wq_ref, q_s, HEAD_DIM ** -0.5),
                                                     (xk_ref, wk_ref, k_s, 1.0),
                                                     (xv_ref, wv_ref, v_s, None))):
        pad_s[j, 0:pad, :] = zeros
        pad_s[j, pad + seq:pad + seq + pad, :] = zeros
        pad_s[j, pad:pad + seq, :] = x_ref[...]
        for r0 in range(0, seq, CONV_TILE):
            acc = jnp.zeros((CONV_TILE, LANES), F32)
            for d in range(CONV_K):
                lo = r0 + pad - CONV_K // 2 + d
                acc = acc + w_ref[d:d + 1, :] * pad_s[j, lo:lo + CONV_TILE, :]
            y = _silu(acc)
            if scale is not None:
                y = y * lax.rsqrt(_split_dot(y * y, head_ones) + NORM_EPS) * scale
            dst[r0:r0 + CONV_TILE, :] = y

    gt = gt_ref[...]
    beta_s[...] = 1.0 / (1.0 + jnp.exp(-gt))
    gx = gt + dtb_ref[0]
    softplus = jnp.maximum(gx, 0.0) + jnp.log(1.0 + jnp.exp(-jnp.abs(gx)))
    g_s[...] = -jnp.exp(alog_ref[0]) * softplus

    wide = 2 * LANES
    lane_w = lax.broadcasted_iota(jnp.int32, (CHUNK, wide), 1)
    rows = lax.broadcasted_iota(jnp.int32, (CHUNK, wide), 0)
    cols = lane_w & (HEAD_DIM - 1)
    fwd = lane_w < LANES
    fwd_row = lax.broadcasted_iota(jnp.int32, (1, wide), 1) < LANES
    head0 = lax.broadcasted_iota(jnp.int32, (CHUNK, LANES), 1) < HEAD_DIM
    eye_b = rows == cols
    rows_p = lax.broadcasted_iota(jnp.int32, (CHUNK, LANES), 0)
    cols_p = lax.broadcasted_iota(jnp.int32, (CHUNK, LANES), 1) & (HEAD_DIM - 1)
    eye = (rows_p == cols_p).astype(F32)
    same_block = {}
    size = BASE_BLOCK
    while size <= CHUNK:
        shift = size.bit_length() - 1
        same_block[size] = (rows_p >> shift) == (cols_p >> shift)
        size *= 2
    r2 = lax.broadcasted_iota(jnp.int32, (2 * CHUNK, CHUNK), 0)
    c2 = lax.broadcasted_iota(jnp.int32, (2 * CHUNK, CHUNK), 1)
    tri2 = (jnp.where(r2 < CHUNK, r2 - c2, c2 - (r2 - CHUNK)) >= 0).astype(BF16)
    below = jnp.where(fwd, rows - cols, cols - rows)
    incl = below >= 0
    strict = below > 0
    bd = lambda x: _pair_block_diag(x, same_head)
    both = lambda x: jnp.concatenate([x, x], axis=-1)
    halves = (slice(0, LANES), slice(LANES, wide))

    def spread(x, c0):
        col = lambda c: jnp.broadcast_to(x[:, c:c + 1], (CHUNK, LANES))
        return jnp.concatenate([jnp.where(head0, col(c0), col(c0 + 1)),
                                jnp.where(head0, col(c0 + 2), col(c0 + 3))], axis=-1)

    def chunk_group(it):
        ns = [it * CHUNK_GROUP + g for g in range(CHUNK_GROUP)]
        sls = [pl.ds(n * CHUNK, CHUNK) for n in ns]
        k2 = [k_s[sl, :] for sl in sls]
        q2 = [q_s[sl, :] for sl in sls]
        gsum = []
        for sl in sls:
            g2 = g_s[sl, :]
            hi = g2.astype(BF16)
            lo = (g2 - hi.astype(F32)).astype(BF16)
            gsum.append(_dot(tri2, jnp.concatenate([hi, lo], axis=-1)))
        kq = [_dot_nt(jnp.concatenate([k, q], axis=0).astype(BF16), bd(k)) for k, q in zip(k2, q2)]
        yield
        beta, egc, kd, ms = [], [], [], []
        for g, sl in enumerate(sls):
            gs = gsum[g][:, :LANES] + gsum[g][:, LANES:]
            gcum = jnp.concatenate([gs[:CHUNK], gs[CHUNK:]], axis=-1)
            beta.append(spread(beta_s[sl, :], 0))
            gc = jnp.where(fwd, spread(gcum[:, :LANES], 4), spread(gcum[:, LANES:], 4))
            gc_row = jnp.sum(jnp.where(eye_b, gc, 0.0), axis=0, keepdims=True)
            decay = jnp.where(incl, jnp.exp(gc - gc_row), 0.0)
            ms.append(jnp.where(strict, beta[g] * both(kq[g][:CHUNK]) * decay, 0.0))
            a = (both(kq[g][CHUNK:]) * decay).astype(BF16)
            egc.append(jnp.exp(gc))
            g_last = jnp.where(fwd_row, gc[CHUNK - 1:CHUNK, :], gc[0:1, :])
            kd.append((both(k2[g]) * jnp.exp(g_last - gc)).astype(BF16))
            eg = jnp.exp(g_last)
            rows8 = pl.ds(ns[g] * 8, 8)
            for d in range(2):
                a_s[d, sl, :] = a[:, d * LANES:(d + 1) * LANES]
                eg_s[d, rows8, :] = jnp.broadcast_to(eg[:, d * LANES:(d + 1) * LANES], (8, LANES))
        chains = [(g, d) for g in range(CHUNK_GROUP) for d in range(2)]
        inv = yield from _inv_unit_triangular_pairs(
            [ms[g][:, halves[d]] for g, d in chains], same_block, same_head, eye)
        ys = [jnp.concatenate([bd((k2[g] * (beta[g] * egc[g])[:, halves[d]])),
                               bd(v_s[sls[g], :] * beta[g][:, halves[d]])], axis=-1) for g, d in chains]
        wus = [_dot(x.astype(BF16), y) for x, y in zip(inv, ys)]
        yield
        kwu = [_dot_tn(kd[g][:, halves[d]], wu.astype(BF16)) for wu, (g, d) in zip(wus, chains)]
        for wu, x, (g, d) in zip(wus, kwu, chains):
            u_s[d, sls[g], :] = wu[:, LANES:]
            qg = q2[g] * egc[g][:, halves[d]]
            lq_s[d, ns[g]] = jnp.concatenate([wu[:, :LANES], qg], axis=0).astype(BF16)
            kw_s[d, ns[g]] = jnp.where(same_head, x[:, :LANES], 0.0).astype(BF16)
            ku_s[d, ns[g]] = jnp.where(same_head, x[:, LANES:], 0.0)

    state = [jnp.zeros((LANES, LANES), F32) for _ in range(2)]
    outs = [[None] * n_chunks for _ in range(2)]
    pending = [None, None]
    steps = [0, 0]
    finished = []

    def chunk_of(d, k):
        return k if d == 0 else n_chunks - 1 - k

    def finish_chunk(n):
        sl = pl.ds(n * CHUNK, CHUNK)
        o = outs[0][n] + outs[1][n]
        ms = _split_dot(o * o, head_ones) * (1.0 / HEAD_DIM)
        o = o * lax.rsqrt(ms + NORM_EPS) * gain_ref[...]
        o_ref[sl, :] = (o * _silu(z_ref[sl, :].astype(F32))).astype(BF16)

    def emit_output(d):
        n, res = pending[d]
        sl = pl.ds(n * CHUNK, CHUNK)
        v_new = u_s[d, sl, :] - res[LANES:LANES + CHUNK]
        outs[d][n] = res[LANES + CHUNK:] + _dot(a_s[d, sl, :], bd(v_new))
        pending[d] = None
        if outs[1 - d][n] is not None:
            finished.append(n)

    def scan_step(ds):
        older = list(finished)
        del finished[:]
        new = []
        for d in ds:
            n = chunk_of(d, steps[d])
            res = _dot(jnp.concatenate([kw_s[d, n], lq_s[d, n]], axis=0), state[d].astype(BF16))
            state[d] = state[d] * eg_s[d, n * 8:n * 8 + 1, :] - res[:LANES] + ku_s[d, n]
            new.append((d, n, res))
            steps[d] += 1
        for n in older:
            finish_chunk(n)
        for d, n, res in new:
            if pending[d] is not None:
                emit_output(d)
            pending[d] = (n, res)

    n_groups = n_chunks // CHUNK_GROUP
    order = [g // 2 if g % 2 == 0 else n_groups - 1 - g // 2 for g in range(n_groups)]
    done = set()

    def ready(d):
        return steps[d] < n_chunks and chunk_of(d, steps[d]) // CHUNK_GROUP in done

    for it in order:
        for _ in chunk_group(it):
            ds = [d for d in range(2) if ready(d)]
            if ds:
                scan_step(ds)
        done.add(it)
    while steps[0] < n_chunks or steps[1] < n_chunks:
        scan_step([d for d in range(2) if ready(d)])
    for d in range(2):
        emit_output(d)
    for n in finished:
        finish_chunk(n)


def _split_dot_left(m, x):
    hi = x.astype(BF16)
    lo = (x - hi.astype(F32)).astype(BF16)
    return _dot(m, hi) + _dot(m, lo)


def _delta(cx, cz, gt, conv_w, alog, dtb, gain, batch, seq):
    t = cx.shape[0]
    xspec = lambda part: pl.BlockSpec((seq, LANES), lambda b, p: (b, part * C_PAIRS + p))
    wspec = lambda part: pl.BlockSpec((CONV_K, LANES), lambda b, p: (0, part * C_PAIRS + p))
    pair = pl.BlockSpec((seq, LANES), lambda b, p: (b, p))
    prow = pl.BlockSpec((1, 1, LANES), lambda b, p: (p, 0, 0))
    return pl.pallas_call(
        functools.partial(_delta_kernel, seq=seq),
        out_shape=jax.ShapeDtypeStruct((t, C_W), BF16),
        grid=(batch, C_PAIRS),
        in_specs=[xspec(0), xspec(1), xspec(2), wspec(0), wspec(1), wspec(2),
                  pair, pair, prow, prow,
                  _resident((1, LANES))],
        out_specs=pair,
        scratch_shapes=[pltpu.VMEM((3, seq + 16, LANES), F32),
                        pltpu.VMEM((seq, LANES), F32),
                        pltpu.VMEM((seq, LANES), F32),
                        pltpu.VMEM((seq, LANES), F32),
                        pltpu.VMEM((seq, LANES), F32),
                        pltpu.VMEM((seq, LANES), F32),
                        pltpu.VMEM((2, seq, LANES), F32),
                        pltpu.VMEM((2, seq // CHUNK, LANES, LANES), BF16),
                        pltpu.VMEM((2, seq, LANES), BF16),
                        pltpu.VMEM((2, seq // CHUNK, LANES, LANES), BF16),
                        pltpu.VMEM((2, seq // CHUNK, LANES, LANES), F32),
                        pltpu.VMEM((2, seq // CHUNK * 8, LANES), F32)],
        compiler_params=_cparams("parallel", "parallel"),
        name="delta",
    )(cx, cx, cx, conv_w, conv_w, conv_w, cz, gt, alog, dtb, gain)


def _out_ffn_kernel(h_ref, a_ref, b_ref, c_ref, w_ref, g_ref, wg_ref, wu_ref, wd_ref, o_ref):
    y = _dot(a_ref[...], w_ref[0:A_QW, :])
    y = y + _dot(b_ref[...], w_ref[A_QW:A_QW + B_W, :])
    y = y + _dot(c_ref[...], w_ref[A_QW + B_W:, :])
    o_ref[...] = _swiglu_residual(h_ref[...] + y, g_ref, wg_ref, wu_ref, wd_ref)


def _mix_out_ffn(h, a, b, c, layer, w, g, wg, wu, wd):
    t, d = h.shape
    tm = min(FFN_TILE, t)
    row = lambda width: pl.BlockSpec((tm, width), lambda i: (i, 0))
    return pl.pallas_call(
        _out_ffn_kernel,
        out_shape=jax.ShapeDtypeStruct((t, d), F32),
        grid=(t // tm,),
        in_specs=[row(d), row(A_QW), row(B_W), row(C_W)]
                 + [_layer_resident(p, layer) for p in (w, g, wg, wu, wd)],
        out_specs=row(d),
        compiler_params=_cparams("parallel"),
        name="mix_out_ffn",
    )(h, a, b, c, w, g, wg, wu, wd)


def _rope_tables(seq):
    rows = seq // GRID_W
    row = jnp.repeat(jnp.arange(rows), GRID_W).astype(F32)
    col = jnp.tile(jnp.arange(GRID_W), rows).astype(F32)
    n_freq = HEAD_DIM // 4
    inv = ROPE_THETA ** (-jnp.arange(n_freq, dtype=F32) / n_freq)
    ang = jnp.concatenate([row[:, None] * inv, col[:, None] * inv], axis=-1)
    cos, sin = jnp.cos(ang), jnp.sin(ang)
    cos_h = jnp.concatenate([cos, cos], axis=-1)
    sin_h = jnp.concatenate([-sin, sin], axis=-1)
    return jnp.tile(cos_h, (1, A_HEADS)), jnp.tile(sin_h, (1, A_HEADS))


def _pair_columns(x):
    lead = x.shape[:-1]
    x = x.reshape(lead + (2, C_PAIRS, 2))
    return jnp.moveaxis(x, -2, -3).reshape(lead + (C_PAIRS, 4))


def _pad_w_in(w_in):
    lead = w_in.shape[:-1]
    w_in = w_in.astype(BF16)
    aq = [w_in[..., hq * HEAD_DIM:(hq + 1) * HEAD_DIM] for hq in A_Q_HEAD_ORDER]
    cb = _pair_columns(w_in[..., OFF_G:OFF_G + 2 * C_HEADS])
    ca = _pair_columns(w_in[..., OFF_G + 2 * C_HEADS:])
    gates = jnp.concatenate([cb, ca, jnp.zeros(lead + (C_PAIRS, LANES - 8), BF16)], axis=-1)
    return jnp.concatenate(aq + [w_in[..., A_QW:OFF_G], gates.reshape(lead + (GATE_W,))], axis=-1)


def _pair_row(p):
    v = _pair_columns(p.reshape(2 * C_HEADS).astype(F32))
    z4 = jnp.zeros((C_PAIRS, 4), F32)
    return jnp.concatenate([z4, v, jnp.zeros((C_PAIRS, LANES - 8), F32)], axis=-1)[:, None, :]


def kernel(x, rel_bias, ffn1_norm, ffn1_w_gate, ffn1_w_up, ffn1_w_down, mix_norm, w_in, a_q_norm, a_k_norm, b_q_norm, b_k_norm, c_conv, c_A_log, c_dt_bias, c_out_norm, w_out, ffn2_norm, ffn2_w_gate, ffn2_w_up, ffn2_w_down):
    batch, seq, d = x.shape
    depth = w_in.shape[0]
    scale = HEAD_DIM ** -0.5 * LOG2E
    h = x.reshape(batch * seq, d)
    cos, sin = _rope_tables(seq)
    idx = np.arange(MXU_W)
    mblk = jnp.asarray((idx[:, None] // HEAD_DIM == idx[None, :] // HEAD_DIM) / HEAD_DIM, BF16)
    biases = [_branch_bias(rel_bias, dil, seq // dil) for _, dil in DIL_PAIRS]
    row = lambda v: v.reshape(1, -1).astype(F32)
    ffn1 = (ffn1_norm.astype(F32)[:, None, :], ffn1_w_gate.astype(BF16), ffn1_w_up.astype(BF16),
            ffn1_w_down.astype(BF16))
    ffn2 = (ffn2_norm.astype(F32)[:, None, :], ffn2_w_gate.astype(BF16), ffn2_w_up.astype(BF16),
            ffn2_w_down.astype(BF16))
    wo_a = [w_out[:, hq * HEAD_DIM:(hq + 1) * HEAD_DIM] for hq in A_Q_HEAD_ORDER]
    wo = jnp.concatenate(wo_a + [w_out[:, A_QW:]], axis=1).astype(BF16)
    mix_g = mix_norm.astype(F32)[:, None, :]
    w_proj = _pad_w_in(w_in)
    for l in range(depth):
        h = _ffn(h, l, *ffn1)
        aq, ak, avt, bq, bk, bv, cx, cz, gt = _proj(
            h, l, mix_g, w_proj, cos, sin,
            row(jnp.tile(a_q_norm[l], A_HEADS)) * scale, row(jnp.tile(a_k_norm[l], A_KV_HEADS)),
            row(jnp.tile(b_q_norm[l], B_HEADS)) * scale, row(jnp.tile(b_k_norm[l], B_HEADS)),
            mblk, seq)
        out_a = _attn_a(aq, ak, avt, seq)
        out_b = _attn_b(bq, bk, bv, biases, batch, seq)
        out_c = _delta(cx, cz, gt, c_conv[l].astype(F32), _pair_row(c_A_log[l]), _pair_row(c_dt_bias[l]),
                       row(jnp.tile(c_out_norm[l], 2)), batch, seq)
        h = _mix_out_ffn(h, out_a, out_b, out_c, l, wo, *ffn2)
    return h.reshape(batch, seq, d)
```

```python
import functools
import math

import numpy as np
import jax
import jax.numpy as jnp
from jax import lax
from jax.experimental import pallas as pl
from jax.experimental.pallas import tpu as pltpu

F32 = jnp.float32
BF16 = jnp.bfloat16

HEAD_DIM = 64
A_HEADS = 4
A_KV_HEADS = 2
B_HEADS = 6
C_HEADS = 6
GRID_W = 64
ROPE_THETA = 10000.0
DIL_PAIRS = ((128, 1), (512, 4), (2048, 16))
REL_BUCKETS = 32
REL_MAX_DIST = 1024
CONV_K = 5
CHUNK = 64
NORM_EPS = 1e-6
NEG_INF = -1e30
LOG2E = math.log2(math.e)

A_QW = A_HEADS * HEAD_DIM
A_KVW = A_KV_HEADS * HEAD_DIM
B_W = B_HEADS * HEAD_DIM
C_W = C_HEADS * HEAD_DIM
C_PAIRS = C_HEADS // 2
LANES = 128
MXU_W = 256
GATE_W = C_PAIRS * LANES

OFF_A = 0
OFF_B = OFF_A + A_QW + 2 * A_KVW
OFF_C = OFF_B + 3 * B_W
OFF_Z = OFF_C + 3 * C_W
OFF_G = OFF_Z + C_W
N_PROJ = OFF_G + GATE_W

TOKEN_TILE = 512
FFN_TILE = 1024
FFN_SPLIT = 2
A_Q_TILE = 512
A_K_CHUNK = 256
B_Q_TILE = 128
B_K_TILE = 256
B_SIDE = 64
B_PRE = 4
CONV_TILE = 256
CHUNK_GROUP = 8
VMEM_LIMIT = 56 * 1024 * 1024

assert A_KV_HEADS * HEAD_DIM == LANES and A_HEADS == 2 * A_KV_HEADS
assert B_HEADS % 2 == 0 and C_HEADS % 2 == 0 and 2 * HEAD_DIM == LANES
assert all(window == 2 * B_SIDE * dil for window, dil in DIL_PAIRS)


def _cparams(*sem):
    return pltpu.CompilerParams(dimension_semantics=sem, vmem_limit_bytes=VMEM_LIMIT)


def _resident(shape):
    return pl.BlockSpec(shape, lambda *_: (0,) * len(shape), pipeline_mode=pl.Buffered(1))


def _dot(a, b):
    return jnp.dot(a, b, preferred_element_type=F32)


def _dot_nt(a, b):
    return lax.dot_general(a, b, (((1,), (1,)), ((), ())), preferred_element_type=F32)


def _dot_tn(a, b):
    return lax.dot_general(a, b, (((0,), (0,)), ((), ())), preferred_element_type=F32)


def _silu(x):
    return x / (1.0 + jnp.exp(-x))


def _split_dot(x, m):
    hi = x.astype(BF16)
    lo = (x - hi.astype(F32)).astype(BF16)
    return _dot(hi, m) + _dot(lo, m)


def _rms_rows(x, g):
    ms = jnp.mean(x * x, axis=-1, keepdims=True)
    return x * lax.rsqrt(ms + NORM_EPS) * g


def _layer_resident(stacked, layer):
    shape = stacked.shape[1:]
    return pl.BlockSpec((None,) + shape, lambda *_: (layer,) + (0,) * len(shape),
                        pipeline_mode=pl.Buffered(1))


def _swiglu_residual(x, g_ref, wg_ref, wu_ref, wd_ref):
    xn = _rms_rows(x, g_ref[...]).astype(BF16)
    f = wg_ref.shape[1]
    step = -(-f // (FFN_SPLIT * MXU_W)) * MXU_W
    y = None
    for lo in range(0, f, step):
        hi = min(lo + step, f)
        gate = _dot(xn, wg_ref[:, lo:hi])
        up = _dot(xn, wu_ref[:, lo:hi])
        part = _dot((_silu(gate) * up).astype(BF16), wd_ref[lo:hi, :])
        y = part if y is None else y + part
    return x + 0.5 * y


def _ffn_kernel(h_ref, g_ref, wg_ref, wu_ref, wd_ref, o_ref):
    o_ref[...] = _swiglu_residual(h_ref[...], g_ref, wg_ref, wu_ref, wd_ref)


def _ffn(h, layer, g, wg, wu, wd):
    t, d = h.shape
    tm = min(FFN_TILE, t)
    return pl.pallas_call(
        _ffn_kernel,
        out_shape=jax.ShapeDtypeStruct((t, d), F32),
        grid=(t // tm,),
        in_specs=[pl.BlockSpec((tm, d), lambda i: (i, 0))]
                 + [_layer_resident(p, layer) for p in (g, wg, wu, wd)],
        out_specs=pl.BlockSpec((tm, d), lambda i: (i, 0)),
        compiler_params=_cparams("parallel"),
        name="ffn",
    )(h, g, wg, wu, wd)


def _rope(x, cos, sin):
    w = x.shape[1]
    lane = lax.broadcasted_iota(jnp.int32, x.shape, 1)
    first = (lane & (HEAD_DIM - 1)) < HEAD_DIM // 2
    half = HEAD_DIM // 2
    partner = jnp.where(first, pltpu.roll(x, w - half, 1), pltpu.roll(x, half, 1))
    return x * cos + partner * sin


def _proj_kernel(h_ref, g_ref, w_ref, cos_ref, sin_ref, gaq_ref, gak_ref, gbq_ref, gbk_ref,
                 mblk_ref, aq_o, ak_o, avt_o, bq_o, bk_o, bv_o, cx_o, cz_o, gt_o):
    xn = _rms_rows(h_ref[...], g_ref[...]).astype(BF16)
    pr = _dot(xn, w_ref[...])
    avt_o[...] = pr[:, OFF_A + A_QW + A_KVW:OFF_B].T.astype(BF16)

    def unit_rms(lo, width):
        outs = []
        for c in range(lo, lo + width, MXU_W):
            w = min(MXU_W, lo + width - c)
            x = pr[:, c:c + w]
            outs.append(x * lax.rsqrt(_split_dot(x * x, mblk_ref[:w, :w]) + NORM_EPS))
        return outs[0] if len(outs) == 1 else jnp.concatenate(outs, axis=-1)

    cos = cos_ref[...]
    sin = sin_ref[...]
    aq_o[...] = _rope(unit_rms(OFF_A, A_QW) * gaq_ref[...], cos, sin).astype(BF16)
    ak = unit_rms(OFF_A + A_QW, A_KVW) * gak_ref[...]
    ak_o[...] = _rope(ak, cos[:, :A_KVW], sin[:, :A_KVW]).astype(BF16)
    bqk = unit_rms(OFF_B, 2 * B_W)
    bq_o[...] = bqk[:, :B_W] * gbq_ref[...]
    bk_o[...] = bqk[:, B_W:] * gbk_ref[...]
    bv_o[...] = pr[:, OFF_B + 2 * B_W:OFF_C]
    cx_o[...] = pr[:, OFF_C:OFF_Z]
    cz_o[...] = pr[:, OFF_Z:OFF_G].astype(BF16)
    gt_o[...] = pr[:, OFF_G:]


def _proj(h, layer, g, w, cos, sin, gaq, gak, gbq, gbk, mblk, seq):
    t, d = h.shape
    tm = min(TOKEN_TILE, seq)
    per_seq = seq // tm
    row = lambda i: (i, 0)
    pos = lambda i: (i % per_seq, 0)
    widths = (A_QW, A_KVW, None, B_W, B_W, B_W, 3 * C_W, C_W, GATE_W)
    dtypes = (BF16, BF16, BF16, F32, F32, F32, F32, BF16, F32)
    shapes = [(t, wd) if wd else (A_KVW, t) for wd in widths]
    specs = [pl.BlockSpec((tm, wd), row) if wd else pl.BlockSpec((A_KVW, tm), lambda i: (0, i))
             for wd in widths]
    return pl.pallas_call(
        _proj_kernel,
        out_shape=[jax.ShapeDtypeStruct(s, dt) for s, dt in zip(shapes, dtypes)],
        grid=(t // tm,),
        in_specs=[pl.BlockSpec((tm, d), row),
                  _layer_resident(g, layer), _layer_resident(w, layer),
                  pl.BlockSpec((tm, A_QW), pos),
                  pl.BlockSpec((tm, A_QW), pos),
                  _resident((1, A_QW)), _resident((1, A_KVW)), _resident((1, B_W)),
                  _resident((1, B_W)), _resident((MXU_W, MXU_W))],
        out_specs=specs,
        compiler_params=_cparams("parallel"),
        name="mix_in",
    )(h, g, w, cos, sin, gaq, gak, gbq, gbk, mblk)


A_Q_HEAD_ORDER = tuple(hk * (A_HEADS // A_KV_HEADS) + g
                       for g in range(A_HEADS // A_KV_HEADS) for hk in range(A_KV_HEADS))


def _attn_a_kernel(q_ref, k_ref, vt_ref, o_ref):
    seq = k_ref.shape[0]
    lane_hi = lax.broadcasted_iota(jnp.int32, (1, LANES), 1) >= HEAD_DIM
    row_hi = lax.broadcasted_iota(jnp.int32, (LANES, 1), 0) >= HEAD_DIM
    heads = [(g, hk) for g in range(A_HEADS // A_KV_HEADS) for hk in range(A_KV_HEADS)]
    qm = []
    for g, hk in heads:
        qb = q_ref[:, g * LANES:(g + 1) * LANES].astype(F32)
        qm.append(jnp.where(lane_hi == (hk == 1), qb, 0.0).astype(BF16))
    def scores(c):
        k_c = k_ref[c:c + A_K_CHUNK, :]
        return [_dot_nt(k_c, q) for q in qm]

    m_run = [None] * len(heads)
    acc = [None] * len(heads)
    sts_next = scores(0)
    for c in range(0, seq, A_K_CHUNK):
        sts = sts_next
        if c + A_K_CHUNK < seq:
            sts_next = scores(c + A_K_CHUNK)
        vt_c = vt_ref[:, c:c + A_K_CHUNK].astype(F32)
        vt_ones = [jnp.where(row_hi == (hk == 1), vt_c, 1.0).astype(BF16) for hk in range(A_KV_HEADS)]
        ps, m_news = [], []
        for i, st in enumerate(sts):
            m_new = jnp.max(st, axis=0, keepdims=True)
            if m_run[i] is not None:
                m_new = jnp.maximum(m_run[i], m_new)
            m_news.append(m_new)
            ps.append(jnp.exp2(st - m_new).astype(BF16))
        for i, (g, hk) in enumerate(heads):
            ov = _dot(vt_ones[hk], ps[i])
            acc[i] = ov if acc[i] is None else acc[i] * jnp.exp2(m_run[i] - m_news[i]) + ov
            m_run[i] = m_news[i]
    for g in range(A_HEADS // A_KV_HEADS):
        halves = []
        for hk in range(A_KV_HEADS):
            ov = acc[g * A_KV_HEADS + hk]
            den_row = (1 - hk) * HEAD_DIM
            halves.append(ov / ov[den_row:den_row + 1, :])
        ot = jnp.where(row_hi, halves[1], halves[0])
        o_ref[:, g * LANES:(g + 1) * LANES] = ot.T.astype(BF16)


def _attn_a(q, k, vt, seq):
    t = q.shape[0]
    tq = min(A_Q_TILE, seq)
    per_seq = seq // tq
    return pl.pallas_call(
        _attn_a_kernel,
        out_shape=jax.ShapeDtypeStruct((t, A_QW), BF16),
        grid=(t // seq, per_seq),
        in_specs=[pl.BlockSpec((tq, A_QW), lambda b, i: (b * per_seq + i, 0)),
                  pl.BlockSpec((seq, A_KVW), lambda b, i: (b, 0)),
                  pl.BlockSpec((A_KVW, seq), lambda b, i: (0, b))],
        out_specs=pl.BlockSpec((tq, A_QW), lambda b, i: (b * per_seq + i, 0)),
        compiler_params=_cparams("parallel", "parallel"),
        name="attn_a",
    )(q, k, vt)


def _t5_bucket_np(rel):
    half = REL_BUCKETS // 2
    exact = half // 2
    sign = np.where(rel > 0, half, 0)
    n = np.abs(rel)
    nf = np.maximum(n, 1).astype(np.float32)
    large = exact + (np.log(nf / np.float32(exact)) / np.float32(math.log(REL_MAX_DIST / exact))
                     * np.float32(half - exact)).astype(np.int32)
    large = np.minimum(large, half - 1)
    return sign + np.where(n < exact, n, large)


def _branch_tiles(sub_len):
    if sub_len >= B_K_TILE:
        return B_Q_TILE, B_K_TILE, (0, -B_SIDE, B_Q_TILE - B_K_TILE)
    return sub_len, sub_len, (0,)


def _branch_bias(rel_bias, dil, sub_len):
    tq, tk, offs = _branch_tiles(sub_len)
    period = tq + tk
    slot = np.arange(period)
    tabs = []
    for off in offs:
        delta = np.where(slot < tk, slot, slot - period) + off
        bucket = _t5_bucket_np(delta * dil)
        vec = jnp.where((np.abs(delta) <= B_SIDE)[:, None], rel_bias.astype(F32)[bucket] * LOG2E, NEG_INF).T
        flat = jnp.tile(vec, (1, tq))[:, :tq * (period - 1)]
        tabs.append(flat.reshape(B_HEADS, tq, period - 1)[:, :, :tk])
    return jnp.stack(tabs)


def _attn_b_kernel(q_ref, k_ref, v_ref, b0_ref, b1_ref, b2_ref, o_ref, num_s, mx_s, den_s, pre_s, *, seq):
    head0 = lax.broadcasted_iota(jnp.int32, (1, LANES), 1) < HEAD_DIM
    pick = lambda a, b: jnp.where(head0, a, b)

    pre_len = seq // B_PRE
    for x, src in enumerate((q_ref, k_ref, v_ref)):
        for j in range(B_PRE):
            pre_s[x, j * pre_len:(j + 1) * pre_len, :] = src[pl.ds(j, pre_len, stride=B_PRE), :]

    tiles = []
    for g, ((_, dil), bias_ref) in enumerate(zip(DIL_PAIRS, (b0_ref, b1_ref, b2_ref))):
        sub_len = seq // dil
        tq, tk, _ = _branch_tiles(sub_len)
        n_tiles = sub_len // tq
        for r in range(dil):
            for t in range(n_tiles):
                if n_tiles == 1:
                    k0, tab = 0, 0
                else:
                    k0 = min(max(t * tq - B_SIDE, 0), sub_len - tk)
                    tab = 0 if t == 0 else 2 if t == n_tiles - 1 else 1
                if dil % B_PRE == 0:
                    srcs = tuple(pre_s.at[x] for x in range(3))
                    base, step = (r % B_PRE) * pre_len + r // B_PRE, dil // B_PRE
                else:
                    srcs, base, step = (q_ref, k_ref, v_ref), r, dil
                tiles.append((g, bias_ref, tab, srcs,
                              pl.ds(base + step * t * tq, tq, stride=step),
                              pl.ds(base + step * k0, tk, stride=step),
                              pl.ds(r + dil * t * tq, tq, stride=dil)))

    for g, bias_ref, tab, (q_src, k_src, v_src), q_rows, keys, rows in tiles:
        tq = rows.size
        qt = q_src[q_rows, :]
        q2 = jnp.concatenate([pick(qt, 0.0), pick(0.0, qt)], axis=0).astype(BF16)
        s = _dot_nt(q2, k_src[keys, :].astype(BF16)) + bias_ref[tab]
        m = jnp.max(s, axis=-1, keepdims=True)
        p = jnp.exp2(s - m)
        l = jnp.sum(p, axis=-1, keepdims=True)
        pv = _dot(p.astype(BF16), v_src[keys, :].astype(BF16))
        num_s.at[g][rows, :] = pick(pv[:tq], pv[tq:])
        mx_s.at[g][rows, :] = pick(m[:tq], m[tq:])
        den_s.at[g][rows, :] = pick(l[:tq], l[tq:])

    def merge(t, carry):
        rows = pl.ds(pl.multiple_of(t * CONV_TILE, CONV_TILE), CONV_TILE)
        ms = [mx_s[g, rows, :] for g in range(3)]
        top = jnp.maximum(jnp.maximum(ms[0], ms[1]), ms[2])
        es = [jnp.exp2(m - top) for m in ms]
        num = sum(e * num_s[g, rows, :] for g, e in enumerate(es))
        den = sum(e * den_s[g, rows, :] for g, e in enumerate(es))
        o_ref[rows, :] = (num / den).astype(BF16)
        return carry

    lax.fori_loop(0, seq // CONV_TILE, merge, 0)


def _attn_b(q, k, v, biases, batch, seq):
    t = q.shape[0]
    pair = pl.BlockSpec((seq, LANES), lambda b, p: (b, p))
    biases = [x.reshape(x.shape[0], B_HEADS // 2, 2 * x.shape[2], x.shape[3]) for x in biases]
    bias_spec = lambda x: pl.BlockSpec((x.shape[0], None) + x.shape[2:], lambda b, p: (0, p, 0, 0))
    return pl.pallas_call(
        functools.partial(_attn_b_kernel, seq=seq),
        out_shape=jax.ShapeDtypeStruct((t, B_W), BF16),
        grid=(batch, B_HEADS // 2),
        in_specs=[pair, pair, pair] + [bias_spec(x) for x in biases],
        out_specs=pair,
        scratch_shapes=[pltpu.VMEM((len(DIL_PAIRS), seq, LANES), F32)] * 4,
        compiler_params=_cparams("parallel", "parallel"),
        name="attn_b",
    )(q, k, v, *biases)


BASE_BLOCK = 8


def _pair_block_diag(x, same_head):
    reps = same_head.shape[0] // x.shape[0]
    return jnp.where(same_head, jnp.concatenate([x] * reps, axis=0), 0.0).astype(BF16)


def _inv_unit_triangular_pairs(ms, same_block, same_head, eye):
    bd = lambda x: _pair_block_diag(x, same_head)
    b16 = lambda x: x.astype(BF16)
    m8 = [b16(jnp.where(same_block[BASE_BLOCK], m, 0.0)) for m in ms]
    p2 = [_dot(a, bd(a.astype(F32))) for a in m8]
    yield
    p2h = [b16(p) for p in p2]
    p4 = [_dot(a, bd(p)) for a, p in zip(p2h, p2)]
    yield
    p6 = [_dot(a, bd(p)) for a, p in zip(p2h, p4)]
    yield
    even = [eye + a + b + c for a, b, c in zip(p2, p4, p6)]
    inv = [e - _dot(a, bd(e)) for a, e in zip(m8, even)]
    yield
    size = BASE_BLOCK
    while size < CHUNK:
        joins = same_block[2 * size] & jnp.logical_not(same_block[size])
        t = [_dot(b16(jnp.where(joins, m, 0.0)), bd(x)) for m, x in zip(ms, inv)]
        yield
        inv = [x - _dot(b16(x), bd(y)) for x, y in zip(inv, t)]
        yield
        size *= 2
    return inv


def _delta_kernel(xq_ref, xk_ref, xv_ref, wq_ref, wk_ref, wv_ref, z_ref, gt_ref,
                  alog_ref, dtb_ref, gain_ref, o_ref,
                  pad_s, q_s, k_s, v_s, beta_s, g_s,
                  u_s, lq_s, a_s, kw_s, ku_s, eg_s, *, seq):
    n_chunks = seq // CHUNK
    pad = 8
    sq_r = lax.broadcasted_iota(jnp.int32, (LANES, LANES), 0)
    sq_c = lax.broadcasted_iota(jnp.int32, (LANES, LANES), 1)
    same_head = (sq_r < HEAD_DIM) == (sq_c < HEAD_DIM)
    head_ones = same_head.astype(BF16)

    zeros = jnp.zeros((pad, LANES), F32)
    for j, (x_ref, w_ref, dst, scale) in enumerate(((xq_ref, wq_ref, q_s, HEAD_DIM ** -0.5),
                                                     (xk_ref, wk_ref, k_s, 1.0),
                                                     (xv_ref, wv_ref, v_s, None))):
        pad_s[j, 0:pad, :] = zeros
        pad_s[j, pad + seq:pad + seq + pad, :] = zeros
        pad_s[j, pad:pad + seq, :] = x_ref[...]
        for r0 in range(0, seq, CONV_TILE):
            acc = jnp.zeros((CONV_TILE, LANES), F32)
            for d in range(CONV_K):
                lo = r0 + pad - CONV_K // 2 + d
                acc = acc + w_ref[d:d + 1, :] * pad_s[j, lo:lo + CONV_TILE, :]
            y = _silu(acc)
            if scale is not None:
                y = y * lax.rsqrt(_split_dot(y * y, head_ones) + NORM_EPS) * scale
            dst[r0:r0 + CONV_TILE, :] = y

    gt = gt_ref[...]
    beta_s[...] = 1.0 / (1.0 + jnp.exp(-gt))
    gx = gt + dtb_ref[0]
    softplus = jnp.maximum(gx, 0.0) + jnp.log(1.0 + jnp.exp(-jnp.abs(gx)))
    g_s[...] = -jnp.exp(alog_ref[0]) * softplus

    wide = 2 * LANES
    lane_w = lax.broadcasted_iota(jnp.int32, (CHUNK, wide), 1)
    rows = lax.broadcasted_iota(jnp.int32, (CHUNK, wide), 0)
    cols = lane_w & (HEAD_DIM - 1)
    fwd = lane_w < LANES
    fwd_row = lax.broadcasted_iota(jnp.int32, (1, wide), 1) < LANES
    head0 = lax.broadcasted_iota(jnp.int32, (CHUNK, LANES), 1) < HEAD_DIM
    eye_b = rows == cols
    rows_p = lax.broadcasted_iota(jnp.int32, (CHUNK, LANES), 0)
    cols_p = lax.broadcasted_iota(jnp.int32, (CHUNK, LANES), 1) & (HEAD_DIM - 1)
    eye = (rows_p == cols_p).astype(F32)
    same_block = {}
    size = BASE_BLOCK
    while size <= CHUNK:
        shift = size.bit_length() - 1
        same_block[size] = (rows_p >> shift) == (cols_p >> shift)
        size *= 2
    r2 = lax.broadcasted_iota(jnp.int32, (2 * CHUNK, CHUNK), 0)
    c2 = lax.broadcasted_iota(jnp.int32, (2 * CHUNK, CHUNK), 1)
    tri2 = (jnp.where(r2 < CHUNK, r2 - c2, c2 - (r2 - CHUNK)) >= 0).astype(BF16)
    below = jnp.where(fwd, rows - cols, cols - rows)
    incl = below >= 0
    strict = below > 0
    bd = lambda x: _pair_block_diag(x, same_head)
    both = lambda x: jnp.concatenate([x, x], axis=-1)
    halves = (slice(0, LANES), slice(LANES, wide))

    def spread(x, c0):
        col = lambda c: jnp.broadcast_to(x[:, c:c + 1], (CHUNK, LANES))
        return jnp.concatenate([jnp.where(head0, col(c0), col(c0 + 1)),
                                jnp.where(head0, col(c0 + 2), col(c0 + 3))], axis=-1)

    def chunk_group(it):
        ns = [it * CHUNK_GROUP + g for g in range(CHUNK_GROUP)]
        sls = [pl.ds(n * CHUNK, CHUNK) for n in ns]
        k2 = [k_s[sl, :] for sl in sls]
        q2 = [q_s[sl, :] for sl in sls]
        gsum = []
        for sl in sls:
            g2 = g_s[sl, :]
            hi = g2.astype(BF16)
            lo = (g2 - hi.astype(F32)).astype(BF16)
            gsum.append(_dot(tri2, jnp.concatenate([hi, lo], axis=-1)))
        kq = [_dot_nt(jnp.concatenate([k, q], axis=0).astype(BF16), bd(k)) for k, q in zip(k2, q2)]
        yield
        beta, egc, kd, ms = [], [], [], []
        for g, sl in enumerate(sls):
            gs = gsum[g][:, :LANES] + gsum[g][:, LANES:]
            gcum = jnp.concatenate([gs[:CHUNK], gs[CHUNK:]], axis=-1)
            beta.append(spread(beta_s[sl, :], 0))
            gc = jnp.where(fwd, spread(gcum[:, :LANES], 4), spread(gcum[:, LANES:], 4))
            gc_row = jnp.sum(jnp.where(eye_b, gc, 0.0), axis=0, keepdims=True)
            decay = jnp.where(incl, jnp.exp(gc - gc_row), 0.0)
            ms.append(jnp.where(strict, beta[g] * both(kq[g][:CHUNK]) * decay, 0.0))
            a = (both(kq[g][CHUNK:]) * decay).astype(BF16)
            egc.append(jnp.exp(gc))
            g_last = jnp.where(fwd_row, gc[CHUNK - 1:CHUNK, :], gc[0:1, :])
            kd.append((both(k2[g]) * jnp.exp(g_last - gc)).astype(BF16))
            eg = jnp.exp(g_last)
            rows8 = pl.ds(ns[g] * 8, 8)
            for d in range(2):
                a_s[d, sl, :] = a[:, d * LANES:(d + 1) * LANES]
                eg_s[d, rows8, :] = jnp.broadcast_to(eg[:, d * LANES:(d + 1) * LANES], (8, LANES))
        chains = [(g, d) for g in range(CHUNK_GROUP) for d in range(2)]
        inv = yield from _inv_unit_triangular_pairs(
            [ms[g][:, halves[d]] for g, d in chains], same_block, same_head, eye)
        ys = [jnp.concatenate([bd((k2[g] * (beta[g] * egc[g])[:, halves[d]])),
                               bd(v_s[sls[g], :] * beta[g][:, halves[d]])], axis=-1) for g, d in chains]
        wus = [_dot(x.astype(BF16), y) for x, y in zip(inv, ys)]
        yield
        kwu = [_dot_tn(kd[g][:, halves[d]], wu.astype(BF16)) for wu, (g, d) in zip(wus, chains)]
        for wu, x, (g, d) in zip(wus, kwu, chains):
            u_s[d, sls[g], :] = wu[:, LANES:]
            qg = q2[g] * egc[g][:, halves[d]]
            lq_s[d, ns[g]] = jnp.concatenate([wu[:, :LANES], qg], axis=0).astype(BF16)
            kw_s[d, ns[g]] = jnp.where(same_head, x[:, :LANES], 0.0).astype(BF16)
            ku_s[d, ns[g]] = jnp.where(same_head, x[:, LANES:], 0.0)

    state = [jnp.zeros((LANES, LANES), F32) for _ in range(2)]
    outs = [[None] * n_chunks for _ in range(2)]
    pending = [None, None]
    steps = [0, 0]
    finished = []

    def chunk_of(d, k):
        return k if d == 0 else n_chunks - 1 - k

    def finish_chunk(n):
        sl = pl.ds(n * CHUNK, CHUNK)
        o = outs[0][n] + outs[1][n]
        ms = _split_dot(o * o, head_ones) * (1.0 / HEAD_DIM)
        o = o * lax.rsqrt(ms + NORM_EPS) * gain_ref[...]
        o_ref[sl, :] = (o * _silu(z_ref[sl, :].astype(F32))).astype(BF16)

    def emit_output(d):
        n, res = pending[d]
        sl = pl.ds(n * CHUNK, CHUNK)
        v_new = u_s[d, sl, :] - res[LANES:LANES + CHUNK]
        outs[d][n] = res[LANES + CHUNK:] + _dot(a_s[d, sl, :], bd(v_new))
        pending[d] = None
        if outs[1 - d][n] is not None:
            finished.append(n)

    def scan_step(ds):
        older = list(finished)
        del finished[:]
        new = []
        for d in ds:
            n = chunk_of(d, steps[d])
            res = _dot(jnp.concatenate([kw_s[d, n], lq_s[d, n]], axis=0), state[d].astype(BF16))
            state[d] = state[d] * eg_s[d, n * 8:n * 8 + 1, :] - res[:LANES] + ku_s[d, n]
            new.append((d, n, res))
            steps[d] += 1
        for n in older:
            finish_chunk(n)
        for d, n, res in new:
            if pending[d] is not None:
                emit_output(d)
            pending[d] = (n, res)

    n_groups = n_chunks // CHUNK_GROUP
    order = [g // 2 if g % 2 == 0 else n_groups - 1 - g // 2 for g in range(n_groups)]
    done = set()

    def ready(d):
        return steps[d] < n_chunks and chunk_of(d, steps[d]) // CHUNK_GROUP in done

    for it in order:
        for _ in chunk_group(it):
            ds = [d for d in range(2) if ready(d)]
            if ds:
                scan_step(ds)
        done.add(it)
    while steps[0] < n_chunks or steps[1] < n_chunks:
        scan_step([d for d in range(2) if ready(d)])
    for d in range(2):
        emit_output(d)
    for n in finished:
        finish_chunk(n)


def _split_dot_left(m, x):
    hi = x.astype(BF16)
    lo = (x - hi.astype(F32)).astype(BF16)
    return _dot(m, hi) + _dot(m, lo)


def _delta(cx, cz, gt, conv_w, alog, dtb, gain, batch, seq):
    t = cx.shape[0]
    xspec = lambda part: pl.BlockSpec((seq, LANES), lambda b, p: (b, part * C_PAIRS + p))
    wspec = lambda part: pl.BlockSpec((CONV_K, LANES), lambda b, p: (0, part * C_PAIRS + p))
    pair = pl.BlockSpec((seq, LANES), lambda b, p: (b, p))
    prow = pl.BlockSpec((1, 1, LANES), lambda b, p: (p, 0, 0))
    return pl.pallas_call(
        functools.partial(_delta_kernel, seq=seq),
        out_shape=jax.ShapeDtypeStruct((t, C_W), BF16),
        grid=(batch, C_PAIRS),
        in_specs=[xspec(0), xspec(1), xspec(2), wspec(0), wspec(1), wspec(2),
                  pair, pair, prow, prow,
                  _resident((1, LANES))],
        out_specs=pair,
        scratch_shapes=[pltpu.VMEM((3, seq + 16, LANES), F32),
                        pltpu.VMEM((seq, LANES), F32),
                        pltpu.VMEM((seq, LANES), F32),
                        pltpu.VMEM((seq, LANES), F32),
                        pltpu.VMEM((seq, LANES), F32),
                        pltpu.VMEM((seq, LANES), F32),
                        pltpu.VMEM((2, seq, LANES), F32),
                        pltpu.VMEM((2, seq // CHUNK, LANES, LANES), BF16),
                        pltpu.VMEM((2, seq, LANES), BF16),
                        pltpu.VMEM((2, seq // CHUNK, LANES, LANES), BF16),
                        pltpu.VMEM((2, seq // CHUNK, LANES, LANES), F32),
                        pltpu.VMEM((2, seq // CHUNK * 8, LANES), F32)],
        compiler_params=_cparams("parallel", "parallel"),
        name="delta",
    )(cx, cx, cx, conv_w, conv_w, conv_w, cz, gt, alog, dtb, gain)


def _out_ffn_kernel(h_ref, a_ref, b_ref, c_ref, w_ref, g_ref, wg_ref, wu_ref, wd_ref, o_ref):
    y = _dot(a_ref[...], w_ref[0:A_QW, :])
    y = y + _dot(b_ref[...], w_ref[A_QW:A_QW + B_W, :])
    y = y + _dot(c_ref[...], w_ref[A_QW + B_W:, :])
    o_ref[...] = _swiglu_residual(h_ref[...] + y, g_ref, wg_ref, wu_ref, wd_ref)


def _mix_out_ffn(h, a, b, c, layer, w, g, wg, wu, wd):
    t, d = h.shape
    tm = min(FFN_TILE, t)
    row = lambda width: pl.BlockSpec((tm, width), lambda i: (i, 0))
    return pl.pallas_call(
        _out_ffn_kernel,
        out_shape=jax.ShapeDtypeStruct((t, d), F32),
        grid=(t // tm,),
        in_specs=[row(d), row(A_QW), row(B_W), row(C_W)]
                 + [_layer_resident(p, layer) for p in (w, g, wg, wu, wd)],
        out_specs=row(d),
        compiler_params=_cparams("parallel"),
        name="mix_out_ffn",
    )(h, a, b, c, w, g, wg, wu, wd)


def _rope_tables(seq):
    rows = seq // GRID_W
    row = jnp.repeat(jnp.arange(rows), GRID_W).astype(F32)
    col = jnp.tile(jnp.arange(GRID_W), rows).astype(F32)
    n_freq = HEAD_DIM // 4
    inv = ROPE_THETA ** (-jnp.arange(n_freq, dtype=F32) / n_freq)
    ang = jnp.concatenate([row[:, None] * inv, col[:, None] * inv], axis=-1)
    cos, sin = jnp.cos(ang), jnp.sin(ang)
    cos_h = jnp.concatenate([cos, cos], axis=-1)
    sin_h = jnp.concatenate([-sin, sin], axis=-1)
    return jnp.tile(cos_h, (1, A_HEADS)), jnp.tile(sin_h, (1, A_HEADS))


def _pair_columns(x):
    lead = x.shape[:-1]
    x = x.reshape(lead + (2, C_PAIRS, 2))
    return jnp.moveaxis(x, -2, -3).reshape(lead + (C_PAIRS, 4))


def _pad_w_in(w_in):
    lead = w_in.shape[:-1]
    w_in = w_in.astype(BF16)
    aq = [w_in[..., hq * HEAD_DIM:(hq + 1) * HEAD_DIM] for hq in A_Q_HEAD_ORDER]
    cb = _pair_columns(w_in[..., OFF_G:OFF_G + 2 * C_HEADS])
    ca = _pair_columns(w_in[..., OFF_G + 2 * C_HEADS:])
    gates = jnp.concatenate([cb, ca, jnp.zeros(lead + (C_PAIRS, LANES - 8), BF16)], axis=-1)
    return jnp.concatenate(aq + [w_in[..., A_QW:OFF_G], gates.reshape(lead + (GATE_W,))], axis=-1)


def _pair_row(p):
    v = _pair_columns(p.reshape(2 * C_HEADS).astype(F32))
    z4 = jnp.zeros((C_PAIRS, 4), F32)
    return jnp.concatenate([z4, v, jnp.zeros((C_PAIRS, LANES - 8), F32)], axis=-1)[:, None, :]


def kernel(x, rel_bias, ffn1_norm, ffn1_w_gate, ffn1_w_up, ffn1_w_down, mix_norm, w_in, a_q_norm, a_k_norm, b_q_norm, b_k_norm, c_conv, c_A_log, c_dt_bias, c_out_norm, w_out, ffn2_norm, ffn2_w_gate, ffn2_w_up, ffn2_w_down):
    batch, seq, d = x.shape
    depth = w_in.shape[0]
    scale = HEAD_DIM ** -0.5 * LOG2E
    h = x.reshape(batch * seq, d)
    cos, sin = _rope_tables(seq)
    idx = np.arange(MXU_W)
    mblk = jnp.asarray((idx[:, None] // HEAD_DIM == idx[None, :] // HEAD_DIM) / HEAD_DIM, BF16)
    biases = [_branch_bias(rel_bias, dil, seq // dil) for _, dil in DIL_PAIRS]
    row = lambda v: v.reshape(1, -1).astype(F32)
    ffn1 = (ffn1_norm.astype(F32)[:, None, :], ffn1_w_gate.astype(BF16), ffn1_w_up.astype(BF16),
            ffn1_w_down.astype(BF16))
    ffn2 = (ffn2_norm.astype(F32)[:, None, :], ffn2_w_gate.astype(BF16), ffn2_w_up.astype(BF16),
            ffn2_w_down.astype(BF16))
    wo_a = [w_out[:, hq * HEAD_DIM:(hq + 1) * HEAD_DIM] for hq in A_Q_HEAD_ORDER]
    wo = jnp.concatenate(wo_a + [w_out[:, A_QW:]], axis=1).astype(BF16)
    mix_g = mix_norm.astype(F32)[:, None, :]
    w_proj = _pad_w_in(w_in)
    for l in range(depth):
        h = _ffn(h, l, *ffn1)
        aq, ak, avt, bq, bk, bv, cx, cz, gt = _proj(
            h, l, mix_g, w_proj, cos, sin,
            row(jnp.tile(a_q_norm[l], A_HEADS)) * scale, row(jnp.tile(a_k_norm[l], A_KV_HEADS)),
            row(jnp.tile(b_q_norm[l], B_HEADS)) * scale, row(jnp.tile(b_k_norm[l], B_HEADS)),
            mblk, seq)
        out_a = _attn_a(aq, ak, avt, seq)
        out_b = _attn_b(bq, bk, bv, biases, batch, seq)
        out_c = _delta(cx, cz, gt, c_conv[l].astype(F32), _pair_row(c_A_log[l]), _pair_row(c_dt_bias[l]),
                       row(jnp.tile(c_out_norm[l], 2)), batch, seq)
        h = _mix_out_ffn(h, out_a, out_b, out_c, l, wo, *ffn2)
    return h.reshape(batch, seq, d)
```

```python
import functools
import math

import numpy as np
import jax
import jax.numpy as jnp
from jax import lax
from jax.experimental import pallas as pl
from jax.experimental.pallas import tpu as pltpu

F32 = jnp.float32
BF16 = jnp.bfloat16

HEAD_DIM = 64
A_HEADS = 4
A_KV_HEADS = 2
B_HEADS = 6
C_HEADS = 6
GRID_W = 64
ROPE_THETA = 10000.0
DIL_PAIRS = ((128, 1), (512, 4), (2048, 16))
REL_BUCKETS = 32
REL_MAX_DIST = 1024
CONV_K = 5
CHUNK = 64
NORM_EPS = 1e-6
NEG_INF = -1e30
LOG2E = math.log2(math.e)

A_QW = A_HEADS * HEAD_DIM
A_KVW = A_KV_HEADS * HEAD_DIM
B_W = B_HEADS * HEAD_DIM
C_W = C_HEADS * HEAD_DIM
C_PAIRS = C_HEADS // 2
LANES = 128
MXU_W = 256
GATE_W = C_PAIRS * LANES

OFF_A = 0
OFF_B = OFF_A + A_QW + 2 * A_KVW
OFF_C = OFF_B + 3 * B_W
OFF_Z = OFF_C + 3 * C_W
OFF_G = OFF_Z + C_W
N_PROJ = OFF_G + GATE_W

TOKEN_TILE = 512
FFN_TILE = 1024
FFN_SPLIT = 2
A_Q_TILE = 512
A_K_CHUNK = 256
B_Q_TILE = 128
B_K_TILE = 256
B_SIDE = 64
B_PRE = 4
CONV_TILE = 256
CHUNK_GROUP = 8
VMEM_LIMIT = 56 * 1024 * 1024

assert A_KV_HEADS * HEAD_DIM == LANES and A_HEADS == 2 * A_KV_HEADS
assert B_HEADS % 2 == 0 and C_HEADS % 2 == 0 and 2 * HEAD_DIM == LANES
assert all(window == 2 * B_SIDE * dil for window, dil in DIL_PAIRS)


def _cparams(*sem):
    return pltpu.CompilerParams(dimension_semantics=sem, vmem_limit_bytes=VMEM_LIMIT)


def _resident(shape):
    return pl.BlockSpec(shape, lambda *_: (0,) * len(shape), pipeline_mode=pl.Buffered(1))


def _dot(a, b):
    return jnp.dot(a, b, preferred_element_type=F32)


def _dot_nt(a, b):
    return lax.dot_general(a, b, (((1,), (1,)), ((), ())), preferred_element_type=F32)


def _dot_tn(a, b):
    return lax.dot_general(a, b, (((0,), (0,)), ((), ())), preferred_element_type=F32)


def _silu(x):
    return x / (1.0 + jnp.exp(-x))


def _split_dot(x, m):
    hi = x.astype(BF16)
    lo = (x - hi.astype(F32)).astype(BF16)
    return _dot(hi, m) + _dot(lo, m)


def _rms_rows(x, g):
    ms = jnp.mean(x * x, axis=-1, keepdims=True)
    return x * lax.rsqrt(ms + NORM_EPS) * g


def _layer_resident(stacked, layer):
    shape = stacked.shape[1:]
    return pl.BlockSpec((None,) + shape, lambda *_: (layer,) + (0,) * len(shape),
                        pipeline_mode=pl.Buffered(1))


def _swiglu_residual(x, g_ref, wg_ref, wu_ref, wd_ref):
    xn = _rms_rows(x, g_ref[...]).astype(BF16)
    f = wg_ref.shape[1]
    step = -(-f // (FFN_SPLIT * MXU_W)) * MXU_W
    y = None
    for lo in range(0, f, step):
        hi = min(lo + step, f)
        gate = _dot(xn, wg_ref[:, lo:hi])
        up = _dot(xn, wu_ref[:, lo:hi])
        part = _dot((_silu(gate) * up).astype(BF16), wd_ref[lo:hi, :])
        y = part if y is None else y + part
    return x + 0.5 * y


def _ffn_kernel(h_ref, g_ref, wg_ref, wu_ref, wd_ref, o_ref):
    o_ref[...] = _swiglu_residual(h_ref[...], g_ref, wg_ref, wu_ref, wd_ref)


def _ffn(h, layer, g, wg, wu, wd):
    t, d = h.shape
    tm = min(FFN_TILE, t)
    return pl.pallas_call(
        _ffn_kernel,
        out_shape=jax.ShapeDtypeStruct((t, d), F32),
        grid=(t // tm,),
        in_specs=[pl.BlockSpec((tm, d), lambda i: (i, 0))]
                 + [_layer_resident(p, layer) for p in (g, wg, wu, wd)],
        out_specs=pl.BlockSpec((tm, d), lambda i: (i, 0)),
        compiler_params=_cparams("parallel"),
        name="ffn",
    )(h, g, wg, wu, wd)


def _rope(x, cos, sin):
    w = x.shape[1]
    lane = lax.broadcasted_iota(jnp.int32, x.shape, 1)
    first = (lane & (HEAD_DIM - 1)) < HEAD_DIM // 2
    half = HEAD_DIM // 2
    partner = jnp.where(first, pltpu.roll(x, w - half, 1), pltpu.roll(x, half, 1))
    return x * cos + partner * sin


def _proj_kernel(h_ref, g_ref, w_ref, cos_ref, sin_ref, gaq_ref, gak_ref, gbq_ref, gbk_ref,
                 mblk_ref, aq_o, ak_o, avt_o, bq_o, bk_o, bv_o, cx_o, cz_o, gt_o):
    xn = _rms_rows(h_ref[...], g_ref[...]).astype(BF16)
    pr = _dot(xn, w_ref[...])
    avt_o[...] = pr[:, OFF_A + A_QW + A_KVW:OFF_B].T.astype(BF16)

    def unit_rms(lo, width):
        outs = []
        for c in range(lo, lo + width, MXU_W):
            w = min(MXU_W, lo + width - c)
            x = pr[:, c:c + w]
            outs.append(x * lax.rsqrt(_split_dot(x * x, mblk_ref[:w, :w]) + NORM_EPS))
        return outs[0] if len(outs) == 1 else jnp.concatenate(outs, axis=-1)

    cos = cos_ref[...]
    sin = sin_ref[...]
    aq_o[...] = _rope(unit_rms(OFF_A, A_QW) * gaq_ref[...], cos, sin).astype(BF16)
    ak = unit_rms(OFF_A + A_QW, A_KVW) * gak_ref[...]
    ak_o[...] = _rope(ak, cos[:, :A_KVW], sin[:, :A_KVW]).astype(BF16)
    bqk = unit_rms(OFF_B, 2 * B_W)
    bq_o[...] = bqk[:, :B_W] * gbq_ref[...]
    bk_o[...] = bqk[:, B_W:] * gbk_ref[...]
    bv_o[...] = pr[:, OFF_B + 2 * B_W:OFF_C]
    cx_o[...] = pr[:, OFF_C:OFF_Z]
    cz_o[...] = pr[:, OFF_Z:OFF_G].astype(BF16)
    gt_o[...] = pr[:, OFF_G:]


def _proj(h, layer, g, w, cos, sin, gaq, gak, gbq, gbk, mblk, seq):
    t, d = h.shape
    tm = min(TOKEN_TILE, seq)
    per_seq = seq // tm
    row = lambda i: (i, 0)
    pos = lambda i: (i % per_seq, 0)
    widths = (A_QW, A_KVW, None, B_W, B_W, B_W, 3 * C_W, C_W, GATE_W)
    dtypes = (BF16, BF16, BF16, F32, F32, F32, F32, BF16, F32)
    shapes = [(t, wd) if wd else (A_KVW, t) for wd in widths]
    specs = [pl.BlockSpec((tm, wd), row) if wd else pl.BlockSpec((A_KVW, tm), lambda i: (0, i))
             for wd in widths]
    return pl.pallas_call(
        _proj_kernel,
        out_shape=[jax.ShapeDtypeStruct(s, dt) for s, dt in zip(shapes, dtypes)],
        grid=(t // tm,),
        in_specs=[pl.BlockSpec((tm, d), row),
                  _layer_resident(g, layer), _layer_resident(w, layer),
                  pl.BlockSpec((tm, A_QW), pos),
                  pl.BlockSpec((tm, A_QW), pos),
                  _resident((1, A_QW)), _resident((1, A_KVW)), _resident((1, B_W)),
                  _resident((1, B_W)), _resident((MXU_W, MXU_W))],
        out_specs=specs,
        compiler_params=_cparams("parallel"),
        name="mix_in",
    )(h, g, w, cos, sin, gaq, gak, gbq, gbk, mblk)


A_Q_HEAD_ORDER = tuple(hk * (A_HEADS // A_KV_HEADS) + g
                       for g in range(A_HEADS // A_KV_HEADS) for hk in range(A_KV_HEADS))


def _attn_a_kernel(q_ref, k_ref, vt_ref, o_ref):
    seq = k_ref.shape[0]
    lane_hi = lax.broadcasted_iota(jnp.int32, (1, LANES), 1) >= HEAD_DIM
    row_hi = lax.broadcasted_iota(jnp.int32, (LANES, 1), 0) >= HEAD_DIM
    heads = [(g, hk) for g in range(A_HEADS // A_KV_HEADS) for hk in range(A_KV_HEADS)]
    qm = []
    for g, hk in heads:
        qb = q_ref[:, g * LANES:(g + 1) * LANES].astype(F32)
        qm.append(jnp.where(lane_hi == (hk == 1), qb, 0.0).astype(BF16))
    def scores(c):
        k_c = k_ref[c:c + A_K_CHUNK, :]
        return [_dot_nt(k_c, q) for q in qm]

    m_run = [None] * len(heads)
    acc = [None] * len(heads)
    sts_next = scores(0)
    for c in range(0, seq, A_K_CHUNK):
        sts = sts_next
        if c + A_K_CHUNK < seq:
            sts_next = scores(c + A_K_CHUNK)
        vt_c = vt_ref[:, c:c + A_K_CHUNK].astype(F32)
        vt_ones = [jnp.where(row_hi == (hk == 1), vt_c, 1.0).astype(BF16) for hk in range(A_KV_HEADS)]
        ps, m_news = [], []
        for i, st in enumerate(sts):
            m_new = jnp.max(st, axis=0, keepdims=True)
            if m_run[i] is not None:
                m_new = jnp.maximum(m_run[i], m_new)
            m_news.append(m_new)
            ps.append(jnp.exp2(st - m_new).astype(BF16))
        for i, (g, hk) in enumerate(heads):
            ov = _dot(vt_ones[hk], ps[i])
            acc[i] = ov if acc[i] is None else acc[i] * jnp.exp2(m_run[i] - m_news[i]) + ov
            m_run[i] = m_news[i]
    for g in range(A_HEADS // A_KV_HEADS):
        halves = []
        for hk in range(A_KV_HEADS):
            ov = acc[g * A_KV_HEADS + hk]
            den_row = (1 - hk) * HEAD_DIM
            halves.append(ov / ov[den_row:den_row + 1, :])
        ot = jnp.where(row_hi, halves[1], halves[0])
        o_ref[:, g * LANES:(g + 1) * LANES] = ot.T.astype(BF16)


def _attn_a(q, k, vt, seq):
    t = q.shape[0]
    tq = min(A_Q_TILE, seq)
    per_seq = seq // tq
    return pl.pallas_call(
        _attn_a_kernel,
        out_shape=jax.ShapeDtypeStruct((t, A_QW), BF16),
        grid=(t // seq, per_seq),
        in_specs=[pl.BlockSpec((tq, A_QW), lambda b, i: (b * per_seq + i, 0)),
                  pl.BlockSpec((seq, A_KVW), lambda b, i: (b, 0)),
                  pl.BlockSpec((A_KVW, seq), lambda b, i: (0, b))],
        out_specs=pl.BlockSpec((tq, A_QW), lambda b, i: (b * per_seq + i, 0)),
        compiler_params=_cparams("parallel", "parallel"),
        name="attn_a",
    )(q, k, vt)


def _t5_bucket_np(rel):
    half = REL_BUCKETS // 2
    exact = half // 2
    sign = np.where(rel > 0, half, 0)
    n = np.abs(rel)
    nf = np.maximum(n, 1).astype(np.float32)
    large = exact + (np.log(nf / np.float32(exact)) / np.float32(math.log(REL_MAX_DIST / exact))
                     * np.float32(half - exact)).astype(np.int32)
    large = np.minimum(large, half - 1)
    return sign + np.where(n < exact, n, large)


def _branch_tiles(sub_len):
    if sub_len >= B_K_TILE:
        return B_Q_TILE, B_K_TILE, (0, -B_SIDE, B_Q_TILE - B_K_TILE)
    return sub_len, sub_len, (0,)


def _branch_bias(rel_bias, dil, sub_len):
    tq, tk, offs = _branch_tiles(sub_len)
    period = tq + tk
    slot = np.arange(period)
    tabs = []
    for off in offs:
        delta = np.where(slot < tk, slot, slot - period) + off
        bucket = _t5_bucket_np(delta * dil)
        vec = jnp.where((np.abs(delta) <= B_SIDE)[:, None], rel_bias.astype(F32)[bucket] * LOG2E, NEG_INF).T
        flat = jnp.tile(vec, (1, tq))[:, :tq * (period - 1)]
        tabs.append(flat.reshape(B_HEADS, tq, period - 1)[:, :, :tk])
    return jnp.stack(tabs)


def _attn_b_kernel(q_ref, k_ref, v_ref, b0_ref, b1_ref, b2_ref, o_ref, num_s, mx_s, den_s, pre_s, *, seq):
    head0 = lax.broadcasted_iota(jnp.int32, (1, LANES), 1) < HEAD_DIM
    pick = lambda a, b: jnp.where(head0, a, b)

    pre_len = seq // B_PRE
    for x, src in enumerate((q_ref, k_ref, v_ref)):
        for j in range(B_PRE):
            pre_s[x, j * pre_len:(j + 1) * pre_len, :] = src[pl.ds(j, pre_len, stride=B_PRE), :]

    tiles = []
    for g, ((_, dil), bias_ref) in enumerate(zip(DIL_PAIRS, (b0_ref, b1_ref, b2_ref))):
        sub_len = seq // dil
        tq, tk, _ = _branch_tiles(sub_len)
        n_tiles = sub_len // tq
        for r in range(dil):
            for t in range(n_tiles):
                if n_tiles == 1:
                    k0, tab = 0, 0
                else:
                    k0 = min(max(t * tq - B_SIDE, 0), sub_len - tk)
                    tab = 0 if t == 0 else 2 if t == n_tiles - 1 else 1
                if dil % B_PRE == 0:
                    srcs = tuple(pre_s.at[x] for x in range(3))
                    base, step = (r % B_PRE) * pre_len + r // B_PRE, dil // B_PRE
                else:
                    srcs, base, step = (q_ref, k_ref, v_ref), r, dil
                tiles.append((g, bias_ref, tab, srcs,
                              pl.ds(base + step * t * tq, tq, stride=step),
                              pl.ds(base + step * k0, tk, stride=step),
                              pl.ds(r + dil * t * tq, tq, stride=dil)))

    for g, bias_ref, tab, (q_src, k_src, v_src), q_rows, keys, rows in tiles:
        tq = rows.size
        qt = q_src[q_rows, :]
        q2 = jnp.concatenate([pick(qt, 0.0), pick(0.0, qt)], axis=0).astype(BF16)
        s = _dot_nt(q2, k_src[keys, :].astype(BF16)) + bias_ref[tab]
        m = jnp.max(s, axis=-1, keepdims=True)
        p = jnp.exp2(s - m)
        l = jnp.sum(p, axis=-1, keepdims=True)
        pv = _dot(p.astype(BF16), v_src[keys, :].astype(BF16))
        num_s.at[g][rows, :] = pick(pv[:tq], pv[tq:])
        mx_s.at[g][rows, :] = pick(m[:tq], m[tq:])
        den_s.at[g][rows, :] = pick(l[:tq], l[tq:])

    def merge(t, carry):
        rows = pl.ds(pl.multiple_of(t * CONV_TILE, CONV_TILE), CONV_TILE)
        ms = [mx_s[g, rows, :] for g in range(3)]
        top = jnp.maximum(jnp.maximum(ms[0], ms[1]), ms[2])
        es = [jnp.exp2(m - top) for m in ms]
        num = sum(e * num_s[g, rows, :] for g, e in enumerate(es))
        den = sum(e * den_s[g, rows, :] for g, e in enumerate(es))
        o_ref[rows, :] = (num / den).astype(BF16)
        return carry

    lax.fori_loop(0, seq // CONV_TILE, merge, 0)


def _attn_b(q, k, v, biases, batch, seq):
    t = q.shape[0]
    pair = pl.BlockSpec((seq, LANES), lambda b, p: (b, p))
    biases = [x.reshape(x.shape[0], B_HEADS // 2, 2 * x.shape[2], x.shape[3]) for x in biases]
    bias_spec = lambda x: pl.BlockSpec((x.shape[0], None) + x.shape[2:], lambda b, p: (0, p, 0, 0))
    return pl.pallas_call(
        functools.partial(_attn_b_kernel, seq=seq),
        out_shape=jax.ShapeDtypeStruct((t, B_W), BF16),
        grid=(batch, B_HEADS // 2),
        in_specs=[pair, pair, pair] + [bias_spec(x) for x in biases],
        out_specs=pair,
        scratch_shapes=[pltpu.VMEM((len(DIL_PAIRS), seq, LANES), F32)] * 4,
        compiler_params=_cparams("parallel", "parallel"),
        name="attn_b",
    )(q, k, v, *biases)


BASE_BLOCK = 8


def _pair_block_diag(x, same_head):
    reps = same_head.shape[0] // x.shape[0]
    return jnp.where(same_head, jnp.concatenate([x] * reps, axis=0), 0.0).astype(BF16)


def _inv_unit_triangular_pairs(ms, same_block, same_head, eye):
    bd = lambda x: _pair_block_diag(x, same_head)
    b16 = lambda x: x.astype(BF16)
    m8 = [b16(jnp.where(same_block[BASE_BLOCK], m, 0.0)) for m in ms]
    p2 = [_dot(a, bd(a.astype(F32))) for a in m8]
    yield
    p2h = [b16(p) for p in p2]
    p4 = [_dot(a, bd(p)) for a, p in zip(p2h, p2)]
    yield
    p6 = [_dot(a, bd(p)) for a, p in zip(p2h, p4)]
    yield
    even = [eye + a + b + c for a, b, c in zip(p2, p4, p6)]
    inv = [e - _dot(a, bd(e)) for a, e in zip(m8, even)]
    yield
    size = BASE_BLOCK
    while size < CHUNK:
        joins = same_block[2 * size] & jnp.logical_not(same_block[size])
        t = [_dot(b16(jnp.where(joins, m, 0.0)), bd(x)) for m, x in zip(ms, inv)]
        yield
        inv = [x - _dot(b16(x), bd(y)) for x, y in zip(inv, t)]
        yield
        size *= 2
    return inv


def _delta_kernel(xq_ref, xk_ref, xv_ref, wq_ref, wk_ref, wv_ref, z_ref, gt_ref,
                  alog_ref, dtb_ref, gain_ref, o_ref,
                  pad_s, q_s, k_s, v_s, beta_s, g_s,
                  u_s, lq_s, a_s, kw_s, ku_s, eg_s, *, seq):
    n_chunks = seq // CHUNK
    pad = 8
    sq_r = lax.broadcasted_iota(jnp.int32, (LANES, LANES), 0)
    sq_c = lax.broadcasted_iota(jnp.int32, (LANES, LANES), 1)
    same_head = (sq_r < HEAD_DIM) == (sq_c < HEAD_DIM)
    head_ones = same_head.astype(BF16)

    zeros = jnp.zeros((pad, LANES), F32)
    for j, (x_ref, w_ref, dst, scale) in enumerate(((xq_ref, wq_ref, q_s, HEAD_DIM ** -0.5),
                                                     (xk_ref, wk_ref, k_s, 1.0),
                                                     (xv_ref, wv_ref, v_s, None))):
        pad_s[j, 0:pad, :] = zeros
        pad_s[j, pad + seq:pad + seq + pad, :] = zeros
        pad_s[j, pad:pad + seq, :] = x_ref[...]
        for r0 in range(0, seq, CONV_TILE):
            acc = jnp.zeros((CONV_TILE, LANES), F32)
            for d in range(CONV_K):
                lo = r0 + pad - CONV_K // 2 + d
                acc = acc + w_ref[d:d + 1, :] * pad_s[j, lo:lo + CONV_TILE, :]
            y = _silu(acc)
            if scale is not None:
                y = y * lax.rsqrt(_split_dot(y * y, head_ones) + NORM_EPS) * scale
            dst[r0:r0 + CONV_TILE, :] = y

    gt = gt_ref[...]
    beta_s[...] = 1.0 / (1.0 + jnp.exp(-gt))
    gx = gt + dtb_ref[0]
    softplus = jnp.maximum(gx, 0.0) + jnp.log(1.0 + jnp.exp(-jnp.abs(gx)))
    g_s[...] = -jnp.exp(alog_ref[0]) * softplus

    wide = 2 * LANES
    lane_w = lax.broadcasted_iota(jnp.int32, (CHUNK, wide), 1)
    rows = lax.broadcasted_iota(jnp.int32, (CHUNK, wide), 0)
    cols = lane_w & (HEAD_DIM - 1)
    fwd = lane_w < LANES
    fwd_row = lax.broadcasted_iota(jnp.int32, (1, wide), 1) < LANES
    head0 = lax.broadcasted_iota(jnp.int32, (CHUNK, LANES), 1) < HEAD_DIM
    eye_b = rows == cols
    rows_p = lax.broadcasted_iota(jnp.int32, (CHUNK, LANES), 0)
    cols_p = lax.broadcasted_iota(jnp.int32, (CHUNK, LANES), 1) & (HEAD_DIM - 1)
    eye = (rows_p == cols_p).astype(F32)
    same_block = {}
    size = BASE_BLOCK
    while size <= CHUNK:
        shift = size.bit_length() - 1
        same_block[size] = (rows_p >> shift) == (cols_p >> shift)
        size *= 2
    r2 = lax.broadcasted_iota(jnp.int32, (2 * CHUNK, CHUNK), 0)
    c2 = lax.broadcasted_iota(jnp.int32, (2 * CHUNK, CHUNK), 1)
    tri2 = (jnp.where(r2 < CHUNK, r2 - c2, c2 - (r2 - CHUNK)) >= 0).astype(BF16)
    below = jnp.where(fwd, rows - cols, cols - rows)
    incl = below >= 0
    strict = below > 0
    bd = lambda x: _pair_block_diag(x, same_head)
    both = lambda x: jnp.concatenate([x, x], axis=-1)
    halves = (slice(0, LANES), slice(LANES, wide))

    def spread(x, c0):
        col = lambda c: jnp.broadcast_to(x[:, c:c + 1], (CHUNK, LANES))
        return jnp.concatenate([jnp.where(head0, col(c0), col(c0 + 1)),
                                jnp.where(head0, col(c0 + 2), col(c0 + 3))], axis=-1)

    def chunk_group(it):
        ns = [it * CHUNK_GROUP + g for g in range(CHUNK_GROUP)]
        sls = [pl.ds(n * CHUNK, CHUNK) for n in ns]
        k2 = [k_s[sl, :] for sl in sls]
        q2 = [q_s[sl, :] for sl in sls]
        gsum = []
        for sl in sls:
            g2 = g_s[sl, :]
            hi = g2.astype(BF16)
            lo = (g2 - hi.astype(F32)).astype(BF16)
            gsum.append(_dot(tri2, jnp.concatenate([hi, lo], axis=-1)))
        kq = [_dot_nt(jnp.concatenate([k, q], axis=0).astype(BF16), bd(k)) for k, q in zip(k2, q2)]
        yield
        beta, egc, kd, ms = [], [], [], []
        for g, sl in enumerate(sls):
            gs = gsum[g][:, :LANES] + gsum[g][:, LANES:]
            gcum = jnp.concatenate([gs[:CHUNK], gs[CHUNK:]], axis=-1)
            beta.append(spread(beta_s[sl, :], 0))
            gc = jnp.where(fwd, spread(gcum[:, :LANES], 4), spread(gcum[:, LANES:], 4))
            gc_row = jnp.sum(jnp.where(eye_b, gc, 0.0), axis=0, keepdims=True)
            decay = jnp.where(incl, jnp.exp(gc - gc_row), 0.0)
            ms.append(jnp.where(strict, beta[g] * both(kq[g][:CHUNK]) * decay, 0.0))
            a = (both(kq[g][CHUNK:]) * decay).astype(BF16)
            egc.append(jnp.exp(gc))
            g_last = jnp.where(fwd_row, gc[CHUNK - 1:CHUNK, :], gc[0:1, :])
            kd.append((both(k2[g]) * jnp.exp(g_last - gc)).astype(BF16))
            eg = jnp.exp(g_last)
            rows8 = pl.ds(ns[g] * 8, 8)
            for d in range(2):
                a_s[d, sl, :] = a[:, d * LANES:(d + 1) * LANES]
                eg_s[d, rows8, :] = jnp.broadcast_to(eg[:, d * LANES:(d + 1) * LANES], (8, LANES))
        chains = [(g, d) for g in range(CHUNK_GROUP) for d in range(2)]
        inv = yield from _inv_unit_triangular_pairs(
            [ms[g][:, halves[d]] for g, d in chains], same_block, same_head, eye)
        ys = [jnp.concatenate([bd((k2[g] * (beta[g] * egc[g])[:, halves[d]])),
                               bd(v_s[sls[g], :] * beta[g][:, halves[d]])], axis=-1) for g, d in chains]
        wus = [_dot(x.astype(BF16), y) for x, y in zip(inv, ys)]
        yield
        kwu = [_dot_tn(kd[g][:, halves[d]], wu.astype(BF16)) for wu, (g, d) in zip(wus, chains)]
        for wu, x, (g, d) in zip(wus, kwu, chains):
            u_s[d, sls[g], :] = wu[:, LANES:]
            qg = q2[g] * egc[g][:, halves[d]]
            lq_s[d, ns[g]] = jnp.concatenate([wu[:, :LANES], qg], axis=0).astype(BF16)
            kw_s[d, ns[g]] = jnp.where(same_head, x[:, :LANES], 0.0).astype(BF16)
            ku_s[d, ns[g]] = jnp.where(same_head, x[:, LANES:], 0.0)

    state = [jnp.zeros((LANES, LANES), F32) for _ in range(2)]
    outs = [[None] * n_chunks for _ in range(2)]
    pending = [None, None]
    steps = [0, 0]
    finished = []

    def chunk_of(d, k):
        return k if d == 0 else n_chunks - 1 - k

    def finish_chunk(n):
        sl = pl.ds(n * CHUNK, CHUNK)
        o = outs[0][n] + outs[1][n]
        ms = _split_dot(o * o, head_ones) * (1.0 / HEAD_DIM)
        o = o * lax.rsqrt(ms + NORM_EPS) * gain_ref[...]
        o_ref[sl, :] = (o * _silu(z_ref[sl, :].astype(F32))).astype(BF16)

    def emit_output(d):
        n, res = pending[d]
        sl = pl.ds(n * CHUNK, CHUNK)
        v_new = u_s[d, sl, :] - res[LANES:LANES + CHUNK]
        outs[d][n] = res[LANES + CHUNK:] + _dot(a_s[d, sl, :], bd(v_new))
        pending[d] = None
        if outs[1 - d][n] is not None:
            finished.append(n)

    def scan_step(ds):
        older = list(finished)
        del finished[:]
        new = []
        for d in ds:
            n = chunk_of(d, steps[d])
            res = _dot(jnp.concatenate([kw_s[d, n], lq_s[d, n]], axis=0), state[d].astype(BF16))
            state[d] = state[d] * eg_s[d, n * 8:n * 8 + 1, :] - res[:LANES] + ku_s[d, n]
            new.append((d, n, res))
            steps[d] += 1
        for n in older:
            finish_chunk(n)
        for d, n, res in new:
            if pending[d] is not None:
                emit_output(d)
            pending[d] = (n, res)

    n_groups = n_chunks // CHUNK_GROUP
    order = [g // 2 if g % 2 == 0 else n_groups - 1 - g // 2 for g in range(n_groups)]
    done = set()

    def ready(d):
        return steps[d] < n_chunks and chunk_of(d, steps[d]) // CHUNK_GROUP in done

    for it in order:
        for _ in chunk_group(it):
            ds = [d for d in range(2) if ready(d)]
            if ds:
                scan_step(ds)
        done.add(it)
    while steps[0] < n_chunks or steps[1] < n_chunks:
        scan_step([d for d in range(2) if ready(d)])
    for d in range(2):
        emit_output(d)
    for n in finished:
        finish_chunk(n)


def _split_dot_left(m, x):
    hi = x.astype(BF16)
    lo = (x - hi.astype(F32)).astype(BF16)
    return _dot(m, hi) + _dot(m, lo)


def _delta(cx, cz, gt, conv_w, alog, dtb, gain, batch, seq):
    t = cx.shape[0]
    xspec = lambda part: pl.BlockSpec((seq, LANES), lambda b, p: (b, part * C_PAIRS + p))
    wspec = lambda part: pl.BlockSpec((CONV_K, LANES), lambda b, p: (0, part * C_PAIRS + p))
    pair = pl.BlockSpec((seq, LANES), lambda b, p: (b, p))
    prow = pl.BlockSpec((1, 1, LANES), lambda b, p: (p, 0, 0))
    return pl.pallas_call(
        functools.partial(_delta_kernel, seq=seq),
        out_shape=jax.ShapeDtypeStruct((t, C_W), BF16),
        grid=(batch, C_PAIRS),
        in_specs=[xspec(0), xspec(1), xspec(2), wspec(0), wspec(1), wspec(2),
                  pair, pair, prow, prow,
                  _resident((1, LANES))],
        out_specs=pair,
        scratch_shapes=[pltpu.VMEM((3, seq + 16, LANES), F32),
                        pltpu.VMEM((seq, LANES), F32),
                        pltpu.VMEM((seq, LANES), F32),
                        pltpu.VMEM((seq, LANES), F32),
                        pltpu.VMEM((seq, LANES), F32),
                        pltpu.VMEM((seq, LANES), F32),
                        pltpu.VMEM((2, seq, LANES), F32),
                        pltpu.VMEM((2, seq // CHUNK, LANES, LANES), BF16),
                        pltpu.VMEM((2, seq, LANES), BF16),
                        pltpu.VMEM((2, seq // CHUNK, LANES, LANES), BF16),
                        pltpu.VMEM((2, seq // CHUNK, LANES, LANES), F32),
                        pltpu.VMEM((2, seq // CHUNK * 8, LANES), F32)],
        compiler_params=_cparams("parallel", "parallel"),
        name="delta",
    )(cx, cx, cx, conv_w, conv_w, conv_w, cz, gt, alog, dtb, gain)


def _out_ffn_kernel(h_ref, a_ref, b_ref, c_ref, w_ref, g_ref, wg_ref, wu_ref, wd_ref, o_ref):
    y = _dot(a_ref[...], w_ref[0:A_QW, :])
    y = y + _dot(b_ref[...], w_ref[A_QW:A_QW + B_W, :])
    y = y + _dot(c_ref[...], w_ref[A_QW + B_W:, :])
    o_ref[...] = _swiglu_residual(h_ref[...] + y, g_ref, wg_ref, wu_ref, wd_ref)


def _mix_out_ffn(h, a, b, c, layer, w, g, wg, wu, wd):
    t, d = h.shape
    tm = min(FFN_TILE, t)
    row = lambda width: pl.BlockSpec((tm, width), lambda i: (i, 0))
    return pl.pallas_call(
        _out_ffn_kernel,
        out_shape=jax.ShapeDtypeStruct((t, d), F32),
        grid=(t // tm,),
        in_specs=[row(d), row(A_QW), row(B_W), row(C_W)]
                 + [_layer_resident(p, layer) for p in (w, g, wg, wu, wd)],
        out_specs=row(d),
        compiler_params=_cparams("parallel"),
        name="mix_out_ffn",
    )(h, a, b, c, w, g, wg, wu, wd)


def _rope_tables(seq):
    rows = seq // GRID_W
    row = jnp.repeat(jnp.arange(rows), GRID_W).astype(F32)
    col = jnp.tile(jnp.arange(GRID_W), rows).astype(F32)
    n_freq = HEAD_DIM // 4
    inv = ROPE_THETA ** (-jnp.arange(n_freq, dtype=F32) / n_freq)
    ang = jnp.concatenate([row[:, None] * inv, col[:, None] * inv], axis=-1)
    cos, sin = jnp.cos(ang), jnp.sin(ang)
    cos_h = jnp.concatenate([cos, cos], axis=-1)
    sin_h = jnp.concatenate([-sin, sin], axis=-1)
    return jnp.tile(cos_h, (1, A_HEADS)), jnp.tile(sin_h, (1, A_HEADS))


def _pair_columns(x):
    lead = x.shape[:-1]
    x = x.reshape(lead + (2, C_PAIRS, 2))
    return jnp.moveaxis(x, -2, -3).reshape(lead + (C_PAIRS, 4))


def _pad_w_in(w_in):
    lead = w_in.shape[:-1]
    w_in = w_in.astype(BF16)
    group = A_HEADS // A_KV_HEADS
    aq = w_in[..., :A_QW].reshape(lead + (A_KV_HEADS, group, HEAD_DIM))
    aq = jnp.swapaxes(aq, -3, -2).reshape(lead + (A_QW,))
    cb = _pair_columns(w_in[..., OFF_G:OFF_G + 2 * C_HEADS])
    ca = _pair_columns(w_in[..., OFF_G + 2 * C_HEADS:])
    gates = jnp.concatenate([cb, ca, jnp.zeros(lead + (C_PAIRS, LANES - 8), BF16)], axis=-1)
    return jnp.concatenate([aq, w_in[..., A_QW:OFF_G], gates.reshape(lead + (GATE_W,))], axis=-1)


def _pair_row(p):
    v = _pair_columns(p.reshape(2 * C_HEADS).astype(F32))
    z4 = jnp.zeros((C_PAIRS, 4), F32)
    return jnp.concatenate([z4, v, jnp.zeros((C_PAIRS, LANES - 8), F32)], axis=-1)[:, None, :]


def kernel(x, rel_bias, ffn1_norm, ffn1_w_gate, ffn1_w_up, ffn1_w_down, mix_norm, w_in, a_q_norm, a_k_norm, b_q_norm, b_k_norm, c_conv, c_A_log, c_dt_bias, c_out_norm, w_out, ffn2_norm, ffn2_w_gate, ffn2_w_up, ffn2_w_down):
    batch, seq, d = x.shape
    depth = w_in.shape[0]
    scale = HEAD_DIM ** -0.5 * LOG2E
    h = x.reshape(batch * seq, d)
    cos, sin = _rope_tables(seq)
    idx = np.arange(MXU_W)
    mblk = jnp.asarray((idx[:, None] // HEAD_DIM == idx[None, :] // HEAD_DIM) / HEAD_DIM, BF16)
    biases = [_branch_bias(rel_bias, dil, seq // dil) for _, dil in DIL_PAIRS]
    row = lambda v: v.reshape(1, -1).astype(F32)
    ffn1 = (ffn1_norm.astype(F32)[:, None, :], ffn1_w_gate.astype(BF16), ffn1_w_up.astype(BF16),
            ffn1_w_down.astype(BF16))
    ffn2 = (ffn2_norm.astype(F32)[:, None, :], ffn2_w_gate.astype(BF16), ffn2_w_up.astype(BF16),
            ffn2_w_down.astype(BF16))
    wo_a = [w_out[:, hq * HEAD_DIM:(hq + 1) * HEAD_DIM] for hq in A_Q_HEAD_ORDER]
    wo = jnp.concatenate(wo_a + [w_out[:, A_QW:]], axis=1).astype(BF16)
    mix_g = mix_norm.astype(F32)[:, None, :]
    w_proj = _pad_w_in(w_in)
    for l in range(depth):
        h = _ffn(h, l, *ffn1)
        aq, ak, avt, bq, bk, bv, cx, cz, gt = _proj(
            h, l, mix_g, w_proj, cos, sin,
            row(jnp.tile(a_q_norm[l], A_HEADS)) * scale, row(jnp.tile(a_k_norm[l], A_KV_HEADS)),
            row(jnp.tile(b_q_norm[l], B_HEADS)) * scale, row(jnp.tile(b_k_norm[l], B_HEADS)),
            mblk, seq)
        out_a = _attn_a(aq, ak, avt, seq)
        out_b = _attn_b(bq, bk, bv, biases, batch, seq)
        out_c = _delta(cx, cz, gt, c_conv[l].astype(F32), _pair_row(c_A_log[l]), _pair_row(c_dt_bias[l]),
                       row(jnp.tile(c_out_norm[l], 2)), batch, seq)
        h = _mix_out_ffn(h, out_a, out_b, out_c, l, wo, *ffn2)
    return h.reshape(batch, seq, d)
```

```python
import functools
import math

import numpy as np
import jax
import jax.numpy as jnp
from jax import lax
from jax.experimental import pallas as pl
from jax.experimental.pallas import tpu as pltpu

F32 = jnp.float32
BF16 = jnp.bfloat16

HEAD_DIM = 64
A_HEADS = 4
A_KV_HEADS = 2
B_HEADS = 6
C_HEADS = 6
GRID_W = 64
ROPE_THETA = 10000.0
DIL_PAIRS = ((128, 1), (512, 4), (2048, 16))
REL_BUCKETS = 32
REL_MAX_DIST = 1024
CONV_K = 5
CHUNK = 64
NORM_EPS = 1e-6
NEG_INF = -1e30
LOG2E = math.log2(math.e)

A_QW = A_HEADS * HEAD_DIM
A_KVW = A_KV_HEADS * HEAD_DIM
B_W = B_HEADS * HEAD_DIM
C_W = C_HEADS * HEAD_DIM
C_PAIRS = C_HEADS // 2
LANES = 128
MXU_W = 256
GATE_W = C_PAIRS * LANES

OFF_A = 0
OFF_B = OFF_A + A_QW + 2 * A_KVW
OFF_C = OFF_B + 3 * B_W
OFF_Z = OFF_C + 3 * C_W
OFF_G = OFF_Z + C_W
N_PROJ = OFF_G + GATE_W

TOKEN_TILE = 512
FFN_TILE = 1024
FFN_SPLIT = 2
A_Q_TILE = 512
A_K_CHUNK = 256
B_Q_TILE = 128
B_K_TILE = 256
B_SIDE = 64
B_PRE = 4
CONV_TILE = 256
CHUNK_GROUP = 8
VMEM_LIMIT = 56 * 1024 * 1024

assert A_KV_HEADS * HEAD_DIM == LANES and A_HEADS == 2 * A_KV_HEADS
assert B_HEADS % 2 == 0 and C_HEADS % 2 == 0 and 2 * HEAD_DIM == LANES
assert all(window == 2 * B_SIDE * dil for window, dil in DIL_PAIRS)


def _cparams(*sem):
    return pltpu.CompilerParams(dimension_semantics=sem, vmem_limit_bytes=VMEM_LIMIT)


def _resident(shape):
    return pl.BlockSpec(shape, lambda *_: (0,) * len(shape), pipeline_mode=pl.Buffered(1))


def _dot(a, b):
    return jnp.dot(a, b, preferred_element_type=F32)


def _dot_nt(a, b):
    return lax.dot_general(a, b, (((1,), (1,)), ((), ())), preferred_element_type=F32)


def _dot_tn(a, b):
    return lax.dot_general(a, b, (((0,), (0,)), ((), ())), preferred_element_type=F32)


def _silu(x):
    return x / (1.0 + jnp.exp(-x))


def _split_dot(x, m):
    hi = x.astype(BF16)
    lo = (x - hi.astype(F32)).astype(BF16)
    return _dot(hi, m) + _dot(lo, m)


def _rms_rows(x, g):
    ms = jnp.mean(x * x, axis=-1, keepdims=True)
    return x * lax.rsqrt(ms + NORM_EPS) * g


def _layer_resident(stacked, layer):
    shape = stacked.shape[1:]
    return pl.BlockSpec((None,) + shape, lambda *_: (layer,) + (0,) * len(shape),
                        pipeline_mode=pl.Buffered(1))


def _swiglu_residual(x, g_ref, wg_ref, wu_ref, wd_ref):
    xn = _rms_rows(x, g_ref[...]).astype(BF16)
    f = wg_ref.shape[1]
    step = -(-f // (FFN_SPLIT * MXU_W)) * MXU_W
    y = None
    for lo in range(0, f, step):
        hi = min(lo + step, f)
        gate = _dot(xn, wg_ref[:, lo:hi])
        up = _dot(xn, wu_ref[:, lo:hi])
        part = _dot((_silu(gate) * up).astype(BF16), wd_ref[lo:hi, :])
        y = part if y is None else y + part
    return x + 0.5 * y


def _ffn_kernel(h_ref, g_ref, wg_ref, wu_ref, wd_ref, o_ref):
    o_ref[...] = _swiglu_residual(h_ref[...], g_ref, wg_ref, wu_ref, wd_ref)


def _ffn(h, layer, g, wg, wu, wd):
    t, d = h.shape
    tm = min(FFN_TILE, t)
    return pl.pallas_call(
        _ffn_kernel,
        out_shape=jax.ShapeDtypeStruct((t, d), F32),
        grid=(t // tm,),
        in_specs=[pl.BlockSpec((tm, d), lambda i: (i, 0))]
                 + [_layer_resident(p, layer) for p in (g, wg, wu, wd)],
        out_specs=pl.BlockSpec((tm, d), lambda i: (i, 0)),
        compiler_params=_cparams("parallel"),
        name="ffn",
    )(h, g, wg, wu, wd)


def _rope(x, cos, sin):
    w = x.shape[1]
    lane = lax.broadcasted_iota(jnp.int32, x.shape, 1)
    first = (lane & (HEAD_DIM - 1)) < HEAD_DIM // 2
    half = HEAD_DIM // 2
    partner = jnp.where(first, pltpu.roll(x, w - half, 1), pltpu.roll(x, half, 1))
    return x * cos + partner * sin


def _proj_kernel(h_ref, g_ref, w_ref, cos_ref, sin_ref, gaq_ref, gak_ref, gbq_ref, gbk_ref,
                 mblk_ref, aq_o, ak_o, avt_o, bq_o, bk_o, bv_o, cx_o, cz_o, gt_o):
    xn = _rms_rows(h_ref[...], g_ref[...]).astype(BF16)
    pr = _dot(xn, w_ref[...])
    avt_o[...] = pr[:, OFF_A + A_QW + A_KVW:OFF_B].T.astype(BF16)

    def unit_rms(lo, width):
        outs = []
        for c in range(lo, lo + width, MXU_W):
            w = min(MXU_W, lo + width - c)
            x = pr[:, c:c + w]
            outs.append(x * lax.rsqrt(_split_dot(x * x, mblk_ref[:w, :w]) + NORM_EPS))
        return outs[0] if len(outs) == 1 else jnp.concatenate(outs, axis=-1)

    cos = cos_ref[...]
    sin = sin_ref[...]
    aq_o[...] = _rope(unit_rms(OFF_A, A_QW) * gaq_ref[...], cos, sin).astype(BF16)
    ak = unit_rms(OFF_A + A_QW, A_KVW) * gak_ref[...]
    ak_o[...] = _rope(ak, cos[:, :A_KVW], sin[:, :A_KVW]).astype(BF16)
    bqk = unit_rms(OFF_B, 2 * B_W)
    bq_o[...] = bqk[:, :B_W] * gbq_ref[...]
    bk_o[...] = bqk[:, B_W:] * gbk_ref[...]
    bv_o[...] = pr[:, OFF_B + 2 * B_W:OFF_C]
    cx_o[...] = pr[:, OFF_C:OFF_Z]
    cz_o[...] = pr[:, OFF_Z:OFF_G].astype(BF16)
    gt_o[...] = pr[:, OFF_G:]


def _proj(h, layer, g, w, cos, sin, gaq, gak, gbq, gbk, mblk, seq):
    t, d = h.shape
    tm = min(TOKEN_TILE, seq)
    per_seq = seq // tm
    row = lambda i: (i, 0)
    pos = lambda i: (i % per_seq, 0)
    widths = (A_QW, A_KVW, None, B_W, B_W, B_W, 3 * C_W, C_W, GATE_W)
    dtypes = (BF16, BF16, BF16, F32, F32, F32, F32, BF16, F32)
    shapes = [(t, wd) if wd else (A_KVW, t) for wd in widths]
    specs = [pl.BlockSpec((tm, wd), row) if wd else pl.BlockSpec((A_KVW, tm), lambda i: (0, i))
             for wd in widths]
    return pl.pallas_call(
        _proj_kernel,
        out_shape=[jax.ShapeDtypeStruct(s, dt) for s, dt in zip(shapes, dtypes)],
        grid=(t // tm,),
        in_specs=[pl.BlockSpec((tm, d), row),
                  _layer_resident(g, layer), _layer_resident(w, layer),
                  pl.BlockSpec((tm, A_QW), pos),
                  pl.BlockSpec((tm, A_QW), pos),
                  _resident((1, A_QW)), _resident((1, A_KVW)), _resident((1, B_W)),
                  _resident((1, B_W)), _resident((MXU_W, MXU_W))],
        out_specs=specs,
        compiler_params=_cparams("parallel"),
        name="mix_in",
    )(h, g, w, cos, sin, gaq, gak, gbq, gbk, mblk)


A_Q_HEAD_ORDER = tuple(hk * (A_HEADS // A_KV_HEADS) + g
                       for g in range(A_HEADS // A_KV_HEADS) for hk in range(A_KV_HEADS))


def _attn_a_kernel(q_ref, k_ref, vt_ref, o_ref):
    seq = k_ref.shape[0]
    lane_hi = lax.broadcasted_iota(jnp.int32, (1, LANES), 1) >= HEAD_DIM
    row_hi = lax.broadcasted_iota(jnp.int32, (LANES, 1), 0) >= HEAD_DIM
    heads = [(g, hk) for g in range(A_HEADS // A_KV_HEADS) for hk in range(A_KV_HEADS)]
    qm = []
    for g, hk in heads:
        qb = q_ref[:, g * LANES:(g + 1) * LANES].astype(F32)
        qm.append(jnp.where(lane_hi == (hk == 1), qb, 0.0).astype(BF16))
    def scores(c):
        k_c = k_ref[c:c + A_K_CHUNK, :]
        return [_dot_nt(k_c, q) for q in qm]

    m_run = [None] * len(heads)
    acc = [None] * len(heads)
    sts_next = scores(0)
    for c in range(0, seq, A_K_CHUNK):
        sts = sts_next
        if c + A_K_CHUNK < seq:
            sts_next = scores(c + A_K_CHUNK)
        vt_c = vt_ref[:, c:c + A_K_CHUNK].astype(F32)
        vt_ones = [jnp.where(row_hi == (hk == 1), vt_c, 1.0).astype(BF16) for hk in range(A_KV_HEADS)]
        ps, m_news = [], []
        for i, st in enumerate(sts):
            m_new = jnp.max(st, axis=0, keepdims=True)
            if m_run[i] is not None:
                m_new = jnp.maximum(m_run[i], m_new)
            m_news.append(m_new)
            ps.append(jnp.exp2(st - m_new).astype(BF16))
        for i, (g, hk) in enumerate(heads):
            ov = _dot(vt_ones[hk], ps[i])
            acc[i] = ov if acc[i] is None else acc[i] * jnp.exp2(m_run[i] - m_news[i]) + ov
            m_run[i] = m_news[i]
    for g in range(A_HEADS // A_KV_HEADS):
        halves = []
        for hk in range(A_KV_HEADS):
            ov = acc[g * A_KV_HEADS + hk]
            den_row = (1 - hk) * HEAD_DIM
            halves.append(ov / ov[den_row:den_row + 1, :])
        ot = jnp.where(row_hi, halves[1], halves[0])
        o_ref[:, g * LANES:(g + 1) * LANES] = ot.T.astype(BF16)


def _attn_a(q, k, vt, seq):
    t = q.shape[0]
    tq = min(A_Q_TILE, seq)
    per_seq = seq // tq
    return pl.pallas_call(
        _attn_a_kernel,
        out_shape=jax.ShapeDtypeStruct((t, A_QW), BF16),
        grid=(t // seq, per_seq),
        in_specs=[pl.BlockSpec((tq, A_QW), lambda b, i: (b * per_seq + i, 0)),
                  pl.BlockSpec((seq, A_KVW), lambda b, i: (b, 0)),
                  pl.BlockSpec((A_KVW, seq), lambda b, i: (0, b))],
        out_specs=pl.BlockSpec((tq, A_QW), lambda b, i: (b * per_seq + i, 0)),
        compiler_params=_cparams("parallel", "parallel"),
        name="attn_a",
    )(q, k, vt)


def _t5_bucket_np(rel):
    half = REL_BUCKETS // 2
    exact = half // 2
    sign = np.where(rel > 0, half, 0)
    n = np.abs(rel)
    nf = np.maximum(n, 1).astype(np.float32)
    large = exact + (np.log(nf / np.float32(exact)) / np.float32(math.log(REL_MAX_DIST / exact))
                     * np.float32(half - exact)).astype(np.int32)
    large = np.minimum(large, half - 1)
    return sign + np.where(n < exact, n, large)


def _branch_tiles(sub_len):
    if sub_len >= B_K_TILE:
        return B_Q_TILE, B_K_TILE, (0, -B_SIDE, B_Q_TILE - B_K_TILE)
    return sub_len, sub_len, (0,)


def _branch_bias(rel_bias, dil, sub_len):
    tq, tk, offs = _branch_tiles(sub_len)
    period = tq + tk
    slot = np.arange(period)
    tabs = []
    for off in offs:
        delta = np.where(slot < tk, slot, slot - period) + off
        bucket = _t5_bucket_np(delta * dil)
        vec = jnp.where((np.abs(delta) <= B_SIDE)[:, None], rel_bias.astype(F32)[bucket] * LOG2E, NEG_INF).T
        flat = jnp.tile(vec, (1, tq))[:, :tq * (period - 1)]
        tabs.append(flat.reshape(B_HEADS, tq, period - 1)[:, :, :tk])
    return jnp.stack(tabs)


def _attn_b_kernel(q_ref, k_ref, v_ref, b0_ref, b1_ref, b2_ref, o_ref, num_s, mx_s, den_s, pre_s, *, seq):
    head0 = lax.broadcasted_iota(jnp.int32, (1, LANES), 1) < HEAD_DIM
    pick = lambda a, b: jnp.where(head0, a, b)

    pre_len = seq // B_PRE
    for x, src in enumerate((q_ref, k_ref, v_ref)):
        for j in range(B_PRE):
            pre_s[x, j * pre_len:(j + 1) * pre_len, :] = src[pl.ds(j, pre_len, stride=B_PRE), :]

    tiles = []
    for g, ((_, dil), bias_ref) in enumerate(zip(DIL_PAIRS, (b0_ref, b1_ref, b2_ref))):
        sub_len = seq // dil
        tq, tk, _ = _branch_tiles(sub_len)
        n_tiles = sub_len // tq
        for r in range(dil):
            for t in range(n_tiles):
                if n_tiles == 1:
                    k0, tab = 0, 0
                else:
                    k0 = min(max(t * tq - B_SIDE, 0), sub_len - tk)
                    tab = 0 if t == 0 else 2 if t == n_tiles - 1 else 1
                if dil % B_PRE == 0:
                    srcs = tuple(pre_s.at[x] for x in range(3))
                    base, step = (r % B_PRE) * pre_len + r // B_PRE, dil // B_PRE
                else:
                    srcs, base, step = (q_ref, k_ref, v_ref), r, dil
                tiles.append((g, bias_ref, tab, srcs,
                              pl.ds(base + step * t * tq, tq, stride=step),
                              pl.ds(base + step * k0, tk, stride=step),
                              pl.ds(r + dil * t * tq, tq, stride=dil)))

    for g, bias_ref, tab, (q_src, k_src, v_src), q_rows, keys, rows in tiles:
        tq = rows.size
        qt = q_src[q_rows, :]
        q2 = jnp.concatenate([pick(qt, 0.0), pick(0.0, qt)], axis=0).astype(BF16)
        s = _dot_nt(q2, k_src[keys, :].astype(BF16)) + bias_ref[tab]
        m = jnp.max(s, axis=-1, keepdims=True)
        p = jnp.exp2(s - m)
        l = jnp.sum(p, axis=-1, keepdims=True)
        pv = _dot(p.astype(BF16), v_src[keys, :].astype(BF16))
        num_s.at[g][rows, :] = pick(pv[:tq], pv[tq:])
        mx_s.at[g][rows, :] = pick(m[:tq], m[tq:])
        den_s.at[g][rows, :] = pick(l[:tq], l[tq:])

    def merge(t, carry):
        rows = pl.ds(pl.multiple_of(t * CONV_TILE, CONV_TILE), CONV_TILE)
        ms = [mx_s[g, rows, :] for g in range(3)]
        top = jnp.maximum(jnp.maximum(ms[0], ms[1]), ms[2])
        es = [jnp.exp2(m - top) for m in ms]
        num = sum(e * num_s[g, rows, :] for g, e in enumerate(es))
        den = sum(e * den_s[g, rows, :] for g, e in enumerate(es))
        o_ref[rows, :] = (num / den).astype(BF16)
        return carry

    lax.fori_loop(0, seq // CONV_TILE, merge, 0)


def _attn_b(q, k, v, biases, batch, seq):
    t = q.shape[0]
    pair = pl.BlockSpec((seq, LANES), lambda b, p: (b, p))
    biases = [x.reshape(x.shape[0], B_HEADS // 2, 2 * x.shape[2], x.shape[3]) for x in biases]
    bias_spec = lambda x: pl.BlockSpec((x.shape[0], None) + x.shape[2:], lambda b, p: (0, p, 0, 0))
    return pl.pallas_call(
        functools.partial(_attn_b_kernel, seq=seq),
        out_shape=jax.ShapeDtypeStruct((t, B_W), BF16),
        grid=(batch, B_HEADS // 2),
        in_specs=[pair, pair, pair] + [bias_spec(x) for x in biases],
        out_specs=pair,
        scratch_shapes=[pltpu.VMEM((len(DIL_PAIRS), seq, LANES), F32)] * 4,
        compiler_params=_cparams("parallel", "parallel"),
        name="attn_b",
    )(q, k, v, *biases)


BASE_BLOCK = 8


def _pair_block_diag(x, same_head):
    reps = same_head.shape[0] // x.shape[0]
    return jnp.where(same_head, jnp.concatenate([x] * reps, axis=0), 0.0).astype(BF16)


def _inv_unit_triangular_pairs(ms, same_block, same_head, eye):
    bd = lambda x: _pair_block_diag(x, same_head)
    b16 = lambda x: x.astype(BF16)
    m8 = [b16(jnp.where(same_block[BASE_BLOCK], m, 0.0)) for m in ms]
    p2 = [_dot(a, bd(a.astype(F32))) for a in m8]
    yield
    p2h = [b16(p) for p in p2]
    p4 = [_dot(a, bd(p)) for a, p in zip(p2h, p2)]
    yield
    p6 = [_dot(a, bd(p)) for a, p in zip(p2h, p4)]
    yield
    even = [eye + a + b + c for a, b, c in zip(p2, p4, p6)]
    inv = [e - _dot(a, bd(e)) for a, e in zip(m8, even)]
    yield
    size = BASE_BLOCK
    while size < CHUNK:
        joins = same_block[2 * size] & jnp.logical_not(same_block[size])
        t = [_dot(b16(jnp.where(joins, m, 0.0)), bd(x)) for m, x in zip(ms, inv)]
        yield
        inv = [x - _dot(b16(x), bd(y)) for x, y in zip(inv, t)]
        yield
        size *= 2
    return inv


def _delta_kernel(xq_ref, xk_ref, xv_ref, wq_ref, wk_ref, wv_ref, z_ref, gt_ref,
                  alog_ref, dtb_ref, gain_ref, o_ref,
                  pad_s, q_s, k_s, v_s, beta_s, g_s,
                  u_s, lq_s, a_s, kw_s, ku_s, eg_s, *, seq):
    n_chunks = seq // CHUNK
    pad = 8
    sq_r = lax.broadcasted_iota(jnp.int32, (LANES, LANES), 0)
    sq_c = lax.broadcasted_iota(jnp.int32, (LANES, LANES), 1)
    same_head = (sq_r < HEAD_DIM) == (sq_c < HEAD_DIM)
    head_ones = same_head.astype(BF16)

    zeros = jnp.zeros((pad, LANES), F32)
    for j, (x_ref, w_ref, dst, scale) in enumerate(((xq_ref, wq_ref, q_s, HEAD_DIM ** -0.5),
                                                     (xk_ref, wk_ref, k_s, 1.0),
                                                     (xv_ref, wv_ref, v_s, None))):
        pad_s[j, 0:pad, :] = zeros
        pad_s[j, pad + seq:pad + seq + pad, :] = zeros
        pad_s[j, pad:pad + seq, :] = x_ref[...]
        for r0 in range(0, seq, CONV_TILE):
            acc = jnp.zeros((CONV_TILE, LANES), F32)
            for d in range(CONV_K):
                lo = r0 + pad - CONV_K // 2 + d
                acc = acc + w_ref[d:d + 1, :] * pad_s[j, lo:lo + CONV_TILE, :]
            y = _silu(acc)
            if scale is not None:
                y = y * lax.rsqrt(_split_dot(y * y, head_ones) + NORM_EPS) * scale
            dst[r0:r0 + CONV_TILE, :] = y

    gt = gt_ref[...]
    beta_s[...] = 1.0 / (1.0 + jnp.exp(-gt))
    gx = gt + dtb_ref[0]
    softplus = jnp.maximum(gx, 0.0) + jnp.log(1.0 + jnp.exp(-jnp.abs(gx)))
    g_s[...] = -jnp.exp(alog_ref[0]) * softplus

    wide = 2 * LANES
    lane_w = lax.broadcasted_iota(jnp.int32, (CHUNK, wide), 1)
    rows = lax.broadcasted_iota(jnp.int32, (CHUNK, wide), 0)
    cols = lane_w & (HEAD_DIM - 1)
    fwd = lane_w < LANES
    fwd_row = lax.broadcasted_iota(jnp.int32, (1, wide), 1) < LANES
    head0 = lax.broadcasted_iota(jnp.int32, (CHUNK, LANES), 1) < HEAD_DIM
    eye_b = rows == cols
    rows_p = lax.broadcasted_iota(jnp.int32, (CHUNK, LANES), 0)
    cols_p = lax.broadcasted_iota(jnp.int32, (CHUNK, LANES), 1) & (HEAD_DIM - 1)
    eye = (rows_p == cols_p).astype(F32)
    same_block = {}
    size = BASE_BLOCK
    while size <= CHUNK:
        shift = size.bit_length() - 1
        same_block[size] = (rows_p >> shift) == (cols_p >> shift)
        size *= 2
    r2 = lax.broadcasted_iota(jnp.int32, (2 * CHUNK, CHUNK), 0)
    c2 = lax.broadcasted_iota(jnp.int32, (2 * CHUNK, CHUNK), 1)
    tri2 = (jnp.where(r2 < CHUNK, r2 - c2, c2 - (r2 - CHUNK)) >= 0).astype(BF16)
    below = jnp.where(fwd, rows - cols, cols - rows)
    incl = below >= 0
    strict = below > 0
    bd = lambda x: _pair_block_diag(x, same_head)
    both = lambda x: jnp.concatenate([x, x], axis=-1)
    halves = (slice(0, LANES), slice(LANES, wide))

    def spread(x, c0):
        col = lambda c: jnp.broadcast_to(x[:, c:c + 1], (CHUNK, LANES))
        return jnp.concatenate([jnp.where(head0, col(c0), col(c0 + 1)),
                                jnp.where(head0, col(c0 + 2), col(c0 + 3))], axis=-1)

    def chunk_group(it):
        ns = [it * CHUNK_GROUP + g for g in range(CHUNK_GROUP)]
        sls = [pl.ds(n * CHUNK, CHUNK) for n in ns]
        k2 = [k_s[sl, :] for sl in sls]
        q2 = [q_s[sl, :] for sl in sls]
        gsum = []
        for sl in sls:
            g2 = g_s[sl, :]
            hi = g2.astype(BF16)
            lo = (g2 - hi.astype(F32)).astype(BF16)
            gsum.append(_dot(tri2, jnp.concatenate([hi, lo], axis=-1)))
        kq = [_dot_nt(jnp.concatenate([k, q], axis=0).astype(BF16), bd(k)) for k, q in zip(k2, q2)]
        yield
        beta, egc, kd, ms = [], [], [], []
        for g, sl in enumerate(sls):
            gs = gsum[g][:, :LANES] + gsum[g][:, LANES:]
            gcum = jnp.concatenate([gs[:CHUNK], gs[CHUNK:]], axis=-1)
            beta.append(spread(beta_s[sl, :], 0))
            gc = jnp.where(fwd, spread(gcum[:, :LANES], 4), spread(gcum[:, LANES:], 4))
            gc_row = jnp.sum(jnp.where(eye_b, gc, 0.0), axis=0, keepdims=True)
            decay = jnp.where(incl, jnp.exp(gc - gc_row), 0.0)
            ms.append(jnp.where(strict, beta[g] * both(kq[g][:CHUNK]) * decay, 0.0))
            a = (both(kq[g][CHUNK:]) * decay).astype(BF16)
            egc.append(jnp.exp(gc))
            g_last = jnp.where(fwd_row, gc[CHUNK - 1:CHUNK, :], gc[0:1, :])
            kd.append((both(k2[g]) * jnp.exp(g_last - gc)).astype(BF16))
            eg = jnp.exp(g_last)
            rows8 = pl.ds(ns[g] * 8, 8)
            for d in range(2):
                a_s[d, sl, :] = a[:, d * LANES:(d + 1) * LANES]
                eg_s[d, rows8, :] = jnp.broadcast_to(eg[:, d * LANES:(d + 1) * LANES], (8, LANES))
        chains = [(g, d) for g in range(CHUNK_GROUP) for d in range(2)]
        inv = yield from _inv_unit_triangular_pairs(
            [ms[g][:, halves[d]] for g, d in chains], same_block, same_head, eye)
        ys = [jnp.concatenate([bd((k2[g] * (beta[g] * egc[g])[:, halves[d]])),
                               bd(v_s[sls[g], :] * beta[g][:, halves[d]])], axis=-1) for g, d in chains]
        wus = [_dot(x.astype(BF16), y) for x, y in zip(inv, ys)]
        yield
        kwu = [_dot_tn(kd[g][:, halves[d]], wu.astype(BF16)) for wu, (g, d) in zip(wus, chains)]
        for wu, x, (g, d) in zip(wus, kwu, chains):
            u_s[d, sls[g], :] = wu[:, LANES:]
            qg = q2[g] * egc[g][:, halves[d]]
            lq_s[d, ns[g]] = jnp.concatenate([wu[:, :LANES], qg], axis=0).astype(BF16)
            kw_s[d, ns[g]] = jnp.where(same_head, x[:, :LANES], 0.0).astype(BF16)
            ku_s[d, ns[g]] = jnp.where(same_head, x[:, LANES:], 0.0)

    state = [jnp.zeros((LANES, LANES), F32) for _ in range(2)]
    outs = [[None] * n_chunks for _ in range(2)]
    pending = [None, None]
    steps = [0, 0]
    finished = []

    def chunk_of(d, k):
        return k if d == 0 else n_chunks - 1 - k

    def finish_chunk(n):
        sl = pl.ds(n * CHUNK, CHUNK)
        o = outs[0][n] + outs[1][n]
        ms = _split_dot(o * o, head_ones) * (1.0 / HEAD_DIM)
        o = o * lax.rsqrt(ms + NORM_EPS) * gain_ref[...]
        o_ref[sl, :] = (o * _silu(z_ref[sl, :].astype(F32))).astype(BF16)

    def emit_output(d):
        n, res = pending[d]
        sl = pl.ds(n * CHUNK, CHUNK)
        v_new = u_s[d, sl, :] - res[LANES:LANES + CHUNK]
        outs[d][n] = res[LANES + CHUNK:] + _dot(a_s[d, sl, :], bd(v_new))
        pending[d] = None
        if outs[1 - d][n] is not None:
            finished.append(n)

    def scan_step(ds):
        older = list(finished)
        del finished[:]
        new = []
        for d in ds:
            n = chunk_of(d, steps[d])
            res = _dot(jnp.concatenate([kw_s[d, n], lq_s[d, n]], axis=0), state[d].astype(BF16))
            state[d] = state[d] * eg_s[d, n * 8:n * 8 + 1, :] - res[:LANES] + ku_s[d, n]
            new.append((d, n, res))
            steps[d] += 1
        for n in older:
            finish_chunk(n)
        for d, n, res in new:
            if pending[d] is not None:
                emit_output(d)
            pending[d] = (n, res)

    n_groups = n_chunks // CHUNK_GROUP
    order = [g // 2 if g % 2 == 0 else n_groups - 1 - g // 2 for g in range(n_groups)]
    done = set()

    def ready(d):
        return steps[d] < n_chunks and chunk_of(d, steps[d]) // CHUNK_GROUP in done

    for it in order:
        for _ in chunk_group(it):
            ds = [d for d in range(2) if ready(d)]
            if ds:
                scan_step(ds)
        done.add(it)
    while steps[0] < n_chunks or steps[1] < n_chunks:
        scan_step([d for d in range(2) if ready(d)])
    for d in range(2):
        emit_output(d)
    for n in finished:
        finish_chunk(n)


def _delta(cx, cz, gt, conv_w, alog, dtb, gain, batch, seq):
    t = cx.shape[0]
    xspec = lambda part: pl.BlockSpec((seq, LANES), lambda b, p: (b, part * C_PAIRS + p))
    wspec = lambda part: pl.BlockSpec((CONV_K, LANES), lambda b, p: (0, part * C_PAIRS + p))
    pair = pl.BlockSpec((seq, LANES), lambda b, p: (b, p))
    prow = pl.BlockSpec((1, 1, LANES), lambda b, p: (p, 0, 0))
    return pl.pallas_call(
        functools.partial(_delta_kernel, seq=seq),
        out_shape=jax.ShapeDtypeStruct((t, C_W), BF16),
        grid=(batch, C_PAIRS),
        in_specs=[xspec(0), xspec(1), xspec(2), wspec(0), wspec(1), wspec(2),
                  pair, pair, prow, prow,
                  _resident((1, LANES))],
        out_specs=pair,
        scratch_shapes=[pltpu.VMEM((3, seq + 16, LANES), F32),
                        pltpu.VMEM((seq, LANES), F32),
                        pltpu.VMEM((seq, LANES), F32),
                        pltpu.VMEM((seq, LANES), F32),
                        pltpu.VMEM((seq, LANES), F32),
                        pltpu.VMEM((seq, LANES), F32),
                        pltpu.VMEM((2, seq, LANES), F32),
                        pltpu.VMEM((2, seq // CHUNK, LANES, LANES), BF16),
                        pltpu.VMEM((2, seq, LANES), BF16),
                        pltpu.VMEM((2, seq // CHUNK, LANES, LANES), BF16),
                        pltpu.VMEM((2, seq // CHUNK, LANES, LANES), F32),
                        pltpu.VMEM((2, seq // CHUNK * 8, LANES), F32)],
        compiler_params=_cparams("parallel", "parallel"),
        name="delta",
    )(cx, cx, cx, conv_w, conv_w, conv_w, cz, gt, alog, dtb, gain)


def _out_ffn_kernel(h_ref, a_ref, b_ref, c_ref, w_ref, g_ref, wg_ref, wu_ref, wd_ref, o_ref):
    y = _dot(a_ref[...], w_ref[0:A_QW, :])
    y = y + _dot(b_ref[...], w_ref[A_QW:A_QW + B_W, :])
    y = y + _dot(c_ref[...], w_ref[A_QW + B_W:, :])
    o_ref[...] = _swiglu_residual(h_ref[...] + y, g_ref, wg_ref, wu_ref, wd_ref)


def _mix_out_ffn(h, a, b, c, layer, w, g, wg, wu, wd):
    t, d = h.shape
    tm = min(FFN_TILE, t)
    row = lambda width: pl.BlockSpec((tm, width), lambda i: (i, 0))
    return pl.pallas_call(
        _out_ffn_kernel,
        out_shape=jax.ShapeDtypeStruct((t, d), F32),
        grid=(t // tm,),
        in_specs=[row(d), row(A_QW), row(B_W), row(C_W)]
                 + [_layer_resident(p, layer) for p in (w, g, wg, wu, wd)],
        out_specs=row(d),
        compiler_params=_cparams("parallel"),
        name="mix_out_ffn",
    )(h, a, b, c, w, g, wg, wu, wd)


def _rope_tables(seq):
    rows = seq // GRID_W
    row = jnp.repeat(jnp.arange(rows), GRID_W).astype(F32)
    col = jnp.tile(jnp.arange(GRID_W), rows).astype(F32)
    n_freq = HEAD_DIM // 4
    inv = ROPE_THETA ** (-jnp.arange(n_freq, dtype=F32) / n_freq)
    ang = jnp.concatenate([row[:, None] * inv, col[:, None] * inv], axis=-1)
    cos, sin = jnp.cos(ang), jnp.sin(ang)
    cos_h = jnp.concatenate([cos, cos], axis=-1)
    sin_h = jnp.concatenate([-sin, sin], axis=-1)
    return jnp.tile(cos_h, (1, A_HEADS)), jnp.tile(sin_h, (1, A_HEADS))


def _pair_columns(x):
    lead = x.shape[:-1]
    x = x.reshape(lead + (2, C_PAIRS, 2))
    return jnp.moveaxis(x, -2, -3).reshape(lead + (C_PAIRS, 4))


def _pad_w_in(w_in):
    lead = w_in.shape[:-1]
    w_in = w_in.astype(BF16)
    group = A_HEADS // A_KV_HEADS
    aq = w_in[..., :A_QW].reshape(lead + (A_KV_HEADS, group, HEAD_DIM))
    aq = jnp.swapaxes(aq, -3, -2).reshape(lead + (A_QW,))
    cb = _pair_columns(w_in[..., OFF_G:OFF_G + 2 * C_HEADS])
    ca = _pair_columns(w_in[..., OFF_G + 2 * C_HEADS:])
    gates = jnp.concatenate([cb, ca, jnp.zeros(lead + (C_PAIRS, LANES - 8), BF16)], axis=-1)
    return jnp.concatenate([aq, w_in[..., A_QW:OFF_G], gates.reshape(lead + (GATE_W,))], axis=-1)


def _pair_row(p):
    v = _pair_columns(p.reshape(2 * C_HEADS).astype(F32))
    z4 = jnp.zeros((C_PAIRS, 4), F32)
    return jnp.concatenate([z4, v, jnp.zeros((C_PAIRS, LANES - 8), F32)], axis=-1)[:, None, :]


def kernel(x, rel_bias, ffn1_norm, ffn1_w_gate, ffn1_w_up, ffn1_w_down, mix_norm, w_in, a_q_norm, a_k_norm, b_q_norm, b_k_norm, c_conv, c_A_log, c_dt_bias, c_out_norm, w_out, ffn2_norm, ffn2_w_gate, ffn2_w_up, ffn2_w_down):
    batch, seq, d = x.shape
    depth = w_in.shape[0]
    scale = HEAD_DIM ** -0.5 * LOG2E
    h = x.reshape(batch * seq, d)
    cos, sin = _rope_tables(seq)
    idx = np.arange(MXU_W)
    mblk = jnp.asarray((idx[:, None] // HEAD_DIM == idx[None, :] // HEAD_DIM) / HEAD_DIM, BF16)
    biases = [_branch_bias(rel_bias, dil, seq // dil) for _, dil in DIL_PAIRS]
    row = lambda v: v.reshape(1, -1).astype(F32)
    ffn1 = (ffn1_norm.astype(F32)[:, None, :], ffn1_w_gate.astype(BF16), ffn1_w_up.astype(BF16),
            ffn1_w_down.astype(BF16))
    ffn2 = (ffn2_norm.astype(F32)[:, None, :], ffn2_w_gate.astype(BF16), ffn2_w_up.astype(BF16),
            ffn2_w_down.astype(BF16))
    wo_a = [w_out[:, hq * HEAD_DIM:(hq + 1) * HEAD_DIM] for hq in A_Q_HEAD_ORDER]
    wo = jnp.concatenate(wo_a + [w_out[:, A_QW:]], axis=1).astype(BF16)
    mix_g = mix_norm.astype(F32)[:, None, :]
    w_proj = _pad_w_in(w_in)
    for l in range(depth):
        h = _ffn(h, l, *ffn1)
        aq, ak, avt, bq, bk, bv, cx, cz, gt = _proj(
            h, l, mix_g, w_proj, cos, sin,
            row(jnp.tile(a_q_norm[l], A_HEADS)) * scale, row(jnp.tile(a_k_norm[l], A_KV_HEADS)),
            row(jnp.tile(b_q_norm[l], B_HEADS)) * scale, row(jnp.tile(b_k_norm[l], B_HEADS)),
            mblk, seq)
        out_a = _attn_a(aq, ak, avt, seq)
        out_b = _attn_b(bq, bk, bv, biases, batch, seq)
        out_c = _delta(cx, cz, gt, c_conv[l].astype(F32), _pair_row(c_A_log[l]), _pair_row(c_dt_bias[l]),
                       row(jnp.tile(c_out_norm[l], 2)), batch, seq)
        h = _mix_out_ffn(h, out_a, out_b, out_c, l, wo, *ffn2)
    return h.reshape(batch, seq, d)
```

```python
import functools
import math

import numpy as np
import jax
import jax.numpy as jnp
from jax import lax
from jax.experimental import pallas as pl
from jax.experimental.pallas import tpu as pltpu

F32 = jnp.float32
BF16 = jnp.bfloat16

HEAD_DIM = 64
A_HEADS = 4
A_KV_HEADS = 2
B_HEADS = 6
C_HEADS = 6
GRID_W = 64
ROPE_THETA = 10000.0
DIL_PAIRS = ((128, 1), (512, 4), (2048, 16))
REL_BUCKETS = 32
REL_MAX_DIST = 1024
CONV_K = 5
CHUNK = 64
NORM_EPS = 1e-6
NEG_INF = -1e30
LOG2E = math.log2(math.e)

A_QW = A_HEADS * HEAD_DIM
A_KVW = A_KV_HEADS * HEAD_DIM
B_W = B_HEADS * HEAD_DIM
C_W = C_HEADS * HEAD_DIM
C_PAIRS = C_HEADS // 2
LANES = 128
MXU_W = 256
GATE_W = C_PAIRS * LANES

OFF_A = 0
OFF_B = OFF_A + A_QW + 2 * A_KVW
OFF_C = OFF_B + 3 * B_W
OFF_Z = OFF_C + 3 * C_W
OFF_G = OFF_Z + C_W
N_PROJ = OFF_G + GATE_W

TOKEN_TILE = 512
FFN_TILE = 1024
FFN_SPLIT = 2
A_Q_TILE = 512
A_K_CHUNK = 256
B_Q_TILE = 128
B_K_TILE = 256
B_SIDE = 64
B_PRE = 4
CONV_TILE = 256
CHUNK_GROUP = 8
VMEM_LIMIT = 56 * 1024 * 1024

assert A_KV_HEADS * HEAD_DIM == LANES and A_HEADS == 2 * A_KV_HEADS
assert B_HEADS % 2 == 0 and C_HEADS % 2 == 0 and 2 * HEAD_DIM == LANES
assert all(window == 2 * B_SIDE * dil for window, dil in DIL_PAIRS)


def _cparams(*sem):
    return pltpu.CompilerParams(dimension_semantics=sem, vmem_limit_bytes=VMEM_LIMIT)


def _resident(shape):
    return pl.BlockSpec(shape, lambda *_: (0,) * len(shape), pipeline_mode=pl.Buffered(1))


def _dot(a, b):
    return jnp.dot(a, b, preferred_element_type=F32)


def _dot_nt(a, b):
    return lax.dot_general(a, b, (((1,), (1,)), ((), ())), preferred_element_type=F32)


def _dot_tn(a, b):
    return lax.dot_general(a, b, (((0,), (0,)), ((), ())), preferred_element_type=F32)


def _silu(x):
    return x / (1.0 + jnp.exp(-x))


def _split_dot(x, m):
    hi = x.astype(BF16)
    lo = (x - hi.astype(F32)).astype(BF16)
    return _dot(hi, m) + _dot(lo, m)


def _rms_rows(x, g):
    ms = jnp.mean(x * x, axis=-1, keepdims=True)
    return x * lax.rsqrt(ms + NORM_EPS) * g


def _layer_resident(stacked, layer):
    shape = stacked.shape[1:]
    return pl.BlockSpec((None,) + shape, lambda *_: (layer,) + (0,) * len(shape),
                        pipeline_mode=pl.Buffered(1))


def _swiglu_residual(x, g_ref, wg_ref, wu_ref, wd_ref):
    xn = _rms_rows(x, g_ref[...]).astype(BF16)
    f = wg_ref.shape[1]
    step = -(-f // (FFN_SPLIT * MXU_W)) * MXU_W
    y = None
    for lo in range(0, f, step):
        hi = min(lo + step, f)
        gate = _dot(xn, wg_ref[:, lo:hi])
        up = _dot(xn, wu_ref[:, lo:hi])
        part = _dot((_silu(gate) * up).astype(BF16), wd_ref[lo:hi, :])
        y = part if y is None else y + part
    return x + 0.5 * y


def _ffn_kernel(h_ref, g_ref, wg_ref, wu_ref, wd_ref, o_ref):
    o_ref[...] = _swiglu_residual(h_ref[...], g_ref, wg_ref, wu_ref, wd_ref)


def _ffn(h, layer, g, wg, wu, wd):
    t, d = h.shape
    tm = min(FFN_TILE, t)
    return pl.pallas_call(
        _ffn_kernel,
        out_shape=jax.ShapeDtypeStruct((t, d), F32),
        grid=(t // tm,),
        in_specs=[pl.BlockSpec((tm, d), lambda i: (i, 0))]
                 + [_layer_resident(p, layer) for p in (g, wg, wu, wd)],
        out_specs=pl.BlockSpec((tm, d), lambda i: (i, 0)),
        compiler_params=_cparams("parallel"),
        name="ffn",
    )(h, g, wg, wu, wd)


def _rope(x, cos, sin):
    w = x.shape[1]
    lane = lax.broadcasted_iota(jnp.int32, x.shape, 1)
    first = (lane & (HEAD_DIM - 1)) < HEAD_DIM // 2
    half = HEAD_DIM // 2
    partner = jnp.where(first, pltpu.roll(x, w - half, 1), pltpu.roll(x, half, 1))
    return x * cos + partner * sin


def _proj_kernel(h_ref, g_ref, w_ref, cos_ref, sin_ref, gaq_ref, gak_ref, gbq_ref, gbk_ref,
                 mblk_ref, aq_o, ak_o, avt_o, bq_o, bk_o, bv_o, cx_o, cz_o, gt_o):
    xn = _rms_rows(h_ref[...], g_ref[...]).astype(BF16)
    pr = _dot(xn, w_ref[...])
    avt_o[...] = pr[:, OFF_A + A_QW + A_KVW:OFF_B].T.astype(BF16)

    def unit_rms(lo, width):
        outs = []
        for c in range(lo, lo + width, MXU_W):
            w = min(MXU_W, lo + width - c)
            x = pr[:, c:c + w]
            outs.append(x * lax.rsqrt(_split_dot(x * x, mblk_ref[:w, :w]) + NORM_EPS))
        return outs[0] if len(outs) == 1 else jnp.concatenate(outs, axis=-1)

    cos = cos_ref[...]
    sin = sin_ref[...]
    aq_o[...] = _rope(unit_rms(OFF_A, A_QW) * gaq_ref[...], cos, sin).astype(BF16)
    ak = unit_rms(OFF_A + A_QW, A_KVW) * gak_ref[...]
    ak_o[...] = _rope(ak, cos[:, :A_KVW], sin[:, :A_KVW]).astype(BF16)
    bqk = unit_rms(OFF_B, 2 * B_W)

    def put(out, x):
        for j in range(out.shape[0]):
            out[j] = x[:, j * LANES:(j + 1) * LANES].astype(out.dtype)

    put(bq_o, bqk[:, :B_W] * gbq_ref[...])
    put(bk_o, bqk[:, B_W:] * gbk_ref[...])
    put(bv_o, pr[:, OFF_B + 2 * B_W:OFF_C])
    put(cx_o, pr[:, OFF_C:OFF_Z])
    put(cz_o, pr[:, OFF_Z:OFF_G])
    put(gt_o, pr[:, OFF_G:])


def _proj(h, layer, g, w, cos, sin, gaq, gak, gbq, gbk, mblk, seq):
    t, d = h.shape
    tm = min(TOKEN_TILE, seq)
    per_seq = seq // tm
    row = lambda i: (i, 0)
    pos = lambda i: (i % per_seq, 0)
    widths = (A_QW, A_KVW, None, B_W, B_W, B_W, 3 * C_W, C_W, GATE_W)
    dtypes = (BF16, BF16, BF16, F32, F32, F32, F32, BF16, F32)
    shapes = [(t, wd) if wd else (A_KVW, t) for wd in widths]
    specs = [pl.BlockSpec((tm, wd), row) if wd else pl.BlockSpec((A_KVW, tm), lambda i: (0, i))
             for wd in widths]
    for k in range(3, len(widths)):
        n = widths[k] // LANES
        shapes[k] = (n, t, LANES)
        specs[k] = pl.BlockSpec((n, tm, LANES), lambda i: (0, i, 0))
    return pl.pallas_call(
        _proj_kernel,
        out_shape=[jax.ShapeDtypeStruct(s, dt) for s, dt in zip(shapes, dtypes)],
        grid=(t // tm,),
        in_specs=[pl.BlockSpec((tm, d), row),
                  _layer_resident(g, layer), _layer_resident(w, layer),
                  pl.BlockSpec((tm, A_QW), pos),
                  pl.BlockSpec((tm, A_QW), pos),
                  _resident((1, A_QW)), _resident((1, A_KVW)), _resident((1, B_W)),
                  _resident((1, B_W)), _resident((MXU_W, MXU_W))],
        out_specs=specs,
        compiler_params=_cparams("parallel"),
        name="mix_in",
    )(h, g, w, cos, sin, gaq, gak, gbq, gbk, mblk)


A_Q_HEAD_ORDER = tuple(hk * (A_HEADS // A_KV_HEADS) + g
                       for g in range(A_HEADS // A_KV_HEADS) for hk in range(A_KV_HEADS))


def _attn_a_kernel(q_ref, k_ref, vt_ref, o_ref):
    seq = k_ref.shape[0]
    lane_hi = lax.broadcasted_iota(jnp.int32, (1, LANES), 1) >= HEAD_DIM
    row_hi = lax.broadcasted_iota(jnp.int32, (LANES, 1), 0) >= HEAD_DIM
    heads = [(g, hk) for g in range(A_HEADS // A_KV_HEADS) for hk in range(A_KV_HEADS)]
    qm = []
    for g, hk in heads:
        qb = q_ref[:, g * LANES:(g + 1) * LANES].astype(F32)
        qm.append(jnp.where(lane_hi == (hk == 1), qb, 0.0).astype(BF16))
    def scores(c):
        k_c = k_ref[c:c + A_K_CHUNK, :]
        return [_dot_nt(k_c, q) for q in qm]

    m_run = [None] * len(heads)
    acc = [None] * len(heads)
    sts_next = scores(0)
    for c in range(0, seq, A_K_CHUNK):
        sts = sts_next
        if c + A_K_CHUNK < seq:
            sts_next = scores(c + A_K_CHUNK)
        vt_c = vt_ref[:, c:c + A_K_CHUNK].astype(F32)
        vt_ones = [jnp.where(row_hi == (hk == 1), vt_c, 1.0).astype(BF16) for hk in range(A_KV_HEADS)]
        ps, m_news = [], []
        for i, st in enumerate(sts):
            m_new = jnp.max(st, axis=0, keepdims=True)
            if m_run[i] is not None:
                m_new = jnp.maximum(m_run[i], m_new)
            m_news.append(m_new)
            ps.append(jnp.exp2(st - m_new).astype(BF16))
        for i, (g, hk) in enumerate(heads):
            ov = _dot(vt_ones[hk], ps[i])
            acc[i] = ov if acc[i] is None else acc[i] * jnp.exp2(m_run[i] - m_news[i]) + ov
            m_run[i] = m_news[i]
    for g in range(A_HEADS // A_KV_HEADS):
        halves = []
        for hk in range(A_KV_HEADS):
            ov = acc[g * A_KV_HEADS + hk]
            den_row = (1 - hk) * HEAD_DIM
            halves.append(ov / ov[den_row:den_row + 1, :])
        ot = jnp.where(row_hi, halves[1], halves[0])
        o_ref[:, g * LANES:(g + 1) * LANES] = ot.T.astype(BF16)


def _attn_a(q, k, vt, seq):
    t = q.shape[0]
    tq = min(A_Q_TILE, seq)
    per_seq = seq // tq
    return pl.pallas_call(
        _attn_a_kernel,
        out_shape=jax.ShapeDtypeStruct((t, A_QW), BF16),
        grid=(t // seq, per_seq),
        in_specs=[pl.BlockSpec((tq, A_QW), lambda b, i: (b * per_seq + i, 0)),
                  pl.BlockSpec((seq, A_KVW), lambda b, i: (b, 0)),
                  pl.BlockSpec((A_KVW, seq), lambda b, i: (0, b))],
        out_specs=pl.BlockSpec((tq, A_QW), lambda b, i: (b * per_seq + i, 0)),
        compiler_params=_cparams("parallel", "parallel"),
        name="attn_a",
    )(q, k, vt)


def _t5_bucket_np(rel):
    half = REL_BUCKETS // 2
    exact = half // 2
    sign = np.where(rel > 0, half, 0)
    n = np.abs(rel)
    nf = np.maximum(n, 1).astype(np.float32)
    large = exact + (np.log(nf / np.float32(exact)) / np.float32(math.log(REL_MAX_DIST / exact))
                     * np.float32(half - exact)).astype(np.int32)
    large = np.minimum(large, half - 1)
    return sign + np.where(n < exact, n, large)


def _branch_tiles(sub_len):
    if sub_len >= B_K_TILE:
        return B_Q_TILE, B_K_TILE, (0, -B_SIDE, B_Q_TILE - B_K_TILE)
    return sub_len, sub_len, (0,)


def _branch_bias(rel_bias, dil, sub_len):
    tq, tk, offs = _branch_tiles(sub_len)
    period = tq + tk
    slot = np.arange(period)
    tabs = []
    for off in offs:
        delta = np.where(slot < tk, slot, slot - period) + off
        bucket = _t5_bucket_np(delta * dil)
        vec = jnp.where((np.abs(delta) <= B_SIDE)[:, None], rel_bias.astype(F32)[bucket] * LOG2E, NEG_INF).T
        flat = jnp.tile(vec, (1, tq))[:, :tq * (period - 1)]
        tabs.append(flat.reshape(B_HEADS, tq, period - 1)[:, :, :tk])
    return jnp.stack(tabs)


def _attn_b_kernel(q_ref, k_ref, v_ref, b0_ref, b1_ref, b2_ref, o_ref, num_s, mx_s, den_s, pre_s, *, seq):
    head0 = lax.broadcasted_iota(jnp.int32, (1, LANES), 1) < HEAD_DIM
    pick = lambda a, b: jnp.where(head0, a, b)

    pre_len = seq // B_PRE
    for x, src in enumerate((q_ref, k_ref, v_ref)):
        for j in range(B_PRE):
            pre_s[x, j * pre_len:(j + 1) * pre_len, :] = src[pl.ds(j, pre_len, stride=B_PRE), :]

    tiles = []
    for g, ((_, dil), bias_ref) in enumerate(zip(DIL_PAIRS, (b0_ref, b1_ref, b2_ref))):
        sub_len = seq // dil
        tq, tk, _ = _branch_tiles(sub_len)
        n_tiles = sub_len // tq
        for r in range(dil):
            for t in range(n_tiles):
                if n_tiles == 1:
                    k0, tab = 0, 0
                else:
                    k0 = min(max(t * tq - B_SIDE, 0), sub_len - tk)
                    tab = 0 if t == 0 else 2 if t == n_tiles - 1 else 1
                if dil % B_PRE == 0:
                    srcs = tuple(pre_s.at[x] for x in range(3))
                    base, step = (r % B_PRE) * pre_len + r // B_PRE, dil // B_PRE
                else:
                    srcs, base, step = (q_ref, k_ref, v_ref), r, dil
                tiles.append((g, bias_ref, tab, srcs,
                              pl.ds(base + step * t * tq, tq, stride=step),
                              pl.ds(base + step * k0, tk, stride=step),
                              pl.ds(r + dil * t * tq, tq, stride=dil)))

    for g, bias_ref, tab, (q_src, k_src, v_src), q_rows, keys, rows in tiles:
        tq = rows.size
        qt = q_src[q_rows, :]
        q2 = jnp.concatenate([pick(qt, 0.0), pick(0.0, qt)], axis=0).astype(BF16)
        s = _dot_nt(q2, k_src[keys, :].astype(BF16)) + bias_ref[tab]
        m = jnp.max(s, axis=-1, keepdims=True)
        p = jnp.exp2(s - m)
        l = jnp.sum(p, axis=-1, keepdims=True)
        pv = _dot(p.astype(BF16), v_src[keys, :].astype(BF16))
        num_s.at[g][rows, :] = pick(pv[:tq], pv[tq:])
        mx_s.at[g][rows, :] = pick(m[:tq], m[tq:])
        den_s.at[g][rows, :] = pick(l[:tq], l[tq:])

    def merge(t, carry):
        rows = pl.ds(pl.multiple_of(t * CONV_TILE, CONV_TILE), CONV_TILE)
        ms = [mx_s[g, rows, :] for g in range(3)]
        top = jnp.maximum(jnp.maximum(ms[0], ms[1]), ms[2])
        es = [jnp.exp2(m - top) for m in ms]
        num = sum(e * num_s[g, rows, :] for g, e in enumerate(es))
        den = sum(e * den_s[g, rows, :] for g, e in enumerate(es))
        o_ref[rows, :] = (num / den).astype(BF16)
        return carry

    lax.fori_loop(0, seq // CONV_TILE, merge, 0)


def _attn_b(q, k, v, biases, batch, seq):
    pair = pl.BlockSpec((None, seq, LANES), lambda b, p: (p, b, 0))
    biases = [x.reshape(x.shape[0], B_HEADS // 2, 2 * x.shape[2], x.shape[3]) for x in biases]
    bias_spec = lambda x: pl.BlockSpec((x.shape[0], None) + x.shape[2:], lambda b, p: (0, p, 0, 0))
    return pl.pallas_call(
        functools.partial(_attn_b_kernel, seq=seq),
        out_shape=jax.ShapeDtypeStruct(q.shape, BF16),
        grid=(batch, B_HEADS // 2),
        in_specs=[pair, pair, pair] + [bias_spec(x) for x in biases],
        out_specs=pair,
        scratch_shapes=[pltpu.VMEM((len(DIL_PAIRS), seq, LANES), F32)] * 4,
        compiler_params=_cparams("parallel", "parallel"),
        name="attn_b",
    )(q, k, v, *biases)


BASE_BLOCK = 8


def _pair_block_diag(x, same_head):
    reps = same_head.shape[0] // x.shape[0]
    return jnp.where(same_head, jnp.concatenate([x] * reps, axis=0), 0.0).astype(BF16)


def _inv_unit_triangular_pairs(ms, same_block, same_head, eye):
    bd = lambda x: _pair_block_diag(x, same_head)
    b16 = lambda x: x.astype(BF16)
    m8 = [b16(jnp.where(same_block[BASE_BLOCK], m, 0.0)) for m in ms]
    p2 = [_dot(a, bd(a.astype(F32))) for a in m8]
    yield
    p2h = [b16(p) for p in p2]
    p4 = [_dot(a, bd(p)) for a, p in zip(p2h, p2)]
    yield
    p6 = [_dot(a, bd(p)) for a, p in zip(p2h, p4)]
    yield
    even = [eye + a + b + c for a, b, c in zip(p2, p4, p6)]
    inv = [e - _dot(a, bd(e)) for a, e in zip(m8, even)]
    yield
    size = BASE_BLOCK
    while size < CHUNK:
        joins = same_block[2 * size] & jnp.logical_not(same_block[size])
        t = [_dot(b16(jnp.where(joins, m, 0.0)), bd(x)) for m, x in zip(ms, inv)]
        yield
        inv = [x - _dot(b16(x), bd(y)) for x, y in zip(inv, t)]
        yield
        size *= 2
    return inv


def _delta_kernel(xq_ref, xk_ref, xv_ref, wq_ref, wk_ref, wv_ref, z_ref, gt_ref,
                  alog_ref, dtb_ref, gain_ref, o_ref,
                  pad_s, q_s, k_s, v_s, beta_s, g_s,
                  u_s, lq_s, a_s, kw_s, ku_s, eg_s, *, seq):
    n_chunks = seq // CHUNK
    pad = 8
    sq_r = lax.broadcasted_iota(jnp.int32, (LANES, LANES), 0)
    sq_c = lax.broadcasted_iota(jnp.int32, (LANES, LANES), 1)
    same_head = (sq_r < HEAD_DIM) == (sq_c < HEAD_DIM)
    head_ones = same_head.astype(BF16)

    zeros = jnp.zeros((pad, LANES), F32)
    for j, (x_ref, w_ref, dst, scale) in enumerate(((xq_ref, wq_ref, q_s, HEAD_DIM ** -0.5),
                                                     (xk_ref, wk_ref, k_s, 1.0),
                                                     (xv_ref, wv_ref, v_s, None))):
        pad_s[j, 0:pad, :] = zeros
        pad_s[j, pad + seq:pad + seq + pad, :] = zeros
        pad_s[j, pad:pad + seq, :] = x_ref[...]
        for r0 in range(0, seq, CONV_TILE):
            acc = jnp.zeros((CONV_TILE, LANES), F32)
            for d in range(CONV_K):
                lo = r0 + pad - CONV_K // 2 + d
                acc = acc + w_ref[d:d + 1, :] * pad_s[j, lo:lo + CONV_TILE, :]
            y = _silu(acc)
            if scale is not None:
                y = y * lax.rsqrt(_split_dot(y * y, head_ones) + NORM_EPS) * scale
            dst[r0:r0 + CONV_TILE, :] = y

    gt = gt_ref[...]
    beta_s[...] = 1.0 / (1.0 + jnp.exp(-gt))
    gx = gt + dtb_ref[0]
    softplus = jnp.maximum(gx, 0.0) + jnp.log(1.0 + jnp.exp(-jnp.abs(gx)))
    g_s[...] = -jnp.exp(alog_ref[0]) * softplus

    wide = 2 * LANES
    lane_w = lax.broadcasted_iota(jnp.int32, (CHUNK, wide), 1)
    rows = lax.broadcasted_iota(jnp.int32, (CHUNK, wide), 0)
    cols = lane_w & (HEAD_DIM - 1)
    fwd = lane_w < LANES
    fwd_row = lax.broadcasted_iota(jnp.int32, (1, wide), 1) < LANES
    head0 = lax.broadcasted_iota(jnp.int32, (CHUNK, LANES), 1) < HEAD_DIM
    eye_b = rows == cols
    rows_p = lax.broadcasted_iota(jnp.int32, (CHUNK, LANES), 0)
    cols_p = lax.broadcasted_iota(jnp.int32, (CHUNK, LANES), 1) & (HEAD_DIM - 1)
    eye = (rows_p == cols_p).astype(F32)
    same_block = {}
    size = BASE_BLOCK
    while size <= CHUNK:
        shift = size.bit_length() - 1
        same_block[size] = (rows_p >> shift) == (cols_p >> shift)
        size *= 2
    r2 = lax.broadcasted_iota(jnp.int32, (2 * CHUNK, CHUNK), 0)
    c2 = lax.broadcasted_iota(jnp.int32, (2 * CHUNK, CHUNK), 1)
    tri2 = (jnp.where(r2 < CHUNK, r2 - c2, c2 - (r2 - CHUNK)) >= 0).astype(BF16)
    below = jnp.where(fwd, rows - cols, cols - rows)
    incl = below >= 0
    strict = below > 0
    bd = lambda x: _pair_block_diag(x, same_head)
    both = lambda x: jnp.concatenate([x, x], axis=-1)
    halves = (slice(0, LANES), slice(LANES, wide))

    def spread(x, c0):
        col = lambda c: jnp.broadcast_to(x[:, c:c + 1], (CHUNK, LANES))
        return jnp.concatenate([jnp.where(head0, col(c0), col(c0 + 1)),
                                jnp.where(head0, col(c0 + 2), col(c0 + 3))], axis=-1)

    def chunk_group(it):
        ns = [it * CHUNK_GROUP + g for g in range(CHUNK_GROUP)]
        sls = [pl.ds(n * CHUNK, CHUNK) for n in ns]
        k2 = [k_s[sl, :] for sl in sls]
        q2 = [q_s[sl, :] for sl in sls]
        gsum = []
        for sl in sls:
            g2 = g_s[sl, :]
            hi = g2.astype(BF16)
            lo = (g2 - hi.astype(F32)).astype(BF16)
            gsum.append(_dot(tri2, jnp.concatenate([hi, lo], axis=-1)))
        kq = [_dot_nt(jnp.concatenate([k, q], axis=0).astype(BF16), bd(k)) for k, q in zip(k2, q2)]
        yield
        beta, egc, kd, ms = [], [], [], []
        for g, sl in enumerate(sls):
            gs = gsum[g][:, :LANES] + gsum[g][:, LANES:]
            gcum = jnp.concatenate([gs[:CHUNK], gs[CHUNK:]], axis=-1)
            beta.append(spread(beta_s[sl, :], 0))
            gc = jnp.where(fwd, spread(gcum[:, :LANES], 4), spread(gcum[:, LANES:], 4))
            gc_row = jnp.sum(jnp.where(eye_b, gc, 0.0), axis=0, keepdims=True)
            decay = jnp.where(incl, jnp.exp(gc - gc_row), 0.0)
            ms.append(jnp.where(strict, beta[g] * both(kq[g][:CHUNK]) * decay, 0.0))
            a = (both(kq[g][CHUNK:]) * decay).astype(BF16)
            egc.append(jnp.exp(gc))
            g_last = jnp.where(fwd_row, gc[CHUNK - 1:CHUNK, :], gc[0:1, :])
            kd.append((both(k2[g]) * jnp.exp(g_last - gc)).astype(BF16))
            eg = jnp.exp(g_last)
            rows8 = pl.ds(ns[g] * 8, 8)
            for d in range(2):
                a_s[d, sl, :] = a[:, d * LANES:(d + 1) * LANES]
                eg_s[d, rows8, :] = jnp.broadcast_to(eg[:, d * LANES:(d + 1) * LANES], (8, LANES))
        chains = [(g, d) for g in range(CHUNK_GROUP) for d in range(2)]
        inv = yield from _inv_unit_triangular_pairs(
            [ms[g][:, halves[d]] for g, d in chains], same_block, same_head, eye)
        ys = [jnp.concatenate([bd((k2[g] * (beta[g] * egc[g])[:, halves[d]])),
                               bd(v_s[sls[g], :] * beta[g][:, halves[d]])], axis=-1) for g, d in chains]
        wus = [_dot(x.astype(BF16), y) for x, y in zip(inv, ys)]
        yield
        kwu = [_dot_tn(kd[g][:, halves[d]], wu.astype(BF16)) for wu, (g, d) in zip(wus, chains)]
        for wu, x, (g, d) in zip(wus, kwu, chains):
            u_s[d, sls[g], :] = wu[:, LANES:]
            qg = q2[g] * egc[g][:, halves[d]]
            lq_s[d, ns[g]] = jnp.concatenate([wu[:, :LANES], qg], axis=0).astype(BF16)
            kw_s[d, ns[g]] = jnp.where(same_head, x[:, :LANES], 0.0).astype(BF16)
            ku_s[d, ns[g]] = jnp.where(same_head, x[:, LANES:], 0.0)

    state = [jnp.zeros((LANES, LANES), F32) for _ in range(2)]
    outs = [[None] * n_chunks for _ in range(2)]
    pending = [None, None]
    steps = [0, 0]
    finished = []

    def chunk_of(d, k):
        return k if d == 0 else n_chunks - 1 - k

    def finish_chunk(n):
        sl = pl.ds(n * CHUNK, CHUNK)
        o = outs[0][n] + outs[1][n]
        ms = _split_dot(o * o, head_ones) * (1.0 / HEAD_DIM)
        o = o * lax.rsqrt(ms + NORM_EPS) * gain_ref[...]
        o_ref[sl, :] = (o * _silu(z_ref[sl, :].astype(F32))).astype(BF16)

    def emit_output(d):
        n, res = pending[d]
        sl = pl.ds(n * CHUNK, CHUNK)
        v_new = u_s[d, sl, :] - res[LANES:LANES + CHUNK]
        outs[d][n] = res[LANES + CHUNK:] + _dot(a_s[d, sl, :], bd(v_new))
        pending[d] = None
        if outs[1 - d][n] is not None:
            finished.append(n)

    def scan_step(ds):
        older = list(finished)
        del finished[:]
        new = []
        for d in ds:
            n = chunk_of(d, steps[d])
            res = _dot(jnp.concatenate([kw_s[d, n], lq_s[d, n]], axis=0), state[d].astype(BF16))
            state[d] = state[d] * eg_s[d, n * 8:n * 8 + 1, :] - res[:LANES] + ku_s[d, n]
            new.append((d, n, res))
            steps[d] += 1
        for n in older:
            finish_chunk(n)
        for d, n, res in new:
            if pending[d] is not None:
                emit_output(d)
            pending[d] = (n, res)

    n_groups = n_chunks // CHUNK_GROUP
    order = [g // 2 if g % 2 == 0 else n_groups - 1 - g // 2 for g in range(n_groups)]
    done = set()

    def ready(d):
        return steps[d] < n_chunks and chunk_of(d, steps[d]) // CHUNK_GROUP in done

    for it in order:
        for _ in chunk_group(it):
            ds = [d for d in range(2) if ready(d)]
            if ds:
                scan_step(ds)
        done.add(it)
    while steps[0] < n_chunks or steps[1] < n_chunks:
        scan_step([d for d in range(2) if ready(d)])
    for d in range(2):
        emit_output(d)
    for n in finished:
        finish_chunk(n)


def _delta(cx, cz, gt, conv_w, alog, dtb, gain, batch, seq):
    xspec = lambda part: pl.BlockSpec((None, seq, LANES), lambda b, p: (part * C_PAIRS + p, b, 0))
    wspec = lambda part: pl.BlockSpec((CONV_K, LANES), lambda b, p: (0, part * C_PAIRS + p))
    pair = pl.BlockSpec((None, seq, LANES), lambda b, p: (p, b, 0))
    prow = pl.BlockSpec((1, 1, LANES), lambda b, p: (p, 0, 0))
    return pl.pallas_call(
        functools.partial(_delta_kernel, seq=seq),
        out_shape=jax.ShapeDtypeStruct(cz.shape, BF16),
        grid=(batch, C_PAIRS),
        in_specs=[xspec(0), xspec(1), xspec(2), wspec(0), wspec(1), wspec(2),
                  pair, pair, prow, prow,
                  _resident((1, LANES))],
        out_specs=pair,
        scratch_shapes=[pltpu.VMEM((3, seq + 16, LANES), F32),
                        pltpu.VMEM((seq, LANES), F32),
                        pltpu.VMEM((seq, LANES), F32),
                        pltpu.VMEM((seq, LANES), F32),
                        pltpu.VMEM((seq, LANES), F32),
                        pltpu.VMEM((seq, LANES), F32),
                        pltpu.VMEM((2, seq, LANES), F32),
                        pltpu.VMEM((2, seq // CHUNK, LANES, LANES), BF16),
                        pltpu.VMEM((2, seq, LANES), BF16),
                        pltpu.VMEM((2, seq // CHUNK, LANES, LANES), BF16),
                        pltpu.VMEM((2, seq // CHUNK, LANES, LANES), F32),
                        pltpu.VMEM((2, seq // CHUNK * 8, LANES), F32)],
        compiler_params=_cparams("parallel", "parallel"),
        name="delta",
    )(cx, cx, cx, conv_w, conv_w, conv_w, cz, gt, alog, dtb, gain)


def _out_ffn_kernel(h_ref, a_ref, b_ref, c_ref, w_ref, g_ref, wg_ref, wu_ref, wd_ref, o_ref):
    y = _dot(a_ref[...], w_ref[0:A_QW, :])
    for j, ref in enumerate((b_ref, c_ref)):
        x = jnp.concatenate([ref[p] for p in range(ref.shape[0])], axis=-1)
        y = y + _dot(x, w_ref[A_QW + j * B_W:A_QW + (j + 1) * B_W, :])
    o_ref[...] = _swiglu_residual(h_ref[...] + y, g_ref, wg_ref, wu_ref, wd_ref)


def _mix_out_ffn(h, a, b, c, layer, w, g, wg, wu, wd):
    t, d = h.shape
    tm = min(FFN_TILE, t)
    row = lambda width: pl.BlockSpec((tm, width), lambda i: (i, 0))
    return pl.pallas_call(
        _out_ffn_kernel,
        out_shape=jax.ShapeDtypeStruct((t, d), F32),
        grid=(t // tm,),
        in_specs=[row(d), row(A_QW)]
                 + [pl.BlockSpec((x.shape[0], tm, LANES), lambda i: (0, i, 0)) for x in (b, c)]
                 + [_layer_resident(p, layer) for p in (w, g, wg, wu, wd)],
        out_specs=row(d),
        compiler_params=_cparams("parallel"),
        name="mix_out_ffn",
    )(h, a, b, c, w, g, wg, wu, wd)


def _rope_tables(seq):
    rows = seq // GRID_W
    row = jnp.repeat(jnp.arange(rows), GRID_W).astype(F32)
    col = jnp.tile(jnp.arange(GRID_W), rows).astype(F32)
    n_freq = HEAD_DIM // 4
    inv = ROPE_THETA ** (-jnp.arange(n_freq, dtype=F32) / n_freq)
    ang = jnp.concatenate([row[:, None] * inv, col[:, None] * inv], axis=-1)
    cos, sin = jnp.cos(ang), jnp.sin(ang)
    cos_h = jnp.concatenate([cos, cos], axis=-1)
    sin_h = jnp.concatenate([-sin, sin], axis=-1)
    return jnp.tile(cos_h, (1, A_HEADS)), jnp.tile(sin_h, (1, A_HEADS))


def _pair_columns(x):
    lead = x.shape[:-1]
    x = x.reshape(lead + (2, C_PAIRS, 2))
    return jnp.moveaxis(x, -2, -3).reshape(lead + (C_PAIRS, 4))


def _pad_w_in(w_in):
    lead = w_in.shape[:-1]
    w_in = w_in.astype(BF16)
    group = A_HEADS // A_KV_HEADS
    aq = w_in[..., :A_QW].reshape(lead + (A_KV_HEADS, group, HEAD_DIM))
    aq = jnp.swapaxes(aq, -3, -2).reshape(lead + (A_QW,))
    cb = _pair_columns(w_in[..., OFF_G:OFF_G + 2 * C_HEADS])
    ca = _pair_columns(w_in[..., OFF_G + 2 * C_HEADS:])
    gates = jnp.concatenate([cb, ca, jnp.zeros(lead + (C_PAIRS, LANES - 8), BF16)], axis=-1)
    return jnp.concatenate([aq, w_in[..., A_QW:OFF_G], gates.reshape(lead + (GATE_W,))], axis=-1)


def _pair_row(p):
    v = _pair_columns(p.reshape(2 * C_HEADS).astype(F32))
    z4 = jnp.zeros((C_PAIRS, 4), F32)
    return jnp.concatenate([z4, v, jnp.zeros((C_PAIRS, LANES - 8), F32)], axis=-1)[:, None, :]


def kernel(x, rel_bias, ffn1_norm, ffn1_w_gate, ffn1_w_up, ffn1_w_down, mix_norm, w_in, a_q_norm, a_k_norm, b_q_norm, b_k_norm, c_conv, c_A_log, c_dt_bias, c_out_norm, w_out, ffn2_norm, ffn2_w_gate, ffn2_w_up, ffn2_w_down):
    batch, seq, d = x.shape
    depth = w_in.shape[0]
    scale = HEAD_DIM ** -0.5 * LOG2E
    h = x.reshape(batch * seq, d)
    cos, sin = _rope_tables(seq)
    idx = np.arange(MXU_W)
    mblk = jnp.asarray((idx[:, None] // HEAD_DIM == idx[None, :] // HEAD_DIM) / HEAD_DIM, BF16)
    biases = [_branch_bias(rel_bias, dil, seq // dil) for _, dil in DIL_PAIRS]
    row = lambda v: v.reshape(1, -1).astype(F32)
    ffn1 = (ffn1_norm.astype(F32)[:, None, :], ffn1_w_gate.astype(BF16), ffn1_w_up.astype(BF16),
            ffn1_w_down.astype(BF16))
    ffn2 = (ffn2_norm.astype(F32)[:, None, :], ffn2_w_gate.astype(BF16), ffn2_w_up.astype(BF16),
            ffn2_w_down.astype(BF16))
    wo_a = [w_out[:, hq * HEAD_DIM:(hq + 1) * HEAD_DIM] for hq in A_Q_HEAD_ORDER]
    wo = jnp.concatenate(wo_a + [w_out[:, A_QW:]], axis=1).astype(BF16)
    mix_g = mix_norm.astype(F32)[:, None, :]
    w_proj = _pad_w_in(w_in)
    for l in range(depth):
        h = _ffn(h, l, *ffn1)
        aq, ak, avt, bq, bk, bv, cx, cz, gt = _proj(
            h, l, mix_g, w_proj, cos, sin,
            row(jnp.tile(a_q_norm[l], A_HEADS)) * scale, row(jnp.tile(a_k_norm[l], A_KV_HEADS)),
            row(jnp.tile(b_q_norm[l], B_HEADS)) * scale, row(jnp.tile(b_k_norm[l], B_HEADS)),
            mblk, seq)
        out_a = _attn_a(aq, ak, avt, seq)
        out_b = _attn_b(bq, bk, bv, biases, batch, seq)
        out_c = _delta(cx, cz, gt, c_conv[l].astype(F32), _pair_row(c_A_log[l]), _pair_row(c_dt_bias[l]),
                       row(jnp.tile(c_out_norm[l], 2)), batch, seq)
        h = _mix_out_ffn(h, out_a, out_b, out_c, l, wo, *ffn2)
    return h.reshape(batch, seq, d)
```
